```python
import math
import jax
import jax.numpy as jnp
from jax import lax
import numpy as np

D_MODEL = 1024
BATCH = 4
SEQ = 4096
DEPTH = 4

GRID_W = 64
CTX_LEN = 256

ATT_HEADS = 6
ATT_KV_HEADS = 2
ATT_GROUP = ATT_HEADS // ATT_KV_HEADS
HEAD_DIM = 64
ATT_WIDTH = ATT_HEADS * HEAD_DIM
KV_WIDTH = ATT_KV_HEADS * HEAD_DIM
WINDOW = 128
BLOCK = 128
ROPE_BASE = 10000.0

HY_CH = 384
HY_ORDER = 2
POS_BANDS = 16
POS_EMB = 1 + 2 * POS_BANDS
FILT_HID = 64
N_FILT = 2 * HY_ORDER * HY_CH
DECAY_TARGET = 1e-2
MAX_DECAY_PCT = 0.3
MIN_DECAY_PCT = 1.5

ML_HEADS = 4
ML_DK = 64
ML_DV = 64
ML_WIDTH = ML_HEADS * ML_DV
CHUNK = 128

D_MIX = ATT_WIDTH + HY_CH + ML_WIDTH
PROJ_SIZES = (ATT_WIDTH, KV_WIDTH, KV_WIDTH, 3 * HY_CH, ML_HEADS * ML_DK, ML_HEADS * ML_DK, ML_WIDTH, ML_WIDTH, 4 * ML_HEADS)
P_IN = sum(PROJ_SIZES)
SPLIT_POINTS = tuple(sum(PROJ_SIZES[:i + 1]) for i in range(len(PROJ_SIZES) - 1))
D_FF = -(-8 * D_MODEL // (3 * 256)) * 256
NEG_INF = -1e30
EPS = 1e-6

kernel_name = 'hybrid_hyena_swa_mlstm_dit'


def rms_norm(x, w):
    xf = x.astype(jnp.float32)
    y = xf * lax.rsqrt(jnp.mean(xf * xf, axis=-1, keepdims=True) + EPS)
    return (y * w.astype(jnp.float32)).astype(x.dtype)


def modulate(x, w, shift, scale):
    return rms_norm(x, w) * (1 + scale) + shift


def swiglu(h, wg, wu, wd):
    return (jax.nn.silu(h @ wg) * (h @ wu)) @ wd


def axial_rope_tables(n_rows):
    row = jnp.repeat(jnp.arange(n_rows), GRID_W)
    col = jnp.tile(jnp.arange(GRID_W), n_rows)
    nf = HEAD_DIM // 4
    inv_freq = ROPE_BASE ** (-jnp.arange(nf, dtype=jnp.float32) / nf)
    ang = jnp.stack([row[:, None] * inv_freq, col[:, None] * inv_freq], axis=1)
    return jnp.cos(ang), jnp.sin(ang)


def apply_rope(x, cos, sin):
    B, L, H, _ = x.shape
    xr = x.astype(jnp.float32).reshape(B, L, H, 2, 2, HEAD_DIM // 4)
    x1, x2 = xr[..., 0, :], xr[..., 1, :]
    c, s = cos[None, :, None], sin[None, :, None]
    out = jnp.stack([x1 * c - x2 * s, x2 * c + x1 * s], axis=-2)
    return out.reshape(x.shape).astype(x.dtype)


def short_conv(u, w, b):
    L = u.shape[1]
    up = jnp.pad(u, ((0, 0), (1, 1), (0, 0)))
    return up[:, :L] * w[0] + up[:, 1:L + 1] * w[1] + up[:, 2:] * w[2] + b


def hyena_filters(L, w1, b1, freq, w2, b2, w3, decay):
    f32 = jnp.float32
    t = jnp.linspace(0.0, 1.0, L, dtype=f32)[:, None]
    ang = (2.0 * math.pi / L) * jnp.arange(L, dtype=f32)[:, None]
    bands = jnp.linspace(1e-4, POS_BANDS - 1, POS_BANDS, dtype=f32)[None, :]
    feats = jnp.concatenate([t, jnp.cos(bands * ang), -jnp.sin(bands * ang)], axis=-1)
    fr = freq.astype(f32)
    z = jnp.sin(fr * (feats @ w1.astype(f32) + b1.astype(f32)))
    z = jnp.sin(fr * (z @ w2.astype(f32) + b2.astype(f32)))
    filt = (z @ w3.astype(f32)) * jnp.exp(-t * jnp.abs(decay.astype(f32)))
    filt = filt.reshape(L, 2, HY_ORDER, HY_CH)
    fwd, bwd = filt[:, 0], filt[:, 1]
    circ = jnp.concatenate([fwd, jnp.zeros((1, HY_ORDER, HY_CH), f32), jnp.flip(bwd[:L - 1], axis=0)], axis=0)
    return jnp.fft.rfft(circ, axis=0)


def hyena_mix(u, filt_fft, skip):
    L = u.shape[1]
    v, x1, x2 = jnp.split(u.astype(jnp.float32), 3, axis=-1)
    sk = skip.astype(jnp.float32)

    def long_conv(s, o):
        y = jnp.fft.irfft(jnp.fft.rfft(s, n=2 * L, axis=1) * filt_fft[None, :, o], n=2 * L, axis=1)[:, :L]
        return y + s * sk[o]

    return (x2 * long_conv(x1 * long_conv(v, 0), 1)).astype(u.dtype)


def attn_heads(aq, ak, av, q_norm_w, k_norm_w):
    B, L = aq.shape[:2]
    q = rms_norm(aq.reshape(B, L, ATT_HEADS, HEAD_DIM), q_norm_w)
    k = rms_norm(ak.reshape(B, L, ATT_KV_HEADS, HEAD_DIM), k_norm_w)
    v = av.reshape(B, L, ATT_KV_HEADS, HEAD_DIM)
    return q, k, v


def band_blocks(t, nb):
    B = t.shape[0]
    tp = jnp.pad(t, ((0, 0), (BLOCK, BLOCK), (0, 0), (0, 0)))
    tb = tp.reshape(B, nb + 2, BLOCK, *t.shape[2:])
    return jnp.concatenate([tb[:, :-2], tb[:, 1:-1], tb[:, 2:]], axis=2)


def windowed_attention(q, k, v, kc, vc, sink):
    B, S = q.shape[:2]
    C = kc.shape[1]
    nb = S // BLOCK
    scale = HEAD_DIM ** -0.5
    qb = q.reshape(B, nb, BLOCK, ATT_KV_HEADS, ATT_GROUP, HEAD_DIM)
    kb, vb = band_blocks(k, nb), band_blocks(v, nb)
    qi = jnp.arange(BLOCK)[:, None]
    kj = jnp.arange(3 * BLOCK)[None, :]
    kpos = jnp.arange(nb)[:, None, None] * BLOCK + (kj - BLOCK)[None]
    mask = (jnp.abs(kj - BLOCK - qi) <= WINDOW)[None] & (kpos >= 0) & (kpos < S)
    s_loc = jnp.einsum('bnqgrd,bnkgd->bngrqk', qb, kb).astype(jnp.float32) * scale
    s_loc = jnp.where(mask[None, :, None, None], s_loc, NEG_INF)
    s_ctx = jnp.einsum('bnqgrd,bcgd->bngrqc', qb, kc).astype(jnp.float32) * scale
    s_sink = jnp.broadcast_to(sink.astype(jnp.float32).reshape(ATT_KV_HEADS, ATT_GROUP)[None, None, :, :, None, None], s_loc.shape[:-1] + (1,))
    p = jax.nn.softmax(jnp.concatenate([s_loc, s_ctx, s_sink], axis=-1), axis=-1).astype(v.dtype)
    out = jnp.einsum('bngrqk,bnkgd->bnqgrd', p[..., :3 * BLOCK], vb) + jnp.einsum('bngrqc,bcgd->bnqgrd', p[..., 3 * BLOCK:3 * BLOCK + C], vc)
    return out.reshape(B, S, ATT_WIDTH)


def context_attention(qc, kc, vc, sink):
    B, C = qc.shape[:2]
    qg = qc.reshape(B, C, ATT_KV_HEADS, ATT_GROUP, HEAD_DIM)
    s = jnp.einsum('bqgrd,bkgd->bgrqk', qg, kc).astype(jnp.float32) * HEAD_DIM ** -0.5
    s_sink = jnp.broadcast_to(sink.astype(jnp.float32).reshape(ATT_KV_HEADS, ATT_GROUP)[None, :, :, None, None], s.shape[:-1] + (1,))
    p = jax.nn.softmax(jnp.concatenate([s, s_sink], axis=-1), axis=-1)[..., :C].astype(vc.dtype)
    return jnp.einsum('bgrqk,bkgd->bqgrd', p, vc).reshape(B, C, ATT_WIDTH)


def mlstm_prep(mq, mk, mv, mg, gate_b):
    B, L = mq.shape[:2]
    f32 = jnp.float32
    q = mq.astype(f32).reshape(B, L, ML_HEADS, ML_DK)
    k = mk.astype(f32).reshape(B, L, ML_HEADS, ML_DK) * (ML_DK ** -0.5)
    v = mv.astype(f32).reshape(B, L, ML_HEADS, ML_DV)
    g = mg.astype(f32) + gate_b.astype(f32)
    i_f, f_f, i_b, f_b = jnp.split(g, 4, axis=-1)
    return (q, k, v, i_f, jax.nn.log_sigmoid(f_f), i_b, jax.nn.log_sigmoid(f_b))


def mlstm_scan(q, k, v, log_i, log_f, state):
    B, L = q.shape[:2]
    nc = L // CHUNK

    def to_chunks(t):
        return jnp.moveaxis(t.reshape(B, nc, CHUNK, *t.shape[2:]), 1, 0)

    xs = (to_chunks(q), to_chunks(k), to_chunks(v), to_chunks(log_i), to_chunks(log_f))
    tri = jnp.tril(jnp.ones((CHUNK, CHUNK), dtype=bool))

    def step(carry, inp):
        Cm, n, m = carry
        qh, kh, vh, li, lf = inp
        b = jnp.cumsum(lf, axis=1).transpose(0, 2, 1)
        li = li.transpose(0, 2, 1)
        D = jnp.where(tri, b[..., :, None] - b[..., None, :] + li[..., None, :], -jnp.inf)
        inter = b + m[..., None]
        m_t = jnp.maximum(inter, jnp.max(D, axis=-1))
        w_intra = jnp.exp(D - m_t[..., None])
        w_inter = jnp.exp(inter - m_t)
        s = jnp.einsum('bthd,bshd->bhts', qh, kh) * w_intra
        num = jnp.einsum('bhts,bshe->bthe', s, vh) + w_inter.transpose(0, 2, 1)[..., None] * jnp.einsum('bthd,bhde->bthe', qh, Cm)
        den = jnp.sum(s, axis=-1) + w_inter * jnp.einsum('bthd,bhd->bht', qh, n)
        den = jnp.maximum(jnp.abs(den), jnp.exp(-m_t))
        h = num / den.transpose(0, 2, 1)[..., None]
        bT = b[..., -1]
        g = bT[..., None] - b + li
        m_new = jnp.maximum(bT + m, jnp.max(g, axis=-1))
        ws = jnp.exp(g - m_new[..., None])
        dec = jnp.exp(bT + m - m_new)
        C_new = dec[..., None, None] * Cm + jnp.einsum('bhs,bshd,bshe->bhde', ws, kh, vh)
        n_new = dec[..., None] * n + jnp.einsum('bhs,bshd->bhd', ws, kh)
        return (C_new, n_new, m_new), h

    state, hs = lax.scan(step, state, xs)
    return jnp.moveaxis(hs, 0, 1).reshape(B, L, ML_HEADS, ML_DV), state


def mlstm_bidirectional(lat, ctx):
    q, k, v, i_f, lf_f, i_b, lf_b = lat
    qc, kc, vc, ic_f, lfc_f, ic_b, lfc_b = ctx
    B = q.shape[0]
    zero = (jnp.zeros((B, ML_HEADS, ML_DK, ML_DV), jnp.float32), jnp.zeros((B, ML_HEADS, ML_DK), jnp.float32), jnp.zeros((B, ML_HEADS), jnp.float32))

    def fl(t):
        return jnp.flip(t, axis=1)

    hc_f, st_f = mlstm_scan(qc, kc, vc, ic_f, lfc_f, zero)
    h_f, _ = mlstm_scan(q, k, v, i_f, lf_f, st_f)
    hc_b, st_b = mlstm_scan(fl(qc), fl(kc), fl(vc), fl(ic_b), fl(lfc_b), zero)
    h_b, _ = mlstm_scan(fl(q), fl(k), fl(v), fl(i_b), fl(lf_b), st_b)
    return h_f + fl(h_b), hc_f + fl(hc_b)


def mlstm_output(h, o, norm_w):
    B, L = h.shape[:2]
    hn = rms_norm(h, norm_w.reshape(ML_HEADS, ML_DV)).reshape(B, L, ML_WIDTH)
    return (hn * jax.nn.sigmoid(o.astype(jnp.float32))).astype(o.dtype)


def token_mixer(h, hc, cos, sin, w_in, w_out, q_norm_w, k_norm_w, attn_sink, hy_conv_w, hy_conv_b, hy_w1, hy_b1, hy_freq, hy_w2, hy_b2, hy_w3, hy_decay, hy_skip, ml_gate_b, ml_norm_w, with_ctx_out):
    aq, ak, av, hy, mq, mk, mv, mo, mg = jnp.split(h @ w_in, SPLIT_POINTS, axis=-1)
    aqc, akc, avc, hyc, mqc, mkc, mvc, moc, mgc = jnp.split(hc @ w_in, SPLIT_POINTS, axis=-1)
    q, k, v = attn_heads(aq, ak, av, q_norm_w, k_norm_w)
    q, k = apply_rope(q, cos, sin), apply_rope(k, cos, sin)
    qc, kc, vc = attn_heads(aqc, akc, avc, q_norm_w, k_norm_w)
    att = windowed_attention(q, k, v, kc, vc, attn_sink)
    hyo = hyena_mix(short_conv(hy, hy_conv_w, hy_conv_b), hyena_filters(h.shape[1], hy_w1, hy_b1, hy_freq, hy_w2, hy_b2, hy_w3, hy_decay), hy_skip)
    ml_lat, ml_ctx = mlstm_bidirectional(mlstm_prep(mq, mk, mv, mg, ml_gate_b), mlstm_prep(mqc, mkc, mvc, mgc, ml_gate_b))
    mlo = mlstm_output(ml_lat, mo, ml_norm_w)
    y = jnp.concatenate([att, hyo, mlo], axis=-1) @ w_out
    if not with_ctx_out:
        return y, None
    attc = context_attention(qc, kc, vc, attn_sink)
    hyoc = hyena_mix(short_conv(hyc, hy_conv_w, hy_conv_b), hyena_filters(hc.shape[1], hy_w1, hy_b1, hy_freq, hy_w2, hy_b2, hy_w3, hy_decay), hy_skip)
    mloc = mlstm_output(ml_ctx, moc, ml_norm_w)
    yc = jnp.concatenate([attc, hyoc, mloc], axis=-1) @ w_out
    return y, yc


def setup_inputs(seed: int = 0) -> dict:
    key = jax.random.key(seed)
    ks = iter(jax.random.split(key, 40))

    def nrm(shape, scale):
        return scale * jax.random.normal(next(ks), shape, jnp.float32)

    L = DEPTH
    ig = nrm((L, 2, ML_HEADS), 0.1)
    fg = jnp.linspace(3.0, 6.0, ML_HEADS, dtype=jnp.float32) + nrm((L, 2, ML_HEADS), 0.1)
    ml_gate_b = jnp.stack([ig[:, 0], fg[:, 0], ig[:, 1], fg[:, 1]], axis=1).reshape(L, 4 * ML_HEADS)
    decay0 = jnp.linspace(math.log(DECAY_TARGET) / MIN_DECAY_PCT, math.log(DECAY_TARGET) / MAX_DECAY_PCT, N_FILT, dtype=jnp.float32)
    return {
        'x': nrm((BATCH, SEQ, D_MODEL), 1.0),
        'c': nrm((BATCH, D_MODEL), 1.0),
        'ctx': nrm((BATCH, CTX_LEN, D_MODEL), 1.0),
        'c_ctx': nrm((D_MODEL,), 1.0),
        'w_mod': nrm((L, D_MODEL, 6 * D_MODEL), 0.5 * D_MODEL ** -0.5),
        'b_mod': nrm((L, 6 * D_MODEL), 0.02),
        'norm1_w': 1.0 + nrm((L, D_MODEL), 0.02),
        'norm2_w': 1.0 + nrm((L, D_MODEL), 0.02),
        'w_in': nrm((L, D_MODEL, P_IN), D_MODEL ** -0.5),
        'w_out': nrm((L, D_MIX, D_MODEL), D_MIX ** -0.5),
        'q_norm_w': 1.0 + nrm((L, HEAD_DIM), 0.02),
        'k_norm_w': 1.0 + nrm((L, HEAD_DIM), 0.02),
        'attn_sink': nrm((L, ATT_HEADS), 0.5),
        'hy_conv_w': nrm((L, 3, 3 * HY_CH), 3 ** -0.5),
        'hy_conv_b': nrm((L, 3 * HY_CH), 0.02),
        'hy_w1': nrm((L, POS_EMB, FILT_HID), POS_EMB ** -0.5),
        'hy_b1': nrm((L, FILT_HID), 0.1),
        'hy_freq': 1.0 + nrm((L, FILT_HID), 0.1),
        'hy_w2': nrm((L, FILT_HID, FILT_HID), FILT_HID ** -0.5),
        'hy_b2': nrm((L, FILT_HID), 0.1),
        'hy_w3': nrm((L, FILT_HID, N_FILT), 0.05 * FILT_HID ** -0.5),
        'hy_decay': decay0[None] + nrm((L, N_FILT), 0.1),
        'hy_skip': nrm((L, HY_ORDER, HY_CH), 0.5),
        'ml_gate_b': ml_gate_b,
        'ml_norm_w': 1.0 + nrm((L, ML_WIDTH), 0.02),
        'ffn_w_gate': nrm((L, D_MODEL, D_FF), D_MODEL ** -0.5),
        'ffn_w_up': nrm((L, D_MODEL, D_FF), D_MODEL ** -0.5),
        'ffn_w_down': nrm((L, D_FF, D_MODEL), D_FF ** -0.5),
    }


def reference(x, c, ctx, c_ctx, w_mod, b_mod, norm1_w, norm2_w, w_in, w_out, q_norm_w, k_norm_w, attn_sink, hy_conv_w, hy_conv_b, hy_w1, hy_b1, hy_freq, hy_w2, hy_b2, hy_w3, hy_decay, hy_skip, ml_gate_b, ml_norm_w, ffn_w_gate, ffn_w_up, ffn_w_down):
    n_rows = x.shape[1] // GRID_W
    cos, sin = axial_rope_tables(n_rows)
    s_lat = jax.nn.silu(c)
    s_ctx = jax.nn.silu(c_ctx)
    for l in range(DEPTH):
        last = l == DEPTH - 1
        mod = (s_lat @ w_mod[l] + b_mod[l])[:, None, :]
        mod_c = s_ctx @ w_mod[l] + b_mod[l]
        sh1, sc1, g1, sh2, sc2, g2 = jnp.split(mod, 6, axis=-1)
        sh1c, sc1c, g1c, sh2c, sc2c, g2c = jnp.split(mod_c, 6, axis=-1)
        h = modulate(x, norm1_w[l], sh1, sc1)
        hc = modulate(ctx, norm1_w[l], sh1c, sc1c)
        y, yc = token_mixer(h, hc, cos, sin, w_in[l], w_out[l], q_norm_w[l], k_norm_w[l], attn_sink[l], hy_conv_w[l], hy_conv_b[l], hy_w1[l], hy_b1[l], hy_freq[l], hy_w2[l], hy_b2[l], hy_w3[l], hy_decay[l], hy_skip[l], ml_gate_b[l], ml_norm_w[l], not last)
        x = x + g1 * y
        x = x + g2 * swiglu(modulate(x, norm2_w[l], sh2, sc2), ffn_w_gate[l], ffn_w_up[l], ffn_w_down[l])
        if not last:
            ctx = ctx + g1c * yc
            ctx = ctx + g2c * swiglu(modulate(ctx, norm2_w[l], sh2c, sc2c), ffn_w_gate[l], ffn_w_up[l], ffn_w_down[l])
    return x
```

```python
import functools
import math

import numpy as np
import jax
import jax.numpy as jnp
from jax import lax
from jax.experimental import pallas as pl
from jax.experimental.pallas import tpu as pltpu

F32 = jnp.float32
BF16 = jnp.bfloat16

D = 1024
NB_BATCH = 4
SEQ = 4096
DEPTH = 4
GRID_W = 64
CTX = 256
T_LAT = NB_BATCH * SEQ
T_CTX = NB_BATCH * CTX
R_ALL = T_LAT + T_CTX

HD = 64
ATT_H = 6
ATT_W = ATT_H * HD
KV_W = 2 * HD
BLK = 128
NBLK = SEQ // BLK
ROPE_BASE = 10000.0

HY = 384
N_FILT = 4 * HY
POS_BANDS = 16
FILT_HID = 64

ML_H = 4
ML_W = 256
CHUNK = 128

D_FF = 2816
P_IN = 2832
P_PAD = 2944
NEG = -1e30
EPS = 1e-6

TM = 512
NT_LAT = T_LAT // TM
NT_ALL = R_ALL // TM
TILES_PER_SEQ = SEQ // TM

N_FFT = 2 * SEQ
FFT_N1 = 256
FFT_N2 = 32
FFT_LANES = FFT_N2 * HY
HA_CH = 8
HB_ROWS = 1024
N_CTXF = 2 * CTX


def _np_tables():
    k1 = np.arange(FFT_N1)[None, :, None]
    n1 = np.arange(FFT_N1)[None, None, :]
    n2 = np.arange(FFT_N2)[:, None, None]
    ph = (k1 * (FFT_N2 * n1 + n2)) % N_FFT
    th = 2.0 * np.pi * ph / N_FFT
    cos, sin = np.cos(th), np.sin(th)
    fa = np.concatenate([cos, -sin], axis=1)
    half = FFT_N1 // 2
    g = np.concatenate([cos[:, :, :half].transpose(0, 2, 1), -sin[:, :, :half].transpose(0, 2, 1)], axis=2) / N_FFT
    a = np.arange(FFT_N2)
    th2 = 2.0 * np.pi * ((a[:, None] * a[None, :]) % FFT_N2) / FFT_N2
    eye = np.eye(FFT_N1 // FFT_N2)
    bdc, bds = np.kron(eye, np.cos(th2)), np.kron(eye, np.sin(th2))
    c = np.arange(N_CTXF)
    th3 = 2.0 * np.pi * ((c[:, None] * c[None, :]) % N_CTXF) / N_CTXF
    return dict(fa=fa.astype(np.float32), g=g.astype(np.float32), bdc=bdc.astype(np.float32), bds=bds.astype(np.float32),
                fc=np.cos(th3).astype(np.float32), fs=np.sin(th3).astype(np.float32))


_TAB = _np_tables()


def _blockdiag_ones(width):
    return np.kron(np.eye(width // HD), np.ones((HD, HD))).astype(np.float32)


def _cparams(sem, vmem_mb=48):
    return pltpu.CompilerParams(dimension_semantics=sem, vmem_limit_bytes=vmem_mb * 1024 * 1024)


def _dot(a, b):
    return jnp.dot(a.astype(BF16), b.astype(BF16), preferred_element_type=F32)


def _dot_nt(a, b):
    return lax.dot_general(a.astype(BF16), b.astype(BF16), (((1,), (1,)), ((), ())), preferred_element_type=F32)


def _dot_tn(a, b):
    return lax.dot_general(a.astype(BF16), b.astype(BF16), (((0,), (0,)), ((), ())), preferred_element_type=F32)


def _split3(x):
    x1 = x.astype(BF16)
    r1 = x - x1.astype(F32)
    x2 = r1.astype(BF16)
    x3 = (r1 - x2.astype(F32)).astype(BF16)
    return x1, x2, x3


def _dot_sel_l(sel, x):
    return sum(jnp.dot(sel, p, preferred_element_type=F32) for p in _split3(x))


def _dot_sel_r(x, sel):
    return sum(jnp.dot(p, sel, preferred_element_type=F32) for p in _split3(x))


def _sigmoid(x):
    return 1.0 / (1.0 + jnp.exp(-x))


def _log_sigmoid(x):
    return jnp.minimum(x, 0.0) - jnp.log(1.0 + jnp.exp(-jnp.abs(x)))


def _head_rms(t, bd, w):
    ss = _dot_sel_r(t * t, bd)
    return t * lax.rsqrt(ss * (1.0 / HD) + EPS) * w


def _mod_row(i):
    return jnp.where(i < NT_LAT, i // TILES_PER_SEQ, NB_BATCH)


def _mod_spec(layer, k):
    return pl.BlockSpec((None, None, 1, D), lambda i: (layer, _mod_row(i), 0, k))


def _mod_body(s_ref, w_ref, b_ref, o_ref):
    s = s_ref[...]
    s = s * _sigmoid(s)
    o_ref[...] = _dot(s, w_ref[...]) + b_ref[...]


def _modulation(cc8, w_mod, b_mod):
    nc = 1536
    return pl.pallas_call(
        _mod_body,
        grid=(DEPTH, 6 * D // nc),
        in_specs=[pl.BlockSpec((8, D), lambda l, j: (0, 0)),
                  pl.BlockSpec((None, D, nc), lambda l, j: (l, 0, j)),
                  pl.BlockSpec((None, 1, nc), lambda l, j: (l, 0, j))],
        out_specs=pl.BlockSpec((None, 8, nc), lambda l, j: (l, 0, j)),
        out_shape=jax.ShapeDtypeStruct((DEPTH, 8, 6 * D), F32),
        compiler_params=_cparams(("parallel", "parallel")),
        name="modulation",
    )(cc8, w_mod, b_mod.reshape(DEPTH, 1, 6 * D))


_IN_COLS = ((0, 640), (640, 1792), (1792, 2560), (2560, 2816), (2816, 2944))


def _modulated_norm(x, nw, sc, sh):
    ms = jnp.mean(x * x, axis=-1, keepdims=True)
    return (x * lax.rsqrt(ms + EPS) * nw) * (1.0 + sc) + sh


def _in_body(x_ref, sh_ref, sc_ref, nw_ref, w_ref, *o_refs):
    hb = _modulated_norm(x_ref[...], nw_ref[...], sc_ref[...], sh_ref[...]).astype(BF16)
    for o_ref, (a, b) in zip(o_refs, _IN_COLS):
        o_ref[...] = jnp.dot(hb, w_ref[:, a:b], preferred_element_type=F32)


def _in_proj(xs, mod4, layer, nw, w_in_p):
    widths = [b - a for a, b in _IN_COLS]
    return pl.pallas_call(
        _in_body,
        grid=(NT_ALL,),
        in_specs=[pl.BlockSpec((TM, D), lambda i: (i, 0)),
                  _mod_spec(layer, 0), _mod_spec(layer, 1),
                  pl.BlockSpec((1, D), lambda i: (0, 0)),
                  pl.BlockSpec((D, P_PAD), lambda i: (0, 0))],
        out_specs=[pl.BlockSpec((TM, w), lambda i: (i, 0)) for w in widths],
        out_shape=[jax.ShapeDtypeStruct((R_ALL, w), F32) for w in widths],
        compiler_params=_cparams(("parallel",), 56),
        name="in_proj",
    )(xs, mod4, mod4, nw.reshape(1, D), w_in_p)


def _rope(t, cos, sin_signed, first_half):
    outs = []
    for c in range(t.shape[1] // 128):
        tc = t[:, c * 128:(c + 1) * 128]
        partner = jnp.where(first_half, pltpu.roll(tc, 112, 1), pltpu.roll(tc, 16, 1))
        outs.append(tc * cos + partner * sin_signed)
    return outs[0] if len(outs) == 1 else jnp.concatenate(outs, axis=1)


def _qk_body(a_ref, cos_ref, sin_ref, qw_ref, kw_ref, bdq_ref, bdk_ref, q_ref, k_ref):
    cos, sin = cos_ref[...], sin_ref[...]
    lane = lax.broadcasted_iota(jnp.int32, (TM, 128), 1)
    first_half = (lane & 31) < 16
    q = _head_rms(a_ref[:, 0:ATT_W], bdq_ref[...], qw_ref[...])
    k = _head_rms(a_ref[:, ATT_W:ATT_W + KV_W], bdk_ref[...], kw_ref[...])
    q_ref[...] = (_rope(q, cos, sin, first_half) * (HD ** -0.5)).astype(BF16)
    k_ref[...] = _rope(k, cos, sin, first_half).astype(BF16)


def _qk_prep(att_qkv, cos_t, sin_t, qw, kw):
    def tab(i):
        return (jnp.where(i < NT_LAT, i % TILES_PER_SEQ, TILES_PER_SEQ), 0)
    bdq = jnp.asarray(_blockdiag_ones(ATT_W)).astype(BF16)
    bdk = jnp.asarray(_blockdiag_ones(KV_W)).astype(BF16)
    return pl.pallas_call(
        _qk_body,
        grid=(NT_ALL,),
        in_specs=[pl.BlockSpec((TM, 640), lambda i: (i, 0)),
                  pl.BlockSpec((TM, 128), tab), pl.BlockSpec((TM, 128), tab),
                  pl.BlockSpec((1, ATT_W), lambda i: (0, 0)), pl.BlockSpec((1, KV_W), lambda i: (0, 0)),
                  pl.BlockSpec((ATT_W, ATT_W), lambda i: (0, 0)), pl.BlockSpec((KV_W, KV_W), lambda i: (0, 0))],
        out_specs=[pl.BlockSpec((TM, ATT_W), lambda i: (i, 0)), pl.BlockSpec((TM, KV_W), lambda i: (i, 0))],
        out_shape=[jax.ShapeDtypeStruct((R_ALL, ATT_W), BF16), jax.ShapeDtypeStruct((R_ALL, KV_W), BF16)],
        compiler_params=_cparams(("parallel",)),
        name="qk_prep",
    )(att_qkv, cos_t, sin_t, jnp.tile(qw, ATT_H).reshape(1, ATT_W), jnp.tile(kw, 2).reshape(1, KV_W), bdq, bdk)


def _att_heads(q, kv_list, sink_ref, o_ref):
    outs = []
    for h in range(ATT_H):
        g = h // 3
        qh = q[:, HD * h:HD * (h + 1)]
        sink = sink_ref[0:1, h:h + 1]
        scores = []
        m = sink
        for k, _, mask in kv_list:
            s = _dot_nt(qh, k[:, HD * g:HD * (g + 1)])
            if mask is not None:
                s = jnp.where(mask, s, NEG)
            scores.append(s)
            m = jnp.maximum(m, jnp.max(s, axis=-1, keepdims=True))
        l = jnp.exp(sink - m)
        acc = None
        for s, (_, v, _) in zip(scores, kv_list):
            p = jnp.exp(s - m)
            l = l + jnp.sum(p, axis=-1, keepdims=True)
            pv = _dot(p, v[:, HD * g:HD * (g + 1)])
            acc = pv if acc is None else acc + pv
        outs.append(acc / l)
    o_ref[...] = jnp.concatenate(outs, axis=1)


def _att_body(q_ref, kp_ref, ko_ref, kn_ref, vp_ref, vo_ref, vn_ref, kc_ref, vc_ref, sink_ref, o_ref):
    j = pl.program_id(1)
    q = q_ref[...]
    kc, vc = kc_ref[...], vc_ref[...]

    @pl.when(j < NBLK)
    def _():
        row = lax.broadcasted_iota(jnp.int32, (BLK, BLK), 0)
        col = lax.broadcasted_iota(jnp.int32, (BLK, BLK), 1)
        mask_prev = jnp.logical_and(col >= row, j > 0)
        mask_next = jnp.logical_and(col <= row, j < NBLK - 1)
        _att_heads(q, [(kp_ref[...], vp_ref[...], mask_prev), (ko_ref[...], vo_ref[...], None),
                       (kn_ref[...], vn_ref[...], mask_next), (kc, vc, None)], sink_ref, o_ref)

    @pl.when(j >= NBLK)
    def _():
        _att_heads(q, [(kc, vc, None)], sink_ref, o_ref)


def _attention(qn, kn, att_qkv, sink8, with_ctx):
    steps = NBLK + (CTX // BLK if with_ctx else 0)
    rows = R_ALL if with_ctx else T_LAT

    def qmap(b, j):
        return (jnp.where(j < NBLK, b * NBLK + j, T_LAT // BLK + (CTX // BLK) * b + (j - NBLK)), 0)

    def band(off, colblk):
        return lambda b, j: (b * NBLK + jnp.clip(j + off, 0, NBLK - 1), colblk)

    cmap_k = lambda b, j: (T_LAT // CTX + b, 0)
    cmap_v = lambda b, j: (T_LAT // CTX + b, 4)
    return pl.pallas_call(
        _att_body,
        grid=(NB_BATCH, steps),
        in_specs=[pl.BlockSpec((BLK, ATT_W), qmap),
                  pl.BlockSpec((BLK, KV_W), band(-1, 0)), pl.BlockSpec((BLK, KV_W), band(0, 0)), pl.BlockSpec((BLK, KV_W), band(1, 0)),
                  pl.BlockSpec((BLK, KV_W), band(-1, 4)), pl.BlockSpec((BLK, KV_W), band(0, 4)), pl.BlockSpec((BLK, KV_W), band(1, 4)),
                  pl.BlockSpec((CTX, KV_W), cmap_k), pl.BlockSpec((CTX, KV_W), cmap_v),
                  pl.BlockSpec((8, 128), lambda b, j: (0, 0))],
        out_specs=pl.BlockSpec((BLK, ATT_W), qmap),
        out_shape=jax.ShapeDtypeStruct((rows, ATT_W), F32),
        compiler_params=_cparams(("parallel", "parallel")),
        name="attention",
    )(qn, kn, kn, kn, att_qkv, att_qkv, att_qkv, kn, att_qkv, sink8)


def _short_conv(u, w, b):
    n = u.shape[0]
    row = lax.broadcasted_iota(jnp.int32, u.shape, 0)
    up = jnp.where(row == 0, 0.0, pltpu.roll(u, 1, 0))
    un = jnp.where(row == n - 1, 0.0, pltpu.roll(u, n - 1, 0))
    return up * w[0:1, :] + u * w[1:2, :] + un * w[2:3, :] + b


def _sc_body(u0_ref, u1_ref, u2_ref, w0_ref, w1_ref, w2_ref, b0_ref, b1_ref, b2_ref, o0_ref, o1_ref, o2_ref):
    for u_ref, w_ref, b_ref, o_ref in ((u0_ref, w0_ref, b0_ref, o0_ref), (u1_ref, w1_ref, b1_ref, o1_ref), (u2_ref, w2_ref, b2_ref, o2_ref)):
        o_ref[...] = _short_conv(u_ref[...], w_ref[...], b_ref[...])


def _short_conv_latent(hy, conv_w, conv_b):
    cb = conv_b.reshape(1, 3 * HY)
    ins, ws, bs = [], [], []
    for part in range(3):
        ins.append(pl.BlockSpec((SEQ, 128), lambda b, j, part=part: (b, 3 * part + j)))
        ws.append(pl.BlockSpec((3, 128), lambda b, j, part=part: (0, 3 * part + j)))
        bs.append(pl.BlockSpec((1, 128), lambda b, j, part=part: (0, 3 * part + j)))
    return pl.pallas_call(
        _sc_body,
        grid=(NB_BATCH, 3),
        in_specs=ins + ws + bs,
        out_specs=[pl.BlockSpec((SEQ, 128), lambda b, j: (b, j))] * 3,
        out_shape=[jax.ShapeDtypeStruct((T_LAT, HY), F32)] * 3,
        compiler_params=_cparams(("parallel", "parallel")),
        name="short_conv",
    )(hy, hy, hy, conv_w, conv_w, conv_w, cb, cb, cb)


def _filt_body(f_ref, w1_ref, b1_ref, fr_ref, w2_ref, b2_ref, w3_ref, dec_ref, o_ref):
    f = f_ref[...]
    fr = fr_ref[...]
    z = jnp.sin(fr * (_dot(f, w1_ref[...]) + b1_ref[...]))
    z = jnp.sin(fr * (_dot(z, w2_ref[...]) + b2_ref[...]))
    o_ref[...] = _dot(z, w3_ref[...]) * jnp.exp(-f[:, 0:1] * jnp.abs(dec_ref[...]))


def _filters(feats, w1p, b1p, frp, w2p, b2p, w3p, dec):
    n = feats.shape[0]
    tl = min(n, 512)
    const = lambda shape: pl.BlockSpec(shape, lambda i: (0, 0))
    return pl.pallas_call(
        _filt_body,
        grid=(n // tl,),
        in_specs=[pl.BlockSpec((tl, 128), lambda i: (i, 0)), const((128, 128)), const((1, 128)), const((1, 128)),
                  const((128, 128)), const((1, 128)), const((128, N_FILT)), const((1, N_FILT))],
        out_specs=pl.BlockSpec((tl, N_FILT), lambda i: (i, 0)),
        out_shape=jax.ShapeDtypeStruct((n, N_FILT), F32),
        compiler_params=_cparams(("parallel",)),
        name="hyena_filters",
    )(feats, w1p, b1p, frp, w2p, b2p, w3p, dec)


def _pos_feats(n):
    t = jnp.linspace(0.0, 1.0, n, dtype=F32)[:, None]
    ang = (2.0 * math.pi / n) * jnp.arange(n, dtype=F32)[:, None]
    bands = jnp.linspace(1e-4, POS_BANDS - 1, POS_BANDS, dtype=F32)[None, :]
    feats = jnp.concatenate([t, jnp.cos(bands * ang), -jnp.sin(bands * ang)], axis=-1)
    return jnp.pad(feats, ((0, 0), (0, 128 - feats.shape[1])))


def _circular(filt):
    n = filt.shape[0]
    fwd, bwd = filt[:, :2 * HY], filt[:, 2 * HY:]
    return jnp.concatenate([fwd, jnp.zeros((1, 2 * HY), F32), jnp.flip(bwd[:n - 1], axis=0)], axis=0)


def _ha_body(s_ref, fa_ref, are_ref, aim_ref):
    for n in range(HA_CH):
        sl = slice(n * HY, (n + 1) * HY)
        r = _dot(fa_ref[n], s_ref[:, sl])
        are_ref[:, sl] = r[:FFT_N1].astype(BF16)
        aim_ref[:, sl] = r[FFT_N1:].astype(BF16)


def _dft_outer(s_view, fa_tab):
    nb, k = s_view.shape[0], s_view.shape[1]
    w = HA_CH * HY
    return pl.pallas_call(
        _ha_body,
        grid=(nb, FFT_N2 // HA_CH),
        in_specs=[pl.BlockSpec((None, k, w), lambda b, j: (b, 0, j)),
                  pl.BlockSpec((HA_CH, 2 * FFT_N1, k), lambda b, j: (j, 0, 0))],
        out_specs=[pl.BlockSpec((None, FFT_N1, w), lambda b, j: (b, 0, j))] * 2,
        out_shape=[jax.ShapeDtypeStruct((nb, FFT_N1, FFT_LANES), BF16)] * 2,
        compiler_params=_cparams(("parallel", "parallel")),
        name="dft_outer",
    )(s_view, fa_tab)


def _inner_fwd(bdc, bds, a_re, a_im):
    return _dot(bdc, a_re) + _dot(bds, a_im), _dot(bdc, a_im) - _dot(bds, a_re)


def _hb0_body(are_ref, aim_ref, bdc_ref, bds_ref, hre_ref, him_ref):
    bdc, bds = bdc_ref[...], bds_ref[...]
    for r in range(HB_ROWS // FFT_N1):
        sl = slice(r * FFT_N1, (r + 1) * FFT_N1)
        hre_ref[sl, :], him_ref[sl, :] = _inner_fwd(bdc, bds, are_ref[sl, :], aim_ref[sl, :])


def _filter_spectrum(a_re, a_im, bdc, bds):
    blk = pl.BlockSpec((None, HB_ROWS, HY), lambda r, o: (o, r, 0))
    mat = pl.BlockSpec((FFT_N1, FFT_N1), lambda r, o: (0, 0))
    return pl.pallas_call(
        _hb0_body,
        grid=(N_FFT // HB_ROWS, 2),
        in_specs=[blk, blk, mat, mat],
        out_specs=[blk, blk],
        out_shape=[jax.ShapeDtypeStruct((2, N_FFT, HY), F32)] * 2,
        compiler_params=_cparams(("parallel", "parallel")),
        name="filter_spectrum",
    )(a_re, a_im, bdc, bds)


def _hb_body(are_ref, aim_ref, hre_ref, him_ref, bdc_ref, bds_ref, cre_ref, cim_ref):
    bdc, bds = bdc_ref[...], bds_ref[...]
    for r in range(HB_ROWS // FFT_N1):
        sl = slice(r * FFT_N1, (r + 1) * FFT_N1)
        xre, xim = _inner_fwd(bdc, bds, are_ref[sl, :], aim_ref[sl, :])
        hre, him = hre_ref[sl, :], him_ref[sl, :]
        yre = (xre * hre - xim * him).astype(BF16)
        yim = (xre * him + xim * hre).astype(BF16)
        cre_ref[sl, :] = (_dot(bdc, yre) - _dot(bds, yim)).astype(BF16)
        cim_ref[sl, :] = (_dot(bdc, yim) + _dot(bds, yre)).astype(BF16)


def _spectral_filter(a_re, a_im, h_re, h_im, order, bdc, bds):
    blk = pl.BlockSpec((None, HB_ROWS, HY), lambda r, b: (b, r, 0))
    hblk = pl.BlockSpec((None, HB_ROWS, HY), lambda r, b: (order, r, 0))
    mat = pl.BlockSpec((FFT_N1, FFT_N1), lambda r, b: (0, 0))
    return pl.pallas_call(
        _hb_body,
        grid=(N_FFT // HB_ROWS, NB_BATCH),
        in_specs=[blk, blk, hblk, hblk, mat, mat],
        out_specs=[blk, blk],
        out_shape=[jax.ShapeDtypeStruct((NB_BATCH, N_FFT, HY), BF16)] * 2,
        compiler_params=_cparams(("parallel", "parallel")),
        name="spectral_filter",
    )(a_re, a_im, h_re, h_im, bdc, bds)


def _hc_body(cre_ref, cim_ref, g_ref, a_ref, b_ref, sk_ref, o_ref):
    for n in range(HA_CH):
        sl = slice(n * HY, (n + 1) * HY)
        y = _dot(g_ref[n, :, 0:FFT_N1], cre_ref[:, sl]) + _dot(g_ref[n, :, FFT_N1:2 * FFT_N1], cim_ref[:, sl])
        o_ref[:, sl] = a_ref[:, sl] * (y + b_ref[:, sl] * sk_ref[:, sl])


def _idft_outer_gate(c_re, c_im, g_tab, a_view, b_view, skip_t):
    w = HA_CH * HY
    half = FFT_N1 // 2
    data = pl.BlockSpec((None, half, w), lambda b, j: (b, 0, j))
    spec = pl.BlockSpec((None, FFT_N1, w), lambda b, j: (b, 0, j))
    return pl.pallas_call(
        _hc_body,
        grid=(NB_BATCH, FFT_N2 // HA_CH),
        in_specs=[spec, spec, pl.BlockSpec((HA_CH, half, 2 * FFT_N1), lambda b, j: (j, 0, 0)), data, data,
                  pl.BlockSpec((1, w), lambda b, j: (0, j))],
        out_specs=data,
        out_shape=jax.ShapeDtypeStruct((NB_BATCH, half, FFT_LANES), F32),
        compiler_params=_cparams(("parallel", "parallel")),
        name="idft_outer_gate",
    )(c_re, c_im, g_tab, a_view, b_view, skip_t)


def _hyc_body(u_ref, w_ref, b_ref, circ_ref, fc_ref, fs_ref, sk_ref, o_ref):
    usc = _short_conv(u_ref[...], w_ref[...], b_ref[...])
    fc, fs = fc_ref[...], fs_ref[...]
    circ = circ_ref[...]
    h_re, h_im = _dot(fc, circ), -_dot(fs, circ)

    def long_conv(s, o):
        sl = slice(o * HY, (o + 1) * HY)
        s_re, s_im = _dot(fc[:, :CTX], s), -_dot(fs[:, :CTX], s)
        hre, him = h_re[:, sl], h_im[:, sl]
        y_re = s_re * hre - s_im * him
        y_im = s_re * him + s_im * hre
        y = (_dot(fc[:CTX, :], y_re) - _dot(fs[:CTX, :], y_im)) * (1.0 / N_CTXF)
        return y + s * sk_ref[o:o + 1, :]

    v, x1, x2 = usc[:, 0:HY], usc[:, HY:2 * HY], usc[:, 2 * HY:3 * HY]
    o_ref[...] = x2 * long_conv(x1 * long_conv(v, 0), 1)


def _hyena_ctx(hy, conv_w, conv_b, circ_c, fc, fs, skip):
    const = lambda shape: pl.BlockSpec(shape, lambda b: (0, 0))
    return pl.pallas_call(
        _hyc_body,
        grid=(NB_BATCH,),
        in_specs=[pl.BlockSpec((CTX, 3 * HY), lambda b: (T_LAT // CTX + b, 0)), const((3, 3 * HY)), const((1, 3 * HY)),
                  const((N_CTXF, 2 * HY)), const((N_CTXF, N_CTXF)), const((N_CTXF, N_CTXF)), const((2, HY))],
        out_specs=pl.BlockSpec((CTX, HY), lambda b: (b, 0)),
        out_shape=jax.ShapeDtypeStruct((T_CTX, HY), F32),
        compiler_params=_cparams(("parallel",)),
        name="hyena_ctx",
    )(hy, conv_w, conv_b.reshape(1, 3 * HY), circ_c, fc, fs, skip)


N_ML_STEPS = CTX // CHUNK + SEQ // CHUNK


def _ml_chain(direction, qkv, gates, gates_t, cum_c, cum_r, c_scr, m_scr, mask, ones_col):
    outs = []
    for h in range(ML_H):
        ic, fc = (h, 4 + h) if direction == 0 else (8 + h, 12 + h)
        chain = direction * ML_H + h
        b_c, b_r = cum_c[:, fc:fc + 1], cum_r[fc:fc + 1, :]
        li_c, li_r = gates[:, ic:ic + 1], gates_t[ic:ic + 1, :]
        b_end = b_r[:, CHUNK - 1:CHUNK] if direction == 0 else b_r[:, 0:1]
        q = qkv[:, HD * h:HD * (h + 1)]
        k = qkv[:, ML_W + HD * h:ML_W + HD * (h + 1)] * (HD ** -0.5)
        vext = jnp.concatenate([qkv[:, 2 * ML_W + HD * h:2 * ML_W + HD * (h + 1)], ones_col], axis=1)
        c_prev, m_prev = c_scr[chain], m_scr[chain]
        dmat = jnp.where(mask, b_c - b_r + li_r, NEG)
        m_intra = jnp.max(dmat, axis=-1, keepdims=True)
        s = _dot_nt(q, k) * jnp.exp(dmat - m_intra)
        inter = b_c + m_prev
        m_t = jnp.maximum(inter, m_intra)
        hx = jnp.exp(m_intra - m_t) * _dot(s, vext) + jnp.exp(inter - m_t) * _dot(q, c_prev)
        den = jnp.maximum(jnp.abs(hx[:, HD:HD + 1]), jnp.exp(-m_t))
        outs.append(hx[:, :HD] / den)
        g_c = b_end - b_c + li_c
        m_new = jnp.maximum(b_end + m_prev, jnp.max(g_c, axis=0, keepdims=True))
        c_scr[chain] = jnp.exp(b_end + m_prev - m_new) * c_prev + _dot_tn(k, jnp.exp(g_c - m_new) * vext)
        m_scr[chain] = m_new
    return jnp.concatenate(outs, axis=1)


def _ml_body(qf_ref, gf_ref, gtf_ref, qb_ref, gb_ref, gtb_ref, bias_ref, bias_t_ref, tril_ref, triu_ref,
             hf_ref, hb_ref, c_scr, m_scr):
    @pl.when(pl.program_id(1) == 0)
    def _():
        c_scr[...] = jnp.zeros_like(c_scr)
        m_scr[...] = jnp.zeros_like(m_scr)

    tril, triu = tril_ref[...], triu_ref[...]
    row = lax.broadcasted_iota(jnp.int32, (CHUNK, CHUNK), 0)
    col = lax.broadcasted_iota(jnp.int32, (CHUNK, CHUNK), 1)
    ones_col = (lax.broadcasted_iota(jnp.int32, (CHUNK, HD), 1) == 0).astype(F32)
    for direction, (q_ref, g_ref, gt_ref, o_ref) in enumerate(((qf_ref, gf_ref, gtf_ref, hf_ref), (qb_ref, gb_ref, gtb_ref, hb_ref))):
        gates = g_ref[...] + bias_ref[...]
        gates_t = gt_ref[...] + bias_t_ref[...]
        ls, ls_t = _log_sigmoid(gates), _log_sigmoid(gates_t)
        if direction == 0:
            cum_c, cum_r, mask = _dot_sel_l(tril, ls), _dot_sel_r(ls_t, triu), col <= row
        else:
            cum_c, cum_r, mask = _dot_sel_l(triu, ls), _dot_sel_r(ls_t, tril), col >= row
        o_ref[...] = _ml_chain(direction, q_ref[...], gates, gates_t, cum_c, cum_r, c_scr, m_scr, mask, ones_col)


def _mlstm(mqkv, mg, mg_t, gate_b):
    nctx = CTX // CHUNK
    nlat = SEQ // CHUNK

    def fwd(b, i):
        return jnp.where(i < nctx, T_LAT // CHUNK + nctx * b + i, nlat * b + (i - nctx))

    def bwd(b, i):
        return jnp.where(i < nctx, T_LAT // CHUNK + nctx * b + (nctx - 1 - i), nlat * b + (N_ML_STEPS - 1 - i))

    bias = jnp.pad(gate_b, (0, 128 - 4 * ML_H)).reshape(1, 128)
    bias_t = gate_b.reshape(4 * ML_H, 1)
    tril = jnp.asarray(np.tril(np.ones((CHUNK, CHUNK), np.float32))).astype(BF16)
    triu = jnp.asarray(np.triu(np.ones((CHUNK, CHUNK), np.float32))).astype(BF16)
    const = lambda shape: pl.BlockSpec(shape, lambda b, i: (0, 0))
    ins = []
    for f in (fwd, bwd):
        ins += [pl.BlockSpec((CHUNK, 3 * ML_W), lambda b, i, f=f: (f(b, i), 0)),
                pl.BlockSpec((CHUNK, 128), lambda b, i, f=f: (f(b, i), 0)),
                pl.BlockSpec((4 * ML_H, CHUNK), lambda b, i, f=f: (0, f(b, i)))]
    return pl.pallas_call(
        _ml_body,
        grid=(NB_BATCH, N_ML_STEPS),
        in_specs=ins + [const((1, 128)), const((4 * ML_H, 1)), const((CHUNK, CHUNK)), const((CHUNK, CHUNK))],
        out_specs=[pl.BlockSpec((CHUNK, ML_W), lambda b, i: (fwd(b, i), 0)),
                   pl.BlockSpec((CHUNK, ML_W), lambda b, i: (bwd(b, i), 0))],
        out_shape=[jax.ShapeDtypeStruct((R_ALL, ML_W), F32)] * 2,
        scratch_shapes=[pltpu.VMEM((2 * ML_H, HD, 128), F32), pltpu.VMEM((2 * ML_H, 1, 1), F32)],
        compiler_params=_cparams(("parallel", "arbitrary")),
        name="mlstm",
    )(mqkv, mg, mg_t, mqkv, mg, mg_t, bias, bias_t, tril, triu)


def _out_body(att_ref, hl_ref, hc_ref, hf_ref, hb_ref, mo_ref, nw_ref, bd_ref, w_ref, x_ref, g_ref, o_ref):
    i = pl.program_id(0)
    hn = _head_rms(hf_ref[...] + hb_ref[...], bd_ref[...], nw_ref[...])
    mlo = hn * _sigmoid(mo_ref[...])
    hy = jnp.where(i < NT_LAT, hl_ref[...], hc_ref[...])
    y = _dot(att_ref[...], w_ref[0:ATT_W, :]) + _dot(hy, w_ref[ATT_W:ATT_W + HY, :]) + _dot(mlo, w_ref[ATT_W + HY:, :])
    o_ref[...] = x_ref[...] + g_ref[...] * y


def _out_proj(att, hyo, hyoc, hf, hb, mo, ml_nw, w_out_b, xs, mod4, layer, with_ctx):
    tiles = NT_ALL if with_ctx else NT_LAT
    rows = tiles * TM
    row = lambda w: pl.BlockSpec((TM, w), lambda i: (i, 0))
    bd = jnp.asarray(_blockdiag_ones(ML_W)).astype(BF16)
    return pl.pallas_call(
        _out_body,
        grid=(tiles,),
        in_specs=[row(ATT_W),
                  pl.BlockSpec((TM, HY), lambda i: (jnp.minimum(i, NT_LAT - 1), 0)),
                  pl.BlockSpec((TM, HY), lambda i: (jnp.maximum(i - NT_LAT, 0), 0)),
                  row(ML_W), row(ML_W), row(ML_W),
                  pl.BlockSpec((1, ML_W), lambda i: (0, 0)), pl.BlockSpec((ML_W, ML_W), lambda i: (0, 0)),
                  pl.BlockSpec((D, D), lambda i: (0, 0)), row(D), _mod_spec(layer, 2)],
        out_specs=row(D),
        out_shape=jax.ShapeDtypeStruct((rows, D), F32),
        compiler_params=_cparams(("parallel",)),
        name="out_proj",
    )(att, hyo, hyoc, hf, hb, mo, ml_nw.reshape(1, ML_W), bd, w_out_b, xs, mod4)


FF_CH = D_FF // 2


def _ffn_body(x_ref, sh_ref, sc_ref, g_ref, nw_ref, wg_ref, wu_ref, wd_ref, o_ref):
    x = x_ref[...]
    hb = _modulated_norm(x, nw_ref[...], sc_ref[...], sh_ref[...]).astype(BF16)
    acc = None
    for c in range(D_FF // FF_CH):
        sl = slice(c * FF_CH, (c + 1) * FF_CH)
        a = jnp.dot(hb, wg_ref[:, sl], preferred_element_type=F32)
        u = jnp.dot(hb, wu_ref[:, sl], preferred_element_type=F32)
        part = _dot(a * _sigmoid(a) * u, wd_ref[sl, :])
        acc = part if acc is None else acc + part
    o_ref[...] = x + g_ref[...] * acc


def _ffn(xs, mod4, layer, nw, wg, wu, wd, with_ctx):
    tiles = NT_ALL if with_ctx else NT_LAT
    row = pl.BlockSpec((TM, D), lambda i: (i, 0))
    resident = lambda shape: pl.BlockSpec(shape, lambda i: (0, 0), pipeline_mode=pl.Buffered(1))
    return pl.pallas_call(
        _ffn_body,
        grid=(tiles,),
        in_specs=[row, _mod_spec(layer, 3), _mod_spec(layer, 4), _mod_spec(layer, 5), pl.BlockSpec((1, D), lambda i: (0, 0)),
                  resident((D, D_FF)), resident((D, D_FF)), resident((D_FF, D))],
        out_specs=row,
        out_shape=jax.ShapeDtypeStruct((tiles * TM, D), F32),
        compiler_params=_cparams(("parallel",), 56),
        name="ffn",
    )(xs, mod4, mod4, mod4, nw.reshape(1, D), wg, wu, wd)


def _rope_tables():
    n_rows = SEQ // GRID_W
    row = jnp.repeat(jnp.arange(n_rows), GRID_W)
    col = jnp.tile(jnp.arange(GRID_W), n_rows)
    nf = HD // 4
    inv_freq = ROPE_BASE ** (-jnp.arange(nf, dtype=F32) / nf)
    ang = jnp.stack([row[:, None] * inv_freq, col[:, None] * inv_freq], axis=1)
    cos, sin = jnp.cos(ang), jnp.sin(ang)
    cos_h = jnp.concatenate([cos, cos], axis=-1).reshape(SEQ, HD)
    sin_h = jnp.concatenate([-sin, sin], axis=-1).reshape(SEQ, HD)
    cos_t = jnp.concatenate([jnp.tile(cos_h, (1, 2)), jnp.ones((TM, 128), F32)], axis=0)
    sin_t = jnp.concatenate([jnp.tile(sin_h, (1, 2)), jnp.zeros((TM, 128), F32)], axis=0)
    return cos_t, sin_t


def kernel(x, c, ctx, c_ctx, w_mod, b_mod, norm1_w, norm2_w, w_in, w_out, q_norm_w, k_norm_w, attn_sink, hy_conv_w, hy_conv_b, hy_w1, hy_b1, hy_freq, hy_w2, hy_b2, hy_w3, hy_decay, hy_skip, ml_gate_b, ml_norm_w, ffn_w_gate, ffn_w_up, ffn_w_down):
    xs = jnp.concatenate([x.reshape(T_LAT, D), ctx.reshape(T_CTX, D)], axis=0)
    cc8 = jnp.concatenate([c, c_ctx[None, :], jnp.zeros((8 - NB_BATCH - 1, D), F32)], axis=0)
    mod4 = _modulation(cc8, w_mod, b_mod).reshape(DEPTH, 8, 1, 6 * D)

    cos_t, sin_t = _rope_tables()
    feats_l, feats_c = _pos_feats(SEQ), _pos_feats(CTX)
    fa_tab = jnp.asarray(_TAB["fa"]).astype(BF16)
    g_tab = jnp.asarray(_TAB["g"]).astype(BF16)
    bdc, bds = jnp.asarray(_TAB["bdc"]).astype(BF16), jnp.asarray(_TAB["bds"]).astype(BF16)
    fc, fs = jnp.asarray(_TAB["fc"]).astype(BF16), jnp.asarray(_TAB["fs"]).astype(BF16)
    fa_half = fa_tab[:, :, :FFT_N1 // 2]

    w_in_p = jnp.pad(w_in, ((0, 0), (0, 0), (0, P_PAD - P_IN))).astype(BF16)
    w_out_b = w_out.astype(BF16)
    wg_b, wu_b, wd_b = ffn_w_gate.astype(BF16), ffn_w_up.astype(BF16), ffn_w_down.astype(BF16)
    hpad = 128 - FILT_HID
    w1p = jnp.pad(hy_w1, ((0, 0), (0, 128 - hy_w1.shape[1]), (0, hpad)))
    b1p = jnp.pad(hy_b1, ((0, 0), (0, hpad))).reshape(DEPTH, 1, 128)
    frp = jnp.pad(hy_freq, ((0, 0), (0, hpad))).reshape(DEPTH, 1, 128)
    w2p = jnp.pad(hy_w2, ((0, 0), (0, hpad), (0, hpad)))
    b2p = jnp.pad(hy_b2, ((0, 0), (0, hpad))).reshape(DEPTH, 1, 128)
    w3p = jnp.pad(hy_w3, ((0, 0), (0, hpad), (0, 0)))
    sink8 = jnp.pad(attn_sink, ((0, 0), (0, 128 - ATT_H)))[:, None, :] * jnp.ones((1, 8, 1), F32)
    zeros_tile = jnp.zeros((TM, HY), F32)

    out = None
    for l in range(DEPTH):
        last = l == DEPTH - 1
        att_qkv, hy, mqkv, mo, mg = _in_proj(xs, mod4, l, norm1_w[l], w_in_p[l])

        qn, kn = _qk_prep(att_qkv, cos_t, sin_t, q_norm_w[l], k_norm_w[l])
        att = _attention(qn, kn, att_qkv, sink8[l], not last)

        dec = hy_decay[l].reshape(1, N_FILT)
        filt = _filters(feats_l, w1p[l], b1p[l], frp[l], w2p[l], b2p[l], w3p[l], dec)
        circ = _circular(filt).reshape(N_FFT, 2, HY).transpose(1, 0, 2).reshape(2, FFT_N1, FFT_LANES)
        f_re, f_im = _dft_outer(circ, fa_tab)
        h_re, h_im = _filter_spectrum(f_re.reshape(2, N_FFT, HY), f_im.reshape(2, N_FFT, HY), bdc, bds)

        v, x1, x2 = _short_conv_latent(hy, hy_conv_w[l], hy_conv_b[l])
        view = lambda t: t.reshape(NB_BATCH, FFT_N1 // 2, FFT_LANES)
        s_in, gate_a = view(v), view(x1)
        for order in range(2):
            a_re, a_im = _dft_outer(s_in, fa_half)
            c_re, c_im = _spectral_filter(a_re.reshape(NB_BATCH, N_FFT, HY), a_im.reshape(NB_BATCH, N_FFT, HY), h_re, h_im, order, bdc, bds)
            skip_t = jnp.tile(hy_skip[l, order], FFT_N2).reshape(1, FFT_LANES)
            s_in = _idft_outer_gate(c_re.reshape(NB_BATCH, FFT_N1, FFT_LANES), c_im.reshape(NB_BATCH, FFT_N1, FFT_LANES),
                                    g_tab, gate_a, s_in, skip_t)
            gate_a = view(x2)
        hyo = s_in.reshape(T_LAT, HY)

        if last:
            hyoc = zeros_tile
        else:
            filt_c = _filters(feats_c, w1p[l], b1p[l], frp[l], w2p[l], b2p[l], w3p[l], dec)
            hyoc = _hyena_ctx(hy, hy_conv_w[l], hy_conv_b[l], _circular(filt_c), fc, fs, hy_skip[l])

        hf, hb = _mlstm(mqkv, mg, mg[:, :4 * ML_H].T, ml_gate_b[l])

        xs1 = _out_proj(att, hyo, hyoc, hf, hb, mo, ml_norm_w[l], w_out_b[l], xs, mod4, l, not last)
        xs = _ffn(xs1, mod4, l, norm2_w[l], wg_b[l], wu_b[l], wd_b[l], not last)
        out = xs
    return out.reshape(NB_BATCH, SEQ, D)
```

```python
import functools
import math

import numpy as np
import jax
import jax.numpy as jnp
from jax import lax
from jax.experimental import pallas as pl
from jax.experimental.pallas import tpu as pltpu

F32 = jnp.float32
BF16 = jnp.bfloat16

D = 1024
NB_BATCH = 4
SEQ = 4096
DEPTH = 4
GRID_W = 64
CTX = 256
T_LAT = NB_BATCH * SEQ
T_CTX = NB_BATCH * CTX
R_ALL = T_LAT + T_CTX

HD = 64
ATT_H = 6
ATT_W = ATT_H * HD
KV_W = 2 * HD
BLK = 128
NBLK = SEQ // BLK
ROPE_BASE = 10000.0

HY = 384
N_FILT = 4 * HY
POS_BANDS = 16
FILT_HID = 64

ML_H = 4
ML_W = 256
CHUNK = 128

D_FF = 2816
P_IN = 2832
P_PAD = 2944
NEG = -1e30
EPS = 1e-6

TM = 512
NT_LAT = T_LAT // TM
NT_ALL = R_ALL // TM
TILES_PER_SEQ = SEQ // TM

N_FFT = 2 * SEQ
FFT_A = 32
FFT_R = 256
FFT_T = 16
FFT_B = FFT_R // FFT_T
HALF_A = FFT_A // 2
S1_TILES = 4
HB_ROWS = 1024
N_CTXF = 2 * CTX


def _np_tables():
    ka = np.arange(FFT_A)[None, :, None, None]
    cc = np.arange(FFT_T)[None, None, :, None]
    aa = np.arange(HALF_A)[None, None, None, :]
    bb = np.arange(FFT_B)[:, None, None, None]
    ph = (ka * (FFT_R * aa + FFT_T * bb + cc)) % N_FFT
    th = 2.0 * np.pi * ph / N_FFT
    eye = np.eye(FFT_T)
    cos = np.einsum("bkca,cd->bkcad", np.cos(th), eye)
    sin = np.einsum("bkca,cd->bkcad", np.sin(th), eye)
    rows, cols = FFT_A * FFT_T, HALF_A * FFT_T
    s1 = np.concatenate([cos.reshape(FFT_B, rows, cols), -sin.reshape(FFT_B, rows, cols)], axis=1)
    s1i = np.concatenate([cos.reshape(FFT_B, rows, cols).transpose(0, 2, 1),
                          -sin.reshape(FFT_B, rows, cols).transpose(0, 2, 1)], axis=2) / N_FFT
    r = np.arange(FFT_R)
    th2 = 2.0 * np.pi * ((r[:, None] * r[None, :]) % FFT_R) / FFT_R
    c2, s2 = np.cos(th2), np.sin(th2)
    wf = np.block([[c2, s2], [-s2, c2]])
    wi = np.block([[c2, -s2], [s2, c2]])
    c = np.arange(N_CTXF)
    th3 = 2.0 * np.pi * ((c[:, None] * c[None, :]) % N_CTXF) / N_CTXF
    return dict(s1=s1.astype(np.float32), s1i=s1i.astype(np.float32), wf=wf.astype(np.float32), wi=wi.astype(np.float32),
                fc=np.cos(th3).astype(np.float32), fs=np.sin(th3).astype(np.float32))


_TAB = _np_tables()


def _blockdiag_ones(width):
    return np.kron(np.eye(width // HD), np.ones((HD, HD))).astype(np.float32)


def _cparams(sem, vmem_mb=48):
    return pltpu.CompilerParams(dimension_semantics=sem, vmem_limit_bytes=vmem_mb * 1024 * 1024)


def _dot(a, b):
    return jnp.dot(a.astype(BF16), b.astype(BF16), preferred_element_type=F32)


def _dot_nt(a, b):
    return lax.dot_general(a.astype(BF16), b.astype(BF16), (((1,), (1,)), ((), ())), preferred_element_type=F32)


def _dot_tn(a, b):
    return lax.dot_general(a.astype(BF16), b.astype(BF16), (((0,), (0,)), ((), ())), preferred_element_type=F32)


def _split3(x):
    x1 = x.astype(BF16)
    r1 = x - x1.astype(F32)
    x2 = r1.astype(BF16)
    x3 = (r1 - x2.astype(F32)).astype(BF16)
    return x1, x2, x3


def _dot_sel_l(sel, x):
    return sum(jnp.dot(sel, p, preferred_element_type=F32) for p in _split3(x))


def _dot_sel_r(x, sel):
    return sum(jnp.dot(p, sel, preferred_element_type=F32) for p in _split3(x))


def _sigmoid(x):
    return 1.0 / (1.0 + jnp.exp(-x))


def _log_sigmoid(x):
    return jnp.minimum(x, 0.0) - jnp.log(1.0 + jnp.exp(-jnp.abs(x)))


def _head_rms(t, bd, w):
    ss = _dot_sel_r(t * t, bd)
    return t * lax.rsqrt(ss * (1.0 / HD) + EPS) * w


def _mod_row(i):
    return jnp.where(i < NT_LAT, i // TILES_PER_SEQ, NB_BATCH)


def _mod_spec(layer, k):
    return pl.BlockSpec((None, None, 1, D), lambda i: (layer, _mod_row(i), 0, k))


def _mod_body(s_ref, w_ref, b_ref, o_ref):
    s = s_ref[...]
    s = s * _sigmoid(s)
    o_ref[...] = _dot(s, w_ref[...]) + b_ref[...]


def _modulation(cc8, w_mod, b_mod):
    nc = 1536
    return pl.pallas_call(
        _mod_body,
        grid=(DEPTH, 6 * D // nc),
        in_specs=[pl.BlockSpec((8, D), lambda l, j: (0, 0)),
                  pl.BlockSpec((None, D, nc), lambda l, j: (l, 0, j)),
                  pl.BlockSpec((None, 1, nc), lambda l, j: (l, 0, j))],
        out_specs=pl.BlockSpec((None, 8, nc), lambda l, j: (l, 0, j)),
        out_shape=jax.ShapeDtypeStruct((DEPTH, 8, 6 * D), F32),
        compiler_params=_cparams(("parallel", "parallel")),
        name="modulation",
    )(cc8, w_mod, b_mod.reshape(DEPTH, 1, 6 * D))


_IN_COLS = ((0, 640), (640, 1792), (1792, 2560), (2560, 2816), (2816, 2944))


def _modulated_norm(x, nw, sc, sh):
    ms = jnp.mean(x * x, axis=-1, keepdims=True)
    return (x * lax.rsqrt(ms + EPS) * nw) * (1.0 + sc) + sh


def _in_body(x_ref, sh_ref, sc_ref, nw_ref, w_ref, *o_refs):
    hb = _modulated_norm(x_ref[...], nw_ref[...], sc_ref[...], sh_ref[...]).astype(BF16)
    for o_ref, (a, b) in zip(o_refs, _IN_COLS):
        o_ref[...] = jnp.dot(hb, w_ref[:, a:b], preferred_element_type=F32)


def _in_proj(xs, mod4, layer, nw, w_in_p):
    widths = [b - a for a, b in _IN_COLS]
    return pl.pallas_call(
        _in_body,
        grid=(NT_ALL,),
        in_specs=[pl.BlockSpec((TM, D), lambda i: (i, 0)),
                  _mod_spec(layer, 0), _mod_spec(layer, 1),
                  pl.BlockSpec((1, D), lambda i: (0, 0)),
                  pl.BlockSpec((D, P_PAD), lambda i: (0, 0))],
        out_specs=[pl.BlockSpec((TM, w), lambda i: (i, 0)) for w in widths],
        out_shape=[jax.ShapeDtypeStruct((R_ALL, w), F32) for w in widths],
        compiler_params=_cparams(("parallel",), 56),
        name="in_proj",
    )(xs, mod4, mod4, nw.reshape(1, D), w_in_p)


def _rope(t, cos, sin_signed, first_half):
    outs = []
    for c in range(t.shape[1] // 128):
        tc = t[:, c * 128:(c + 1) * 128]
        partner = jnp.where(first_half, pltpu.roll(tc, 112, 1), pltpu.roll(tc, 16, 1))
        outs.append(tc * cos + partner * sin_signed)
    return outs[0] if len(outs) == 1 else jnp.concatenate(outs, axis=1)


def _qk_body(a_ref, cos_ref, sin_ref, qw_ref, kw_ref, bdq_ref, bdk_ref, q_ref, k_ref):
    cos, sin = cos_ref[...], sin_ref[...]
    lane = lax.broadcasted_iota(jnp.int32, (TM, 128), 1)
    first_half = (lane & 31) < 16
    q = _head_rms(a_ref[:, 0:ATT_W], bdq_ref[...], qw_ref[...])
    k = _head_rms(a_ref[:, ATT_W:ATT_W + KV_W], bdk_ref[...], kw_ref[...])
    q_ref[...] = (_rope(q, cos, sin, first_half) * (HD ** -0.5)).astype(BF16)
    k_ref[...] = _rope(k, cos, sin, first_half).astype(BF16)


def _qk_prep(att_qkv, cos_t, sin_t, qw, kw):
    def tab(i):
        return (jnp.where(i < NT_LAT, i % TILES_PER_SEQ, TILES_PER_SEQ), 0)
    bdq = jnp.asarray(_blockdiag_ones(ATT_W)).astype(BF16)
    bdk = jnp.asarray(_blockdiag_ones(KV_W)).astype(BF16)
    return pl.pallas_call(
        _qk_body,
        grid=(NT_ALL,),
        in_specs=[pl.BlockSpec((TM, 640), lambda i: (i, 0)),
                  pl.BlockSpec((TM, 128), tab), pl.BlockSpec((TM, 128), tab),
                  pl.BlockSpec((1, ATT_W), lambda i: (0, 0)), pl.BlockSpec((1, KV_W), lambda i: (0, 0)),
                  pl.BlockSpec((ATT_W, ATT_W), lambda i: (0, 0)), pl.BlockSpec((KV_W, KV_W), lambda i: (0, 0))],
        out_specs=[pl.BlockSpec((TM, ATT_W), lambda i: (i, 0)), pl.BlockSpec((TM, KV_W), lambda i: (i, 0))],
        out_shape=[jax.ShapeDtypeStruct((R_ALL, ATT_W), BF16), jax.ShapeDtypeStruct((R_ALL, KV_W), BF16)],
        compiler_params=_cparams(("parallel",)),
        name="qk_prep",
    )(att_qkv, cos_t, sin_t, jnp.tile(qw, ATT_H).reshape(1, ATT_W), jnp.tile(kw, 2).reshape(1, KV_W), bdq, bdk)


def _att_heads(q, kv_list, sink_ref, o_ref):
    outs = []
    for h in range(ATT_H):
        g = h // 3
        qh = q[:, HD * h:HD * (h + 1)]
        sink = sink_ref[0:1, h:h + 1]
        scores = []
        m = sink
        for k, _, mask in kv_list:
            s = _dot_nt(qh, k[:, HD * g:HD * (g + 1)])
            if mask is not None:
                s = jnp.where(mask, s, NEG)
            scores.append(s)
            m = jnp.maximum(m, jnp.max(s, axis=-1, keepdims=True))
        l = jnp.exp(sink - m)
        acc = None
        for s, (_, v, _) in zip(scores, kv_list):
            p = jnp.exp(s - m)
            l = l + jnp.sum(p, axis=-1, keepdims=True)
            pv = _dot(p, v[:, HD * g:HD * (g + 1)])
            acc = pv if acc is None else acc + pv
        outs.append(acc / l)
    o_ref[...] = jnp.concatenate(outs, axis=1)


def _att_body(q_ref, kp_ref, ko_ref, kn_ref, vp_ref, vo_ref, vn_ref, kc_ref, vc_ref, sink_ref, o_ref):
    j = pl.program_id(1)
    q = q_ref[...]
    kc, vc = kc_ref[...], vc_ref[...]

    @pl.when(j < NBLK)
    def _():
        row = lax.broadcasted_iota(jnp.int32, (BLK, BLK), 0)
        col = lax.broadcasted_iota(jnp.int32, (BLK, BLK), 1)
        mask_prev = jnp.logical_and(col >= row, j > 0)
        mask_next = jnp.logical_and(col <= row, j < NBLK - 1)
        _att_heads(q, [(kp_ref[...], vp_ref[...], mask_prev), (ko_ref[...], vo_ref[...], None),
                       (kn_ref[...], vn_ref[...], mask_next), (kc, vc, None)], sink_ref, o_ref)

    @pl.when(j >= NBLK)
    def _():
        _att_heads(q, [(kc, vc, None)], sink_ref, o_ref)


def _attention(qn, kn, att_qkv, sink8, with_ctx):
    steps = NBLK + (CTX // BLK if with_ctx else 0)
    rows = R_ALL if with_ctx else T_LAT

    def qmap(b, j):
        return (jnp.where(j < NBLK, b * NBLK + j, T_LAT // BLK + (CTX // BLK) * b + (j - NBLK)), 0)

    def band(off, colblk):
        return lambda b, j: (b * NBLK + jnp.clip(j + off, 0, NBLK - 1), colblk)

    cmap_k = lambda b, j: (T_LAT // CTX + b, 0)
    cmap_v = lambda b, j: (T_LAT // CTX + b, 4)
    return pl.pallas_call(
        _att_body,
        grid=(NB_BATCH, steps),
        in_specs=[pl.BlockSpec((BLK, ATT_W), qmap),
                  pl.BlockSpec((BLK, KV_W), band(-1, 0)), pl.BlockSpec((BLK, KV_W), band(0, 0)), pl.BlockSpec((BLK, KV_W), band(1, 0)),
                  pl.BlockSpec((BLK, KV_W), band(-1, 4)), pl.BlockSpec((BLK, KV_W), band(0, 4)), pl.BlockSpec((BLK, KV_W), band(1, 4)),
                  pl.BlockSpec((CTX, KV_W), cmap_k), pl.BlockSpec((CTX, KV_W), cmap_v),
                  pl.BlockSpec((8, 128), lambda b, j: (0, 0))],
        out_specs=pl.BlockSpec((BLK, ATT_W), qmap),
        out_shape=jax.ShapeDtypeStruct((rows, ATT_W), F32),
        compiler_params=_cparams(("parallel", "parallel")),
        name="attention",
    )(qn, kn, kn, kn, att_qkv, att_qkv, att_qkv, kn, att_qkv, sink8)


def _short_conv(u, w, b):
    n = u.shape[0]
    row = lax.broadcasted_iota(jnp.int32, u.shape, 0)
    up = jnp.where(row == 0, 0.0, pltpu.roll(u, 1, 0))
    un = jnp.where(row == n - 1, 0.0, pltpu.roll(u, n - 1, 0))
    return up * w[0:1, :] + u * w[1:2, :] + un * w[2:3, :] + b


def _sc_body(u0_ref, u1_ref, u2_ref, w0_ref, w1_ref, w2_ref, b0_ref, b1_ref, b2_ref, o0_ref, o1_ref, o2_ref):
    for u_ref, w_ref, b_ref, o_ref in ((u0_ref, w0_ref, b0_ref, o0_ref), (u1_ref, w1_ref, b1_ref, o1_ref), (u2_ref, w2_ref, b2_ref, o2_ref)):
        o_ref[...] = _short_conv(u_ref[...], w_ref[...], b_ref[...])


def _short_conv_latent(hy, conv_w, conv_b):
    cb = conv_b.reshape(1, 3 * HY)
    ins, ws, bs = [], [], []
    for part in range(3):
        ins.append(pl.BlockSpec((SEQ, 128), lambda b, j, part=part: (b, 3 * part + j)))
        ws.append(pl.BlockSpec((3, 128), lambda b, j, part=part: (0, 3 * part + j)))
        bs.append(pl.BlockSpec((1, 128), lambda b, j, part=part: (0, 3 * part + j)))
    return pl.pallas_call(
        _sc_body,
        grid=(NB_BATCH, 3),
        in_specs=ins + ws + bs,
        out_specs=[pl.BlockSpec((SEQ, 128), lambda b, j: (b, j))] * 3,
        out_shape=[jax.ShapeDtypeStruct((T_LAT, HY), F32)] * 3,
        compiler_params=_cparams(("parallel", "parallel")),
        name="short_conv",
    )(hy, hy, hy, conv_w, conv_w, conv_w, cb, cb, cb)


def _filt_body(n, tl, f_ref, w1_ref, b1_ref, fr_ref, w2_ref, b2_ref, w3_ref, dec_ref, o_ref):
    f = f_ref[...]
    fr = fr_ref[...]
    z = jnp.sin(fr * (_dot(f, w1_ref[...]) + b1_ref[...]))
    z = jnp.sin(fr * (_dot(z, w2_ref[...]) + b2_ref[...]))
    filt = _dot(z, w3_ref[...]) * jnp.exp(-f[:, 0:1] * jnp.abs(dec_ref[...]))
    row = pl.program_id(0) * tl + lax.broadcasted_iota(jnp.int32, filt.shape, 0)
    o_ref[...] = jnp.where(row == n, 0.0, filt)


def _circular_filters(feats_circ, w1p, b1p, frp, w2p, b2p, w3p, dec):
    n = feats_circ.shape[0] // 2
    tl = min(n, 512)
    const = lambda shape: pl.BlockSpec(shape, lambda i: (0, 0))
    half = lambda rows: pl.BlockSpec((rows, 2 * HY), lambda i: (0, i // (n // tl)))
    return pl.pallas_call(
        functools.partial(_filt_body, n, tl),
        grid=(2 * n // tl,),
        in_specs=[pl.BlockSpec((tl, 128), lambda i: (i, 0)), const((128, 128)), const((1, 128)), const((1, 128)),
                  const((128, 128)), const((1, 128)), half(128), half(1)],
        out_specs=pl.BlockSpec((tl, 2 * HY), lambda i: (i, 0)),
        out_shape=jax.ShapeDtypeStruct((2 * n, 2 * HY), F32),
        compiler_params=_cparams(("parallel",)),
        name="hyena_filters",
    )(feats_circ, w1p, b1p, frp, w2p, b2p, w3p, dec)


def _pos_feats_circ(n):
    t = jnp.linspace(0.0, 1.0, n, dtype=F32)[:, None]
    ang = (2.0 * math.pi / n) * jnp.arange(n, dtype=F32)[:, None]
    bands = jnp.linspace(1e-4, POS_BANDS - 1, POS_BANDS, dtype=F32)[None, :]
    feats = jnp.concatenate([t, jnp.cos(bands * ang), -jnp.sin(bands * ang)], axis=-1)
    feats = jnp.pad(feats, ((0, 0), (0, 128 - feats.shape[1])))
    return jnp.concatenate([feats, feats[:1], jnp.flip(feats[:n - 1], axis=0)], axis=0)


def _s1_body(x_ref, m_ref, are_ref, aim_ref):
    half = FFT_A * FFT_T
    for p in range(S1_TILES):
        xs = x_ref[:, p].reshape(HALF_A * FFT_T, HY)
        r = _dot(m_ref[p], xs)
        are_ref[:, p] = r[:half].reshape(FFT_A, FFT_T, HY).astype(BF16)
        aim_ref[:, p] = r[half:].reshape(FFT_A, FFT_T, HY).astype(BF16)


def _dft_stage1(x5, s1_tab):
    nb, ncb = x5.shape[0], x5.shape[-1] // HY
    out = pl.BlockSpec((None, None, FFT_A, S1_TILES, FFT_T, HY), lambda b, cb, j: (b, cb, 0, j, 0, 0))
    return pl.pallas_call(
        _s1_body,
        grid=(nb, ncb, FFT_B // S1_TILES),
        in_specs=[pl.BlockSpec((None, HALF_A, S1_TILES, FFT_T, HY), lambda b, cb, j: (b, 0, j, 0, cb)),
                  pl.BlockSpec((S1_TILES, 2 * FFT_A * FFT_T, HALF_A * FFT_T), lambda b, cb, j: (j, 0, 0))],
        out_specs=[out, out],
        out_shape=[jax.ShapeDtypeStruct((nb, ncb, FFT_A, FFT_B, FFT_T, HY), BF16)] * 2,
        compiler_params=_cparams(("parallel", "parallel", "parallel")),
        name="dft_stage1",
    )(x5, s1_tab)


def _block_rows(r):
    return slice(r * FFT_R, (r + 1) * FFT_R)


def _stack_complex(re_ref, im_ref, sl):
    return jnp.concatenate([re_ref[sl, :], im_ref[sl, :]], axis=0)


def _spec_body(are_ref, aim_ref, wf_ref, hre_ref, him_ref):
    wf = wf_ref[...]
    for r in range(HB_ROWS // FFT_R):
        sl = _block_rows(r)
        first = jnp.dot(wf, _stack_complex(are_ref.at[0], aim_ref.at[0], sl), preferred_element_type=F32)
        second = jnp.dot(wf, _stack_complex(are_ref.at[1], aim_ref.at[1], sl), preferred_element_type=F32)
        x = first + second if r % 2 == 0 else first - second
        hre_ref[sl, :] = x[:FFT_R]
        him_ref[sl, :] = x[FFT_R:]


def _filter_spectrum(a_re, a_im, wf):
    assert (HB_ROWS // FFT_R) % 2 == 0
    blk_in = pl.BlockSpec((2, None, HB_ROWS, HY), lambda r, o: (0, o, r, 0))
    blk_out = pl.BlockSpec((None, HB_ROWS, HY), lambda r, o: (o, r, 0))
    return pl.pallas_call(
        _spec_body,
        grid=(N_FFT // HB_ROWS, 2),
        in_specs=[blk_in, blk_in, pl.BlockSpec((2 * FFT_R, 2 * FFT_R), lambda r, o: (0, 0))],
        out_specs=[blk_out, blk_out],
        out_shape=[jax.ShapeDtypeStruct((2, N_FFT, HY), F32)] * 2,
        compiler_params=_cparams(("parallel", "parallel")),
        name="filter_spectrum",
    )(a_re, a_im, wf)


def _s2_body(are_ref, aim_ref, hre_ref, him_ref, wf_ref, wi_ref, cre_ref, cim_ref):
    wf, wi = wf_ref[...], wi_ref[...]
    for r in range(HB_ROWS // FFT_R):
        sl = _block_rows(r)
        x = jnp.dot(wf, _stack_complex(are_ref, aim_ref, sl), preferred_element_type=F32)
        xre, xim = x[:FFT_R], x[FFT_R:]
        hre, him = hre_ref[sl, :], him_ref[sl, :]
        y = jnp.concatenate([(xre * hre - xim * him).astype(BF16), (xre * him + xim * hre).astype(BF16)], axis=0)
        c = jnp.dot(wi, y, preferred_element_type=F32)
        cre_ref[sl, :] = c[:FFT_R].astype(BF16)
        cim_ref[sl, :] = c[FFT_R:].astype(BF16)


def _spectral_filter(a_re, a_im, h_re, h_im, order, wf, wi):
    blk = pl.BlockSpec((None, HB_ROWS, HY), lambda r, b: (b, r, 0))
    hblk = pl.BlockSpec((None, HB_ROWS, HY), lambda r, b: (order, r, 0))
    mat = pl.BlockSpec((2 * FFT_R, 2 * FFT_R), lambda r, b: (0, 0))
    return pl.pallas_call(
        _s2_body,
        grid=(N_FFT // HB_ROWS, NB_BATCH),
        in_specs=[blk, blk, hblk, hblk, mat, mat],
        out_specs=[blk, blk],
        out_shape=[jax.ShapeDtypeStruct((NB_BATCH, N_FFT, HY), BF16)] * 2,
        compiler_params=_cparams(("parallel", "parallel")),
        name="spectral_filter",
    )(a_re, a_im, h_re, h_im, wf, wi)


def _s1i_body(cre_ref, cim_ref, g_ref, a_ref, b_ref, sk_ref, o_ref):
    rows = FFT_A * FFT_T
    for p in range(S1_TILES):
        c = jnp.concatenate([cre_ref[:, p].reshape(rows, HY), cim_ref[:, p].reshape(rows, HY)], axis=0)
        y = jnp.dot(g_ref[p], c, preferred_element_type=F32).reshape(HALF_A, FFT_T, HY)
        o_ref[:, p] = a_ref[:, p] * (y + b_ref[:, p] * sk_ref[...])


def _idft_stage1_gate(c_re, c_im, s1i_tab, a5, b5, skip):
    data = pl.BlockSpec((None, HALF_A, S1_TILES, FFT_T, HY), lambda b, j: (b, 0, j, 0, 0))
    spec = pl.BlockSpec((None, FFT_A, S1_TILES, FFT_T, HY), lambda b, j: (b, 0, j, 0, 0))
    return pl.pallas_call(
        _s1i_body,
        grid=(NB_BATCH, FFT_B // S1_TILES),
        in_specs=[spec, spec, pl.BlockSpec((S1_TILES, HALF_A * FFT_T, 2 * FFT_A * FFT_T), lambda b, j: (j, 0, 0)),
                  data, data, pl.BlockSpec((1, 1, HY), lambda b, j: (0, 0, 0))],
        out_specs=data,
        out_shape=jax.ShapeDtypeStruct((NB_BATCH, HALF_A, FFT_B, FFT_T, HY), F32),
        compiler_params=_cparams(("parallel", "parallel")),
        name="idft_stage1_gate",
    )(c_re, c_im, s1i_tab, a5, b5, skip.reshape(1, 1, HY))


def _hyc_body(u_ref, w_ref, b_ref, circ_ref, fc_ref, fs_ref, sk_ref, o_ref):
    usc = _short_conv(u_ref[...], w_ref[...], b_ref[...])
    fc, fs = fc_ref[...], fs_ref[...]
    circ = circ_ref[...]
    h_re, h_im = _dot(fc, circ), -_dot(fs, circ)

    def long_conv(s, o):
        sl = slice(o * HY, (o + 1) * HY)
        s_re, s_im = _dot(fc[:, :CTX], s), -_dot(fs[:, :CTX], s)
        hre, him = h_re[:, sl], h_im[:, sl]
        y_re = s_re * hre - s_im * him
        y_im = s_re * him + s_im * hre
        y = (_dot(fc[:CTX, :], y_re) - _dot(fs[:CTX, :], y_im)) * (1.0 / N_CTXF)
        return y + s * sk_ref[o:o + 1, :]

    v, x1, x2 = usc[:, 0:HY], usc[:, HY:2 * HY], usc[:, 2 * HY:3 * HY]
    o_ref[...] = x2 * long_conv(x1 * long_conv(v, 0), 1)


def _hyena_ctx(hy, conv_w, conv_b, circ_c, fc, fs, skip):
    const = lambda shape: pl.BlockSpec(shape, lambda b: (0, 0))
    return pl.pallas_call(
        _hyc_body,
        grid=(NB_BATCH,),
        in_specs=[pl.BlockSpec((CTX, 3 * HY), lambda b: (T_LAT // CTX + b, 0)), const((3, 3 * HY)), const((1, 3 * HY)),
                  const((N_CTXF, 2 * HY)), const((N_CTXF, N_CTXF)), const((N_CTXF, N_CTXF)), const((2, HY))],
        out_specs=pl.BlockSpec((CTX, HY), lambda b: (b, 0)),
        out_shape=jax.ShapeDtypeStruct((T_CTX, HY), F32),
        compiler_params=_cparams(("parallel",)),
        name="hyena_ctx",
    )(hy, conv_w, conv_b.reshape(1, 3 * HY), circ_c, fc, fs, skip)


N_ML_STEPS = CTX // CHUNK + SEQ // CHUNK


def _ml_chain(direction, qkv, gates, gates_t, cum_c, cum_r, c_scr, m_scr, mask, ones_col):
    outs = []
    for h in range(ML_H):
        ic, fc = (h, 4 + h) if direction == 0 else (8 + h, 12 + h)
        chain = direction * ML_H + h
        b_c, b_r = cum_c[:, fc:fc + 1], cum_r[fc:fc + 1, :]
        li_c, li_r = gates[:, ic:ic + 1], gates_t[ic:ic + 1, :]
        b_end = b_r[:, CHUNK - 1:CHUNK] if direction == 0 else b_r[:, 0:1]
        q = qkv[:, HD * h:HD * (h + 1)]
        k = qkv[:, ML_W + HD * h:ML_W + HD * (h + 1)] * (HD ** -0.5)
        vext = jnp.concatenate([qkv[:, 2 * ML_W + HD * h:2 * ML_W + HD * (h + 1)], ones_col], axis=1)
        c_prev, m_prev = c_scr[chain], m_scr[chain]
        dmat = jnp.where(mask, b_c - b_r + li_r, NEG)
        m_intra = jnp.max(dmat, axis=-1, keepdims=True)
        s = _dot_nt(q, k) * jnp.exp(dmat - m_intra)
        inter = b_c + m_prev
        m_t = jnp.maximum(inter, m_intra)
        hx = jnp.exp(m_intra - m_t) * _dot(s, vext) + jnp.exp(inter - m_t) * _dot(q, c_prev)
        den = jnp.maximum(jnp.abs(hx[:, HD:HD + 1]), jnp.exp(-m_t))
        outs.append(hx[:, :HD] / den)
        g_c = b_end - b_c + li_c
        m_new = jnp.maximum(b_end + m_prev, jnp.max(g_c, axis=0, keepdims=True))
        c_scr[chain] = jnp.exp(b_end + m_prev - m_new) * c_prev + _dot_tn(k, jnp.exp(g_c - m_new) * vext)
        m_scr[chain] = m_new
    return jnp.concatenate(outs, axis=1)


def _ml_body(qf_ref, gf_ref, gtf_ref, qb_ref, gb_ref, gtb_ref, bias_ref, bias_t_ref, tril_ref, triu_ref,
             hf_ref, hb_ref, c_scr, m_scr):
    @pl.when(pl.program_id(1) == 0)
    def _():
        c_scr[...] = jnp.zeros_like(c_scr)
        m_scr[...] = jnp.zeros_like(m_scr)

    tril, triu = tril_ref[...], triu_ref[...]
    row = lax.broadcasted_iota(jnp.int32, (CHUNK, CHUNK), 0)
    col = lax.broadcasted_iota(jnp.int32, (CHUNK, CHUNK), 1)
    ones_col = (lax.broadcasted_iota(jnp.int32, (CHUNK, HD), 1) == 0).astype(F32)
    for direction, (q_ref, g_ref, gt_ref, o_ref) in enumerate(((qf_ref, gf_ref, gtf_ref, hf_ref), (qb_ref, gb_ref, gtb_ref, hb_ref))):
        gates = g_ref[...] + bias_ref[...]
        gates_t = gt_ref[...] + bias_t_ref[...]
        ls, ls_t = _log_sigmoid(gates), _log_sigmoid(gates_t)
        if direction == 0:
            cum_c, cum_r, mask = _dot_sel_l(tril, ls), _dot_sel_r(ls_t, triu), col <= row
        else:
            cum_c, cum_r, mask = _dot_sel_l(triu, ls), _dot_sel_r(ls_t, tril), col >= row
        o_ref[...] = _ml_chain(direction, q_ref[...], gates, gates_t, cum_c, cum_r, c_scr, m_scr, mask, ones_col)


def _mlstm(mqkv, mg, mg_t, gate_b):
    nctx = CTX // CHUNK
    nlat = SEQ // CHUNK

    def fwd(b, i):
        return jnp.where(i < nctx, T_LAT // CHUNK + nctx * b + i, nlat * b + (i - nctx))

    def bwd(b, i):
        return jnp.where(i < nctx, T_LAT // CHUNK + nctx * b + (nctx - 1 - i), nlat * b + (N_ML_STEPS - 1 - i))

    bias = jnp.pad(gate_b, (0, 128 - 4 * ML_H)).reshape(1, 128)
    bias_t = gate_b.reshape(4 * ML_H, 1)
    tril = jnp.asarray(np.tril(np.ones((CHUNK, CHUNK), np.float32))).astype(BF16)
    triu = jnp.asarray(np.triu(np.ones((CHUNK, CHUNK), np.float32))).astype(BF16)
    const = lambda shape: pl.BlockSpec(shape, lambda b, i: (0, 0))
    ins = []
    for f in (fwd, bwd):
        ins += [pl.BlockSpec((CHUNK, 3 * ML_W), lambda b, i, f=f: (f(b, i), 0)),
                pl.BlockSpec((CHUNK, 128), lambda b, i, f=f: (f(b, i), 0)),
                pl.BlockSpec((4 * ML_H, CHUNK), lambda b, i, f=f: (0, f(b, i)))]
    return pl.pallas_call(
        _ml_body,
        grid=(NB_BATCH, N_ML_STEPS),
        in_specs=ins + [const((1, 128)), const((4 * ML_H, 1)), const((CHUNK, CHUNK)), const((CHUNK, CHUNK))],
        out_specs=[pl.BlockSpec((CHUNK, ML_W), lambda b, i: (fwd(b, i), 0)),
                   pl.BlockSpec((CHUNK, ML_W), lambda b, i: (bwd(b, i), 0))],
        out_shape=[jax.ShapeDtypeStruct((R_ALL, ML_W), F32)] * 2,
        scratch_shapes=[pltpu.VMEM((2 * ML_H, HD, 128), F32), pltpu.VMEM((2 * ML_H, 1, 1), F32)],
        compiler_params=_cparams(("parallel", "arbitrary")),
        name="mlstm",
    )(mqkv, mg, mg_t, mqkv, mg, mg_t, bias, bias_t, tril, triu)


def _out_body(att_ref, hl_ref, hc_ref, hf_ref, hb_ref, mo_ref, nw_ref, bd_ref, w_ref, x_ref, g_ref, o_ref):
    i = pl.program_id(0)
    hn = _head_rms(hf_ref[...] + hb_ref[...], bd_ref[...], nw_ref[...])
    mlo = hn * _sigmoid(mo_ref[...])
    hy = jnp.where(i < NT_LAT, hl_ref[...], hc_ref[...])
    y = _dot(att_ref[...], w_ref[0:ATT_W, :]) + _dot(hy, w_ref[ATT_W:ATT_W + HY, :]) + _dot(mlo, w_ref[ATT_W + HY:, :])
    o_ref[...] = x_ref[...] + g_ref[...] * y


def _out_proj(att, hyo, hyoc, hf, hb, mo, ml_nw, w_out_b, xs, mod4, layer, with_ctx):
    tiles = NT_ALL if with_ctx else NT_LAT
    rows = tiles * TM
    row = lambda w: pl.BlockSpec((TM, w), lambda i: (i, 0))
    bd = jnp.asarray(_blockdiag_ones(ML_W)).astype(BF16)
    return pl.pallas_call(
        _out_body,
        grid=(tiles,),
        in_specs=[row(ATT_W),
                  pl.BlockSpec((TM, HY), lambda i: (jnp.minimum(i, NT_LAT - 1), 0)),
                  pl.BlockSpec((TM, HY), lambda i: (jnp.maximum(i - NT_LAT, 0), 0)),
                  row(ML_W), row(ML_W), row(ML_W),
                  pl.BlockSpec((1, ML_W), lambda i: (0, 0)), pl.BlockSpec((ML_W, ML_W), lambda i: (0, 0)),
                  pl.BlockSpec((D, D), lambda i: (0, 0)), row(D), _mod_spec(layer, 2)],
        out_specs=row(D),
        out_shape=jax.ShapeDtypeStruct((rows, D), F32),
        compiler_params=_cparams(("parallel",)),
        name="out_proj",
    )(att, hyo, hyoc, hf, hb, mo, ml_nw.reshape(1, ML_W), bd, w_out_b, xs, mod4)


FF_CH = D_FF // 2


def _ffn_body(x_ref, sh_ref, sc_ref, g_ref, nw_ref, wg_ref, wu_ref, wd_ref, o_ref):
    x = x_ref[...]
    hb = _modulated_norm(x, nw_ref[...], sc_ref[...], sh_ref[...]).astype(BF16)
    acc = None
    for c in range(D_FF // FF_CH):
        sl = slice(c * FF_CH, (c + 1) * FF_CH)
        a = jnp.dot(hb, wg_ref[:, sl], preferred_element_type=F32)
        u = jnp.dot(hb, wu_ref[:, sl], preferred_element_type=F32)
        part = _dot(a * _sigmoid(a) * u, wd_ref[sl, :])
        acc = part if acc is None else acc + part
    o_ref[...] = x + g_ref[...] * acc


def _ffn(xs, mod4, layer, nw, wg, wu, wd, with_ctx):
    tiles = NT_ALL if with_ctx else NT_LAT
    row = pl.BlockSpec((TM, D), lambda i: (i, 0))
    resident = lambda shape: pl.BlockSpec(shape, lambda i: (0, 0), pipeline_mode=pl.Buffered(1))
    return pl.pallas_call(
        _ffn_body,
        grid=(tiles,),
        in_specs=[row, _mod_spec(layer, 3), _mod_spec(layer, 4), _mod_spec(layer, 5), pl.BlockSpec((1, D), lambda i: (0, 0)),
                  resident((D, D_FF)), resident((D, D_FF)), resident((D_FF, D))],
        out_specs=row,
        out_shape=jax.ShapeDtypeStruct((tiles * TM, D), F32),
        compiler_params=_cparams(("parallel",), 56),
        name="ffn",
    )(xs, mod4, mod4, mod4, nw.reshape(1, D), wg, wu, wd)


def _rope_tables():
    n_rows = SEQ // GRID_W
    row = jnp.repeat(jnp.arange(n_rows), GRID_W)
    col = jnp.tile(jnp.arange(GRID_W), n_rows)
    nf = HD // 4
    inv_freq = ROPE_BASE ** (-jnp.arange(nf, dtype=F32) / nf)
    ang = jnp.stack([row[:, None] * inv_freq, col[:, None] * inv_freq], axis=1)
    cos, sin = jnp.cos(ang), jnp.sin(ang)
    cos_h = jnp.concatenate([cos, cos], axis=-1).reshape(SEQ, HD)
    sin_h = jnp.concatenate([-sin, sin], axis=-1).reshape(SEQ, HD)
    cos_t = jnp.concatenate([jnp.tile(cos_h, (1, 2)), jnp.ones((TM, 128), F32)], axis=0)
    sin_t = jnp.concatenate([jnp.tile(sin_h, (1, 2)), jnp.zeros((TM, 128), F32)], axis=0)
    return cos_t, sin_t


def kernel(x, c, ctx, c_ctx, w_mod, b_mod, norm1_w, norm2_w, w_in, w_out, q_norm_w, k_norm_w, attn_sink, hy_conv_w, hy_conv_b, hy_w1, hy_b1, hy_freq, hy_w2, hy_b2, hy_w3, hy_decay, hy_skip, ml_gate_b, ml_norm_w, ffn_w_gate, ffn_w_up, ffn_w_down):
    xs = jnp.concatenate([x.reshape(T_LAT, D), ctx.reshape(T_CTX, D)], axis=0)
    cc8 = jnp.concatenate([c, c_ctx[None, :], jnp.zeros((8 - NB_BATCH - 1, D), F32)], axis=0)
    mod4 = _modulation(cc8, w_mod, b_mod).reshape(DEPTH, 8, 1, 6 * D)

    cos_t, sin_t = _rope_tables()
    feats_l, feats_c = _pos_feats_circ(SEQ), _pos_feats_circ(CTX)
    s1_tab = jnp.asarray(_TAB["s1"]).astype(BF16)
    s1i_tab = jnp.asarray(_TAB["s1i"]).astype(BF16)
    wf, wi = jnp.asarray(_TAB["wf"]).astype(BF16), jnp.asarray(_TAB["wi"]).astype(BF16)
    fc, fs = jnp.asarray(_TAB["fc"]).astype(BF16), jnp.asarray(_TAB["fs"]).astype(BF16)

    w_in_p = jnp.pad(w_in, ((0, 0), (0, 0), (0, P_PAD - P_IN))).astype(BF16)
    w_out_b = w_out.astype(BF16)
    wg_b, wu_b, wd_b = ffn_w_gate.astype(BF16), ffn_w_up.astype(BF16), ffn_w_down.astype(BF16)
    hpad = 128 - FILT_HID
    w1p = jnp.pad(hy_w1, ((0, 0), (0, 128 - hy_w1.shape[1]), (0, hpad)))
    b1p = jnp.pad(hy_b1, ((0, 0), (0, hpad))).reshape(DEPTH, 1, 128)
    frp = jnp.pad(hy_freq, ((0, 0), (0, hpad))).reshape(DEPTH, 1, 128)
    w2p = jnp.pad(hy_w2, ((0, 0), (0, hpad), (0, hpad)))
    b2p = jnp.pad(hy_b2, ((0, 0), (0, hpad))).reshape(DEPTH, 1, 128)
    w3p = jnp.pad(hy_w3, ((0, 0), (0, hpad), (0, 0)))
    sink8 = jnp.pad(attn_sink, ((0, 0), (0, 128 - ATT_H)))[:, None, :] * jnp.ones((1, 8, 1), F32)
    zeros_tile = jnp.zeros((TM, HY), F32)

    out = None
    for l in range(DEPTH):
        last = l == DEPTH - 1
        att_qkv, hy, mqkv, mo, mg = _in_proj(xs, mod4, l, norm1_w[l], w_in_p[l])

        qn, kn = _qk_prep(att_qkv, cos_t, sin_t, q_norm_w[l], k_norm_w[l])
        att = _attention(qn, kn, att_qkv, sink8[l], not last)

        dec = hy_decay[l].reshape(1, N_FILT)
        circ = _circular_filters(feats_l, w1p[l], b1p[l], frp[l], w2p[l], b2p[l], w3p[l], dec)
        f_re, f_im = _dft_stage1(circ.reshape(2, HALF_A, FFT_B, FFT_T, 2 * HY), s1_tab)
        h_re, h_im = _filter_spectrum(f_re.reshape(2, 2, N_FFT, HY), f_im.reshape(2, 2, N_FFT, HY), wf)

        v, x1, x2 = _short_conv_latent(hy, hy_conv_w[l], hy_conv_b[l])
        view = lambda t: t.reshape(NB_BATCH, HALF_A, FFT_B, FFT_T, HY)
        s_in, gate_a = view(v), view(x1)
        for order in range(2):
            a_re, a_im = _dft_stage1(s_in, s1_tab)
            c_re, c_im = _spectral_filter(a_re.reshape(NB_BATCH, N_FFT, HY), a_im.reshape(NB_BATCH, N_FFT, HY), h_re, h_im, order, wf, wi)
            tiles = lambda t: t.reshape(NB_BATCH, FFT_A, FFT_B, FFT_T, HY)
            s_in = _idft_stage1_gate(tiles(c_re), tiles(c_im), s1i_tab, gate_a, s_in, hy_skip[l, order])
            gate_a = view(x2)
        hyo = s_in.reshape(T_LAT, HY)

        if last:
            hyoc = zeros_tile
        else:
            circ_c = _circular_filters(feats_c, w1p[l], b1p[l], frp[l], w2p[l], b2p[l], w3p[l], dec)
            hyoc = _hyena_ctx(hy, hy_conv_w[l], hy_conv_b[l], circ_c, fc, fs, hy_skip[l])

        hf, hb = _mlstm(mqkv, mg, mg[:, :4 * ML_H].T, ml_gate_b[l])

        xs1 = _out_proj(att, hyo, hyoc, hf, hb, mo, ml_norm_w[l], w_out_b[l], xs, mod4, l, not last)
        xs = _ffn(xs1, mod4, l, norm2_w[l], wg_b[l], wu_b[l], wd_b[l], not last)
        out = xs
    return out.reshape(NB_BATCH, SEQ, D)
```

```python
import functools
import math

import numpy as np
import jax
import jax.numpy as jnp
from jax import lax
from jax.experimental import pallas as pl
from jax.experimental.pallas import tpu as pltpu

F32 = jnp.float32
BF16 = jnp.bfloat16

D = 1024
NB_BATCH = 4
SEQ = 4096
DEPTH = 4
GRID_W = 64
CTX = 256
T_LAT = NB_BATCH * SEQ
T_CTX = NB_BATCH * CTX
R_ALL = T_LAT + T_CTX

HD = 64
ATT_H = 6
ATT_W = ATT_H * HD
KV_W = 2 * HD
BLK = 128
NBLK = SEQ // BLK
ROPE_BASE = 10000.0

HY = 384
N_FILT = 4 * HY
POS_BANDS = 16
FILT_HID = 64

ML_H = 4
ML_W = 256
CHUNK = 128

D_FF = 2816
P_IN = 2832
P_PAD = 2944
NEG = -1e30
EPS = 1e-6

TM = 512
NT_LAT = T_LAT // TM
NT_ALL = R_ALL // TM
TILES_PER_SEQ = SEQ // TM

N_FFT = 2 * SEQ
FFT_A = 32
FFT_R = 256
FFT_T = 16
FFT_B = FFT_R // FFT_T
HALF_A = FFT_A // 2
S1_TILES = 4
HB_ROWS = 1024
N_CTXF = 2 * CTX


def _np_tables():
    ka = np.arange(FFT_A)[None, :, None, None]
    cc = np.arange(FFT_T)[None, None, :, None]
    aa = np.arange(HALF_A)[None, None, None, :]
    bb = np.arange(FFT_B)[:, None, None, None]
    ph = (ka * (FFT_R * aa + FFT_T * bb + cc)) % N_FFT
    th = 2.0 * np.pi * ph / N_FFT
    eye = np.eye(FFT_T)
    cos = np.einsum("bkca,cd->bkcad", np.cos(th), eye)
    sin = np.einsum("bkca,cd->bkcad", np.sin(th), eye)
    rows, cols = FFT_A * FFT_T, HALF_A * FFT_T
    s1 = np.concatenate([cos.reshape(FFT_B, rows, cols), -sin.reshape(FFT_B, rows, cols)], axis=1)
    s1i = np.concatenate([cos.reshape(FFT_B, rows, cols).transpose(0, 2, 1),
                          -sin.reshape(FFT_B, rows, cols).transpose(0, 2, 1)], axis=2) / N_FFT
    r = np.arange(FFT_R)
    th2 = 2.0 * np.pi * ((r[:, None] * r[None, :]) % FFT_R) / FFT_R
    c2, s2 = np.cos(th2), np.sin(th2)
    wf = np.block([[c2, s2], [-s2, c2]])
    wi = np.block([[c2, -s2], [s2, c2]])
    c = np.arange(N_CTXF)
    th3 = 2.0 * np.pi * ((c[:, None] * c[None, :]) % N_CTXF) / N_CTXF
    return dict(s1=s1.astype(np.float32), s1i=s1i.astype(np.float32), wf=wf.astype(np.float32), wi=wi.astype(np.float32),
                fc=np.cos(th3).astype(np.float32), fs=np.sin(th3).astype(np.float32))


_TAB = _np_tables()


def _blockdiag_ones(width):
    return np.kron(np.eye(width // HD), np.ones((HD, HD))).astype(np.float32)


def _cparams(sem, vmem_mb=48):
    return pltpu.CompilerParams(dimension_semantics=sem, vmem_limit_bytes=vmem_mb * 1024 * 1024)


def _dot(a, b):
    return jnp.dot(a.astype(BF16), b.astype(BF16), preferred_element_type=F32)


def _dot_nt(a, b):
    return lax.dot_general(a.astype(BF16), b.astype(BF16), (((1,), (1,)), ((), ())), preferred_element_type=F32)


def _dot_tn(a, b):
    return lax.dot_general(a.astype(BF16), b.astype(BF16), (((0,), (0,)), ((), ())), preferred_element_type=F32)


def _split3(x):
    x1 = x.astype(BF16)
    r1 = x - x1.astype(F32)
    x2 = r1.astype(BF16)
    x3 = (r1 - x2.astype(F32)).astype(BF16)
    return x1, x2, x3


def _dot_sel_l(sel, x):
    return sum(jnp.dot(sel, p, preferred_element_type=F32) for p in _split3(x))


def _dot_sel_r(x, sel):
    return sum(jnp.dot(p, sel, preferred_element_type=F32) for p in _split3(x))


def _sigmoid(x):
    return 1.0 / (1.0 + jnp.exp(-x))


def _log_sigmoid(x):
    return jnp.minimum(x, 0.0) - jnp.log(1.0 + jnp.exp(-jnp.abs(x)))


def _head_rms(t, bd, w):
    ss = _dot_sel_r(t * t, bd)
    return t * lax.rsqrt(ss * (1.0 / HD) + EPS) * w


def _mod_row(i):
    return jnp.where(i < NT_LAT, i // TILES_PER_SEQ, NB_BATCH)


def _mod_spec(layer, k):
    return pl.BlockSpec((None, None, 1, D), lambda i: (layer, _mod_row(i), 0, k))


def _mod_body(s_ref, w_ref, b_ref, o_ref):
    s = s_ref[...]
    s = s * _sigmoid(s)
    o_ref[...] = _dot(s, w_ref[...]) + b_ref[...]


def _modulation(cc8, w_mod, b_mod):
    nc = 1536
    return pl.pallas_call(
        _mod_body,
        grid=(DEPTH, 6 * D // nc),
        in_specs=[pl.BlockSpec((8, D), lambda l, j: (0, 0)),
                  pl.BlockSpec((None, D, nc), lambda l, j: (l, 0, j)),
                  pl.BlockSpec((None, 1, nc), lambda l, j: (l, 0, j))],
        out_specs=pl.BlockSpec((None, 8, nc), lambda l, j: (l, 0, j)),
        out_shape=jax.ShapeDtypeStruct((DEPTH, 8, 6 * D), F32),
        compiler_params=_cparams(("parallel", "parallel")),
        name="modulation",
    )(cc8, w_mod, b_mod.reshape(DEPTH, 1, 6 * D))


_IN_COLS = ((0, 640), (640, 1792), (1792, 2560), (2560, 2816), (2816, 2944))


def _modulated_norm(x, nw, sc, sh):
    ms = jnp.mean(x * x, axis=-1, keepdims=True)
    return (x * lax.rsqrt(ms + EPS) * nw) * (1.0 + sc) + sh


def _in_body(x_ref, sh_ref, sc_ref, nw_ref, w_ref, *o_refs):
    hb = _modulated_norm(x_ref[...], nw_ref[...], sc_ref[...], sh_ref[...]).astype(BF16)
    for o_ref, (a, b) in zip(o_refs, _IN_COLS):
        o_ref[...] = jnp.dot(hb, w_ref[:, a:b], preferred_element_type=F32)


def _in_proj(xs, mod4, layer, nw, w_in_p):
    widths = [b - a for a, b in _IN_COLS]
    return pl.pallas_call(
        _in_body,
        grid=(NT_ALL,),
        in_specs=[pl.BlockSpec((TM, D), lambda i: (i, 0)),
                  _mod_spec(layer, 0), _mod_spec(layer, 1),
                  pl.BlockSpec((1, D), lambda i: (0, 0)),
                  pl.BlockSpec((D, P_PAD), lambda i: (0, 0))],
        out_specs=[pl.BlockSpec((TM, w), lambda i: (i, 0)) for w in widths],
        out_shape=[jax.ShapeDtypeStruct((R_ALL, w), F32) for w in widths],
        compiler_params=_cparams(("parallel",), 56),
        name="in_proj",
    )(xs, mod4, mod4, nw.reshape(1, D), w_in_p)


def _rope(t, cos, sin_signed, first_half):
    outs = []
    for c in range(t.shape[1] // 128):
        tc = t[:, c * 128:(c + 1) * 128]
        partner = jnp.where(first_half, pltpu.roll(tc, 112, 1), pltpu.roll(tc, 16, 1))
        outs.append(tc * cos + partner * sin_signed)
    return outs[0] if len(outs) == 1 else jnp.concatenate(outs, axis=1)


def _qk_body(a_ref, cos_ref, sin_ref, qw_ref, kw_ref, bdq_ref, bdk_ref, qt_ref, k_ref, vt_ref):
    cos, sin = cos_ref[...], sin_ref[...]
    lane = lax.broadcasted_iota(jnp.int32, (TM, 128), 1)
    first_half = (lane & 31) < 16
    q = _head_rms(a_ref[:, 0:ATT_W], bdq_ref[...], qw_ref[...])
    k = _head_rms(a_ref[:, ATT_W:ATT_W + KV_W], bdk_ref[...], kw_ref[...])
    q = _rope(q, cos, sin, first_half) * (HD ** -0.5)
    qt_ref[...] = q.T.astype(BF16)
    k_ref[...] = _rope(k, cos, sin, first_half).astype(BF16)
    vt_ref[...] = a_ref[:, ATT_W + KV_W:ATT_W + 2 * KV_W].T.astype(BF16)


def _qk_prep(att_qkv, cos_t, sin_t, qw, kw):
    def tab(i):
        return (jnp.where(i < NT_LAT, i % TILES_PER_SEQ, TILES_PER_SEQ), 0)
    bdq = jnp.asarray(_blockdiag_ones(ATT_W)).astype(BF16)
    bdk = jnp.asarray(_blockdiag_ones(KV_W)).astype(BF16)
    return pl.pallas_call(
        _qk_body,
        grid=(NT_ALL,),
        in_specs=[pl.BlockSpec((TM, 640), lambda i: (i, 0)),
                  pl.BlockSpec((TM, 128), tab), pl.BlockSpec((TM, 128), tab),
                  pl.BlockSpec((1, ATT_W), lambda i: (0, 0)), pl.BlockSpec((1, KV_W), lambda i: (0, 0)),
                  pl.BlockSpec((ATT_W, ATT_W), lambda i: (0, 0)), pl.BlockSpec((KV_W, KV_W), lambda i: (0, 0))],
        out_specs=[pl.BlockSpec((ATT_W, TM), lambda i: (0, i)), pl.BlockSpec((TM, KV_W), lambda i: (i, 0)),
                   pl.BlockSpec((KV_W, TM), lambda i: (0, i))],
        out_shape=[jax.ShapeDtypeStruct((ATT_W, R_ALL), BF16), jax.ShapeDtypeStruct((R_ALL, KV_W), BF16),
                   jax.ShapeDtypeStruct((KV_W, R_ALL), BF16)],
        compiler_params=_cparams(("parallel",)),
        name="qk_prep",
    )(att_qkv, cos_t, sin_t, jnp.tile(qw, ATT_H).reshape(1, ATT_W), jnp.tile(kw, 2).reshape(1, KV_W), bdq, bdk)


def _att_heads(qt, kv_list, sink_ref, o_ref):
    nq = qt.shape[1]
    zeros = jnp.zeros((HD, nq), BF16)
    outs = []
    for h in range(ATT_H):
        g = h // 3
        qh = qt[HD * h:HD * (h + 1), :]
        qe = jnp.concatenate([qh, zeros] if g == 0 else [zeros, qh], axis=0)
        sink = sink_ref[0:1, h:h + 1]
        scores = []
        m = sink
        for k, _, mask in kv_list:
            s = jnp.dot(k, qe, preferred_element_type=F32)
            if mask is not None:
                s = jnp.where(mask, s, NEG)
            scores.append(s)
            m = jnp.maximum(m, jnp.max(s, axis=0, keepdims=True))
        l = jnp.exp(sink - m)
        acc = None
        for s, (_, vt, _) in zip(scores, kv_list):
            p = jnp.exp(s - m)
            l = l + jnp.sum(p, axis=0, keepdims=True)
            pv = jnp.dot(vt[HD * g:HD * (g + 1), :], p.astype(BF16), preferred_element_type=F32)
            acc = pv if acc is None else acc + pv
        outs.append(acc / l)
    o_ref[...] = jnp.concatenate(outs, axis=0).T


def _att_body(qt_ref, kp_ref, ko_ref, kn_ref, vp_ref, vo_ref, vn_ref, kc_ref, vc_ref, sink_ref, o_ref):
    j = pl.program_id(1)
    qt = qt_ref[...]
    kc, vc = kc_ref[...], vc_ref[...]

    @pl.when(j < NBLK)
    def _():
        key = lax.broadcasted_iota(jnp.int32, (BLK, BLK), 0)
        qry = lax.broadcasted_iota(jnp.int32, (BLK, BLK), 1)
        mask_prev = jnp.logical_and(key >= qry, j > 0)
        mask_next = jnp.logical_and(key <= qry, j < NBLK - 1)
        _att_heads(qt, [(kp_ref[...], vp_ref[...], mask_prev), (ko_ref[...], vo_ref[...], None),
                        (kn_ref[...], vn_ref[...], mask_next), (kc, vc, None)], sink_ref, o_ref)

    @pl.when(j >= NBLK)
    def _():
        _att_heads(qt, [(kc, vc, None)], sink_ref, o_ref)


def _attention(qt, kn, vt, sink8, with_ctx):
    steps = NBLK + (CTX // BLK if with_ctx else 0)
    rows = R_ALL if with_ctx else T_LAT

    def qblk(b, j):
        return jnp.where(j < NBLK, b * NBLK + j, T_LAT // BLK + (CTX // BLK) * b + (j - NBLK))

    def band(off):
        return lambda b, j: b * NBLK + jnp.clip(j + off, 0, NBLK - 1)

    kspec = lambda f: pl.BlockSpec((BLK, KV_W), lambda b, j: (f(b, j), 0))
    vspec = lambda f: pl.BlockSpec((KV_W, BLK), lambda b, j: (0, f(b, j)))
    return pl.pallas_call(
        _att_body,
        grid=(NB_BATCH, steps),
        in_specs=[pl.BlockSpec((ATT_W, BLK), lambda b, j: (0, qblk(b, j))),
                  kspec(band(-1)), kspec(band(0)), kspec(band(1)),
                  vspec(band(-1)), vspec(band(0)), vspec(band(1)),
                  pl.BlockSpec((CTX, KV_W), lambda b, j: (T_LAT // CTX + b, 0)),
                  pl.BlockSpec((KV_W, CTX), lambda b, j: (0, T_LAT // CTX + b)),
                  pl.BlockSpec((8, 128), lambda b, j: (0, 0))],
        out_specs=pl.BlockSpec((BLK, ATT_W), lambda b, j: (qblk(b, j), 0)),
        out_shape=jax.ShapeDtypeStruct((rows, ATT_W), F32),
        compiler_params=_cparams(("parallel", "parallel")),
        name="attention",
    )(qt, kn, kn, kn, vt, vt, vt, kn, vt, sink8)


def _short_conv(u, w, b):
    n = u.shape[0]
    row = lax.broadcasted_iota(jnp.int32, u.shape, 0)
    up = jnp.where(row == 0, 0.0, pltpu.roll(u, 1, 0))
    un = jnp.where(row == n - 1, 0.0, pltpu.roll(u, n - 1, 0))
    return up * w[0:1, :] + u * w[1:2, :] + un * w[2:3, :] + b


def _sc_body(u0_ref, u1_ref, u2_ref, w0_ref, w1_ref, w2_ref, b0_ref, b1_ref, b2_ref, o0_ref, o1_ref, o2_ref):
    for u_ref, w_ref, b_ref, o_ref in ((u0_ref, w0_ref, b0_ref, o0_ref), (u1_ref, w1_ref, b1_ref, o1_ref), (u2_ref, w2_ref, b2_ref, o2_ref)):
        o_ref[...] = _short_conv(u_ref[...], w_ref[...], b_ref[...])


def _short_conv_latent(hy, conv_w, conv_b):
    cb = conv_b.reshape(1, 3 * HY)
    ins, ws, bs = [], [], []
    for part in range(3):
        ins.append(pl.BlockSpec((SEQ, 128), lambda b, j, part=part: (b, 3 * part + j)))
        ws.append(pl.BlockSpec((3, 128), lambda b, j, part=part: (0, 3 * part + j)))
        bs.append(pl.BlockSpec((1, 128), lambda b, j, part=part: (0, 3 * part + j)))
    return pl.pallas_call(
        _sc_body,
        grid=(NB_BATCH, 3),
        in_specs=ins + ws + bs,
        out_specs=[pl.BlockSpec((SEQ, 128), lambda b, j: (b, j))] * 3,
        out_shape=[jax.ShapeDtypeStruct((T_LAT, HY), F32)] * 3,
        compiler_params=_cparams(("parallel", "parallel")),
        name="short_conv",
    )(hy, hy, hy, conv_w, conv_w, conv_w, cb, cb, cb)


def _filt_body(n, tl, f_ref, w1_ref, b1_ref, fr_ref, w2_ref, b2_ref, w3_ref, dec_ref, o_ref):
    f = f_ref[...]
    fr = fr_ref[...]
    z = jnp.sin(fr * (_dot(f, w1_ref[...]) + b1_ref[...]))
    z = jnp.sin(fr * (_dot(z, w2_ref[...]) + b2_ref[...]))
    filt = _dot(z, w3_ref[...]) * jnp.exp(-f[:, 0:1] * jnp.abs(dec_ref[...]))
    row = pl.program_id(0) * tl + lax.broadcasted_iota(jnp.int32, filt.shape, 0)
    o_ref[...] = jnp.where(row == n, 0.0, filt)


def _circular_filters(feats_circ, w1p, b1p, frp, w2p, b2p, w3p, dec):
    n = feats_circ.shape[0] // 2
    tl = min(n, 512)
    const = lambda shape: pl.BlockSpec(shape, lambda i: (0, 0))
    half = lambda rows: pl.BlockSpec((rows, 2 * HY), lambda i: (0, i // (n // tl)))
    return pl.pallas_call(
        functools.partial(_filt_body, n, tl),
        grid=(2 * n // tl,),
        in_specs=[pl.BlockSpec((tl, 128), lambda i: (i, 0)), const((128, 128)), const((1, 128)), const((1, 128)),
                  const((128, 128)), const((1, 128)), half(128), half(1)],
        out_specs=pl.BlockSpec((tl, 2 * HY), lambda i: (i, 0)),
        out_shape=jax.ShapeDtypeStruct((2 * n, 2 * HY), F32),
        compiler_params=_cparams(("parallel",)),
        name="hyena_filters",
    )(feats_circ, w1p, b1p, frp, w2p, b2p, w3p, dec)


def _pos_feats_circ(n):
    t = jnp.linspace(0.0, 1.0, n, dtype=F32)[:, None]
    ang = (2.0 * math.pi / n) * jnp.arange(n, dtype=F32)[:, None]
    bands = jnp.linspace(1e-4, POS_BANDS - 1, POS_BANDS, dtype=F32)[None, :]
    feats = jnp.concatenate([t, jnp.cos(bands * ang), -jnp.sin(bands * ang)], axis=-1)
    feats = jnp.pad(feats, ((0, 0), (0, 128 - feats.shape[1])))
    return jnp.concatenate([feats, feats[:1], jnp.flip(feats[:n - 1], axis=0)], axis=0)


def _s1_body(x_ref, m_ref, are_ref, aim_ref):
    half = FFT_A * FFT_T
    for p in range(S1_TILES):
        xs = x_ref[:, p].reshape(HALF_A * FFT_T, HY)
        r = _dot(m_ref[p], xs)
        are_ref[:, p] = r[:half].reshape(FFT_A, FFT_T, HY).astype(BF16)
        aim_ref[:, p] = r[half:].reshape(FFT_A, FFT_T, HY).astype(BF16)


def _dft_stage1(x5, s1_tab):
    nb, ncb = x5.shape[0], x5.shape[-1] // HY
    out = pl.BlockSpec((None, None, FFT_A, S1_TILES, FFT_T, HY), lambda b, cb, j: (b, cb, 0, j, 0, 0))
    return pl.pallas_call(
        _s1_body,
        grid=(nb, ncb, FFT_B // S1_TILES),
        in_specs=[pl.BlockSpec((None, HALF_A, S1_TILES, FFT_T, HY), lambda b, cb, j: (b, 0, j, 0, cb)),
                  pl.BlockSpec((S1_TILES, 2 * FFT_A * FFT_T, HALF_A * FFT_T), lambda b, cb, j: (j, 0, 0))],
        out_specs=[out, out],
        out_shape=[jax.ShapeDtypeStruct((nb, ncb, FFT_A, FFT_B, FFT_T, HY), BF16)] * 2,
        compiler_params=_cparams(("parallel", "parallel", "parallel")),
        name="dft_stage1",
    )(x5, s1_tab)


def _block_rows(r):
    return slice(r * FFT_R, (r + 1) * FFT_R)


def _stack_complex(re_ref, im_ref, sl):
    return jnp.concatenate([re_ref[sl, :], im_ref[sl, :]], axis=0)


def _spec_body(are_ref, aim_ref, wf_ref, hre_ref, him_ref):
    wf = wf_ref[...]
    for r in range(HB_ROWS // FFT_R):
        sl = _block_rows(r)
        first = jnp.dot(wf, _stack_complex(are_ref.at[0], aim_ref.at[0], sl), preferred_element_type=F32)
        second = jnp.dot(wf, _stack_complex(are_ref.at[1], aim_ref.at[1], sl), preferred_element_type=F32)
        x = first + second if r % 2 == 0 else first - second
        hre_ref[sl, :] = x[:FFT_R]
        him_ref[sl, :] = x[FFT_R:]


def _filter_spectrum(a_re, a_im, wf):
    assert (HB_ROWS // FFT_R) % 2 == 0
    blk_in = pl.BlockSpec((2, None, HB_ROWS, HY), lambda r, o: (0, o, r, 0))
    blk_out = pl.BlockSpec((None, HB_ROWS, HY), lambda r, o: (o, r, 0))
    return pl.pallas_call(
        _spec_body,
        grid=(N_FFT // HB_ROWS, 2),
        in_specs=[blk_in, blk_in, pl.BlockSpec((2 * FFT_R, 2 * FFT_R), lambda r, o: (0, 0))],
        out_specs=[blk_out, blk_out],
        out_shape=[jax.ShapeDtypeStruct((2, N_FFT, HY), F32)] * 2,
        compiler_params=_cparams(("parallel", "parallel")),
        name="filter_spectrum",
    )(a_re, a_im, wf)


def _s2_body(are_ref, aim_ref, hre_ref, him_ref, wf_ref, wi_ref, cre_ref, cim_ref):
    wf, wi = wf_ref[...], wi_ref[...]
    for r in range(HB_ROWS // FFT_R):
        sl = _block_rows(r)
        x = jnp.dot(wf, _stack_complex(are_ref, aim_ref, sl), preferred_element_type=F32)
        xre, xim = x[:FFT_R], x[FFT_R:]
        hre, him = hre_ref[sl, :], him_ref[sl, :]
        y = jnp.concatenate([(xre * hre - xim * him).astype(BF16), (xre * him + xim * hre).astype(BF16)], axis=0)
        c = jnp.dot(wi, y, preferred_element_type=F32)
        cre_ref[sl, :] = c[:FFT_R].astype(BF16)
        cim_ref[sl, :] = c[FFT_R:].astype(BF16)


def _spectral_filter(a_re, a_im, h_re, h_im, order, wf, wi):
    blk = pl.BlockSpec((None, HB_ROWS, HY), lambda r, b: (b, r, 0))
    hblk = pl.BlockSpec((None, HB_ROWS, HY), lambda r, b: (order, r, 0))
    mat = pl.BlockSpec((2 * FFT_R, 2 * FFT_R), lambda r, b: (0, 0))
    return pl.pallas_call(
        _s2_body,
        grid=(N_FFT // HB_ROWS, NB_BATCH),
        in_specs=[blk, blk, hblk, hblk, mat, mat],
        out_specs=[blk, blk],
        out_shape=[jax.ShapeDtypeStruct((NB_BATCH, N_FFT, HY), BF16)] * 2,
        compiler_params=_cparams(("parallel", "parallel")),
        name="spectral_filter",
    )(a_re, a_im, h_re, h_im, wf, wi)


def _s1i_body(cre_ref, cim_ref, g_ref, a_ref, b_ref, sk_ref, o_ref):
    rows = FFT_A * FFT_T
    for p in range(S1_TILES):
        c = jnp.concatenate([cre_ref[:, p].reshape(rows, HY), cim_ref[:, p].reshape(rows, HY)], axis=0)
        y = jnp.dot(g_ref[p], c, preferred_element_type=F32).reshape(HALF_A, FFT_T, HY)
        o_ref[:, p] = a_ref[:, p] * (y + b_ref[:, p] * sk_ref[...])


def _idft_stage1_gate(c_re, c_im, s1i_tab, a5, b5, skip):
    data = pl.BlockSpec((None, HALF_A, S1_TILES, FFT_T, HY), lambda b, j: (b, 0, j, 0, 0))
    spec = pl.BlockSpec((None, FFT_A, S1_TILES, FFT_T, HY), lambda b, j: (b, 0, j, 0, 0))
    return pl.pallas_call(
        _s1i_body,
        grid=(NB_BATCH, FFT_B // S1_TILES),
        in_specs=[spec, spec, pl.BlockSpec((S1_TILES, HALF_A * FFT_T, 2 * FFT_A * FFT_T), lambda b, j: (j, 0, 0)),
                  data, data, pl.BlockSpec((1, 1, HY), lambda b, j: (0, 0, 0))],
        out_specs=data,
        out_shape=jax.ShapeDtypeStruct((NB_BATCH, HALF_A, FFT_B, FFT_T, HY), F32),
        compiler_params=_cparams(("parallel", "parallel")),
        name="idft_stage1_gate",
    )(c_re, c_im, s1i_tab, a5, b5, skip.reshape(1, 1, HY))


def _hyc_body(u_ref, w_ref, b_ref, circ_ref, fc_ref, fs_ref, sk_ref, o_ref):
    usc = _short_conv(u_ref[...], w_ref[...], b_ref[...])
    fc, fs = fc_ref[...], fs_ref[...]
    circ = circ_ref[...]
    h_re, h_im = _dot(fc, circ), -_dot(fs, circ)

    def long_conv(s, o):
        sl = slice(o * HY, (o + 1) * HY)
        s_re, s_im = _dot(fc[:, :CTX], s), -_dot(fs[:, :CTX], s)
        hre, him = h_re[:, sl], h_im[:, sl]
        y_re = s_re * hre - s_im * him
        y_im = s_re * him + s_im * hre
        y = (_dot(fc[:CTX, :], y_re) - _dot(fs[:CTX, :], y_im)) * (1.0 / N_CTXF)
        return y + s * sk_ref[o:o + 1, :]

    v, x1, x2 = usc[:, 0:HY], usc[:, HY:2 * HY], usc[:, 2 * HY:3 * HY]
    o_ref[...] = x2 * long_conv(x1 * long_conv(v, 0), 1)


def _hyena_ctx(hy, conv_w, conv_b, circ_c, fc, fs, skip):
    const = lambda shape: pl.BlockSpec(shape, lambda b: (0, 0))
    return pl.pallas_call(
        _hyc_body,
        grid=(NB_BATCH,),
        in_specs=[pl.BlockSpec((CTX, 3 * HY), lambda b: (T_LAT // CTX + b, 0)), const((3, 3 * HY)), const((1, 3 * HY)),
                  const((N_CTXF, 2 * HY)), const((N_CTXF, N_CTXF)), const((N_CTXF, N_CTXF)), const((2, HY))],
        out_specs=pl.BlockSpec((CTX, HY), lambda b: (b, 0)),
        out_shape=jax.ShapeDtypeStruct((T_CTX, HY), F32),
        compiler_params=_cparams(("parallel",)),
        name="hyena_ctx",
    )(hy, conv_w, conv_b.reshape(1, 3 * HY), circ_c, fc, fs, skip)


N_ML_STEPS = CTX // CHUNK + SEQ // CHUNK


def _ml_chain(direction, qkv, gates, gates_t, cum_c, cum_r, c_scr, m_scr, mask, ones_col):
    base = 8 * direction
    lane = lax.broadcasted_iota(jnp.int32, (CHUNK, 128), 1)
    m_prev = m_scr[direction]
    b_end = cum_c[CHUNK - 1:CHUNK, :] if direction == 0 else cum_c[0:1, :]
    heads = []
    m_intra_all = jnp.zeros((CHUNK, 128), F32)
    for h in range(ML_H):
        ic, fc = base + h, base + 4 + h
        b_c, b_r, li_r = cum_c[:, fc:fc + 1], cum_r[fc:fc + 1, :], gates_t[ic:ic + 1, :]
        q = qkv[:, HD * h:HD * (h + 1)]
        k = qkv[:, ML_W + HD * h:ML_W + HD * (h + 1)] * (HD ** -0.5)
        vext = jnp.concatenate([qkv[:, 2 * ML_W + HD * h:2 * ML_W + HD * (h + 1)], ones_col], axis=1)
        c_prev = c_scr[direction * ML_H + h]
        dmat = jnp.where(mask, b_c - b_r + li_r, NEG)
        m_intra = jnp.max(dmat, axis=-1, keepdims=True)
        s = _dot_nt(q, k) * jnp.exp(dmat - m_intra)
        heads.append((k, vext, c_prev, _dot(s, vext), _dot(q, c_prev)))
        m_intra_all = jnp.where(lane == fc, m_intra, m_intra_all)
    inter = cum_c + m_prev
    m_t = jnp.maximum(inter, m_intra_all)
    a_intra, w_inter, e_neg = jnp.exp(m_intra_all - m_t), jnp.exp(inter - m_t), jnp.exp(-m_t)
    g_c = b_end - cum_c + pltpu.roll(gates, 4, 1)
    m_new = jnp.maximum(b_end + m_prev, jnp.max(g_c, axis=0, keepdims=True))
    w_state, decay = jnp.exp(g_c - m_new), jnp.exp(b_end + m_prev - m_new)
    m_scr[direction] = m_new
    outs = []
    for h, (k, vext, c_prev, r_intra, r_inter) in enumerate(heads):
        fc = base + 4 + h
        hx = a_intra[:, fc:fc + 1] * r_intra + w_inter[:, fc:fc + 1] * r_inter
        den = jnp.maximum(jnp.abs(hx[:, HD:HD + 1]), e_neg[:, fc:fc + 1])
        outs.append(hx[:, :HD] / den)
        c_scr[direction * ML_H + h] = decay[:, fc:fc + 1] * c_prev + _dot_tn(k, w_state[:, fc:fc + 1] * vext)
    return jnp.concatenate(outs, axis=1)


def _ml_body(qf_ref, gf_ref, gtf_ref, qb_ref, gb_ref, gtb_ref, bias_ref, bias_t_ref, tril_ref, triu_ref,
             hf_ref, hb_ref, c_scr, m_scr):
    @pl.when(pl.program_id(1) == 0)
    def _():
        c_scr[...] = jnp.zeros_like(c_scr)
        m_scr[...] = jnp.zeros_like(m_scr)

    tril, triu = tril_ref[...], triu_ref[...]
    row = lax.broadcasted_iota(jnp.int32, (CHUNK, CHUNK), 0)
    col = lax.broadcasted_iota(jnp.int32, (CHUNK, CHUNK), 1)
    ones_col = (lax.broadcasted_iota(jnp.int32, (CHUNK, HD), 1) == 0).astype(F32)
    for direction, (q_ref, g_ref, gt_ref, o_ref) in enumerate(((qf_ref, gf_ref, gtf_ref, hf_ref), (qb_ref, gb_ref, gtb_ref, hb_ref))):
        gates = g_ref[...] + bias_ref[...]
        gates_t = gt_ref[...] + bias_t_ref[...]
        ls, ls_t = _log_sigmoid(gates), _log_sigmoid(gates_t)
        if direction == 0:
            cum_c, cum_r, mask = _dot_sel_l(tril, ls), _dot_sel_r(ls_t, triu), col <= row
        else:
            cum_c, cum_r, mask = _dot_sel_l(triu, ls), _dot_sel_r(ls_t, tril), col >= row
        o_ref[...] = _ml_chain(direction, q_ref[...], gates, gates_t, cum_c, cum_r, c_scr, m_scr, mask, ones_col)


def _mlstm(mqkv, mg, mg_t, gate_b):
    nctx = CTX // CHUNK
    nlat = SEQ // CHUNK

    def fwd(b, i):
        return jnp.where(i < nctx, T_LAT // CHUNK + nctx * b + i, nlat * b + (i - nctx))

    def bwd(b, i):
        return jnp.where(i < nctx, T_LAT // CHUNK + nctx * b + (nctx - 1 - i), nlat * b + (N_ML_STEPS - 1 - i))

    bias = jnp.pad(gate_b, (0, 128 - 4 * ML_H)).reshape(1, 128)
    bias_t = gate_b.reshape(4 * ML_H, 1)
    tril = jnp.asarray(np.tril(np.ones((CHUNK, CHUNK), np.float32))).astype(BF16)
    triu = jnp.asarray(np.triu(np.ones((CHUNK, CHUNK), np.float32))).astype(BF16)
    const = lambda shape: pl.BlockSpec(shape, lambda b, i: (0, 0))
    ins = []
    for f in (fwd, bwd):
        ins += [pl.BlockSpec((CHUNK, 3 * ML_W), lambda b, i, f=f: (f(b, i), 0)),
                pl.BlockSpec((CHUNK, 128), lambda b, i, f=f: (f(b, i), 0)),
                pl.BlockSpec((4 * ML_H, CHUNK), lambda b, i, f=f: (0, f(b, i)))]
    return pl.pallas_call(
        _ml_body,
        grid=(NB_BATCH, N_ML_STEPS),
        in_specs=ins + [const((1, 128)), const((4 * ML_H, 1)), const((CHUNK, CHUNK)), const((CHUNK, CHUNK))],
        out_specs=[pl.BlockSpec((CHUNK, ML_W), lambda b, i: (fwd(b, i), 0)),
                   pl.BlockSpec((CHUNK, ML_W), lambda b, i: (bwd(b, i), 0))],
        out_shape=[jax.ShapeDtypeStruct((R_ALL, ML_W), F32)] * 2,
        scratch_shapes=[pltpu.VMEM((2 * ML_H, HD, 128), F32), pltpu.VMEM((2, 1, 128), F32)],
        compiler_params=_cparams(("parallel", "arbitrary")),
        name="mlstm",
    )(mqkv, mg, mg_t, mqkv, mg, mg_t, bias, bias_t, tril, triu)


def _out_body(att_ref, hl_ref, hc_ref, hf_ref, hb_ref, mo_ref, nw_ref, bd_ref, w_ref, x_ref, g_ref, o_ref):
    i = pl.program_id(0)
    hn = _head_rms(hf_ref[...] + hb_ref[...], bd_ref[...], nw_ref[...])
    mlo = hn * _sigmoid(mo_ref[...])
    hy = jnp.where(i < NT_LAT, hl_ref[...], hc_ref[...])
    y = _dot(att_ref[...], w_ref[0:ATT_W, :]) + _dot(hy, w_ref[ATT_W:ATT_W + HY, :]) + _dot(mlo, w_ref[ATT_W + HY:, :])
    o_ref[...] = x_ref[...] + g_ref[...] * y


def _out_proj(att, hyo, hyoc, hf, hb, mo, ml_nw, w_out_b, xs, mod4, layer, with_ctx):
    tiles = NT_ALL if with_ctx else NT_LAT
    rows = tiles * TM
    row = lambda w: pl.BlockSpec((TM, w), lambda i: (i, 0))
    bd = jnp.asarray(_blockdiag_ones(ML_W)).astype(BF16)
    return pl.pallas_call(
        _out_body,
        grid=(tiles,),
        in_specs=[row(ATT_W),
                  pl.BlockSpec((TM, HY), lambda i: (jnp.minimum(i, NT_LAT - 1), 0)),
                  pl.BlockSpec((TM, HY), lambda i: (jnp.maximum(i - NT_LAT, 0), 0)),
                  row(ML_W), row(ML_W), row(ML_W),
                  pl.BlockSpec((1, ML_W), lambda i: (0, 0)), pl.BlockSpec((ML_W, ML_W), lambda i: (0, 0)),
                  pl.BlockSpec((D, D), lambda i: (0, 0)), row(D), _mod_spec(layer, 2)],
        out_specs=row(D),
        out_shape=jax.ShapeDtypeStruct((rows, D), F32),
        compiler_params=_cparams(("parallel",)),
        name="out_proj",
    )(att, hyo, hyoc, hf, hb, mo, ml_nw.reshape(1, ML_W), bd, w_out_b, xs, mod4)


FF_CH = D_FF // 2


def _ffn_body(x_ref, sh_ref, sc_ref, g_ref, nw_ref, wg_ref, wu_ref, wd_ref, o_ref):
    x = x_ref[...]
    hb = _modulated_norm(x, nw_ref[...], sc_ref[...], sh_ref[...]).astype(BF16)
    acc = None
    for c in range(D_FF // FF_CH):
        sl = slice(c * FF_CH, (c + 1) * FF_CH)
        a = jnp.dot(hb, wg_ref[:, sl], preferred_element_type=F32)
        u = jnp.dot(hb, wu_ref[:, sl], preferred_element_type=F32)
        part = _dot(a * _sigmoid(a) * u, wd_ref[sl, :])
        acc = part if acc is None else acc + part
    o_ref[...] = x + g_ref[...] * acc


def _ffn(xs, mod4, layer, nw, wg, wu, wd, with_ctx):
    tiles = NT_ALL if with_ctx else NT_LAT
    row = pl.BlockSpec((TM, D), lambda i: (i, 0))
    resident = lambda shape: pl.BlockSpec(shape, lambda i: (0, 0), pipeline_mode=pl.Buffered(1))
    return pl.pallas_call(
        _ffn_body,
        grid=(tiles,),
        in_specs=[row, _mod_spec(layer, 3), _mod_spec(layer, 4), _mod_spec(layer, 5), pl.BlockSpec((1, D), lambda i: (0, 0)),
                  resident((D, D_FF)), resident((D, D_FF)), resident((D_FF, D))],
        out_specs=row,
        out_shape=jax.ShapeDtypeStruct((tiles * TM, D), F32),
        compiler_params=_cparams(("parallel",), 56),
        name="ffn",
    )(xs, mod4, mod4, mod4, nw.reshape(1, D), wg, wu, wd)


def _rope_tables():
    n_rows = SEQ // GRID_W
    row = jnp.repeat(jnp.arange(n_rows), GRID_W)
    col = jnp.tile(jnp.arange(GRID_W), n_rows)
    nf = HD // 4
    inv_freq = ROPE_BASE ** (-jnp.arange(nf, dtype=F32) / nf)
    ang = jnp.stack([row[:, None] * inv_freq, col[:, None] * inv_freq], axis=1)
    cos, sin = jnp.cos(ang), jnp.sin(ang)
    cos_h = jnp.concatenate([cos, cos], axis=-1).reshape(SEQ, HD)
    sin_h = jnp.concatenate([-sin, sin], axis=-1).reshape(SEQ, HD)
    cos_t = jnp.concatenate([jnp.tile(cos_h, (1, 2)), jnp.ones((TM, 128), F32)], axis=0)
    sin_t = jnp.concatenate([jnp.tile(sin_h, (1, 2)), jnp.zeros((TM, 128), F32)], axis=0)
    return cos_t, sin_t


def kernel(x, c, ctx, c_ctx, w_mod, b_mod, norm1_w, norm2_w, w_in, w_out, q_norm_w, k_norm_w, attn_sink, hy_conv_w, hy_conv_b, hy_w1, hy_b1, hy_freq, hy_w2, hy_b2, hy_w3, hy_decay, hy_skip, ml_gate_b, ml_norm_w, ffn_w_gate, ffn_w_up, ffn_w_down):
    xs = jnp.concatenate([x.reshape(T_LAT, D), ctx.reshape(T_CTX, D)], axis=0)
    cc8 = jnp.concatenate([c, c_ctx[None, :], jnp.zeros((8 - NB_BATCH - 1, D), F32)], axis=0)
    mod4 = _modulation(cc8, w_mod, b_mod).reshape(DEPTH, 8, 1, 6 * D)

    cos_t, sin_t = _rope_tables()
    feats_l, feats_c = _pos_feats_circ(SEQ), _pos_feats_circ(CTX)
    s1_tab = jnp.asarray(_TAB["s1"]).astype(BF16)
    s1i_tab = jnp.asarray(_TAB["s1i"]).astype(BF16)
    wf, wi = jnp.asarray(_TAB["wf"]).astype(BF16), jnp.asarray(_TAB["wi"]).astype(BF16)
    fc, fs = jnp.asarray(_TAB["fc"]).astype(BF16), jnp.asarray(_TAB["fs"]).astype(BF16)

    w_in_p = jnp.pad(w_in, ((0, 0), (0, 0), (0, P_PAD - P_IN))).astype(BF16)
    w_out_b = w_out.astype(BF16)
    wg_b, wu_b, wd_b = ffn_w_gate.astype(BF16), ffn_w_up.astype(BF16), ffn_w_down.astype(BF16)
    hpad = 128 - FILT_HID
    w1p = jnp.pad(hy_w1, ((0, 0), (0, 128 - hy_w1.shape[1]), (0, hpad)))
    b1p = jnp.pad(hy_b1, ((0, 0), (0, hpad))).reshape(DEPTH, 1, 128)
    frp = jnp.pad(hy_freq, ((0, 0), (0, hpad))).reshape(DEPTH, 1, 128)
    w2p = jnp.pad(hy_w2, ((0, 0), (0, hpad), (0, hpad)))
    b2p = jnp.pad(hy_b2, ((0, 0), (0, hpad))).reshape(DEPTH, 1, 128)
    w3p = jnp.pad(hy_w3, ((0, 0), (0, hpad), (0, 0)))
    sink8 = jnp.pad(attn_sink, ((0, 0), (0, 128 - ATT_H)))[:, None, :] * jnp.ones((1, 8, 1), F32)
    zeros_tile = jnp.zeros((TM, HY), F32)

    out = None
    for l in range(DEPTH):
        last = l == DEPTH - 1
        att_qkv, hy, mqkv, mo, mg = _in_proj(xs, mod4, l, norm1_w[l], w_in_p[l])

        qt, kn, vt = _qk_prep(att_qkv, cos_t, sin_t, q_norm_w[l], k_norm_w[l])
        att = _attention(qt, kn, vt, sink8[l], not last)

        dec = hy_decay[l].reshape(1, N_FILT)
        circ = _circular_filters(feats_l, w1p[l], b1p[l], frp[l], w2p[l], b2p[l], w3p[l], dec)
        f_re, f_im = _dft_stage1(circ.reshape(2, HALF_A, FFT_B, FFT_T, 2 * HY), s1_tab)
        h_re, h_im = _filter_spectrum(f_re.reshape(2, 2, N_FFT, HY), f_im.reshape(2, 2, N_FFT, HY), wf)

        v, x1, x2 = _short_conv_latent(hy, hy_conv_w[l], hy_conv_b[l])
        view = lambda t: t.reshape(NB_BATCH, HALF_A, FFT_B, FFT_T, HY)
        s_in, gate_a = view(v), view(x1)
        for order in range(2):
            a_re, a_im = _dft_stage1(s_in, s1_tab)
            c_re, c_im = _spectral_filter(a_re.reshape(NB_BATCH, N_FFT, HY), a_im.reshape(NB_BATCH, N_FFT, HY), h_re, h_im, order, wf, wi)
            tiles = lambda t: t.reshape(NB_BATCH, FFT_A, FFT_B, FFT_T, HY)
            s_in = _idft_stage1_gate(tiles(c_re), tiles(c_im), s1i_tab, gate_a, s_in, hy_skip[l, order])
            gate_a = view(x2)
        hyo = s_in.reshape(T_LAT, HY)

        if last:
            hyoc = zeros_tile
        else:
            circ_c = _circular_filters(feats_c, w1p[l], b1p[l], frp[l], w2p[l], b2p[l], w3p[l], dec)
            hyoc = _hyena_ctx(hy, hy_conv_w[l], hy_conv_b[l], circ_c, fc, fs, hy_skip[l])

        hf, hb = _mlstm(mqkv, mg, mg[:, :4 * ML_H].T, ml_gate_b[l])

        xs1 = _out_proj(att, hyo, hyoc, hf, hb, mo, ml_norm_w[l], w_out_b[l], xs, mod4, l, not last)
        xs = _ffn(xs1, mod4, l, norm2_w[l], wg_b[l], wu_b[l], wd_b[l], not last)
        out = xs
    return out.reshape(NB_BATCH, SEQ, D)
```

```python
import functools
import math

import numpy as np
import jax
import jax.numpy as jnp
from jax import lax
from jax.experimental import pallas as pl
from jax.experimental.pallas import tpu as pltpu

F32 = jnp.float32
BF16 = jnp.bfloat16

D = 1024
NB_BATCH = 4
SEQ = 4096
DEPTH = 4
GRID_W = 64
CTX = 256
T_LAT = NB_BATCH * SEQ
T_CTX = NB_BATCH * CTX
R_ALL = T_LAT + T_CTX

HD = 64
ATT_H = 6
ATT_W = ATT_H * HD
KV_W = 2 * HD
BLK = 128
NBLK = SEQ // BLK
ROPE_BASE = 10000.0

HY = 384
N_FILT = 4 * HY
POS_BANDS = 16
FILT_HID = 64

ML_H = 4
ML_W = 256
CHUNK = 128

D_FF = 2816
P_IN = 2832
P_PAD = 2944
NEG = -1e30
EPS = 1e-6

TM = 512
NT_LAT = T_LAT // TM
NT_ALL = R_ALL // TM
TILES_PER_SEQ = SEQ // TM

N_FFT = 2 * SEQ
FFT_A = 32
FFT_R = 256
FFT_T = 16
FFT_B = FFT_R // FFT_T
HALF_A = FFT_A // 2
S1_TILES = 4
HB_ROWS = 1024
N_CTXF = 2 * CTX


def _np_tables():
    ka = np.arange(FFT_A)[None, :, None, None]
    cc = np.arange(FFT_T)[None, None, :, None]
    aa = np.arange(HALF_A)[None, None, None, :]
    bb = np.arange(FFT_B)[:, None, None, None]
    ph = (ka * (FFT_R * aa + FFT_T * bb + cc)) % N_FFT
    th = 2.0 * np.pi * ph / N_FFT
    eye = np.eye(FFT_T)
    cos = np.einsum("bkca,cd->bkcad", np.cos(th), eye)
    sin = np.einsum("bkca,cd->bkcad", np.sin(th), eye)
    rows, cols = FFT_A * FFT_T, HALF_A * FFT_T
    s1 = np.concatenate([cos.reshape(FFT_B, rows, cols), -sin.reshape(FFT_B, rows, cols)], axis=1)
    s1i = np.concatenate([cos.reshape(FFT_B, rows, cols).transpose(0, 2, 1),
                          -sin.reshape(FFT_B, rows, cols).transpose(0, 2, 1)], axis=2) / N_FFT
    r = np.arange(FFT_R)
    th2 = 2.0 * np.pi * ((r[:, None] * r[None, :]) % FFT_R) / FFT_R
    c2, s2 = np.cos(th2), np.sin(th2)
    wf = np.block([[c2, s2], [-s2, c2]])
    wi = np.block([[c2, -s2], [s2, c2]])
    c = np.arange(N_CTXF)
    th3 = 2.0 * np.pi * ((c[:, None] * c[None, :]) % N_CTXF) / N_CTXF
    return dict(s1=s1.astype(np.float32), s1i=s1i.astype(np.float32), wf=wf.astype(np.float32), wi=wi.astype(np.float32),
                fc=np.cos(th3).astype(np.float32), fs=np.sin(th3).astype(np.float32))


_TAB = _np_tables()


def _blockdiag_ones(width):
    return np.kron(np.eye(width // HD), np.ones((HD, HD))).astype(np.float32)


def _cparams(sem, vmem_mb=48):
    return pltpu.CompilerParams(dimension_semantics=sem, vmem_limit_bytes=vmem_mb * 1024 * 1024)


def _dot(a, b):
    return jnp.dot(a.astype(BF16), b.astype(BF16), preferred_element_type=F32)


def _dot_nt(a, b):
    return lax.dot_general(a.astype(BF16), b.astype(BF16), (((1,), (1,)), ((), ())), preferred_element_type=F32)


def _dot_tn(a, b):
    return lax.dot_general(a.astype(BF16), b.astype(BF16), (((0,), (0,)), ((), ())), preferred_element_type=F32)


def _split3(x):
    x1 = x.astype(BF16)
    r1 = x - x1.astype(F32)
    x2 = r1.astype(BF16)
    x3 = (r1 - x2.astype(F32)).astype(BF16)
    return x1, x2, x3


def _dot_sel_l(sel, x):
    return sum(jnp.dot(sel, p, preferred_element_type=F32) for p in _split3(x))


def _dot_sel_r(x, sel):
    return sum(jnp.dot(p, sel, preferred_element_type=F32) for p in _split3(x))


def _sigmoid(x):
    return 1.0 / (1.0 + jnp.exp(-x))


def _log_sigmoid(x):
    return jnp.minimum(x, 0.0) - jnp.log(1.0 + jnp.exp(-jnp.abs(x)))


def _head_rms(t, bd, w):
    ss = _dot_sel_r(t * t, bd)
    return t * lax.rsqrt(ss * (1.0 / HD) + EPS) * w


def _mod_row(i):
    return jnp.where(i < NT_LAT, i // TILES_PER_SEQ, NB_BATCH)


def _mod_spec(layer, k):
    return pl.BlockSpec((None, None, 1, D), lambda i: (layer, _mod_row(i), 0, k))


def _mod_body(s_ref, w_ref, b_ref, o_ref):
    s = s_ref[...]
    s = s * _sigmoid(s)
    o_ref[...] = _dot(s, w_ref[...]) + b_ref[...]


def _modulation(cc8, w_mod, b_mod):
    nc = 1536
    return pl.pallas_call(
        _mod_body,
        grid=(DEPTH, 6 * D // nc),
        in_specs=[pl.BlockSpec((8, D), lambda l, j: (0, 0)),
                  pl.BlockSpec((None, D, nc), lambda l, j: (l, 0, j)),
                  pl.BlockSpec((None, 1, nc), lambda l, j: (l, 0, j))],
        out_specs=pl.BlockSpec((None, 8, nc), lambda l, j: (l, 0, j)),
        out_shape=jax.ShapeDtypeStruct((DEPTH, 8, 6 * D), F32),
        compiler_params=_cparams(("parallel", "parallel")),
        name="modulation",
    )(cc8, w_mod, b_mod.reshape(DEPTH, 1, 6 * D))


_IN_COLS = ((0, 640), (640, 1792), (1792, 2560), (2560, 2816), (2816, 2944))


def _modulated_norm(x, nw, sc, sh):
    ms = jnp.mean(x * x, axis=-1, keepdims=True)
    return (x * lax.rsqrt(ms + EPS) * nw) * (1.0 + sc) + sh


def _in_body(x_ref, sh_ref, sc_ref, nw_ref, w_ref, *o_refs):
    hb = _modulated_norm(x_ref[...], nw_ref[...], sc_ref[...], sh_ref[...]).astype(BF16)
    for o_ref, (a, b) in zip(o_refs, _IN_COLS):
        o_ref[...] = jnp.dot(hb, w_ref[:, a:b], preferred_element_type=F32)


def _in_proj(xs, mod4, layer, nw, w_in_p):
    widths = [b - a for a, b in _IN_COLS]
    return pl.pallas_call(
        _in_body,
        grid=(NT_ALL,),
        in_specs=[pl.BlockSpec((TM, D), lambda i: (i, 0)),
                  _mod_spec(layer, 0), _mod_spec(layer, 1),
                  pl.BlockSpec((1, D), lambda i: (0, 0)),
                  pl.BlockSpec((D, P_PAD), lambda i: (0, 0))],
        out_specs=[pl.BlockSpec((TM, w), lambda i: (i, 0)) for w in widths],
        out_shape=[jax.ShapeDtypeStruct((R_ALL, w), F32) for w in widths],
        compiler_params=_cparams(("parallel",), 56),
        name="in_proj",
    )(xs, mod4, mod4, nw.reshape(1, D), w_in_p)


def _rope(t, cos, sin_signed, first_half):
    outs = []
    for c in range(t.shape[1] // 128):
        tc = t[:, c * 128:(c + 1) * 128]
        partner = jnp.where(first_half, pltpu.roll(tc, 112, 1), pltpu.roll(tc, 16, 1))
        outs.append(tc * cos + partner * sin_signed)
    return outs[0] if len(outs) == 1 else jnp.concatenate(outs, axis=1)


def _qk_body(a_ref, cos_ref, sin_ref, qw_ref, kw_ref, bdq_ref, bdk_ref, qt_ref, k_ref, vt_ref):
    cos, sin = cos_ref[...], sin_ref[...]
    lane = lax.broadcasted_iota(jnp.int32, (TM, 128), 1)
    first_half = (lane & 31) < 16
    q = _head_rms(a_ref[:, 0:ATT_W], bdq_ref[...], qw_ref[...])
    k = _head_rms(a_ref[:, ATT_W:ATT_W + KV_W], bdk_ref[...], kw_ref[...])
    q = _rope(q, cos, sin, first_half) * (HD ** -0.5)
    qt_ref[...] = q.T.astype(BF16)
    k_ref[...] = _rope(k, cos, sin, first_half).astype(BF16)
    vt_ref[...] = a_ref[:, ATT_W + KV_W:ATT_W + 2 * KV_W].T.astype(BF16)


def _qk_prep(att_qkv, cos_t, sin_t, qw, kw):
    def tab(i):
        return (jnp.where(i < NT_LAT, i % TILES_PER_SEQ, TILES_PER_SEQ), 0)
    bdq = jnp.asarray(_blockdiag_ones(ATT_W)).astype(BF16)
    bdk = jnp.asarray(_blockdiag_ones(KV_W)).astype(BF16)
    return pl.pallas_call(
        _qk_body,
        grid=(NT_ALL,),
        in_specs=[pl.BlockSpec((TM, 640), lambda i: (i, 0)),
                  pl.BlockSpec((TM, 128), tab), pl.BlockSpec((TM, 128), tab),
                  pl.BlockSpec((1, ATT_W), lambda i: (0, 0)), pl.BlockSpec((1, KV_W), lambda i: (0, 0)),
                  pl.BlockSpec((ATT_W, ATT_W), lambda i: (0, 0)), pl.BlockSpec((KV_W, KV_W), lambda i: (0, 0))],
        out_specs=[pl.BlockSpec((ATT_W, TM), lambda i: (0, i)), pl.BlockSpec((TM, KV_W), lambda i: (i, 0)),
                   pl.BlockSpec((KV_W, TM), lambda i: (0, i))],
        out_shape=[jax.ShapeDtypeStruct((ATT_W, R_ALL), BF16), jax.ShapeDtypeStruct((R_ALL, KV_W), BF16),
                   jax.ShapeDtypeStruct((KV_W, R_ALL), BF16)],
        compiler_params=_cparams(("parallel",)),
        name="qk_prep",
    )(att_qkv, cos_t, sin_t, jnp.tile(qw, ATT_H).reshape(1, ATT_W), jnp.tile(kw, 2).reshape(1, KV_W), bdq, bdk)


def _att_heads(qt, kv_list, sink_ref, o_ref):
    nq = qt.shape[1]
    rep = ATT_H // 2
    zeros = jnp.zeros((HD, rep * nq), BF16)
    outs = []
    for g in range(2):
        heads = range(rep * g, rep * (g + 1))
        qg = jnp.concatenate([qt[HD * h:HD * (h + 1), :] for h in heads], axis=1)
        qe = jnp.concatenate([qg, zeros] if g == 0 else [zeros, qg], axis=0)
        sink = jnp.concatenate([jnp.broadcast_to(sink_ref[0:1, h:h + 1], (1, nq)) for h in heads], axis=1)
        scores = []
        m = sink
        for k, _, mask in kv_list:
            s = jnp.dot(k, qe, preferred_element_type=F32)
            if mask is not None:
                s = jnp.where(mask, s, NEG)
            scores.append(s)
            m = jnp.maximum(m, jnp.max(s, axis=0, keepdims=True))
        l = jnp.exp(sink - m)
        acc = None
        for s, (_, vt, _) in zip(scores, kv_list):
            p = jnp.exp(s - m)
            l = l + jnp.sum(p, axis=0, keepdims=True)
            pv = jnp.dot(vt[HD * g:HD * (g + 1), :], p.astype(BF16), preferred_element_type=F32)
            acc = pv if acc is None else acc + pv
        og = acc / l
        outs += [og[:, nq * i:nq * (i + 1)] for i in range(rep)]
    o_ref[...] = jnp.concatenate(outs, axis=0).T


def _att_body(qt_ref, kp_ref, ko_ref, kn_ref, vp_ref, vo_ref, vn_ref, kc_ref, vc_ref, sink_ref, o_ref):
    j = pl.program_id(1)
    qt = qt_ref[...]
    kc, vc = kc_ref[...], vc_ref[...]

    @pl.when(j < NBLK)
    def _():
        width = (ATT_H // 2) * BLK
        key = lax.broadcasted_iota(jnp.int32, (BLK, width), 0)
        qry = lax.broadcasted_iota(jnp.int32, (BLK, width), 1) & (BLK - 1)
        mask_prev = jnp.logical_and(key >= qry, j > 0)
        mask_next = jnp.logical_and(key <= qry, j < NBLK - 1)
        _att_heads(qt, [(kp_ref[...], vp_ref[...], mask_prev), (ko_ref[...], vo_ref[...], None),
                        (kn_ref[...], vn_ref[...], mask_next), (kc, vc, None)], sink_ref, o_ref)

    @pl.when(j >= NBLK)
    def _():
        _att_heads(qt, [(kc, vc, None)], sink_ref, o_ref)


def _attention(qt, kn, vt, sink8, with_ctx):
    steps = NBLK + (CTX // BLK if with_ctx else 0)
    rows = R_ALL if with_ctx else T_LAT

    def qblk(b, j):
        return jnp.where(j < NBLK, b * NBLK + j, T_LAT // BLK + (CTX // BLK) * b + (j - NBLK))

    def band(off):
        return lambda b, j: b * NBLK + jnp.clip(j + off, 0, NBLK - 1)

    kspec = lambda f: pl.BlockSpec((BLK, KV_W), lambda b, j: (f(b, j), 0))
    vspec = lambda f: pl.BlockSpec((KV_W, BLK), lambda b, j: (0, f(b, j)))
    return pl.pallas_call(
        _att_body,
        grid=(NB_BATCH, steps),
        in_specs=[pl.BlockSpec((ATT_W, BLK), lambda b, j: (0, qblk(b, j))),
                  kspec(band(-1)), kspec(band(0)), kspec(band(1)),
                  vspec(band(-1)), vspec(band(0)), vspec(band(1)),
                  pl.BlockSpec((CTX, KV_W), lambda b, j: (T_LAT // CTX + b, 0)),
                  pl.BlockSpec((KV_W, CTX), lambda b, j: (0, T_LAT // CTX + b)),
                  pl.BlockSpec((8, 128), lambda b, j: (0, 0))],
        out_specs=pl.BlockSpec((BLK, ATT_W), lambda b, j: (qblk(b, j), 0)),
        out_shape=jax.ShapeDtypeStruct((rows, ATT_W), F32),
        compiler_params=_cparams(("parallel", "parallel")),
        name="attention",
    )(qt, kn, kn, kn, vt, vt, vt, kn, vt, sink8)


def _short_conv(u, w, b):
    n = u.shape[0]
    row = lax.broadcasted_iota(jnp.int32, u.shape, 0)
    up = jnp.where(row == 0, 0.0, pltpu.roll(u, 1, 0))
    un = jnp.where(row == n - 1, 0.0, pltpu.roll(u, n - 1, 0))
    return up * w[0:1, :] + u * w[1:2, :] + un * w[2:3, :] + b


def _sc_body(u0_ref, u1_ref, u2_ref, w0_ref, w1_ref, w2_ref, b0_ref, b1_ref, b2_ref, o0_ref, o1_ref, o2_ref):
    for u_ref, w_ref, b_ref, o_ref in ((u0_ref, w0_ref, b0_ref, o0_ref), (u1_ref, w1_ref, b1_ref, o1_ref), (u2_ref, w2_ref, b2_ref, o2_ref)):
        o_ref[...] = _short_conv(u_ref[...], w_ref[...], b_ref[...])


def _short_conv_latent(hy, conv_w, conv_b):
    cb = conv_b.reshape(1, 3 * HY)
    ins, ws, bs = [], [], []
    for part in range(3):
        ins.append(pl.BlockSpec((SEQ, 128), lambda b, j, part=part: (b, 3 * part + j)))
        ws.append(pl.BlockSpec((3, 128), lambda b, j, part=part: (0, 3 * part + j)))
        bs.append(pl.BlockSpec((1, 128), lambda b, j, part=part: (0, 3 * part + j)))
    return pl.pallas_call(
        _sc_body,
        grid=(NB_BATCH, 3),
        in_specs=ins + ws + bs,
        out_specs=[pl.BlockSpec((SEQ, 128), lambda b, j: (b, j))] * 3,
        out_shape=[jax.ShapeDtypeStruct((T_LAT, HY), F32)] * 3,
        compiler_params=_cparams(("parallel", "parallel")),
        name="short_conv",
    )(hy, hy, hy, conv_w, conv_w, conv_w, cb, cb, cb)


def _filt_body(n, tl, f_ref, w1_ref, b1_ref, fr_ref, w2_ref, b2_ref, w3_ref, dec_ref, o_ref):
    f = f_ref[...]
    fr = fr_ref[...]
    z = jnp.sin(fr * (_dot(f, w1_ref[...]) + b1_ref[...]))
    z = jnp.sin(fr * (_dot(z, w2_ref[...]) + b2_ref[...]))
    filt = _dot(z, w3_ref[...]) * jnp.exp(-f[:, 0:1] * jnp.abs(dec_ref[...]))
    row = pl.program_id(0) * tl + lax.broadcasted_iota(jnp.int32, filt.shape, 0)
    o_ref[...] = jnp.where(row == n, 0.0, filt)


def _circular_filters(feats_circ, w1p, b1p, frp, w2p, b2p, w3p, dec):
    n = feats_circ.shape[0] // 2
    tl = min(n, 512)
    const = lambda shape: pl.BlockSpec(shape, lambda i: (0, 0))
    half = lambda rows: pl.BlockSpec((rows, 2 * HY), lambda i: (0, i // (n // tl)))
    return pl.pallas_call(
        functools.partial(_filt_body, n, tl),
        grid=(2 * n // tl,),
        in_specs=[pl.BlockSpec((tl, 128), lambda i: (i, 0)), const((128, 128)), const((1, 128)), const((1, 128)),
                  const((128, 128)), const((1, 128)), half(128), half(1)],
        out_specs=pl.BlockSpec((tl, 2 * HY), lambda i: (i, 0)),
        out_shape=jax.ShapeDtypeStruct((2 * n, 2 * HY), F32),
        compiler_params=_cparams(("parallel",)),
        name="hyena_filters",
    )(feats_circ, w1p, b1p, frp, w2p, b2p, w3p, dec)


def _pos_feats_circ(n):
    t = jnp.linspace(0.0, 1.0, n, dtype=F32)[:, None]
    ang = (2.0 * math.pi / n) * jnp.arange(n, dtype=F32)[:, None]
    bands = jnp.linspace(1e-4, POS_BANDS - 1, POS_BANDS, dtype=F32)[None, :]
    feats = jnp.concatenate([t, jnp.cos(bands * ang), -jnp.sin(bands * ang)], axis=-1)
    feats = jnp.pad(feats, ((0, 0), (0, 128 - feats.shape[1])))
    return jnp.concatenate([feats, feats[:1], jnp.flip(feats[:n - 1], axis=0)], axis=0)


def _s1_body(x_ref, m_ref, are_ref, aim_ref):
    half = FFT_A * FFT_T
    for p in range(S1_TILES):
        xs = x_ref[:, p].reshape(HALF_A * FFT_T, HY)
        r = _dot(m_ref[p], xs)
        are_ref[:, p] = r[:half].reshape(FFT_A, FFT_T, HY).astype(BF16)
        aim_ref[:, p] = r[half:].reshape(FFT_A, FFT_T, HY).astype(BF16)


def _dft_stage1(x5, s1_tab):
    nb, ncb = x5.shape[0], x5.shape[-1] // HY
    out = pl.BlockSpec((None, None, FFT_A, S1_TILES, FFT_T, HY), lambda b, cb, j: (b, cb, 0, j, 0, 0))
    return pl.pallas_call(
        _s1_body,
        grid=(nb, ncb, FFT_B // S1_TILES),
        in_specs=[pl.BlockSpec((None, HALF_A, S1_TILES, FFT_T, HY), lambda b, cb, j: (b, 0, j, 0, cb)),
                  pl.BlockSpec((S1_TILES, 2 * FFT_A * FFT_T, HALF_A * FFT_T), lambda b, cb, j: (j, 0, 0))],
        out_specs=[out, out],
        out_shape=[jax.ShapeDtypeStruct((nb, ncb, FFT_A, FFT_B, FFT_T, HY), BF16)] * 2,
        compiler_params=_cparams(("parallel", "parallel", "parallel")),
        name="dft_stage1",
    )(x5, s1_tab)


def _s1_pair_body(x_ref, m_ref, are_ref, aim_ref):
    half = FFT_A * FFT_T
    for p in range(S1_TILES):
        ra = _dot(m_ref[p], x_ref[0, :, p].reshape(HALF_A * FFT_T, HY))
        rb = _dot(m_ref[p], x_ref[1, :, p].reshape(HALF_A * FFT_T, HY))
        are_ref[:, p] = (ra[:half] - rb[half:]).reshape(FFT_A, FFT_T, HY).astype(BF16)
        aim_ref[:, p] = (ra[half:] + rb[:half]).reshape(FFT_A, FFT_T, HY).astype(BF16)


def _dft_stage1_pair(x5, s1_tab):
    npair = x5.shape[0] // 2
    out = pl.BlockSpec((None, FFT_A, S1_TILES, FFT_T, HY), lambda b, j: (b, 0, j, 0, 0))
    return pl.pallas_call(
        _s1_pair_body,
        grid=(npair, FFT_B // S1_TILES),
        in_specs=[pl.BlockSpec((2, HALF_A, S1_TILES, FFT_T, HY), lambda b, j: (b, 0, j, 0, 0)),
                  pl.BlockSpec((S1_TILES, 2 * FFT_A * FFT_T, HALF_A * FFT_T), lambda b, j: (j, 0, 0))],
        out_specs=[out, out],
        out_shape=[jax.ShapeDtypeStruct((npair, FFT_A, FFT_B, FFT_T, HY), BF16)] * 2,
        compiler_params=_cparams(("parallel", "parallel")),
        name="dft_stage1_pair",
    )(x5, s1_tab)


def _block_rows(r):
    return slice(r * FFT_R, (r + 1) * FFT_R)


def _stack_complex(re_ref, im_ref, sl):
    return jnp.concatenate([re_ref[sl, :], im_ref[sl, :]], axis=0)


def _spec_body(are_ref, aim_ref, wf_ref, hre_ref, him_ref):
    wf = wf_ref[...]
    for r in range(HB_ROWS // FFT_R):
        sl = _block_rows(r)
        first = jnp.dot(wf, _stack_complex(are_ref.at[0], aim_ref.at[0], sl), preferred_element_type=F32)
        second = jnp.dot(wf, _stack_complex(are_ref.at[1], aim_ref.at[1], sl), preferred_element_type=F32)
        x = first + second if r % 2 == 0 else first - second
        hre_ref[sl, :] = x[:FFT_R]
        him_ref[sl, :] = x[FFT_R:]


def _filter_spectrum(a_re, a_im, wf):
    assert (HB_ROWS // FFT_R) % 2 == 0
    blk_in = pl.BlockSpec((2, None, HB_ROWS, HY), lambda r, o: (0, o, r, 0))
    blk_out = pl.BlockSpec((None, HB_ROWS, HY), lambda r, o: (o, r, 0))
    return pl.pallas_call(
        _spec_body,
        grid=(N_FFT // HB_ROWS, 2),
        in_specs=[blk_in, blk_in, pl.BlockSpec((2 * FFT_R, 2 * FFT_R), lambda r, o: (0, 0))],
        out_specs=[blk_out, blk_out],
        out_shape=[jax.ShapeDtypeStruct((2, N_FFT, HY), F32)] * 2,
        compiler_params=_cparams(("parallel", "parallel")),
        name="filter_spectrum",
    )(a_re, a_im, wf)


def _s2_body(are_ref, aim_ref, hre_ref, him_ref, wf_ref, wi_ref, cre_ref, cim_ref):
    wf, wi = wf_ref[...], wi_ref[...]
    for r in range(HB_ROWS // FFT_R):
        sl = _block_rows(r)
        x = jnp.dot(wf, _stack_complex(are_ref, aim_ref, sl), preferred_element_type=F32)
        xre, xim = x[:FFT_R], x[FFT_R:]
        hre, him = hre_ref[sl, :], him_ref[sl, :]
        y = jnp.concatenate([(xre * hre - xim * him).astype(BF16), (xre * him + xim * hre).astype(BF16)], axis=0)
        c = jnp.dot(wi, y, preferred_element_type=F32)
        cre_ref[sl, :] = c[:FFT_R].astype(BF16)
        cim_ref[sl, :] = c[FFT_R:].astype(BF16)


def _spectral_filter(a_re, a_im, h_re, h_im, order, wf, wi):
    nb = a_re.shape[0]
    blk = pl.BlockSpec((None, HB_ROWS, HY), lambda r, b: (b, r, 0))
    hblk = pl.BlockSpec((None, HB_ROWS, HY), lambda r, b: (order, r, 0))
    mat = pl.BlockSpec((2 * FFT_R, 2 * FFT_R), lambda r, b: (0, 0))
    return pl.pallas_call(
        _s2_body,
        grid=(N_FFT // HB_ROWS, nb),
        in_specs=[blk, blk, hblk, hblk, mat, mat],
        out_specs=[blk, blk],
        out_shape=[jax.ShapeDtypeStruct((nb, N_FFT, HY), BF16)] * 2,
        compiler_params=_cparams(("parallel", "parallel")),
        name="spectral_filter",
    )(a_re, a_im, h_re, h_im, wf, wi)


def _s1i_body(cre_ref, cim_ref, g_ref, a_ref, b_ref, sk_ref, o_ref):
    rows = FFT_A * FFT_T
    for p in range(S1_TILES):
        cre, cim = cre_ref[:, p].reshape(rows, HY), cim_ref[:, p].reshape(rows, HY)
        parts = (jnp.concatenate([cre, cim], axis=0), jnp.concatenate([cim, -cre], axis=0))
        for i, c in enumerate(parts):
            y = jnp.dot(g_ref[p], c, preferred_element_type=F32).reshape(HALF_A, FFT_T, HY)
            o_ref[i, :, p] = a_ref[i, :, p] * (y + b_ref[i, :, p] * sk_ref[...])


def _idft_stage1_gate(c_re, c_im, s1i_tab, a5, b5, skip):
    data = pl.BlockSpec((2, HALF_A, S1_TILES, FFT_T, HY), lambda b, j: (b, 0, j, 0, 0))
    spec = pl.BlockSpec((None, FFT_A, S1_TILES, FFT_T, HY), lambda b, j: (b, 0, j, 0, 0))
    return pl.pallas_call(
        _s1i_body,
        grid=(NB_BATCH // 2, FFT_B // S1_TILES),
        in_specs=[spec, spec, pl.BlockSpec((S1_TILES, HALF_A * FFT_T, 2 * FFT_A * FFT_T), lambda b, j: (j, 0, 0)),
                  data, data, pl.BlockSpec((1, 1, HY), lambda b, j: (0, 0, 0))],
        out_specs=data,
        out_shape=jax.ShapeDtypeStruct((NB_BATCH, HALF_A, FFT_B, FFT_T, HY), F32),
        compiler_params=_cparams(("parallel", "parallel")),
        name="idft_stage1_gate",
    )(c_re, c_im, s1i_tab, a5, b5, skip.reshape(1, 1, HY))


def _hyc_body(u_ref, w_ref, b_ref, circ_ref, fc_ref, fs_ref, sk_ref, o_ref):
    usc = _short_conv(u_ref[...], w_ref[...], b_ref[...])
    fc, fs = fc_ref[...], fs_ref[...]
    circ = circ_ref[...]
    h_re, h_im = _dot(fc, circ), -_dot(fs, circ)

    def long_conv(s, o):
        sl = slice(o * HY, (o + 1) * HY)
        s_re, s_im = _dot(fc[:, :CTX], s), -_dot(fs[:, :CTX], s)
        hre, him = h_re[:, sl], h_im[:, sl]
        y_re = s_re * hre - s_im * him
        y_im = s_re * him + s_im * hre
        y = (_dot(fc[:CTX, :], y_re) - _dot(fs[:CTX, :], y_im)) * (1.0 / N_CTXF)
        return y + s * sk_ref[o:o + 1, :]

    v, x1, x2 = usc[:, 0:HY], usc[:, HY:2 * HY], usc[:, 2 * HY:3 * HY]
    o_ref[...] = x2 * long_conv(x1 * long_conv(v, 0), 1)


def _hyena_ctx(hy, conv_w, conv_b, circ_c, fc, fs, skip):
    const = lambda shape: pl.BlockSpec(shape, lambda b: (0, 0))
    return pl.pallas_call(
        _hyc_body,
        grid=(NB_BATCH,),
        in_specs=[pl.BlockSpec((CTX, 3 * HY), lambda b: (T_LAT // CTX + b, 0)), const((3, 3 * HY)), const((1, 3 * HY)),
                  const((N_CTXF, 2 * HY)), const((N_CTXF, N_CTXF)), const((N_CTXF, N_CTXF)), const((2, HY))],
        out_specs=pl.BlockSpec((CTX, HY), lambda b: (b, 0)),
        out_shape=jax.ShapeDtypeStruct((T_CTX, HY), F32),
        compiler_params=_cparams(("parallel",)),
        name="hyena_ctx",
    )(hy, conv_w, conv_b.reshape(1, 3 * HY), circ_c, fc, fs, skip)


N_ML_STEPS = CTX // CHUNK + SEQ // CHUNK


def _ml_chain(direction, qkv, gates, gates_t, cum_c, cum_r, c_scr, m_scr, mask, ones_col):
    base = 8 * direction
    lane = lax.broadcasted_iota(jnp.int32, (CHUNK, 128), 1)
    m_prev = m_scr[direction]
    b_end = cum_c[CHUNK - 1:CHUNK, :] if direction == 0 else cum_c[0:1, :]
    heads = []
    m_intra_all = jnp.zeros((CHUNK, 128), F32)
    for h in range(ML_H):
        ic, fc = base + h, base + 4 + h
        b_c, b_r, li_r = cum_c[:, fc:fc + 1], cum_r[fc:fc + 1, :], gates_t[ic:ic + 1, :]
        q = qkv[:, HD * h:HD * (h + 1)]
        k = qkv[:, ML_W + HD * h:ML_W + HD * (h + 1)] * (HD ** -0.5)
        vext = jnp.concatenate([qkv[:, 2 * ML_W + HD * h:2 * ML_W + HD * (h + 1)], ones_col], axis=1)
        c_prev = c_scr[direction * ML_H + h]
        dmat = jnp.where(mask, b_c - b_r + li_r, NEG)
        m_intra = jnp.max(dmat, axis=-1, keepdims=True)
        s = _dot_nt(q, k) * jnp.exp(dmat - m_intra)
        heads.append((k, vext, c_prev, _dot(s, vext), _dot(q, c_prev)))
        m_intra_all = jnp.where(lane == fc, m_intra, m_intra_all)
    inter = cum_c + m_prev
    m_t = jnp.maximum(inter, m_intra_all)
    a_intra, w_inter, e_neg = jnp.exp(m_intra_all - m_t), jnp.exp(inter - m_t), jnp.exp(-m_t)
    g_c = b_end - cum_c + pltpu.roll(gates, 4, 1)
    m_new = jnp.maximum(b_end + m_prev, jnp.max(g_c, axis=0, keepdims=True))
    w_state, decay = jnp.exp(g_c - m_new), jnp.exp(b_end + m_prev - m_new)
    m_scr[direction] = m_new
    outs = []
    for h, (k, vext, c_prev, r_intra, r_inter) in enumerate(heads):
        fc = base + 4 + h
        hx = a_intra[:, fc:fc + 1] * r_intra + w_inter[:, fc:fc + 1] * r_inter
        den = jnp.maximum(jnp.abs(hx[:, HD:HD + 1]), e_neg[:, fc:fc + 1])
        outs.append(hx[:, :HD] / den)
        c_scr[direction * ML_H + h] = decay[:, fc:fc + 1] * c_prev + _dot_tn(k, w_state[:, fc:fc + 1] * vext)
    return jnp.concatenate(outs, axis=1)


def _ml_body(qf_ref, gf_ref, gtf_ref, qb_ref, gb_ref, gtb_ref, bias_ref, bias_t_ref, tril_ref, triu_ref,
             hf_ref, hb_ref, c_scr, m_scr):
    @pl.when(pl.program_id(1) == 0)
    def _():
        c_scr[...] = jnp.zeros_like(c_scr)
        m_scr[...] = jnp.zeros_like(m_scr)

    tril, triu = tril_ref[...], triu_ref[...]
    row = lax.broadcasted_iota(jnp.int32, (CHUNK, CHUNK), 0)
    col = lax.broadcasted_iota(jnp.int32, (CHUNK, CHUNK), 1)
    ones_col = (lax.broadcasted_iota(jnp.int32, (CHUNK, HD), 1) == 0).astype(F32)
    for direction, (q_ref, g_ref, gt_ref, o_ref) in enumerate(((qf_ref, gf_ref, gtf_ref, hf_ref), (qb_ref, gb_ref, gtb_ref, hb_ref))):
        gates = g_ref[...] + bias_ref[...]
        gates_t = gt_ref[...] + bias_t_ref[...]
        ls, ls_t = _log_sigmoid(gates), _log_sigmoid(gates_t)
        if direction == 0:
            cum_c, cum_r, mask = _dot_sel_l(tril, ls), _dot_sel_r(ls_t, triu), col <= row
        else:
            cum_c, cum_r, mask = _dot_sel_l(triu, ls), _dot_sel_r(ls_t, tril), col >= row
        o_ref[...] = _ml_chain(direction, q_ref[...], gates, gates_t, cum_c, cum_r, c_scr, m_scr, mask, ones_col)


def _mlstm(mqkv, mg, mg_t, gate_b):
    nctx = CTX // CHUNK
    nlat = SEQ // CHUNK

    def fwd(b, i):
        return jnp.where(i < nctx, T_LAT // CHUNK + nctx * b + i, nlat * b + (i - nctx))

    def bwd(b, i):
        return jnp.where(i < nctx, T_LAT // CHUNK + nctx * b + (nctx - 1 - i), nlat * b + (N_ML_STEPS - 1 - i))

    bias = jnp.pad(gate_b, (0, 128 - 4 * ML_H)).reshape(1, 128)
    bias_t = gate_b.reshape(4 * ML_H, 1)
    tril = jnp.asarray(np.tril(np.ones((CHUNK, CHUNK), np.float32))).astype(BF16)
    triu = jnp.asarray(np.triu(np.ones((CHUNK, CHUNK), np.float32))).astype(BF16)
    const = lambda shape: pl.BlockSpec(shape, lambda b, i: (0, 0))
    ins = []
    for f in (fwd, bwd):
        ins += [pl.BlockSpec((CHUNK, 3 * ML_W), lambda b, i, f=f: (f(b, i), 0)),
                pl.BlockSpec((CHUNK, 128), lambda b, i, f=f: (f(b, i), 0)),
                pl.BlockSpec((4 * ML_H, CHUNK), lambda b, i, f=f: (0, f(b, i)))]
    return pl.pallas_call(
        _ml_body,
        grid=(NB_BATCH, N_ML_STEPS),
        in_specs=ins + [const((1, 128)), const((4 * ML_H, 1)), const((CHUNK, CHUNK)), const((CHUNK, CHUNK))],
        out_specs=[pl.BlockSpec((CHUNK, ML_W), lambda b, i: (fwd(b, i), 0)),
                   pl.BlockSpec((CHUNK, ML_W), lambda b, i: (bwd(b, i), 0))],
        out_shape=[jax.ShapeDtypeStruct((R_ALL, ML_W), F32)] * 2,
        scratch_shapes=[pltpu.VMEM((2 * ML_H, HD, 128), F32), pltpu.VMEM((2, 1, 128), F32)],
        compiler_params=_cparams(("parallel", "arbitrary")),
        name="mlstm",
    )(mqkv, mg, mg_t, mqkv, mg, mg_t, bias, bias_t, tril, triu)


def _out_body(att_ref, hl_ref, hc_ref, hf_ref, hb_ref, mo_ref, nw_ref, bd_ref, w_ref, x_ref, g_ref, o_ref):
    i = pl.program_id(0)
    hn = _head_rms(hf_ref[...] + hb_ref[...], bd_ref[...], nw_ref[...])
    mlo = hn * _sigmoid(mo_ref[...])
    hy = jnp.where(i < NT_LAT, hl_ref[...], hc_ref[...])
    y = _dot(att_ref[...], w_ref[0:ATT_W, :]) + _dot(hy, w_ref[ATT_W:ATT_W + HY, :]) + _dot(mlo, w_ref[ATT_W + HY:, :])
    o_ref[...] = x_ref[...] + g_ref[...] * y


def _out_proj(att, hyo, hyoc, hf, hb, mo, ml_nw, w_out_b, xs, mod4, layer, with_ctx):
    tiles = NT_ALL if with_ctx else NT_LAT
    rows = tiles * TM
    row = lambda w: pl.BlockSpec((TM, w), lambda i: (i, 0))
    bd = jnp.asarray(_blockdiag_ones(ML_W)).astype(BF16)
    return pl.pallas_call(
        _out_body,
        grid=(tiles,),
        in_specs=[row(ATT_W),
                  pl.BlockSpec((TM, HY), lambda i: (jnp.minimum(i, NT_LAT - 1), 0)),
                  pl.BlockSpec((TM, HY), lambda i: (jnp.maximum(i - NT_LAT, 0), 0)),
                  row(ML_W), row(ML_W), row(ML_W),
                  pl.BlockSpec((1, ML_W), lambda i: (0, 0)), pl.BlockSpec((ML_W, ML_W), lambda i: (0, 0)),
                  pl.BlockSpec((D, D), lambda i: (0, 0)), row(D), _mod_spec(layer, 2)],
        out_specs=row(D),
        out_shape=jax.ShapeDtypeStruct((rows, D), F32),
        compiler_params=_cparams(("parallel",)),
        name="out_proj",
    )(att, hyo, hyoc, hf, hb, mo, ml_nw.reshape(1, ML_W), bd, w_out_b, xs, mod4)


FF_CH = D_FF // 2


def _ffn_body(x_ref, sh_ref, sc_ref, g_ref, nw_ref, wg_ref, wu_ref, wd_ref, o_ref):
    x = x_ref[...]
    hb = _modulated_norm(x, nw_ref[...], sc_ref[...], sh_ref[...]).astype(BF16)
    acc = None
    for c in range(D_FF // FF_CH):
        sl = slice(c * FF_CH, (c + 1) * FF_CH)
        a = jnp.dot(hb, wg_ref[:, sl], preferred_element_type=F32)
        u = jnp.dot(hb, wu_ref[:, sl], preferred_element_type=F32)
        part = _dot(a * _sigmoid(a) * u, wd_ref[sl, :])
        acc = part if acc is None else acc + part
    o_ref[...] = x + g_ref[...] * acc


def _ffn(xs, mod4, layer, nw, wg, wu, wd, with_ctx):
    tiles = NT_ALL if with_ctx else NT_LAT
    row = pl.BlockSpec((TM, D), lambda i: (i, 0))
    resident = lambda shape: pl.BlockSpec(shape, lambda i: (0, 0), pipeline_mode=pl.Buffered(1))
    return pl.pallas_call(
        _ffn_body,
        grid=(tiles,),
        in_specs=[row, _mod_spec(layer, 3), _mod_spec(layer, 4), _mod_spec(layer, 5), pl.BlockSpec((1, D), lambda i: (0, 0)),
                  resident((D, D_FF)), resident((D, D_FF)), resident((D_FF, D))],
        out_specs=row,
        out_shape=jax.ShapeDtypeStruct((tiles * TM, D), F32),
        compiler_params=_cparams(("parallel",), 56),
        name="ffn",
    )(xs, mod4, mod4, mod4, nw.reshape(1, D), wg, wu, wd)


def _rope_tables():
    n_rows = SEQ // GRID_W
    row = jnp.repeat(jnp.arange(n_rows), GRID_W)
    col = jnp.tile(jnp.arange(GRID_W), n_rows)
    nf = HD // 4
    inv_freq = ROPE_BASE ** (-jnp.arange(nf, dtype=F32) / nf)
    ang = jnp.stack([row[:, None] * inv_freq, col[:, None] * inv_freq], axis=1)
    cos, sin = jnp.cos(ang), jnp.sin(ang)
    cos_h = jnp.concatenate([cos, cos], axis=-1).reshape(SEQ, HD)
    sin_h = jnp.concatenate([-sin, sin], axis=-1).reshape(SEQ, HD)
    cos_t = jnp.concatenate([jnp.tile(cos_h, (1, 2)), jnp.ones((TM, 128), F32)], axis=0)
    sin_t = jnp.concatenate([jnp.tile(sin_h, (1, 2)), jnp.zeros((TM, 128), F32)], axis=0)
    return cos_t, sin_t


def kernel(x, c, ctx, c_ctx, w_mod, b_mod, norm1_w, norm2_w, w_in, w_out, q_norm_w, k_norm_w, attn_sink, hy_conv_w, hy_conv_b, hy_w1, hy_b1, hy_freq, hy_w2, hy_b2, hy_w3, hy_decay, hy_skip, ml_gate_b, ml_norm_w, ffn_w_gate, ffn_w_up, ffn_w_down):
    xs = jnp.concatenate([x.reshape(T_LAT, D), ctx.reshape(T_CTX, D)], axis=0)
    cc8 = jnp.concatenate([c, c_ctx[None, :], jnp.zeros((8 - NB_BATCH - 1, D), F32)], axis=0)
    mod4 = _modulation(cc8, w_mod, b_mod).reshape(DEPTH, 8, 1, 6 * D)

    cos_t, sin_t = _rope_tables()
    feats_l, feats_c = _pos_feats_circ(SEQ), _pos_feats_circ(CTX)
    s1_tab = jnp.asarray(_TAB["s1"]).astype(BF16)
    s1i_tab = jnp.asarray(_TAB["s1i"]).astype(BF16)
    wf, wi = jnp.asarray(_TAB["wf"]).astype(BF16), jnp.asarray(_TAB["wi"]).astype(BF16)
    fc, fs = jnp.asarray(_TAB["fc"]).astype(BF16), jnp.asarray(_TAB["fs"]).astype(BF16)

    w_in_p = jnp.pad(w_in, ((0, 0), (0, 0), (0, P_PAD - P_IN))).astype(BF16)
    w_out_b = w_out.astype(BF16)
    wg_b, wu_b, wd_b = ffn_w_gate.astype(BF16), ffn_w_up.astype(BF16), ffn_w_down.astype(BF16)
    hpad = 128 - FILT_HID
    w1p = jnp.pad(hy_w1, ((0, 0), (0, 128 - hy_w1.shape[1]), (0, hpad)))
    b1p = jnp.pad(hy_b1, ((0, 0), (0, hpad))).reshape(DEPTH, 1, 128)
    frp = jnp.pad(hy_freq, ((0, 0), (0, hpad))).reshape(DEPTH, 1, 128)
    w2p = jnp.pad(hy_w2, ((0, 0), (0, hpad), (0, hpad)))
    b2p = jnp.pad(hy_b2, ((0, 0), (0, hpad))).reshape(DEPTH, 1, 128)
    w3p = jnp.pad(hy_w3, ((0, 0), (0, hpad), (0, 0)))
    sink8 = jnp.pad(attn_sink, ((0, 0), (0, 128 - ATT_H)))[:, None, :] * jnp.ones((1, 8, 1), F32)
    zeros_tile = jnp.zeros((TM, HY), F32)

    out = None
    for l in range(DEPTH):
        last = l == DEPTH - 1
        att_qkv, hy, mqkv, mo, mg = _in_proj(xs, mod4, l, norm1_w[l], w_in_p[l])

        qt, kn, vt = _qk_prep(att_qkv, cos_t, sin_t, q_norm_w[l], k_norm_w[l])
        att = _attention(qt, kn, vt, sink8[l], not last)

        dec = hy_decay[l].reshape(1, N_FILT)
        circ = _circular_filters(feats_l, w1p[l], b1p[l], frp[l], w2p[l], b2p[l], w3p[l], dec)
        f_re, f_im = _dft_stage1(circ.reshape(2, HALF_A, FFT_B, FFT_T, 2 * HY), s1_tab)
        h_re, h_im = _filter_spectrum(f_re.reshape(2, 2, N_FFT, HY), f_im.reshape(2, 2, N_FFT, HY), wf)

        v, x1, x2 = _short_conv_latent(hy, hy_conv_w[l], hy_conv_b[l])
        view = lambda t: t.reshape(NB_BATCH, HALF_A, FFT_B, FFT_T, HY)
        s_in, gate_a = view(v), view(x1)
        for order in range(2):
            a_re, a_im = _dft_stage1_pair(s_in, s1_tab)
            rows = lambda t: t.reshape(NB_BATCH // 2, N_FFT, HY)
            c_re, c_im = _spectral_filter(rows(a_re), rows(a_im), h_re, h_im, order, wf, wi)
            tiles = lambda t: t.reshape(NB_BATCH // 2, FFT_A, FFT_B, FFT_T, HY)
            s_in = _idft_stage1_gate(tiles(c_re), tiles(c_im), s1i_tab, gate_a, s_in, hy_skip[l, order])
            gate_a = view(x2)
        hyo = s_in.reshape(T_LAT, HY)

        if last:
            hyoc = zeros_tile
        else:
            circ_c = _circular_filters(feats_c, w1p[l], b1p[l], frp[l], w2p[l], b2p[l], w3p[l], dec)
            hyoc = _hyena_ctx(hy, hy_conv_w[l], hy_conv_b[l], circ_c, fc, fs, hy_skip[l])

        hf, hb = _mlstm(mqkv, mg, mg[:, :4 * ML_H].T, ml_gate_b[l])

        xs1 = _out_proj(att, hyo, hyoc, hf, hb, mo, ml_norm_w[l], w_out_b[l], xs, mod4, l, not last)
        xs = _ffn(xs1, mod4, l, norm2_w[l], wg_b[l], wu_b[l], wd_b[l], not last)
        out = xs
    return out.reshape(NB_BATCH, SEQ, D)
```

```python
import functools
import math

import numpy as np
import jax
import jax.numpy as jnp
from jax import lax
from jax.experimental import pallas as pl
from jax.experimental.pallas import tpu as pltpu

F32 = jnp.float32
BF16 = jnp.bfloat16

D = 1024
NB_BATCH = 4
SEQ = 4096
DEPTH = 4
GRID_W = 64
CTX = 256
T_LAT = NB_BATCH * SEQ
T_CTX = NB_BATCH * CTX
R_ALL = T_LAT + T_CTX

HD = 64
ATT_H = 6
ATT_W = ATT_H * HD
KV_W = 2 * HD
BLK = 128
NBLK = SEQ // BLK
ROPE_BASE = 10000.0

HY = 384
N_FILT = 4 * HY
POS_BANDS = 16
FILT_HID = 64

ML_H = 4
ML_W = 256
CHUNK = 128

D_FF = 2816
P_IN = 2832
P_PAD = 2944
NEG = -1e30
EPS = 1e-6

TM = 512
NT_LAT = T_LAT // TM
NT_ALL = R_ALL // TM
TILES_PER_SEQ = SEQ // TM

N_FFT = 2 * SEQ
FFT_A = 32
FFT_R = 256
FFT_T = 16
FFT_B = FFT_R // FFT_T
HALF_A = FFT_A // 2
S1_TILES = 4
HB_ROWS = 1024
N_CTXF = 2 * CTX


def _np_tables():
    ka = np.arange(FFT_A)[None, :, None, None]
    cc = np.arange(FFT_T)[None, None, :, None]
    aa = np.arange(HALF_A)[None, None, None, :]
    bb = np.arange(FFT_B)[:, None, None, None]
    ph = (ka * (FFT_R * aa + FFT_T * bb + cc)) % N_FFT
    th = 2.0 * np.pi * ph / N_FFT
    eye = np.eye(FFT_T)
    cos = np.einsum("bkca,cd->bkcad", np.cos(th), eye)
    sin = np.einsum("bkca,cd->bkcad", np.sin(th), eye)
    rows, cols = FFT_A * FFT_T, HALF_A * FFT_T
    s1 = np.concatenate([cos.reshape(FFT_B, rows, cols), -sin.reshape(FFT_B, rows, cols)], axis=1)
    s1i = np.concatenate([cos.reshape(FFT_B, rows, cols).transpose(0, 2, 1),
                          -sin.reshape(FFT_B, rows, cols).transpose(0, 2, 1)], axis=2) / N_FFT
    r = np.arange(FFT_R)
    th2 = 2.0 * np.pi * ((r[:, None] * r[None, :]) % FFT_R) / FFT_R
    c2, s2 = np.cos(th2), np.sin(th2)
    wf = np.block([[c2, s2], [-s2, c2]])
    wi = np.block([[c2, -s2], [s2, c2]])
    c = np.arange(N_CTXF)
    th3 = 2.0 * np.pi * ((c[:, None] * c[None, :]) % N_CTXF) / N_CTXF
    return dict(s1=s1.astype(np.float32), s1i=s1i.astype(np.float32), wf=wf.astype(np.float32), wi=wi.astype(np.float32),
                fc=np.cos(th3).astype(np.float32), fs=np.sin(th3).astype(np.float32))


_TAB = _np_tables()


def _blockdiag_ones(width):
    return np.kron(np.eye(width // HD), np.ones((HD, HD))).astype(np.float32)


def _cparams(sem, vmem_mb=48):
    return pltpu.CompilerParams(dimension_semantics=sem, vmem_limit_bytes=vmem_mb * 1024 * 1024)


def _dot(a, b):
    return jnp.dot(a.astype(BF16), b.astype(BF16), preferred_element_type=F32)


def _dot_nt(a, b):
    return lax.dot_general(a.astype(BF16), b.astype(BF16), (((1,), (1,)), ((), ())), preferred_element_type=F32)


def _dot_tn(a, b):
    return lax.dot_general(a.astype(BF16), b.astype(BF16), (((0,), (0,)), ((), ())), preferred_element_type=F32)


def _split3(x):
    x1 = x.astype(BF16)
    r1 = x - x1.astype(F32)
    x2 = r1.astype(BF16)
    x3 = (r1 - x2.astype(F32)).astype(BF16)
    return x1, x2, x3


def _dot_sel_l(sel, x):
    return sum(jnp.dot(sel, p, preferred_element_type=F32) for p in _split3(x))


def _dot_sel_r(x, sel):
    return sum(jnp.dot(p, sel, preferred_element_type=F32) for p in _split3(x))


def _sigmoid(x):
    return 1.0 / (1.0 + jnp.exp(-x))


def _log_sigmoid(x):
    return jnp.minimum(x, 0.0) - jnp.log(1.0 + jnp.exp(-jnp.abs(x)))


def _head_rms(t, bd, w):
    ss = _dot_sel_r(t * t, bd)
    return t * lax.rsqrt(ss * (1.0 / HD) + EPS) * w


def _mod_row(i):
    return jnp.where(i < NT_LAT, i // TILES_PER_SEQ, NB_BATCH)


def _mod_spec(layer, k):
    return pl.BlockSpec((None, None, 1, D), lambda i: (layer, _mod_row(i), 0, k))


def _mod_body(s_ref, w_ref, b_ref, o_ref):
    s = s_ref[...]
    s = s * _sigmoid(s)
    o_ref[...] = _dot(s, w_ref[...]) + b_ref[...]


def _modulation(cc8, w_mod, b_mod):
    nc = 1536
    return pl.pallas_call(
        _mod_body,
        grid=(DEPTH, 6 * D // nc),
        in_specs=[pl.BlockSpec((8, D), lambda l, j: (0, 0)),
                  pl.BlockSpec((None, D, nc), lambda l, j: (l, 0, j)),
                  pl.BlockSpec((None, 1, nc), lambda l, j: (l, 0, j))],
        out_specs=pl.BlockSpec((None, 8, nc), lambda l, j: (l, 0, j)),
        out_shape=jax.ShapeDtypeStruct((DEPTH, 8, 6 * D), F32),
        compiler_params=_cparams(("parallel", "parallel")),
        name="modulation",
    )(cc8, w_mod, b_mod.reshape(DEPTH, 1, 6 * D))


_COL_ATT, _COL_HY, _COL_MQV, _COL_MK, _COL_MO, _COL_MG = (0, 640), (640, 1792), (1792, 2304), (2304, 2560), (2560, 2816), (2816, 2944)
N_GATES = 4 * ML_H


def _reorder_w_in(w_in):
    mq_end, mk_end, mv_end = 2048, 2304, 2560
    w = jnp.concatenate([w_in[:, :, :mq_end], w_in[:, :, mk_end:mv_end], w_in[:, :, mq_end:mk_end], w_in[:, :, mv_end:]], axis=2)
    return jnp.pad(w, ((0, 0), (0, 0), (0, P_PAD - P_IN))).astype(BF16)


def _modulated_norm(x, nw, sc, sh):
    ms = jnp.mean(x * x, axis=-1, keepdims=True)
    return (x * lax.rsqrt(ms + EPS) * nw) * (1.0 + sc) + sh


def _in_body(x_ref, sh_ref, sc_ref, nw_ref, w_ref, att_ref, hy_ref, mqvt_ref, mk_ref, mo_ref, mg_ref, mgt_ref):
    hb = _modulated_norm(x_ref[...], nw_ref[...], sc_ref[...], sh_ref[...]).astype(BF16)
    proj = lambda cols: jnp.dot(hb, w_ref[:, cols[0]:cols[1]], preferred_element_type=F32)
    att_ref[...] = proj(_COL_ATT)
    hy_ref[...] = proj(_COL_HY)
    mqvt_ref[...] = proj(_COL_MQV).T.astype(BF16)
    mk_ref[...] = (proj(_COL_MK) * (HD ** -0.5)).astype(BF16)
    mo_ref[...] = proj(_COL_MO)
    mg = proj(_COL_MG)
    mg_ref[...] = mg
    mgt_ref[...] = mg.T[0:N_GATES, :]


def _in_proj(xs, mod4, layer, nw, w_in_p):
    rows = lambda w: pl.BlockSpec((TM, w), lambda i: (i, 0))
    cols = lambda h: pl.BlockSpec((h, TM), lambda i: (0, i))
    return pl.pallas_call(
        _in_body,
        grid=(NT_ALL,),
        in_specs=[pl.BlockSpec((TM, D), lambda i: (i, 0)),
                  _mod_spec(layer, 0), _mod_spec(layer, 1),
                  pl.BlockSpec((1, D), lambda i: (0, 0)),
                  pl.BlockSpec((D, P_PAD), lambda i: (0, 0))],
        out_specs=[rows(640), rows(3 * HY), cols(2 * ML_W), rows(ML_W), rows(ML_W), rows(128), cols(N_GATES)],
        out_shape=[jax.ShapeDtypeStruct((R_ALL, 640), F32), jax.ShapeDtypeStruct((R_ALL, 3 * HY), F32),
                   jax.ShapeDtypeStruct((2 * ML_W, R_ALL), BF16), jax.ShapeDtypeStruct((R_ALL, ML_W), BF16),
                   jax.ShapeDtypeStruct((R_ALL, ML_W), F32), jax.ShapeDtypeStruct((R_ALL, 128), F32),
                   jax.ShapeDtypeStruct((N_GATES, R_ALL), F32)],
        compiler_params=_cparams(("parallel",), 56),
        name="in_proj",
    )(xs, mod4, mod4, nw.reshape(1, D), w_in_p)


def _rope(t, cos, sin_signed, first_half):
    outs = []
    for c in range(t.shape[1] // 128):
        tc = t[:, c * 128:(c + 1) * 128]
        partner = jnp.where(first_half, pltpu.roll(tc, 112, 1), pltpu.roll(tc, 16, 1))
        outs.append(tc * cos + partner * sin_signed)
    return outs[0] if len(outs) == 1 else jnp.concatenate(outs, axis=1)


def _qk_body(a_ref, cos_ref, sin_ref, qw_ref, kw_ref, bdq_ref, bdk_ref, qt_ref, k_ref, vt_ref):
    cos, sin = cos_ref[...], sin_ref[...]
    lane = lax.broadcasted_iota(jnp.int32, (TM, 128), 1)
    first_half = (lane & 31) < 16
    q = _head_rms(a_ref[:, 0:ATT_W], bdq_ref[...], qw_ref[...])
    k = _head_rms(a_ref[:, ATT_W:ATT_W + KV_W], bdk_ref[...], kw_ref[...])
    q = _rope(q, cos, sin, first_half) * (HD ** -0.5)
    qt_ref[...] = q.T.astype(BF16)
    k_ref[...] = _rope(k, cos, sin, first_half).astype(BF16)
    vt_ref[...] = a_ref[:, ATT_W + KV_W:ATT_W + 2 * KV_W].T.astype(BF16)


def _qk_prep(att_qkv, cos_t, sin_t, qw, kw):
    def tab(i):
        return (jnp.where(i < NT_LAT, i % TILES_PER_SEQ, TILES_PER_SEQ), 0)
    bdq = jnp.asarray(_blockdiag_ones(ATT_W)).astype(BF16)
    bdk = jnp.asarray(_blockdiag_ones(KV_W)).astype(BF16)
    return pl.pallas_call(
        _qk_body,
        grid=(NT_ALL,),
        in_specs=[pl.BlockSpec((TM, 640), lambda i: (i, 0)),
                  pl.BlockSpec((TM, 128), tab), pl.BlockSpec((TM, 128), tab),
                  pl.BlockSpec((1, ATT_W), lambda i: (0, 0)), pl.BlockSpec((1, KV_W), lambda i: (0, 0)),
                  pl.BlockSpec((ATT_W, ATT_W), lambda i: (0, 0)), pl.BlockSpec((KV_W, KV_W), lambda i: (0, 0))],
        out_specs=[pl.BlockSpec((ATT_W, TM), lambda i: (0, i)), pl.BlockSpec((TM, KV_W), lambda i: (i, 0)),
                   pl.BlockSpec((KV_W, TM), lambda i: (0, i))],
        out_shape=[jax.ShapeDtypeStruct((ATT_W, R_ALL), BF16), jax.ShapeDtypeStruct((R_ALL, KV_W), BF16),
                   jax.ShapeDtypeStruct((KV_W, R_ALL), BF16)],
        compiler_params=_cparams(("parallel",)),
        name="qk_prep",
    )(att_qkv, cos_t, sin_t, jnp.tile(qw, ATT_H).reshape(1, ATT_W), jnp.tile(kw, 2).reshape(1, KV_W), bdq, bdk)


def _att_heads(qt, kv_list, sink_ref, o_ref):
    nq = qt.shape[1]
    rep = ATT_H // 2
    zeros = jnp.zeros((HD, rep * nq), BF16)
    outs = []
    for g in range(2):
        heads = range(rep * g, rep * (g + 1))
        qg = jnp.concatenate([qt[HD * h:HD * (h + 1), :] for h in heads], axis=1)
        qe = jnp.concatenate([qg, zeros] if g == 0 else [zeros, qg], axis=0)
        sink = jnp.concatenate([jnp.broadcast_to(sink_ref[0:1, h:h + 1], (1, nq)) for h in heads], axis=1)
        scores = []
        m = sink
        for k, _, mask in kv_list:
            s = jnp.dot(k, qe, preferred_element_type=F32)
            if mask is not None:
                s = jnp.where(mask, s, NEG)
            scores.append(s)
            m = jnp.maximum(m, jnp.max(s, axis=0, keepdims=True))
        l = jnp.exp(sink - m)
        acc = None
        for s, (_, vt, _) in zip(scores, kv_list):
            p = jnp.exp(s - m)
            l = l + jnp.sum(p, axis=0, keepdims=True)
            pv = jnp.dot(vt[HD * g:HD * (g + 1), :], p.astype(BF16), preferred_element_type=F32)
            acc = pv if acc is None else acc + pv
        og = acc / l
        outs += [og[:, nq * i:nq * (i + 1)] for i in range(rep)]
    o_ref[...] = jnp.concatenate(outs, axis=0).T


def _att_body(qt_ref, kp_ref, ko_ref, kn_ref, vp_ref, vo_ref, vn_ref, kc_ref, vc_ref, sink_ref, o_ref):
    j = pl.program_id(1)
    qt = qt_ref[...]
    kc, vc = kc_ref[...], vc_ref[...]

    @pl.when(j < NBLK)
    def _():
        width = (ATT_H // 2) * BLK
        key = lax.broadcasted_iota(jnp.int32, (BLK, width), 0)
        qry = lax.broadcasted_iota(jnp.int32, (BLK, width), 1) & (BLK - 1)
        mask_prev = jnp.logical_and(key >= qry, j > 0)
        mask_next = jnp.logical_and(key <= qry, j < NBLK - 1)
        _att_heads(qt, [(kp_ref[...], vp_ref[...], mask_prev), (ko_ref[...], vo_ref[...], None),
                        (kn_ref[...], vn_ref[...], mask_next), (kc, vc, None)], sink_ref, o_ref)

    @pl.when(j >= NBLK)
    def _():
        _att_heads(qt, [(kc, vc, None)], sink_ref, o_ref)


def _attention(qt, kn, vt, sink8, with_ctx):
    steps = NBLK + (CTX // BLK if with_ctx else 0)
    rows = R_ALL if with_ctx else T_LAT

    def qblk(b, j):
        return jnp.where(j < NBLK, b * NBLK + j, T_LAT // BLK + (CTX // BLK) * b + (j - NBLK))

    def band(off):
        return lambda b, j: b * NBLK + jnp.clip(j + off, 0, NBLK - 1)

    kspec = lambda f: pl.BlockSpec((BLK, KV_W), lambda b, j: (f(b, j), 0))
    vspec = lambda f: pl.BlockSpec((KV_W, BLK), lambda b, j: (0, f(b, j)))
    return pl.pallas_call(
        _att_body,
        grid=(NB_BATCH, steps),
        in_specs=[pl.BlockSpec((ATT_W, BLK), lambda b, j: (0, qblk(b, j))),
                  kspec(band(-1)), kspec(band(0)), kspec(band(1)),
                  vspec(band(-1)), vspec(band(0)), vspec(band(1)),
                  pl.BlockSpec((CTX, KV_W), lambda b, j: (T_LAT // CTX + b, 0)),
                  pl.BlockSpec((KV_W, CTX), lambda b, j: (0, T_LAT // CTX + b)),
                  pl.BlockSpec((8, 128), lambda b, j: (0, 0))],
        out_specs=pl.BlockSpec((BLK, ATT_W), lambda b, j: (qblk(b, j), 0)),
        out_shape=jax.ShapeDtypeStruct((rows, ATT_W), F32),
        compiler_params=_cparams(("parallel", "parallel")),
        name="attention",
    )(qt, kn, kn, kn, vt, vt, vt, kn, vt, sink8)


def _short_conv(u, w, b):
    n = u.shape[0]
    row = lax.broadcasted_iota(jnp.int32, u.shape, 0)
    up = jnp.where(row == 0, 0.0, pltpu.roll(u, 1, 0))
    un = jnp.where(row == n - 1, 0.0, pltpu.roll(u, n - 1, 0))
    return up * w[0:1, :] + u * w[1:2, :] + un * w[2:3, :] + b


def _sc_body(u0_ref, u1_ref, u2_ref, w0_ref, w1_ref, w2_ref, b0_ref, b1_ref, b2_ref, o0_ref, o1_ref, o2_ref):
    for u_ref, w_ref, b_ref, o_ref in ((u0_ref, w0_ref, b0_ref, o0_ref), (u1_ref, w1_ref, b1_ref, o1_ref), (u2_ref, w2_ref, b2_ref, o2_ref)):
        o_ref[...] = _short_conv(u_ref[...], w_ref[...], b_ref[...])


def _short_conv_latent(hy, conv_w, conv_b):
    cb = conv_b.reshape(1, 3 * HY)
    ins, ws, bs = [], [], []
    for part in range(3):
        ins.append(pl.BlockSpec((SEQ, 128), lambda b, j, part=part: (b, 3 * part + j)))
        ws.append(pl.BlockSpec((3, 128), lambda b, j, part=part: (0, 3 * part + j)))
        bs.append(pl.BlockSpec((1, 128), lambda b, j, part=part: (0, 3 * part + j)))
    return pl.pallas_call(
        _sc_body,
        grid=(NB_BATCH, 3),
        in_specs=ins + ws + bs,
        out_specs=[pl.BlockSpec((SEQ, 128), lambda b, j: (b, j))] * 3,
        out_shape=[jax.ShapeDtypeStruct((T_LAT, HY), F32)] * 3,
        compiler_params=_cparams(("parallel", "parallel")),
        name="short_conv",
    )(hy, hy, hy, conv_w, conv_w, conv_w, cb, cb, cb)


def _filt_body(n, tl, f_ref, w1_ref, b1_ref, fr_ref, w2_ref, b2_ref, w3_ref, dec_ref, o_ref):
    f = f_ref[...]
    fr = fr_ref[...]
    z = jnp.sin(fr * (_dot(f, w1_ref[...]) + b1_ref[...]))
    z = jnp.sin(fr * (_dot(z, w2_ref[...]) + b2_ref[...]))
    filt = _dot(z, w3_ref[...]) * jnp.exp(-f[:, 0:1] * jnp.abs(dec_ref[...]))
    row = pl.program_id(0) * tl + lax.broadcasted_iota(jnp.int32, filt.shape, 0)
    o_ref[...] = jnp.where(row == n, 0.0, filt)


def _circular_filters(feats_circ, w1p, b1p, frp, w2p, b2p, w3p, dec):
    n = feats_circ.shape[0] // 2
    tl = min(n, 512)
    const = lambda shape: pl.BlockSpec(shape, lambda i: (0, 0))
    half = lambda rows: pl.BlockSpec((rows, 2 * HY), lambda i: (0, i // (n // tl)))
    return pl.pallas_call(
        functools.partial(_filt_body, n, tl),
        grid=(2 * n // tl,),
        in_specs=[pl.BlockSpec((tl, 128), lambda i: (i, 0)), const((128, 128)), const((1, 128)), const((1, 128)),
                  const((128, 128)), const((1, 128)), half(128), half(1)],
        out_specs=pl.BlockSpec((tl, 2 * HY), lambda i: (i, 0)),
        out_shape=jax.ShapeDtypeStruct((2 * n, 2 * HY), F32),
        compiler_params=_cparams(("parallel",)),
        name="hyena_filters",
    )(feats_circ, w1p, b1p, frp, w2p, b2p, w3p, dec)


def _pos_feats_circ(n):
    t = jnp.linspace(0.0, 1.0, n, dtype=F32)[:, None]
    ang = (2.0 * math.pi / n) * jnp.arange(n, dtype=F32)[:, None]
    bands = jnp.linspace(1e-4, POS_BANDS - 1, POS_BANDS, dtype=F32)[None, :]
    feats = jnp.concatenate([t, jnp.cos(bands * ang), -jnp.sin(bands * ang)], axis=-1)
    feats = jnp.pad(feats, ((0, 0), (0, 128 - feats.shape[1])))
    return jnp.concatenate([feats, feats[:1], jnp.flip(feats[:n - 1], axis=0)], axis=0)


def _s1_body(x_ref, m_ref, are_ref, aim_ref):
    half = FFT_A * FFT_T
    for p in range(S1_TILES):
        xs = x_ref[:, p].reshape(HALF_A * FFT_T, HY)
        r = _dot(m_ref[p], xs)
        are_ref[:, p] = r[:half].reshape(FFT_A, FFT_T, HY).astype(BF16)
        aim_ref[:, p] = r[half:].reshape(FFT_A, FFT_T, HY).astype(BF16)


def _dft_stage1(x5, s1_tab):
    nb, ncb = x5.shape[0], x5.shape[-1] // HY
    out = pl.BlockSpec((None, None, FFT_A, S1_TILES, FFT_T, HY), lambda b, cb, j: (b, cb, 0, j, 0, 0))
    return pl.pallas_call(
        _s1_body,
        grid=(nb, ncb, FFT_B // S1_TILES),
        in_specs=[pl.BlockSpec((None, HALF_A, S1_TILES, FFT_T, HY), lambda b, cb, j: (b, 0, j, 0, cb)),
                  pl.BlockSpec((S1_TILES, 2 * FFT_A * FFT_T, HALF_A * FFT_T), lambda b, cb, j: (j, 0, 0))],
        out_specs=[out, out],
        out_shape=[jax.ShapeDtypeStruct((nb, ncb, FFT_A, FFT_B, FFT_T, HY), BF16)] * 2,
        compiler_params=_cparams(("parallel", "parallel", "parallel")),
        name="dft_stage1",
    )(x5, s1_tab)


def _s1_pair_body(x_ref, m_ref, are_ref, aim_ref):
    half = FFT_A * FFT_T
    for p in range(S1_TILES):
        ra = _dot(m_ref[p], x_ref[0, :, p].reshape(HALF_A * FFT_T, HY))
        rb = _dot(m_ref[p], x_ref[1, :, p].reshape(HALF_A * FFT_T, HY))
        are_ref[:, p] = (ra[:half] - rb[half:]).reshape(FFT_A, FFT_T, HY).astype(BF16)
        aim_ref[:, p] = (ra[half:] + rb[:half]).reshape(FFT_A, FFT_T, HY).astype(BF16)


def _dft_stage1_pair(x5, s1_tab):
    npair = x5.shape[0] // 2
    out = pl.BlockSpec((None, FFT_A, S1_TILES, FFT_T, HY), lambda b, j: (b, 0, j, 0, 0))
    return pl.pallas_call(
        _s1_pair_body,
        grid=(npair, FFT_B // S1_TILES),
        in_specs=[pl.BlockSpec((2, HALF_A, S1_TILES, FFT_T, HY), lambda b, j: (b, 0, j, 0, 0)),
                  pl.BlockSpec((S1_TILES, 2 * FFT_A * FFT_T, HALF_A * FFT_T), lambda b, j: (j, 0, 0))],
        out_specs=[out, out],
        out_shape=[jax.ShapeDtypeStruct((npair, FFT_A, FFT_B, FFT_T, HY), BF16)] * 2,
        compiler_params=_cparams(("parallel", "parallel")),
        name="dft_stage1_pair",
    )(x5, s1_tab)


def _block_rows(r):
    return slice(r * FFT_R, (r + 1) * FFT_R)


def _stack_complex(re_ref, im_ref, sl):
    return jnp.concatenate([re_ref[sl, :], im_ref[sl, :]], axis=0)


def _spec_body(are_ref, aim_ref, wf_ref, hre_ref, him_ref):
    wf = wf_ref[...]
    for r in range(HB_ROWS // FFT_R):
        sl = _block_rows(r)
        first = jnp.dot(wf, _stack_complex(are_ref.at[0], aim_ref.at[0], sl), preferred_element_type=F32)
        second = jnp.dot(wf, _stack_complex(are_ref.at[1], aim_ref.at[1], sl), preferred_element_type=F32)
        x = first + second if r % 2 == 0 else first - second
        hre_ref[sl, :] = x[:FFT_R]
        him_ref[sl, :] = x[FFT_R:]


def _filter_spectrum(a_re, a_im, wf):
    assert (HB_ROWS // FFT_R) % 2 == 0
    blk_in = pl.BlockSpec((2, None, HB_ROWS, HY), lambda r, o: (0, o, r, 0))
    blk_out = pl.BlockSpec((None, HB_ROWS, HY), lambda r, o: (o, r, 0))
    return pl.pallas_call(
        _spec_body,
        grid=(N_FFT // HB_ROWS, 2),
        in_specs=[blk_in, blk_in, pl.BlockSpec((2 * FFT_R, 2 * FFT_R), lambda r, o: (0, 0))],
        out_specs=[blk_out, blk_out],
        out_shape=[jax.ShapeDtypeStruct((2, N_FFT, HY), F32)] * 2,
        compiler_params=_cparams(("parallel", "parallel")),
        name="filter_spectrum",
    )(a_re, a_im, wf)


def _s2_body(are_ref, aim_ref, hre_ref, him_ref, wf_ref, wi_ref, cre_ref, cim_ref):
    wf, wi = wf_ref[...], wi_ref[...]
    for r in range(HB_ROWS // FFT_R):
        sl = _block_rows(r)
        x = jnp.dot(wf, _stack_complex(are_ref, aim_ref, sl), preferred_element_type=F32)
        xre, xim = x[:FFT_R], x[FFT_R:]
        hre, him = hre_ref[sl, :], him_ref[sl, :]
        y = jnp.concatenate([(xre * hre - xim * him).astype(BF16), (xre * him + xim * hre).astype(BF16)], axis=0)
        c = jnp.dot(wi, y, preferred_element_type=F32)
        cre_ref[sl, :] = c[:FFT_R].astype(BF16)
        cim_ref[sl, :] = c[FFT_R:].astype(BF16)


def _spectral_filter(a_re, a_im, h_re, h_im, order, wf, wi):
    nb = a_re.shape[0]
    blk = pl.BlockSpec((None, HB_ROWS, HY), lambda r, b: (b, r, 0))
    hblk = pl.BlockSpec((None, HB_ROWS, HY), lambda r, b: (order, r, 0))
    mat = pl.BlockSpec((2 * FFT_R, 2 * FFT_R), lambda r, b: (0, 0))
    return pl.pallas_call(
        _s2_body,
        grid=(N_FFT // HB_ROWS, nb),
        in_specs=[blk, blk, hblk, hblk, mat, mat],
        out_specs=[blk, blk],
        out_shape=[jax.ShapeDtypeStruct((nb, N_FFT, HY), BF16)] * 2,
        compiler_params=_cparams(("parallel", "parallel")),
        name="spectral_filter",
    )(a_re, a_im, h_re, h_im, wf, wi)


def _s1i_body(cre_ref, cim_ref, g_ref, a_ref, b_ref, sk_ref, o_ref):
    rows = FFT_A * FFT_T
    for p in range(S1_TILES):
        cre, cim = cre_ref[:, p].reshape(rows, HY), cim_ref[:, p].reshape(rows, HY)
        parts = (jnp.concatenate([cre, cim], axis=0), jnp.concatenate([cim, -cre], axis=0))
        for i, c in enumerate(parts):
            y = jnp.dot(g_ref[p], c, preferred_element_type=F32).reshape(HALF_A, FFT_T, HY)
            o_ref[i, :, p] = a_ref[i, :, p] * (y + b_ref[i, :, p] * sk_ref[...])


def _idft_stage1_gate(c_re, c_im, s1i_tab, a5, b5, skip):
    data = pl.BlockSpec((2, HALF_A, S1_TILES, FFT_T, HY), lambda b, j: (b, 0, j, 0, 0))
    spec = pl.BlockSpec((None, FFT_A, S1_TILES, FFT_T, HY), lambda b, j: (b, 0, j, 0, 0))
    return pl.pallas_call(
        _s1i_body,
        grid=(NB_BATCH // 2, FFT_B // S1_TILES),
        in_specs=[spec, spec, pl.BlockSpec((S1_TILES, HALF_A * FFT_T, 2 * FFT_A * FFT_T), lambda b, j: (j, 0, 0)),
                  data, data, pl.BlockSpec((1, 1, HY), lambda b, j: (0, 0, 0))],
        out_specs=data,
        out_shape=jax.ShapeDtypeStruct((NB_BATCH, HALF_A, FFT_B, FFT_T, HY), F32),
        compiler_params=_cparams(("parallel", "parallel")),
        name="idft_stage1_gate",
    )(c_re, c_im, s1i_tab, a5, b5, skip.reshape(1, 1, HY))


def _hyc_body(u_ref, w_ref, b_ref, circ_ref, fc_ref, fs_ref, sk_ref, o_ref):
    usc = _short_conv(u_ref[...], w_ref[...], b_ref[...])
    fc, fs = fc_ref[...], fs_ref[...]
    circ = circ_ref[...]
    h_re, h_im = _dot(fc, circ), -_dot(fs, circ)

    def long_conv(s, o):
        sl = slice(o * HY, (o + 1) * HY)
        s_re, s_im = _dot(fc[:, :CTX], s), -_dot(fs[:, :CTX], s)
        hre, him = h_re[:, sl], h_im[:, sl]
        y_re = s_re * hre - s_im * him
        y_im = s_re * him + s_im * hre
        y = (_dot(fc[:CTX, :], y_re) - _dot(fs[:CTX, :], y_im)) * (1.0 / N_CTXF)
        return y + s * sk_ref[o:o + 1, :]

    v, x1, x2 = usc[:, 0:HY], usc[:, HY:2 * HY], usc[:, 2 * HY:3 * HY]
    o_ref[...] = x2 * long_conv(x1 * long_conv(v, 0), 1)


def _hyena_ctx(hy, conv_w, conv_b, circ_c, fc, fs, skip):
    const = lambda shape: pl.BlockSpec(shape, lambda b: (0, 0))
    return pl.pallas_call(
        _hyc_body,
        grid=(NB_BATCH,),
        in_specs=[pl.BlockSpec((CTX, 3 * HY), lambda b: (T_LAT // CTX + b, 0)), const((3, 3 * HY)), const((1, 3 * HY)),
                  const((N_CTXF, 2 * HY)), const((N_CTXF, N_CTXF)), const((N_CTXF, N_CTXF)), const((2, HY))],
        out_specs=pl.BlockSpec((CTX, HY), lambda b: (b, 0)),
        out_shape=jax.ShapeDtypeStruct((T_CTX, HY), F32),
        compiler_params=_cparams(("parallel",)),
        name="hyena_ctx",
    )(hy, conv_w, conv_b.reshape(1, 3 * HY), circ_c, fc, fs, skip)


N_ML_STEPS = CTX // CHUNK + SEQ // CHUNK


def _ml_chain(direction, qv_ref, k_ref, src_col, cum_r, gates_t, c_scr, m_scr, mask, ones_rows):
    base = 8 * direction
    outs = []
    for h in range(ML_H):
        ic, fc = base + h, base + 4 + h
        chain = direction * ML_H + h
        b_row, li_row = cum_r[fc:fc + 1, :], gates_t[ic:ic + 1, :]
        b_end = b_row[:, CHUNK - 1:CHUNK] if direction == 0 else b_row[:, 0:1]
        q_t = qv_ref[HD * h:HD * (h + 1), :]
        vext_t = jnp.concatenate([qv_ref[ML_W + HD * h:ML_W + HD * (h + 1), :], ones_rows], axis=0)
        k = k_ref[:, HD * h:HD * (h + 1)]
        c_prev, m_prev = c_scr[chain], m_scr[chain]
        dmat = jnp.where(mask, src_col[:, fc:fc + 1] + b_row, NEG)
        m_intra = jnp.max(dmat, axis=0, keepdims=True)
        s_t = jnp.dot(k, q_t, preferred_element_type=F32) * jnp.exp(dmat - m_intra)
        inter = b_row + m_prev
        m_t = jnp.maximum(inter, m_intra)
        hx = jnp.exp(m_intra - m_t) * _dot(vext_t, s_t) + jnp.exp(inter - m_t) * _dot(c_prev, q_t)
        den = jnp.maximum(jnp.abs(hx[HD:HD + 1, :]), jnp.exp(-m_t))
        outs.append(hx[:HD, :] / den)
        g_row = b_end - b_row + li_row
        m_new = jnp.maximum(b_end + m_prev, jnp.max(g_row, axis=1, keepdims=True))
        c_scr[chain] = jnp.exp(b_end + m_prev - m_new) * c_prev + _dot(vext_t * jnp.exp(g_row - m_new), k)
        m_scr[chain] = m_new
    return jnp.concatenate(outs, axis=0)


def _ml_body(qvf_ref, kf_ref, gf_ref, gtf_ref, qvb_ref, kb_ref, gb_ref, gtb_ref, bias_ref, bias_t_ref, tril_ref, triu_ref,
             hf_ref, hb_ref, c_scr, m_scr):
    @pl.when(pl.program_id(1) == 0)
    def _():
        c_scr[...] = jnp.zeros_like(c_scr)
        m_scr[...] = jnp.zeros_like(m_scr)

    tril, triu = tril_ref[...], triu_ref[...]
    src = lax.broadcasted_iota(jnp.int32, (CHUNK, CHUNK), 0)
    dst = lax.broadcasted_iota(jnp.int32, (CHUNK, CHUNK), 1)
    ones_rows = (lax.broadcasted_iota(jnp.int32, (HD, CHUNK), 0) == 0).astype(BF16)
    dirs = ((qvf_ref, kf_ref, gf_ref, gtf_ref, hf_ref), (qvb_ref, kb_ref, gb_ref, gtb_ref, hb_ref))
    for direction, (qv_ref, k_ref, g_ref, gt_ref, o_ref) in enumerate(dirs):
        gates = g_ref[...] + bias_ref[...]
        gates_t = gt_ref[...] + bias_t_ref[...]
        ls, ls_t = _log_sigmoid(gates), _log_sigmoid(gates_t)
        if direction == 0:
            cum_c, cum_r, mask = _dot_sel_l(tril, ls), _dot_sel_r(ls_t, triu), src <= dst
        else:
            cum_c, cum_r, mask = _dot_sel_l(triu, ls), _dot_sel_r(ls_t, tril), src >= dst
        src_col = pltpu.roll(gates, 4, 1) - cum_c
        o_ref[...] = _ml_chain(direction, qv_ref, k_ref, src_col, cum_r, gates_t, c_scr, m_scr, mask, ones_rows)


def _mlstm(mqv_t, mk, mg, mg_t, gate_b):
    nctx = CTX // CHUNK
    nlat = SEQ // CHUNK

    def fwd(b, i):
        return jnp.where(i < nctx, T_LAT // CHUNK + nctx * b + i, nlat * b + (i - nctx))

    def bwd(b, i):
        return jnp.where(i < nctx, T_LAT // CHUNK + nctx * b + (nctx - 1 - i), nlat * b + (N_ML_STEPS - 1 - i))

    bias = jnp.pad(gate_b, (0, 128 - 4 * ML_H)).reshape(1, 128)
    bias_t = gate_b.reshape(4 * ML_H, 1)
    tril = jnp.asarray(np.tril(np.ones((CHUNK, CHUNK), np.float32))).astype(BF16)
    triu = jnp.asarray(np.triu(np.ones((CHUNK, CHUNK), np.float32))).astype(BF16)
    const = lambda shape: pl.BlockSpec(shape, lambda b, i: (0, 0))
    ins = []
    for f in (fwd, bwd):
        ins += [pl.BlockSpec((2 * ML_W, CHUNK), lambda b, i, f=f: (0, f(b, i))),
                pl.BlockSpec((CHUNK, ML_W), lambda b, i, f=f: (f(b, i), 0)),
                pl.BlockSpec((CHUNK, 128), lambda b, i, f=f: (f(b, i), 0)),
                pl.BlockSpec((N_GATES, CHUNK), lambda b, i, f=f: (0, f(b, i)))]
    return pl.pallas_call(
        _ml_body,
        grid=(NB_BATCH, N_ML_STEPS),
        in_specs=ins + [const((1, 128)), const((N_GATES, 1)), const((CHUNK, CHUNK)), const((CHUNK, CHUNK))],
        out_specs=[pl.BlockSpec((ML_W, CHUNK), lambda b, i: (0, fwd(b, i))),
                   pl.BlockSpec((ML_W, CHUNK), lambda b, i: (0, bwd(b, i)))],
        out_shape=[jax.ShapeDtypeStruct((ML_W, R_ALL), F32)] * 2,
        scratch_shapes=[pltpu.VMEM((2 * ML_H, 2 * HD, HD), F32), pltpu.VMEM((2 * ML_H, 1, 1), F32)],
        compiler_params=_cparams(("parallel", "arbitrary")),
        name="mlstm",
    )(mqv_t, mk, mg, mg_t, mqv_t, mk, mg, mg_t, bias, bias_t, tril, triu)


FF_CH = D_FF // 2


def _out_ffn_body(att_ref, hl_ref, hc_ref, hf_ref, hb_ref, mo_ref, mnw_ref, bd_ref, wo_ref, x_ref,
                  g1_ref, sh2_ref, sc2_ref, g2_ref, nw2_ref, wg_ref, wu_ref, wd_ref, o_ref):
    i = pl.program_id(0)
    hn = _head_rms((hf_ref[...] + hb_ref[...]).T, bd_ref[...], mnw_ref[...])
    mlo = hn * _sigmoid(mo_ref[...])
    hy = jnp.where(i < NT_LAT, hl_ref[...], hc_ref[...])
    y = _dot(att_ref[...], wo_ref[0:ATT_W, :]) + _dot(hy, wo_ref[ATT_W:ATT_W + HY, :]) + _dot(mlo, wo_ref[ATT_W + HY:, :])
    x = x_ref[...] + g1_ref[...] * y
    hb = _modulated_norm(x, nw2_ref[...], sc2_ref[...], sh2_ref[...]).astype(BF16)
    acc = None
    for c in range(D_FF // FF_CH):
        sl = slice(c * FF_CH, (c + 1) * FF_CH)
        a = jnp.dot(hb, wg_ref[:, sl], preferred_element_type=F32)
        u = jnp.dot(hb, wu_ref[:, sl], preferred_element_type=F32)
        part = _dot(a * _sigmoid(a) * u, wd_ref[sl, :])
        acc = part if acc is None else acc + part
    o_ref[...] = x + g2_ref[...] * acc


def _out_ffn(att, hyo, hyoc, hf_t, hb_t, mo, ml_nw, w_out_b, xs, mod4, layer, nw2, wg, wu, wd, with_ctx):
    tiles = NT_ALL if with_ctx else NT_LAT
    row = lambda w: pl.BlockSpec((TM, w), lambda i: (i, 0))
    col = pl.BlockSpec((ML_W, TM), lambda i: (0, i))
    resident = lambda shape: pl.BlockSpec(shape, lambda i: (0, 0), pipeline_mode=pl.Buffered(1))
    vec = lambda w: pl.BlockSpec((1, w), lambda i: (0, 0))
    bd = jnp.asarray(_blockdiag_ones(ML_W)).astype(BF16)
    return pl.pallas_call(
        _out_ffn_body,
        grid=(tiles,),
        in_specs=[row(ATT_W),
                  pl.BlockSpec((TM, HY), lambda i: (jnp.minimum(i, NT_LAT - 1), 0)),
                  pl.BlockSpec((TM, HY), lambda i: (jnp.maximum(i - NT_LAT, 0), 0)),
                  col, col, row(ML_W), vec(ML_W), resident((ML_W, ML_W)), resident((D, D)), row(D),
                  _mod_spec(layer, 2), _mod_spec(layer, 3), _mod_spec(layer, 4), _mod_spec(layer, 5), vec(D),
                  resident((D, D_FF)), resident((D, D_FF)), resident((D_FF, D))],
        out_specs=row(D),
        out_shape=jax.ShapeDtypeStruct((tiles * TM, D), F32),
        compiler_params=_cparams(("parallel",), 56),
        name="out_ffn",
    )(att, hyo, hyoc, hf_t, hb_t, mo, ml_nw.reshape(1, ML_W), bd, w_out_b, xs, mod4, mod4, mod4, mod4,
      nw2.reshape(1, D), wg, wu, wd)


def _rope_tables():
    n_rows = SEQ // GRID_W
    row = jnp.repeat(jnp.arange(n_rows), GRID_W)
    col = jnp.tile(jnp.arange(GRID_W), n_rows)
    nf = HD // 4
    inv_freq = ROPE_BASE ** (-jnp.arange(nf, dtype=F32) / nf)
    ang = jnp.stack([row[:, None] * inv_freq, col[:, None] * inv_freq], axis=1)
    cos, sin = jnp.cos(ang), jnp.sin(ang)
    cos_h = jnp.concatenate([cos, cos], axis=-1).reshape(SEQ, HD)
    sin_h = jnp.concatenate([-sin, sin], axis=-1).reshape(SEQ, HD)
    cos_t = jnp.concatenate([jnp.tile(cos_h, (1, 2)), jnp.ones((TM, 128), F32)], axis=0)
    sin_t = jnp.concatenate([jnp.tile(sin_h, (1, 2)), jnp.zeros((TM, 128), F32)], axis=0)
    return cos_t, sin_t


def kernel(x, c, ctx, c_ctx, w_mod, b_mod, norm1_w, norm2_w, w_in, w_out, q_norm_w, k_norm_w, attn_sink, hy_conv_w, hy_conv_b, hy_w1, hy_b1, hy_freq, hy_w2, hy_b2, hy_w3, hy_decay, hy_skip, ml_gate_b, ml_norm_w, ffn_w_gate, ffn_w_up, ffn_w_down):
    xs = jnp.concatenate([x.reshape(T_LAT, D), ctx.reshape(T_CTX, D)], axis=0)
    cc8 = jnp.concatenate([c, c_ctx[None, :], jnp.zeros((8 - NB_BATCH - 1, D), F32)], axis=0)
    mod4 = _modulation(cc8, w_mod, b_mod).reshape(DEPTH, 8, 1, 6 * D)

    cos_t, sin_t = _rope_tables()
    feats_l, feats_c = _pos_feats_circ(SEQ), _pos_feats_circ(CTX)
    s1_tab = jnp.asarray(_TAB["s1"]).astype(BF16)
    s1i_tab = jnp.asarray(_TAB["s1i"]).astype(BF16)
    wf, wi = jnp.asarray(_TAB["wf"]).astype(BF16), jnp.asarray(_TAB["wi"]).astype(BF16)
    fc, fs = jnp.asarray(_TAB["fc"]).astype(BF16), jnp.asarray(_TAB["fs"]).astype(BF16)

    w_in_p = _reorder_w_in(w_in)
    w_out_b = w_out.astype(BF16)
    wg_b, wu_b, wd_b = ffn_w_gate.astype(BF16), ffn_w_up.astype(BF16), ffn_w_down.astype(BF16)
    hpad = 128 - FILT_HID
    w1p = jnp.pad(hy_w1, ((0, 0), (0, 128 - hy_w1.shape[1]), (0, hpad)))
    b1p = jnp.pad(hy_b1, ((0, 0), (0, hpad))).reshape(DEPTH, 1, 128)
    frp = jnp.pad(hy_freq, ((0, 0), (0, hpad))).reshape(DEPTH, 1, 128)
    w2p = jnp.pad(hy_w2, ((0, 0), (0, hpad), (0, hpad)))
    b2p = jnp.pad(hy_b2, ((0, 0), (0, hpad))).reshape(DEPTH, 1, 128)
    w3p = jnp.pad(hy_w3, ((0, 0), (0, hpad), (0, 0)))
    sink8 = jnp.pad(attn_sink, ((0, 0), (0, 128 - ATT_H)))[:, None, :] * jnp.ones((1, 8, 1), F32)
    zeros_tile = jnp.zeros((TM, HY), F32)

    out = None
    for l in range(DEPTH):
        last = l == DEPTH - 1
        att_qkv, hy, mqv_t, mk, mo, mg, mg_t = _in_proj(xs, mod4, l, norm1_w[l], w_in_p[l])

        qt, kn, vt = _qk_prep(att_qkv, cos_t, sin_t, q_norm_w[l], k_norm_w[l])
        att = _attention(qt, kn, vt, sink8[l], not last)

        dec = hy_decay[l].reshape(1, N_FILT)
        circ = _circular_filters(feats_l, w1p[l], b1p[l], frp[l], w2p[l], b2p[l], w3p[l], dec)
        f_re, f_im = _dft_stage1(circ.reshape(2, HALF_A, FFT_B, FFT_T, 2 * HY), s1_tab)
        h_re, h_im = _filter_spectrum(f_re.reshape(2, 2, N_FFT, HY), f_im.reshape(2, 2, N_FFT, HY), wf)

        v, x1, x2 = _short_conv_latent(hy, hy_conv_w[l], hy_conv_b[l])
        view = lambda t: t.reshape(NB_BATCH, HALF_A, FFT_B, FFT_T, HY)
        s_in, gate_a = view(v), view(x1)
        for order in range(2):
            a_re, a_im = _dft_stage1_pair(s_in, s1_tab)
            rows = lambda t: t.reshape(NB_BATCH // 2, N_FFT, HY)
            c_re, c_im = _spectral_filter(rows(a_re), rows(a_im), h_re, h_im, order, wf, wi)
            tiles = lambda t: t.reshape(NB_BATCH // 2, FFT_A, FFT_B, FFT_T, HY)
            s_in = _idft_stage1_gate(tiles(c_re), tiles(c_im), s1i_tab, gate_a, s_in, hy_skip[l, order])
            gate_a = view(x2)
        hyo = s_in.reshape(T_LAT, HY)

        if last:
            hyoc = zeros_tile
        else:
            circ_c = _circular_filters(feats_c, w1p[l], b1p[l], frp[l], w2p[l], b2p[l], w3p[l], dec)
            hyoc = _hyena_ctx(hy, hy_conv_w[l], hy_conv_b[l], circ_c, fc, fs, hy_skip[l])

        hf_t, hb_t = _mlstm(mqv_t, mk, mg, mg_t, ml_gate_b[l])

        xs = _out_ffn(att, hyo, hyoc, hf_t, hb_t, mo, ml_norm_w[l], w_out_b[l], xs, mod4, l,
                      norm2_w[l], wg_b[l], wu_b[l], wd_b[l], not last)
        out = xs
    return out.reshape(NB_BATCH, SEQ, D)
```

```python
import functools
import math

import numpy as np
import jax
import jax.numpy as jnp
from jax import lax
from jax.experimental import pallas as pl
from jax.experimental.pallas import tpu as pltpu

F32 = jnp.float32
BF16 = jnp.bfloat16

D = 1024
NB_BATCH = 4
SEQ = 4096
DEPTH = 4
GRID_W = 64
CTX = 256
T_LAT = NB_BATCH * SEQ
T_CTX = NB_BATCH * CTX
R_ALL = T_LAT + T_CTX

HD = 64
ATT_H = 6
ATT_W = ATT_H * HD
KV_W = 2 * HD
BLK = 128
NBLK = SEQ // BLK
ROPE_BASE = 10000.0

HY = 384
N_FILT = 4 * HY
POS_BANDS = 16
FILT_HID = 64

ML_H = 4
ML_W = 256
CHUNK = 128

D_FF = 2816
P_IN = 2832
P_PAD = 2944
NEG = -1e30
EPS = 1e-6

TM = 512
NT_LAT = T_LAT // TM
NT_ALL = R_ALL // TM
TILES_PER_SEQ = SEQ // TM

N_FFT = 2 * SEQ
FFT_A = 32
FFT_R = 256
FFT_T = 16
FFT_B = FFT_R // FFT_T
HALF_A = FFT_A // 2
S1_TILES = 4
HB_ROWS = 1024
N_CTXF = 2 * CTX


def _np_tables():
    ka = np.arange(FFT_A)[None, :, None, None]
    cc = np.arange(FFT_T)[None, None, :, None]
    aa = np.arange(HALF_A)[None, None, None, :]
    bb = np.arange(FFT_B)[:, None, None, None]
    ph = (ka * (FFT_R * aa + FFT_T * bb + cc)) % N_FFT
    th = 2.0 * np.pi * ph / N_FFT
    eye = np.eye(FFT_T)
    cos = np.einsum("bkca,cd->bkcad", np.cos(th), eye)
    sin = np.einsum("bkca,cd->bkcad", np.sin(th), eye)
    rows, cols = FFT_A * FFT_T, HALF_A * FFT_T
    s1 = np.concatenate([cos.reshape(FFT_B, rows, cols), -sin.reshape(FFT_B, rows, cols)], axis=1)
    s1i = np.concatenate([cos.reshape(FFT_B, rows, cols).transpose(0, 2, 1),
                          -sin.reshape(FFT_B, rows, cols).transpose(0, 2, 1)], axis=2) / N_FFT
    r = np.arange(FFT_R)
    th2 = 2.0 * np.pi * ((r[:, None] * r[None, :]) % FFT_R) / FFT_R
    c2, s2 = np.cos(th2), np.sin(th2)
    wf = np.block([[c2, s2], [-s2, c2]])
    wi = np.block([[c2, -s2], [s2, c2]])
    c = np.arange(N_CTXF)
    th3 = 2.0 * np.pi * ((c[:, None] * c[None, :]) % N_CTXF) / N_CTXF
    return dict(s1=s1.astype(np.float32), s1i=s1i.astype(np.float32), wf=wf.astype(np.float32), wi=wi.astype(np.float32),
                fc=np.cos(th3).astype(np.float32), fs=np.sin(th3).astype(np.float32))


_TAB = _np_tables()


def _blockdiag_ones(width):
    return np.kron(np.eye(width // HD), np.ones((HD, HD))).astype(np.float32)


def _cparams(sem, vmem_mb=48):
    return pltpu.CompilerParams(dimension_semantics=sem, vmem_limit_bytes=vmem_mb * 1024 * 1024)


def _dot(a, b):
    return jnp.dot(a.astype(BF16), b.astype(BF16), preferred_element_type=F32)


def _dot_nt(a, b):
    return lax.dot_general(a.astype(BF16), b.astype(BF16), (((1,), (1,)), ((), ())), preferred_element_type=F32)


def _dot_tn(a, b):
    return lax.dot_general(a.astype(BF16), b.astype(BF16), (((0,), (0,)), ((), ())), preferred_element_type=F32)


def _split3(x):
    x1 = x.astype(BF16)
    r1 = x - x1.astype(F32)
    x2 = r1.astype(BF16)
    x3 = (r1 - x2.astype(F32)).astype(BF16)
    return x1, x2, x3


def _dot_sel_l(sel, x):
    return sum(jnp.dot(sel, p, preferred_element_type=F32) for p in _split3(x))


def _dot_sel_r(x, sel):
    return sum(jnp.dot(p, sel, preferred_element_type=F32) for p in _split3(x))


def _sigmoid(x):
    return 1.0 / (1.0 + jnp.exp(-x))


def _log_sigmoid(x):
    return jnp.minimum(x, 0.0) - jnp.log(1.0 + jnp.exp(-jnp.abs(x)))


def _head_rms(t, bd, w):
    ss = _dot_sel_r(t * t, bd)
    return t * lax.rsqrt(ss * (1.0 / HD) + EPS) * w


def _mod_row(i):
    return jnp.where(i < NT_LAT, i // TILES_PER_SEQ, NB_BATCH)


def _mod_spec(layer, k):
    return pl.BlockSpec((None, None, 1, D), lambda i: (layer, _mod_row(i), 0, k))


def _mod_body(s_ref, w_ref, b_ref, o_ref):
    s = s_ref[...]
    s = s * _sigmoid(s)
    o_ref[...] = _dot(s, w_ref[...]) + b_ref[...]


def _modulation(cc8, w_mod, b_mod):
    nc = 1536
    return pl.pallas_call(
        _mod_body,
        grid=(DEPTH, 6 * D // nc),
        in_specs=[pl.BlockSpec((8, D), lambda l, j: (0, 0)),
                  pl.BlockSpec((None, D, nc), lambda l, j: (l, 0, j)),
                  pl.BlockSpec((None, 1, nc), lambda l, j: (l, 0, j))],
        out_specs=pl.BlockSpec((None, 8, nc), lambda l, j: (l, 0, j)),
        out_shape=jax.ShapeDtypeStruct((DEPTH, 8, 6 * D), F32),
        compiler_params=_cparams(("parallel", "parallel")),
        name="modulation",
    )(cc8, w_mod, b_mod.reshape(DEPTH, 1, 6 * D))


_COL_ATT, _COL_HY, _COL_MQV, _COL_MK, _COL_MO, _COL_MG = (0, 640), (640, 1792), (1792, 2304), (2304, 2560), (2560, 2816), (2816, 2944)
N_GATES = 4 * ML_H


def _reorder_w_in(w_in):
    mq_end, mk_end, mv_end = 2048, 2304, 2560
    w = jnp.concatenate([w_in[:, :, :mq_end], w_in[:, :, mk_end:mv_end], w_in[:, :, mq_end:mk_end], w_in[:, :, mv_end:]], axis=2)
    return jnp.pad(w, ((0, 0), (0, 0), (0, P_PAD - P_IN))).astype(BF16)


def _modulated_norm(x, nw, sc, sh):
    ms = jnp.mean(x * x, axis=-1, keepdims=True)
    return (x * lax.rsqrt(ms + EPS) * nw) * (1.0 + sc) + sh


def _in_body(x_ref, sh_ref, sc_ref, nw_ref, w_ref, att_ref, hy_ref, mqvt_ref, mk_ref, mo_ref, mg_ref, mgt_ref):
    hb = _modulated_norm(x_ref[...], nw_ref[...], sc_ref[...], sh_ref[...]).astype(BF16)
    proj = lambda cols: jnp.dot(hb, w_ref[:, cols[0]:cols[1]], preferred_element_type=F32)
    att_ref[...] = proj(_COL_ATT)
    hy_ref[...] = proj(_COL_HY)
    mqvt_ref[...] = proj(_COL_MQV).T.astype(BF16)
    mk_ref[...] = (proj(_COL_MK) * (HD ** -0.5)).astype(BF16)
    mo_ref[...] = proj(_COL_MO)
    mg = proj(_COL_MG)
    mg_ref[...] = mg
    mgt_ref[...] = mg.T[0:N_GATES, :]


def _in_proj(xs, mod4, layer, nw, w_in_p):
    rows = lambda w: pl.BlockSpec((TM, w), lambda i: (i, 0))
    cols = lambda h: pl.BlockSpec((h, TM), lambda i: (0, i))
    return pl.pallas_call(
        _in_body,
        grid=(NT_ALL,),
        in_specs=[pl.BlockSpec((TM, D), lambda i: (i, 0)),
                  _mod_spec(layer, 0), _mod_spec(layer, 1),
                  pl.BlockSpec((1, D), lambda i: (0, 0)),
                  pl.BlockSpec((D, P_PAD), lambda i: (0, 0))],
        out_specs=[rows(640), rows(3 * HY), cols(2 * ML_W), rows(ML_W), rows(ML_W), rows(128), cols(N_GATES)],
        out_shape=[jax.ShapeDtypeStruct((R_ALL, 640), F32), jax.ShapeDtypeStruct((R_ALL, 3 * HY), F32),
                   jax.ShapeDtypeStruct((2 * ML_W, R_ALL), BF16), jax.ShapeDtypeStruct((R_ALL, ML_W), BF16),
                   jax.ShapeDtypeStruct((R_ALL, ML_W), F32), jax.ShapeDtypeStruct((R_ALL, 128), F32),
                   jax.ShapeDtypeStruct((N_GATES, R_ALL), F32)],
        compiler_params=_cparams(("parallel",), 56),
        name="in_proj",
    )(xs, mod4, mod4, nw.reshape(1, D), w_in_p)


def _rope(t, cos, sin_signed, first_half):
    outs = []
    for c in range(t.shape[1] // 128):
        tc = t[:, c * 128:(c + 1) * 128]
        partner = jnp.where(first_half, pltpu.roll(tc, 112, 1), pltpu.roll(tc, 16, 1))
        outs.append(tc * cos + partner * sin_signed)
    return outs[0] if len(outs) == 1 else jnp.concatenate(outs, axis=1)


def _qk_body(a_ref, cos_ref, sin_ref, qw_ref, kw_ref, bdq_ref, bdk_ref, qt_ref, k_ref, vt_ref):
    cos, sin = cos_ref[...], sin_ref[...]
    lane = lax.broadcasted_iota(jnp.int32, (TM, 128), 1)
    first_half = (lane & 31) < 16
    q = _head_rms(a_ref[:, 0:ATT_W], bdq_ref[...], qw_ref[...])
    k = _head_rms(a_ref[:, ATT_W:ATT_W + KV_W], bdk_ref[...], kw_ref[...])
    q = _rope(q, cos, sin, first_half) * (HD ** -0.5)
    qt_ref[...] = q.T.astype(BF16)
    k_ref[...] = _rope(k, cos, sin, first_half).astype(BF16)
    vt_ref[...] = a_ref[:, ATT_W + KV_W:ATT_W + 2 * KV_W].T.astype(BF16)


def _qk_prep(att_qkv, cos_t, sin_t, qw, kw):
    def tab(i):
        return (jnp.where(i < NT_LAT, i % TILES_PER_SEQ, TILES_PER_SEQ), 0)
    bdq = jnp.asarray(_blockdiag_ones(ATT_W)).astype(BF16)
    bdk = jnp.asarray(_blockdiag_ones(KV_W)).astype(BF16)
    return pl.pallas_call(
        _qk_body,
        grid=(NT_ALL,),
        in_specs=[pl.BlockSpec((TM, 640), lambda i: (i, 0)),
                  pl.BlockSpec((TM, 128), tab), pl.BlockSpec((TM, 128), tab),
                  pl.BlockSpec((1, ATT_W), lambda i: (0, 0)), pl.BlockSpec((1, KV_W), lambda i: (0, 0)),
                  pl.BlockSpec((ATT_W, ATT_W), lambda i: (0, 0)), pl.BlockSpec((KV_W, KV_W), lambda i: (0, 0))],
        out_specs=[pl.BlockSpec((ATT_W, TM), lambda i: (0, i)), pl.BlockSpec((TM, KV_W), lambda i: (i, 0)),
                   pl.BlockSpec((KV_W, TM), lambda i: (0, i))],
        out_shape=[jax.ShapeDtypeStruct((ATT_W, R_ALL), BF16), jax.ShapeDtypeStruct((R_ALL, KV_W), BF16),
                   jax.ShapeDtypeStruct((KV_W, R_ALL), BF16)],
        compiler_params=_cparams(("parallel",)),
        name="qk_prep",
    )(att_qkv, cos_t, sin_t, jnp.tile(qw, ATT_H).reshape(1, ATT_W), jnp.tile(kw, 2).reshape(1, KV_W), bdq, bdk)


def _att_heads(qt, kv_list, sink_ref, o_ref):
    nq = qt.shape[1]
    rep = ATT_H // 2
    zeros = jnp.zeros((HD, rep * nq), BF16)
    outs = []
    for g in range(2):
        heads = range(rep * g, rep * (g + 1))
        qg = jnp.concatenate([qt[HD * h:HD * (h + 1), :] for h in heads], axis=1)
        qe = jnp.concatenate([qg, zeros] if g == 0 else [zeros, qg], axis=0)
        sink = jnp.concatenate([jnp.broadcast_to(sink_ref[0:1, h:h + 1], (1, nq)) for h in heads], axis=1)
        scores = []
        m = sink
        for k, _, mask in kv_list:
            s = jnp.dot(k, qe, preferred_element_type=F32)
            if mask is not None:
                s = jnp.where(mask, s, NEG)
            scores.append(s)
            m = jnp.maximum(m, jnp.max(s, axis=0, keepdims=True))
        l = jnp.exp(sink - m)
        acc = None
        for s, (_, vt, _) in zip(scores, kv_list):
            p = jnp.exp(s - m)
            l = l + jnp.sum(p, axis=0, keepdims=True)
            pv = jnp.dot(vt[HD * g:HD * (g + 1), :], p.astype(BF16), preferred_element_type=F32)
            acc = pv if acc is None else acc + pv
        og = acc / l
        outs += [og[:, nq * i:nq * (i + 1)] for i in range(rep)]
    o_ref[...] = jnp.concatenate(outs, axis=0).T


QB = 2
N_ATT_STEPS = NBLK // QB


def _att_body(qt_ref, k0_ref, k1_ref, k2_ref, k3_ref, v0_ref, v1_ref, v2_ref, v3_ref, kc_ref, vc_ref, sink_ref, o_ref):
    j = pl.program_id(1)
    kc, vc = kc_ref[...], vc_ref[...]
    out = lambda sub: o_ref.at[sub * BLK:(sub + 1) * BLK, :]

    @pl.when(j < N_ATT_STEPS)
    def _():
        width = (ATT_H // 2) * BLK
        key = lax.broadcasted_iota(jnp.int32, (BLK, width), 0)
        qry = lax.broadcasted_iota(jnp.int32, (BLK, width), 1) & (BLK - 1)
        ks = (k0_ref, k1_ref, k2_ref, k3_ref)
        vs = (v0_ref, v1_ref, v2_ref, v3_ref)
        for sub in range(QB):
            blk = QB * j + sub
            mask_prev = jnp.logical_and(key >= qry, blk > 0)
            mask_next = jnp.logical_and(key <= qry, blk < NBLK - 1)
            _att_heads(qt_ref[:, sub * BLK:(sub + 1) * BLK],
                       [(ks[sub][...], vs[sub][...], mask_prev), (ks[sub + 1][...], vs[sub + 1][...], None),
                        (ks[sub + 2][...], vs[sub + 2][...], mask_next), (kc, vc, None)], sink_ref, out(sub))

    @pl.when(j >= N_ATT_STEPS)
    def _():
        for sub in range(QB):
            _att_heads(qt_ref[:, sub * BLK:(sub + 1) * BLK], [(kc, vc, None)], sink_ref, out(sub))


def _attention(qt, kn, vt, sink8, with_ctx):
    assert CTX == QB * BLK
    steps = N_ATT_STEPS + (1 if with_ctx else 0)
    rows = R_ALL if with_ctx else T_LAT
    qrows = QB * BLK

    def qblk(b, j):
        return jnp.where(j < N_ATT_STEPS, b * N_ATT_STEPS + j, T_LAT // qrows + b)

    def band(off):
        return lambda b, j: b * NBLK + jnp.clip(QB * j + off, 0, NBLK - 1)

    kspec = lambda f: pl.BlockSpec((BLK, KV_W), lambda b, j: (f(b, j), 0))
    vspec = lambda f: pl.BlockSpec((KV_W, BLK), lambda b, j: (0, f(b, j)))
    offs = range(-1, QB + 1)
    return pl.pallas_call(
        _att_body,
        grid=(NB_BATCH, steps),
        in_specs=[pl.BlockSpec((ATT_W, qrows), lambda b, j: (0, qblk(b, j)))]
                 + [kspec(band(o)) for o in offs] + [vspec(band(o)) for o in offs]
                 + [pl.BlockSpec((CTX, KV_W), lambda b, j: (T_LAT // CTX + b, 0)),
                    pl.BlockSpec((KV_W, CTX), lambda b, j: (0, T_LAT // CTX + b)),
                    pl.BlockSpec((8, 128), lambda b, j: (0, 0))],
        out_specs=pl.BlockSpec((qrows, ATT_W), lambda b, j: (qblk(b, j), 0)),
        out_shape=jax.ShapeDtypeStruct((rows, ATT_W), F32),
        compiler_params=_cparams(("parallel", "parallel")),
        name="attention",
    )(qt, *([kn] * 4), *([vt] * 4), kn, vt, sink8)


def _short_conv(u, w, b):
    n = u.shape[0]
    row = lax.broadcasted_iota(jnp.int32, u.shape, 0)
    up = jnp.where(row == 0, 0.0, pltpu.roll(u, 1, 0))
    un = jnp.where(row == n - 1, 0.0, pltpu.roll(u, n - 1, 0))
    return up * w[0:1, :] + u * w[1:2, :] + un * w[2:3, :] + b


def _sc_body(u0_ref, u1_ref, u2_ref, w0_ref, w1_ref, w2_ref, b0_ref, b1_ref, b2_ref, o0_ref, o1_ref, o2_ref):
    for u_ref, w_ref, b_ref, o_ref in ((u0_ref, w0_ref, b0_ref, o0_ref), (u1_ref, w1_ref, b1_ref, o1_ref), (u2_ref, w2_ref, b2_ref, o2_ref)):
        o_ref[...] = _short_conv(u_ref[...], w_ref[...], b_ref[...])


def _short_conv_latent(hy, conv_w, conv_b):
    cb = conv_b.reshape(1, 3 * HY)
    ins, ws, bs = [], [], []
    for part in range(3):
        ins.append(pl.BlockSpec((SEQ, 128), lambda b, j, part=part: (b, 3 * part + j)))
        ws.append(pl.BlockSpec((3, 128), lambda b, j, part=part: (0, 3 * part + j)))
        bs.append(pl.BlockSpec((1, 128), lambda b, j, part=part: (0, 3 * part + j)))
    return pl.pallas_call(
        _sc_body,
        grid=(NB_BATCH, 3),
        in_specs=ins + ws + bs,
        out_specs=[pl.BlockSpec((SEQ, 128), lambda b, j: (b, j))] * 3,
        out_shape=[jax.ShapeDtypeStruct((T_LAT, HY), F32)] * 3,
        compiler_params=_cparams(("parallel", "parallel")),
        name="short_conv",
    )(hy, hy, hy, conv_w, conv_w, conv_w, cb, cb, cb)


def _filt_body(n, tl, f_ref, w1_ref, b1_ref, fr_ref, w2_ref, b2_ref, w3_ref, dec_ref, o_ref):
    f = f_ref[...]
    fr = fr_ref[...]
    z = jnp.sin(fr * (_dot(f, w1_ref[...]) + b1_ref[...]))
    z = jnp.sin(fr * (_dot(z, w2_ref[...]) + b2_ref[...]))
    filt = _dot(z, w3_ref[...]) * jnp.exp(-f[:, 0:1] * jnp.abs(dec_ref[...]))
    row = pl.program_id(0) * tl + lax.broadcasted_iota(jnp.int32, filt.shape, 0)
    o_ref[...] = jnp.where(row == n, 0.0, filt)


def _circular_filters(feats_circ, w1p, b1p, frp, w2p, b2p, w3p, dec):
    n = feats_circ.shape[0] // 2
    tl = min(n, 512)
    const = lambda shape: pl.BlockSpec(shape, lambda i: (0, 0))
    half = lambda rows: pl.BlockSpec((rows, 2 * HY), lambda i: (0, i // (n // tl)))
    return pl.pallas_call(
        functools.partial(_filt_body, n, tl),
        grid=(2 * n // tl,),
        in_specs=[pl.BlockSpec((tl, 128), lambda i: (i, 0)), const((128, 128)), const((1, 128)), const((1, 128)),
                  const((128, 128)), const((1, 128)), half(128), half(1)],
        out_specs=pl.BlockSpec((tl, 2 * HY), lambda i: (i, 0)),
        out_shape=jax.ShapeDtypeStruct((2 * n, 2 * HY), F32),
        compiler_params=_cparams(("parallel",)),
        name="hyena_filters",
    )(feats_circ, w1p, b1p, frp, w2p, b2p, w3p, dec)


def _pos_feats_circ(n):
    t = jnp.linspace(0.0, 1.0, n, dtype=F32)[:, None]
    ang = (2.0 * math.pi / n) * jnp.arange(n, dtype=F32)[:, None]
    bands = jnp.linspace(1e-4, POS_BANDS - 1, POS_BANDS, dtype=F32)[None, :]
    feats = jnp.concatenate([t, jnp.cos(bands * ang), -jnp.sin(bands * ang)], axis=-1)
    feats = jnp.pad(feats, ((0, 0), (0, 128 - feats.shape[1])))
    return jnp.concatenate([feats, feats[:1], jnp.flip(feats[:n - 1], axis=0)], axis=0)


def _s1_body(x_ref, m_ref, are_ref, aim_ref):
    half = FFT_A * FFT_T
    for p in range(S1_TILES):
        xs = x_ref[:, p].reshape(HALF_A * FFT_T, HY)
        r = _dot(m_ref[p], xs)
        are_ref[:, p] = r[:half].reshape(FFT_A, FFT_T, HY).astype(BF16)
        aim_ref[:, p] = r[half:].reshape(FFT_A, FFT_T, HY).astype(BF16)


def _dft_stage1(x5, s1_tab):
    nb, ncb = x5.shape[0], x5.shape[-1] // HY
    out = pl.BlockSpec((None, None, FFT_A, S1_TILES, FFT_T, HY), lambda b, cb, j: (b, cb, 0, j, 0, 0))
    return pl.pallas_call(
        _s1_body,
        grid=(nb, ncb, FFT_B // S1_TILES),
        in_specs=[pl.BlockSpec((None, HALF_A, S1_TILES, FFT_T, HY), lambda b, cb, j: (b, 0, j, 0, cb)),
                  pl.BlockSpec((S1_TILES, 2 * FFT_A * FFT_T, HALF_A * FFT_T), lambda b, cb, j: (j, 0, 0))],
        out_specs=[out, out],
        out_shape=[jax.ShapeDtypeStruct((nb, ncb, FFT_A, FFT_B, FFT_T, HY), BF16)] * 2,
        compiler_params=_cparams(("parallel", "parallel", "parallel")),
        name="dft_stage1",
    )(x5, s1_tab)


def _s1_pair_body(x_ref, m_ref, are_ref, aim_ref):
    half = FFT_A * FFT_T
    for p in range(S1_TILES):
        ra = _dot(m_ref[p], x_ref[0, :, p].reshape(HALF_A * FFT_T, HY))
        rb = _dot(m_ref[p], x_ref[1, :, p].reshape(HALF_A * FFT_T, HY))
        are_ref[:, p] = (ra[:half] - rb[half:]).reshape(FFT_A, FFT_T, HY).astype(BF16)
        aim_ref[:, p] = (ra[half:] + rb[:half]).reshape(FFT_A, FFT_T, HY).astype(BF16)


def _dft_stage1_pair(x5, s1_tab):
    npair = x5.shape[0] // 2
    out = pl.BlockSpec((None, FFT_A, S1_TILES, FFT_T, HY), lambda b, j: (b, 0, j, 0, 0))
    return pl.pallas_call(
        _s1_pair_body,
        grid=(npair, FFT_B // S1_TILES),
        in_specs=[pl.BlockSpec((2, HALF_A, S1_TILES, FFT_T, HY), lambda b, j: (b, 0, j, 0, 0)),
                  pl.BlockSpec((S1_TILES, 2 * FFT_A * FFT_T, HALF_A * FFT_T), lambda b, j: (j, 0, 0))],
        out_specs=[out, out],
        out_shape=[jax.ShapeDtypeStruct((npair, FFT_A, FFT_B, FFT_T, HY), BF16)] * 2,
        compiler_params=_cparams(("parallel", "parallel")),
        name="dft_stage1_pair",
    )(x5, s1_tab)


def _block_rows(r):
    return slice(r * FFT_R, (r + 1) * FFT_R)


def _stack_complex(re_ref, im_ref, sl):
    return jnp.concatenate([re_ref[sl, :], im_ref[sl, :]], axis=0)


def _spec_body(are_ref, aim_ref, wf_ref, hre_ref, him_ref):
    wf = wf_ref[...]
    for r in range(HB_ROWS // FFT_R):
        sl = _block_rows(r)
        first = jnp.dot(wf, _stack_complex(are_ref.at[0], aim_ref.at[0], sl), preferred_element_type=F32)
        second = jnp.dot(wf, _stack_complex(are_ref.at[1], aim_ref.at[1], sl), preferred_element_type=F32)
        x = first + second if r % 2 == 0 else first - second
        hre_ref[sl, :] = x[:FFT_R]
        him_ref[sl, :] = x[FFT_R:]


def _filter_spectrum(a_re, a_im, wf):
    assert (HB_ROWS // FFT_R) % 2 == 0
    blk_in = pl.BlockSpec((2, None, HB_ROWS, HY), lambda r, o: (0, o, r, 0))
    blk_out = pl.BlockSpec((None, HB_ROWS, HY), lambda r, o: (o, r, 0))
    return pl.pallas_call(
        _spec_body,
        grid=(N_FFT // HB_ROWS, 2),
        in_specs=[blk_in, blk_in, pl.BlockSpec((2 * FFT_R, 2 * FFT_R), lambda r, o: (0, 0))],
        out_specs=[blk_out, blk_out],
        out_shape=[jax.ShapeDtypeStruct((2, N_FFT, HY), F32)] * 2,
        compiler_params=_cparams(("parallel", "parallel")),
        name="filter_spectrum",
    )(a_re, a_im, wf)


def _s2_body(are_ref, aim_ref, hre_ref, him_ref, wf_ref, wi_ref, cre_ref, cim_ref):
    wf, wi = wf_ref[...], wi_ref[...]
    for r in range(HB_ROWS // FFT_R):
        sl = _block_rows(r)
        x = jnp.dot(wf, _stack_complex(are_ref, aim_ref, sl), preferred_element_type=F32)
        xre, xim = x[:FFT_R], x[FFT_R:]
        hre, him = hre_ref[sl, :], him_ref[sl, :]
        y = jnp.concatenate([(xre * hre - xim * him).astype(BF16), (xre * him + xim * hre).astype(BF16)], axis=0)
        c = jnp.dot(wi, y, preferred_element_type=F32)
        cre_ref[sl, :] = c[:FFT_R].astype(BF16)
        cim_ref[sl, :] = c[FFT_R:].astype(BF16)


def _spectral_filter(a_re, a_im, h_re, h_im, order, wf, wi):
    nb = a_re.shape[0]
    blk = pl.BlockSpec((None, HB_ROWS, HY), lambda r, b: (b, r, 0))
    hblk = pl.BlockSpec((None, HB_ROWS, HY), lambda r, b: (order, r, 0))
    mat = pl.BlockSpec((2 * FFT_R, 2 * FFT_R), lambda r, b: (0, 0))
    return pl.pallas_call(
        _s2_body,
        grid=(N_FFT // HB_ROWS, nb),
        in_specs=[blk, blk, hblk, hblk, mat, mat],
        out_specs=[blk, blk],
        out_shape=[jax.ShapeDtypeStruct((nb, N_FFT, HY), BF16)] * 2,
        compiler_params=_cparams(("parallel", "parallel")),
        name="spectral_filter",
    )(a_re, a_im, h_re, h_im, wf, wi)


def _s1i_body(cre_ref, cim_ref, g_ref, a_ref, b_ref, sk_ref, o_ref):
    rows = FFT_A * FFT_T
    for p in range(S1_TILES):
        cre, cim = cre_ref[:, p].reshape(rows, HY), cim_ref[:, p].reshape(rows, HY)
        parts = (jnp.concatenate([cre, cim], axis=0), jnp.concatenate([cim, -cre], axis=0))
        for i, c in enumerate(parts):
            y = jnp.dot(g_ref[p], c, preferred_element_type=F32).reshape(HALF_A, FFT_T, HY)
            o_ref[i, :, p] = a_ref[i, :, p] * (y + b_ref[i, :, p] * sk_ref[...])


def _idft_stage1_gate(c_re, c_im, s1i_tab, a5, b5, skip):
    data = pl.BlockSpec((2, HALF_A, S1_TILES, FFT_T, HY), lambda b, j: (b, 0, j, 0, 0))
    spec = pl.BlockSpec((None, FFT_A, S1_TILES, FFT_T, HY), lambda b, j: (b, 0, j, 0, 0))
    return pl.pallas_call(
        _s1i_body,
        grid=(NB_BATCH // 2, FFT_B // S1_TILES),
        in_specs=[spec, spec, pl.BlockSpec((S1_TILES, HALF_A * FFT_T, 2 * FFT_A * FFT_T), lambda b, j: (j, 0, 0)),
                  data, data, pl.BlockSpec((1, 1, HY), lambda b, j: (0, 0, 0))],
        out_specs=data,
        out_shape=jax.ShapeDtypeStruct((NB_BATCH, HALF_A, FFT_B, FFT_T, HY), F32),
        compiler_params=_cparams(("parallel", "parallel")),
        name="idft_stage1_gate",
    )(c_re, c_im, s1i_tab, a5, b5, skip.reshape(1, 1, HY))


def _hyc_body(u_ref, w_ref, b_ref, circ_ref, fc_ref, fs_ref, sk_ref, o_ref):
    usc = _short_conv(u_ref[...], w_ref[...], b_ref[...])
    fc, fs = fc_ref[...], fs_ref[...]
    circ = circ_ref[...]
    h_re, h_im = _dot(fc, circ), -_dot(fs, circ)

    def long_conv(s, o):
        sl = slice(o * HY, (o + 1) * HY)
        s_re, s_im = _dot(fc[:, :CTX], s), -_dot(fs[:, :CTX], s)
        hre, him = h_re[:, sl], h_im[:, sl]
        y_re = s_re * hre - s_im * him
        y_im = s_re * him + s_im * hre
        y = (_dot(fc[:CTX, :], y_re) - _dot(fs[:CTX, :], y_im)) * (1.0 / N_CTXF)
        return y + s * sk_ref[o:o + 1, :]

    v, x1, x2 = usc[:, 0:HY], usc[:, HY:2 * HY], usc[:, 2 * HY:3 * HY]
    o_ref[...] = x2 * long_conv(x1 * long_conv(v, 0), 1)


def _hyena_ctx(hy, conv_w, conv_b, circ_c, fc, fs, skip):
    const = lambda shape: pl.BlockSpec(shape, lambda b: (0, 0))
    return pl.pallas_call(
        _hyc_body,
        grid=(NB_BATCH,),
        in_specs=[pl.BlockSpec((CTX, 3 * HY), lambda b: (T_LAT // CTX + b, 0)), const((3, 3 * HY)), const((1, 3 * HY)),
                  const((N_CTXF, 2 * HY)), const((N_CTXF, N_CTXF)), const((N_CTXF, N_CTXF)), const((2, HY))],
        out_specs=pl.BlockSpec((CTX, HY), lambda b: (b, 0)),
        out_shape=jax.ShapeDtypeStruct((T_CTX, HY), F32),
        compiler_params=_cparams(("parallel",)),
        name="hyena_ctx",
    )(hy, conv_w, conv_b.reshape(1, 3 * HY), circ_c, fc, fs, skip)


N_ML_STEPS = CTX // CHUNK + SEQ // CHUNK


def _ml_chain(direction, qv_ref, k_ref, src_col, cum_r, gates_t, c_scr, m_scr, mask, ones_rows):
    base = 8 * direction
    outs = []
    for h in range(ML_H):
        ic, fc = base + h, base + 4 + h
        chain = direction * ML_H + h
        b_row, li_row = cum_r[fc:fc + 1, :], gates_t[ic:ic + 1, :]
        b_end = b_row[:, CHUNK - 1:CHUNK] if direction == 0 else b_row[:, 0:1]
        q_t = qv_ref[HD * h:HD * (h + 1), :]
        vext_t = jnp.concatenate([qv_ref[ML_W + HD * h:ML_W + HD * (h + 1), :], ones_rows], axis=0)
        k = k_ref[:, HD * h:HD * (h + 1)]
        c_prev, m_prev = c_scr[chain], m_scr[chain]
        dmat = jnp.where(mask, src_col[:, fc:fc + 1] + b_row, NEG)
        m_intra = jnp.max(dmat, axis=0, keepdims=True)
        s_t = jnp.dot(k, q_t, preferred_element_type=F32) * jnp.exp(dmat - m_intra)
        inter = b_row + m_prev
        m_t = jnp.maximum(inter, m_intra)
        hx = jnp.exp(m_intra - m_t) * _dot(vext_t, s_t) + jnp.exp(inter - m_t) * _dot(c_prev, q_t)
        den = jnp.maximum(jnp.abs(hx[HD:HD + 1, :]), jnp.exp(-m_t))
        outs.append(hx[:HD, :] / den)
        g_row = b_end - b_row + li_row
        m_new = jnp.maximum(b_end + m_prev, jnp.max(g_row, axis=1, keepdims=True))
        c_scr[chain] = jnp.exp(b_end + m_prev - m_new) * c_prev + _dot(vext_t * jnp.exp(g_row - m_new), k)
        m_scr[chain] = m_new
    return jnp.concatenate(outs, axis=0)


ML_BPS = 2
ML_GROUP_COLS = (NB_BATCH // ML_BPS) * (SEQ + CTX)


def _ml_body(*refs):
    n_in = 8 * ML_BPS
    seq_refs, (bias_ref, bias_t_ref, tril_ref, triu_ref) = refs[:n_in], refs[n_in:n_in + 4]
    (hf_ref, hb_ref), (c_scr, m_scr) = refs[n_in + 4:n_in + 6], refs[n_in + 6:]

    @pl.when(pl.program_id(1) == 0)
    def _():
        c_scr[...] = jnp.zeros_like(c_scr)
        m_scr[...] = jnp.zeros_like(m_scr)

    tril, triu = tril_ref[...], triu_ref[...]
    src = lax.broadcasted_iota(jnp.int32, (CHUNK, CHUNK), 0)
    dst = lax.broadcasted_iota(jnp.int32, (CHUNK, CHUNK), 1)
    ones_rows = (lax.broadcasted_iota(jnp.int32, (HD, CHUNK), 0) == 0).astype(BF16)
    for sub in range(ML_BPS):
        c_sub, m_sub = c_scr.at[sub], m_scr.at[sub]
        for direction in range(2):
            qv_ref, k_ref, g_ref, gt_ref = seq_refs[8 * sub + 4 * direction:8 * sub + 4 * direction + 4]
            gates = g_ref[...] + bias_ref[...]
            gates_t = gt_ref[...] + bias_t_ref[...]
            ls, ls_t = _log_sigmoid(gates), _log_sigmoid(gates_t)
            if direction == 0:
                cum_c, cum_r, mask = _dot_sel_l(tril, ls), _dot_sel_r(ls_t, triu), src <= dst
            else:
                cum_c, cum_r, mask = _dot_sel_l(triu, ls), _dot_sel_r(ls_t, tril), src >= dst
            src_col = pltpu.roll(gates, 4, 1) - cum_c
            o_ref = hf_ref if direction == 0 else hb_ref
            o_ref[sub] = _ml_chain(direction, qv_ref, k_ref, src_col, cum_r, gates_t, c_sub, m_sub, mask, ones_rows)


def _mlstm(mqv_t, mk, mg, mg_t, gate_b):
    nctx = CTX // CHUNK
    nlat = SEQ // CHUNK
    per_group = NB_BATCH // ML_BPS

    def step_chunk(i, backward):
        ctx_chunk = (nctx - 1 - i) if backward else i
        lat_chunk = (N_ML_STEPS - 1 - i) if backward else (i - nctx)
        return i < nctx, ctx_chunk, lat_chunk

    def in_chunk(sub, backward):
        def f(p, i):
            b = p + per_group * sub
            is_ctx, cc, lc = step_chunk(i, backward)
            return jnp.where(is_ctx, T_LAT // CHUNK + nctx * b + cc, nlat * b + lc)
        return f

    def out_chunk(backward):
        def f(p, i):
            is_ctx, cc, lc = step_chunk(i, backward)
            return jnp.where(is_ctx, per_group * nlat + nctx * p + cc, nlat * p + lc)
        return f

    bias = jnp.pad(gate_b, (0, 128 - N_GATES)).reshape(1, 128)
    bias_t = gate_b.reshape(N_GATES, 1)
    tril = jnp.asarray(np.tril(np.ones((CHUNK, CHUNK), np.float32))).astype(BF16)
    triu = jnp.asarray(np.triu(np.ones((CHUNK, CHUNK), np.float32))).astype(BF16)
    const = lambda shape: pl.BlockSpec(shape, lambda p, i: (0, 0))
    ins, args = [], []
    for sub in range(ML_BPS):
        for backward in (False, True):
            f = in_chunk(sub, backward)
            ins += [pl.BlockSpec((2 * ML_W, CHUNK), lambda p, i, f=f: (0, f(p, i))),
                    pl.BlockSpec((CHUNK, ML_W), lambda p, i, f=f: (f(p, i), 0)),
                    pl.BlockSpec((CHUNK, 128), lambda p, i, f=f: (f(p, i), 0)),
                    pl.BlockSpec((N_GATES, CHUNK), lambda p, i, f=f: (0, f(p, i)))]
            args += [mqv_t, mk, mg, mg_t]
    outs = [pl.BlockSpec((ML_BPS, ML_W, CHUNK), lambda p, i, f=out_chunk(bw): (0, 0, f(p, i))) for bw in (False, True)]
    return pl.pallas_call(
        _ml_body,
        grid=(per_group, N_ML_STEPS),
        in_specs=ins + [const((1, 128)), const((N_GATES, 1)), const((CHUNK, CHUNK)), const((CHUNK, CHUNK))],
        out_specs=outs,
        out_shape=[jax.ShapeDtypeStruct((ML_BPS, ML_W, ML_GROUP_COLS), F32)] * 2,
        scratch_shapes=[pltpu.VMEM((ML_BPS, 2 * ML_H, 2 * HD, HD), F32), pltpu.VMEM((ML_BPS, 2 * ML_H, 1, 1), F32)],
        compiler_params=_cparams(("parallel", "arbitrary")),
        name="mlstm",
    )(*args, bias, bias_t, tril, triu)


FF_CHUNKS = ((0, 1536), (1536, D_FF))


def _out_ffn_body(att_ref, hl_ref, hc_ref, hf_ref, hb_ref, mo_ref, mnw_ref, bd_ref, wo_ref, x_ref,
                  g1_ref, sh2_ref, sc2_ref, g2_ref, nw2_ref, wg_ref, wu_ref, wd_ref, o_ref):
    i = pl.program_id(0)
    hn = _head_rms((hf_ref[...] + hb_ref[...]).T, bd_ref[...], mnw_ref[...])
    mlo = hn * _sigmoid(mo_ref[...])
    hy = jnp.where(i < NT_LAT, hl_ref[...], hc_ref[...])
    mixed = jnp.concatenate([att_ref[...].astype(BF16), hy.astype(BF16), mlo.astype(BF16)], axis=1)
    x = x_ref[...] + g1_ref[...] * jnp.dot(mixed, wo_ref[...], preferred_element_type=F32)
    hb = _modulated_norm(x, nw2_ref[...], sc2_ref[...], sh2_ref[...]).astype(BF16)
    acc = None
    for lo, hi in FF_CHUNKS:
        sl = slice(lo, hi)
        a = jnp.dot(hb, wg_ref[:, sl], preferred_element_type=F32)
        u = jnp.dot(hb, wu_ref[:, sl], preferred_element_type=F32)
        part = _dot(a * _sigmoid(a) * u, wd_ref[sl, :])
        acc = part if acc is None else acc + part
    o_ref[...] = x + g2_ref[...] * acc


def _out_ffn(att, hyo, hyoc, hf_t, hb_t, mo, ml_nw, w_out_b, xs, mod4, layer, nw2, wg, wu, wd, with_ctx):
    tiles = NT_ALL if with_ctx else NT_LAT
    row = lambda w: pl.BlockSpec((TM, w), lambda i: (i, 0))
    lat_tiles = (NB_BATCH // ML_BPS) * TILES_PER_SEQ
    col = pl.BlockSpec((None, ML_W, TM), lambda i: (jnp.where(i < NT_LAT, i // lat_tiles, i - NT_LAT), 0,
                                                    jnp.where(i < NT_LAT, i % lat_tiles, lat_tiles)))
    resident = lambda shape: pl.BlockSpec(shape, lambda i: (0, 0), pipeline_mode=pl.Buffered(1))
    vec = lambda w: pl.BlockSpec((1, w), lambda i: (0, 0))
    bd = jnp.asarray(_blockdiag_ones(ML_W)).astype(BF16)
    return pl.pallas_call(
        _out_ffn_body,
        grid=(tiles,),
        in_specs=[row(ATT_W),
                  pl.BlockSpec((TM, HY), lambda i: (jnp.minimum(i, NT_LAT - 1), 0)),
                  pl.BlockSpec((TM, HY), lambda i: (jnp.maximum(i - NT_LAT, 0), 0)),
                  col, col, row(ML_W), vec(ML_W), resident((ML_W, ML_W)), resident((D, D)), row(D),
                  _mod_spec(layer, 2), _mod_spec(layer, 3), _mod_spec(layer, 4), _mod_spec(layer, 5), vec(D),
                  resident((D, D_FF)), resident((D, D_FF)), resident((D_FF, D))],
        out_specs=row(D),
        out_shape=jax.ShapeDtypeStruct((tiles * TM, D), F32),
        compiler_params=_cparams(("parallel",), 56),
        name="out_ffn",
    )(att, hyo, hyoc, hf_t, hb_t, mo, ml_nw.reshape(1, ML_W), bd, w_out_b, xs, mod4, mod4, mod4, mod4,
      nw2.reshape(1, D), wg, wu, wd)


def _rope_tables():
    n_rows = SEQ // GRID_W
    row = jnp.repeat(jnp.arange(n_rows), GRID_W)
    col = jnp.tile(jnp.arange(GRID_W), n_rows)
    nf = HD // 4
    inv_freq = ROPE_BASE ** (-jnp.arange(nf, dtype=F32) / nf)
    ang = jnp.stack([row[:, None] * inv_freq, col[:, None] * inv_freq], axis=1)
    cos, sin = jnp.cos(ang), jnp.sin(ang)
    cos_h = jnp.concatenate([cos, cos], axis=-1).reshape(SEQ, HD)
    sin_h = jnp.concatenate([-sin, sin], axis=-1).reshape(SEQ, HD)
    cos_t = jnp.concatenate([jnp.tile(cos_h, (1, 2)), jnp.ones((TM, 128), F32)], axis=0)
    sin_t = jnp.concatenate([jnp.tile(sin_h, (1, 2)), jnp.zeros((TM, 128), F32)], axis=0)
    return cos_t, sin_t


def kernel(x, c, ctx, c_ctx, w_mod, b_mod, norm1_w, norm2_w, w_in, w_out, q_norm_w, k_norm_w, attn_sink, hy_conv_w, hy_conv_b, hy_w1, hy_b1, hy_freq, hy_w2, hy_b2, hy_w3, hy_decay, hy_skip, ml_gate_b, ml_norm_w, ffn_w_gate, ffn_w_up, ffn_w_down):
    xs = jnp.concatenate([x.reshape(T_LAT, D), ctx.reshape(T_CTX, D)], axis=0)
    cc8 = jnp.concatenate([c, c_ctx[None, :], jnp.zeros((8 - NB_BATCH - 1, D), F32)], axis=0)
    mod4 = _modulation(cc8, w_mod, b_mod).reshape(DEPTH, 8, 1, 6 * D)

    cos_t, sin_t = _rope_tables()
    feats_l, feats_c = _pos_feats_circ(SEQ), _pos_feats_circ(CTX)
    s1_tab = jnp.asarray(_TAB["s1"]).astype(BF16)
    s1i_tab = jnp.asarray(_TAB["s1i"]).astype(BF16)
    wf, wi = jnp.asarray(_TAB["wf"]).astype(BF16), jnp.asarray(_TAB["wi"]).astype(BF16)
    fc, fs = jnp.asarray(_TAB["fc"]).astype(BF16), jnp.asarray(_TAB["fs"]).astype(BF16)

    w_in_p = _reorder_w_in(w_in)
    w_out_b = w_out.astype(BF16)
    wg_b, wu_b, wd_b = ffn_w_gate.astype(BF16), ffn_w_up.astype(BF16), ffn_w_down.astype(BF16)
    hpad = 128 - FILT_HID
    w1p = jnp.pad(hy_w1, ((0, 0), (0, 128 - hy_w1.shape[1]), (0, hpad)))
    b1p = jnp.pad(hy_b1, ((0, 0), (0, hpad))).reshape(DEPTH, 1, 128)
    frp = jnp.pad(hy_freq, ((0, 0), (0, hpad))).reshape(DEPTH, 1, 128)
    w2p = jnp.pad(hy_w2, ((0, 0), (0, hpad), (0, hpad)))
    b2p = jnp.pad(hy_b2, ((0, 0), (0, hpad))).reshape(DEPTH, 1, 128)
    w3p = jnp.pad(hy_w3, ((0, 0), (0, hpad), (0, 0)))
    sink8 = jnp.pad(attn_sink, ((0, 0), (0, 128 - ATT_H)))[:, None, :] * jnp.ones((1, 8, 1), F32)
    zeros_tile = jnp.zeros((TM, HY), F32)

    out = None
    for l in range(DEPTH):
        last = l == DEPTH - 1
        att_qkv, hy, mqv_t, mk, mo, mg, mg_t = _in_proj(xs, mod4, l, norm1_w[l], w_in_p[l])

        qt, kn, vt = _qk_prep(att_qkv, cos_t, sin_t, q_norm_w[l], k_norm_w[l])
        att = _attention(qt, kn, vt, sink8[l], not last)

        dec = hy_decay[l].reshape(1, N_FILT)
        circ = _circular_filters(feats_l, w1p[l], b1p[l], frp[l], w2p[l], b2p[l], w3p[l], dec)
        f_re, f_im = _dft_stage1(circ.reshape(2, HALF_A, FFT_B, FFT_T, 2 * HY), s1_tab)
        h_re, h_im = _filter_spectrum(f_re.reshape(2, 2, N_FFT, HY), f_im.reshape(2, 2, N_FFT, HY), wf)

        v, x1, x2 = _short_conv_latent(hy, hy_conv_w[l], hy_conv_b[l])
        view = lambda t: t.reshape(NB_BATCH, HALF_A, FFT_B, FFT_T, HY)
        s_in, gate_a = view(v), view(x1)
        for order in range(2):
            a_re, a_im = _dft_stage1_pair(s_in, s1_tab)
            rows = lambda t: t.reshape(NB_BATCH // 2, N_FFT, HY)
            c_re, c_im = _spectral_filter(rows(a_re), rows(a_im), h_re, h_im, order, wf, wi)
            tiles = lambda t: t.reshape(NB_BATCH // 2, FFT_A, FFT_B, FFT_T, HY)
            s_in = _idft_stage1_gate(tiles(c_re), tiles(c_im), s1i_tab, gate_a, s_in, hy_skip[l, order])
            gate_a = view(x2)
        hyo = s_in.reshape(T_LAT, HY)

        if last:
            hyoc = zeros_tile
        else:
            circ_c = _circular_filters(feats_c, w1p[l], b1p[l], frp[l], w2p[l], b2p[l], w3p[l], dec)
            hyoc = _hyena_ctx(hy, hy_conv_w[l], hy_conv_b[l], circ_c, fc, fs, hy_skip[l])

        hf_t, hb_t = _mlstm(mqv_t, mk, mg, mg_t, ml_gate_b[l])

        xs = _out_ffn(att, hyo, hyoc, hf_t, hb_t, mo, ml_norm_w[l], w_out_b[l], xs, mod4, l,
                      norm2_w[l], wg_b[l], wu_b[l], wd_b[l], not last)
        out = xs
    return out.reshape(NB_BATCH, SEQ, D)
```

```python
import functools
import math

import numpy as np
import jax
import jax.numpy as jnp
from jax import lax
from jax.experimental import pallas as pl
from jax.experimental.pallas import tpu as pltpu

F32 = jnp.float32
BF16 = jnp.bfloat16

D = 1024
NB_BATCH = 4
SEQ = 4096
DEPTH = 4
GRID_W = 64
CTX = 256
T_LAT = NB_BATCH * SEQ
T_CTX = NB_BATCH * CTX
R_ALL = T_LAT + T_CTX

HD = 64
ATT_H = 6
ATT_W = ATT_H * HD
KV_W = 2 * HD
BLK = 128
NBLK = SEQ // BLK
ROPE_BASE = 10000.0

HY = 384
N_FILT = 4 * HY
POS_BANDS = 16
FILT_HID = 64

ML_H = 4
ML_W = 256
CHUNK = 128

D_FF = 2816
P_IN = 2832
P_PAD = 2944
NEG = -1e30
EPS = 1e-6

TM = 512
NT_LAT = T_LAT // TM
NT_ALL = R_ALL // TM
TILES_PER_SEQ = SEQ // TM

N_FFT = 2 * SEQ
FFT_A = 32
FFT_R = 256
FFT_T = 16
FFT_B = FFT_R // FFT_T
HALF_A = FFT_A // 2
S1_TILES = 4
HB_ROWS = 1024
N_CTXF = 2 * CTX


def _np_tables():
    ka = np.arange(FFT_A)[None, :, None, None]
    cc = np.arange(FFT_T)[None, None, :, None]
    aa = np.arange(HALF_A)[None, None, None, :]
    bb = np.arange(FFT_B)[:, None, None, None]
    ph = (ka * (FFT_R * aa + FFT_T * bb + cc)) % N_FFT
    th = 2.0 * np.pi * ph / N_FFT
    eye = np.eye(FFT_T)
    cos = np.einsum("bkca,cd->bkcad", np.cos(th), eye)
    sin = np.einsum("bkca,cd->bkcad", np.sin(th), eye)
    rows, cols = FFT_A * FFT_T, HALF_A * FFT_T
    s1 = np.concatenate([cos.reshape(FFT_B, rows, cols), -sin.reshape(FFT_B, rows, cols)], axis=1)
    s1i = np.concatenate([cos.reshape(FFT_B, rows, cols).transpose(0, 2, 1),
                          -sin.reshape(FFT_B, rows, cols).transpose(0, 2, 1)], axis=2) / N_FFT
    r = np.arange(FFT_R)
    th2 = 2.0 * np.pi * ((r[:, None] * r[None, :]) % FFT_R) / FFT_R
    c2, s2 = np.cos(th2), np.sin(th2)
    wf = np.block([[c2, s2], [-s2, c2]])
    wi = np.block([[c2, -s2], [s2, c2]])
    c = np.arange(N_CTXF)
    th3 = 2.0 * np.pi * ((c[:, None] * c[None, :]) % N_CTXF) / N_CTXF
    return dict(s1=s1.astype(np.float32), s1i=s1i.astype(np.float32), wf=wf.astype(np.float32), wi=wi.astype(np.float32),
                fc=np.cos(th3).astype(np.float32), fs=np.sin(th3).astype(np.float32))


_TAB = _np_tables()


def _blockdiag_ones(width):
    return np.kron(np.eye(width // HD), np.ones((HD, HD))).astype(np.float32)


def _cparams(sem, vmem_mb=48):
    return pltpu.CompilerParams(dimension_semantics=sem, vmem_limit_bytes=vmem_mb * 1024 * 1024)


def _dot(a, b):
    return jnp.dot(a.astype(BF16), b.astype(BF16), preferred_element_type=F32)


def _dot_nt(a, b):
    return lax.dot_general(a.astype(BF16), b.astype(BF16), (((1,), (1,)), ((), ())), preferred_element_type=F32)


def _dot_tn(a, b):
    return lax.dot_general(a.astype(BF16), b.astype(BF16), (((0,), (0,)), ((), ())), preferred_element_type=F32)


def _split3(x):
    x1 = x.astype(BF16)
    r1 = x - x1.astype(F32)
    x2 = r1.astype(BF16)
    x3 = (r1 - x2.astype(F32)).astype(BF16)
    return x1, x2, x3


def _dot_sel_l(sel, x):
    return sum(jnp.dot(sel, p, preferred_element_type=F32) for p in _split3(x))


def _dot_sel_r(x, sel):
    return sum(jnp.dot(p, sel, preferred_element_type=F32) for p in _split3(x))


def _sigmoid(x):
    return 1.0 / (1.0 + jnp.exp(-x))


def _log_sigmoid(x):
    return jnp.minimum(x, 0.0) - jnp.log(1.0 + jnp.exp(-jnp.abs(x)))


def _head_rms(t, bd, w):
    ss = _dot_sel_r(t * t, bd)
    return t * lax.rsqrt(ss * (1.0 / HD) + EPS) * w


def _mod_row(i):
    return jnp.where(i < NT_LAT, i // TILES_PER_SEQ, NB_BATCH)


def _mod_spec(layer, k):
    return pl.BlockSpec((None, None, 1, D), lambda i: (layer, _mod_row(i), 0, k))


def _mod_body(s_ref, w_ref, b_ref, o_ref):
    s = s_ref[...]
    s = s * _sigmoid(s)
    o_ref[...] = _dot(s, w_ref[...]) + b_ref[...]


def _modulation(cc8, w_mod, b_mod):
    nc = 1536
    return pl.pallas_call(
        _mod_body,
        grid=(DEPTH, 6 * D // nc),
        in_specs=[pl.BlockSpec((8, D), lambda l, j: (0, 0)),
                  pl.BlockSpec((None, D, nc), lambda l, j: (l, 0, j)),
                  pl.BlockSpec((None, 1, nc), lambda l, j: (l, 0, j))],
        out_specs=pl.BlockSpec((None, 8, nc), lambda l, j: (l, 0, j)),
        out_shape=jax.ShapeDtypeStruct((DEPTH, 8, 6 * D), F32),
        compiler_params=_cparams(("parallel", "parallel")),
        name="modulation",
    )(cc8, w_mod, b_mod.reshape(DEPTH, 1, 6 * D))


_COL_ATT, _COL_HY, _COL_MQV, _COL_MK, _COL_MO, _COL_MG = (0, 640), (640, 1792), (1792, 2304), (2304, 2560), (2560, 2816), (2816, 2944)
N_GATES = 4 * ML_H


def _reorder_w_in(w_in):
    mq_end, mk_end, mv_end = 2048, 2304, 2560
    w = jnp.concatenate([w_in[:, :, :mq_end], w_in[:, :, mk_end:mv_end], w_in[:, :, mq_end:mk_end], w_in[:, :, mv_end:]], axis=2)
    return jnp.pad(w, ((0, 0), (0, 0), (0, P_PAD - P_IN))).astype(BF16)


def _modulated_norm(x, nw, sc, sh):
    ms = jnp.mean(x * x, axis=-1, keepdims=True)
    return (x * lax.rsqrt(ms + EPS) * nw) * (1.0 + sc) + sh


HALO = 8


def _in_body(x_ref, xp_ref, xn_ref, sh_ref, sc_ref, nw_ref, w_ref, cos_ref, sin_ref, qw_ref, kw_ref, bdq_ref, bdk_ref,
             cw_ref, cb_ref, qt_ref, k_ref, vt_ref, hv_ref, hx1_ref, hx2_ref, mqvt_ref, mk_ref, mo_ref, mg_ref, mgt_ref):
    i = pl.program_id(0)
    nw, sc, sh = nw_ref[...], sc_ref[...], sh_ref[...]
    hb = _modulated_norm(x_ref[...], nw, sc, sh).astype(BF16)
    proj = lambda cols: jnp.dot(hb, w_ref[:, cols[0]:cols[1]], preferred_element_type=F32)

    att = proj(_COL_ATT)
    cos, sin = cos_ref[...], sin_ref[...]
    lane = lax.broadcasted_iota(jnp.int32, (TM, 128), 1)
    first_half = (lane & 31) < 16
    q = _head_rms(att[:, 0:ATT_W], bdq_ref[...], qw_ref[...])
    k = _head_rms(att[:, ATT_W:ATT_W + KV_W], bdk_ref[...], kw_ref[...])
    qt_ref[...] = (_rope(q, cos, sin, first_half) * (HD ** -0.5)).T.astype(BF16)
    k_ref[...] = _rope(k, cos, sin, first_half).astype(BF16)
    vt_ref[...] = att[:, ATT_W + KV_W:ATT_W + 2 * KV_W].T.astype(BF16)

    hy = proj(_COL_HY)
    halo = jnp.concatenate([xp_ref[...], xn_ref[...]], axis=0)
    hy_halo = jnp.dot(_modulated_norm(halo, nw, sc, sh).astype(BF16), w_ref[:, _COL_HY[0]:_COL_HY[1]],
                      preferred_element_type=F32)
    seq_len = jnp.where(i < NT_LAT, SEQ, CTX)
    pos = (i * TM + lax.broadcasted_iota(jnp.int32, (TM, 1), 0)) & (seq_len - 1)
    usc = _short_conv_rows(hy, hy_halo[HALO - 1:HALO, :], hy_halo[HALO:HALO + 1, :], cw_ref[...], cb_ref[...],
                           pos == 0, pos == seq_len - 1)
    hv_ref[...] = usc[:, 0:HY]
    hx1_ref[...] = usc[:, HY:2 * HY]
    hx2_ref[...] = usc[:, 2 * HY:3 * HY]

    mqvt_ref[...] = proj(_COL_MQV).T.astype(BF16)
    mk_ref[...] = (proj(_COL_MK) * (HD ** -0.5)).astype(BF16)
    mo_ref[...] = proj(_COL_MO)
    mg = proj(_COL_MG)
    mg_ref[...] = mg
    mgt_ref[...] = mg.T[0:N_GATES, :]


def _in_proj(xs, mod4, layer, nw, w_in_p, cos_t, sin_t, qw, kw, conv_w, conv_b):
    rows = lambda w: pl.BlockSpec((TM, w), lambda i: (i, 0))
    cols = lambda h: pl.BlockSpec((h, TM), lambda i: (0, i))
    const = lambda shape: pl.BlockSpec(shape, lambda i: (0, 0))
    tab = lambda i: (jnp.where(i < NT_LAT, i % TILES_PER_SEQ, TILES_PER_SEQ), 0)
    per_tile = TM // HALO
    prev = pl.BlockSpec((HALO, D), lambda i: (jnp.maximum(i * per_tile - 1, 0), 0))
    nxt = pl.BlockSpec((HALO, D), lambda i: (jnp.minimum((i + 1) * per_tile, R_ALL // HALO - 1), 0))
    bdq = jnp.asarray(_blockdiag_ones(ATT_W)).astype(BF16)
    bdk = jnp.asarray(_blockdiag_ones(KV_W)).astype(BF16)
    bf = lambda shape: jax.ShapeDtypeStruct(shape, BF16)
    f32 = lambda shape: jax.ShapeDtypeStruct(shape, F32)
    return pl.pallas_call(
        _in_body,
        grid=(NT_ALL,),
        in_specs=[rows(D), prev, nxt, _mod_spec(layer, 0), _mod_spec(layer, 1), const((1, D)),
                  pl.BlockSpec((D, P_PAD), lambda i: (0, 0), pipeline_mode=pl.Buffered(1)),
                  pl.BlockSpec((TM, 128), tab), pl.BlockSpec((TM, 128), tab),
                  const((1, ATT_W)), const((1, KV_W)), const((ATT_W, ATT_W)), const((KV_W, KV_W)),
                  const((3, 3 * HY)), const((1, 3 * HY))],
        out_specs=[cols(ATT_W), rows(KV_W), cols(KV_W), rows(HY), rows(HY), rows(HY),
                   cols(2 * ML_W), rows(ML_W), rows(ML_W), rows(128), cols(N_GATES)],
        out_shape=[bf((ATT_W, R_ALL)), bf((R_ALL, KV_W)), bf((KV_W, R_ALL)), f32((R_ALL, HY)), f32((R_ALL, HY)), f32((R_ALL, HY)),
                   bf((2 * ML_W, R_ALL)), bf((R_ALL, ML_W)), f32((R_ALL, ML_W)), f32((R_ALL, 128)), f32((N_GATES, R_ALL))],
        compiler_params=_cparams(("parallel",), 56),
        name="in_proj",
    )(xs, xs, xs, mod4, mod4, nw.reshape(1, D), w_in_p, cos_t, sin_t,
      jnp.tile(qw, ATT_H).reshape(1, ATT_W), jnp.tile(kw, 2).reshape(1, KV_W), bdq, bdk, conv_w, conv_b.reshape(1, 3 * HY))


def _rope(t, cos, sin_signed, first_half):
    outs = []
    for c in range(t.shape[1] // 128):
        tc = t[:, c * 128:(c + 1) * 128]
        partner = jnp.where(first_half, pltpu.roll(tc, 112, 1), pltpu.roll(tc, 16, 1))
        outs.append(tc * cos + partner * sin_signed)
    return outs[0] if len(outs) == 1 else jnp.concatenate(outs, axis=1)


def _att_heads(qt, kv_list, sink_ref, o_ref):
    nq = qt.shape[1]
    rep = ATT_H // 2
    zeros = jnp.zeros((HD, rep * nq), BF16)
    outs = []
    for g in range(2):
        heads = range(rep * g, rep * (g + 1))
        qg = jnp.concatenate([qt[HD * h:HD * (h + 1), :] for h in heads], axis=1)
        qe = jnp.concatenate([qg, zeros] if g == 0 else [zeros, qg], axis=0)
        sink = jnp.concatenate([jnp.broadcast_to(sink_ref[0:1, h:h + 1], (1, nq)) for h in heads], axis=1)
        scores = []
        m = sink
        for k, _, mask in kv_list:
            s = jnp.dot(k, qe, preferred_element_type=F32)
            if mask is not None:
                s = jnp.where(mask, s, NEG)
            scores.append(s)
            m = jnp.maximum(m, jnp.max(s, axis=0, keepdims=True))
        l = jnp.exp(sink - m)
        acc = None
        for s, (_, vt, _) in zip(scores, kv_list):
            p = jnp.exp(s - m)
            l = l + jnp.sum(p, axis=0, keepdims=True)
            pv = jnp.dot(vt[HD * g:HD * (g + 1), :], p.astype(BF16), preferred_element_type=F32)
            acc = pv if acc is None else acc + pv
        og = acc / l
        outs += [og[:, nq * i:nq * (i + 1)] for i in range(rep)]
    o_ref[...] = jnp.concatenate(outs, axis=0).T


QB = 2
N_ATT_STEPS = NBLK // QB


def _att_body(qt_ref, k0_ref, k1_ref, k2_ref, k3_ref, v0_ref, v1_ref, v2_ref, v3_ref, kc_ref, vc_ref, sink_ref, o_ref):
    j = pl.program_id(1)
    kc, vc = kc_ref[...], vc_ref[...]
    out = lambda sub: o_ref.at[sub * BLK:(sub + 1) * BLK, :]

    @pl.when(j < N_ATT_STEPS)
    def _():
        width = (ATT_H // 2) * BLK
        key = lax.broadcasted_iota(jnp.int32, (BLK, width), 0)
        qry = lax.broadcasted_iota(jnp.int32, (BLK, width), 1) & (BLK - 1)
        ks = (k0_ref, k1_ref, k2_ref, k3_ref)
        vs = (v0_ref, v1_ref, v2_ref, v3_ref)
        for sub in range(QB):
            blk = QB * j + sub
            mask_prev = jnp.logical_and(key >= qry, blk > 0)
            mask_next = jnp.logical_and(key <= qry, blk < NBLK - 1)
            _att_heads(qt_ref[:, sub * BLK:(sub + 1) * BLK],
                       [(ks[sub][...], vs[sub][...], mask_prev), (ks[sub + 1][...], vs[sub + 1][...], None),
                        (ks[sub + 2][...], vs[sub + 2][...], mask_next), (kc, vc, None)], sink_ref, out(sub))

    @pl.when(j >= N_ATT_STEPS)
    def _():
        for sub in range(QB):
            _att_heads(qt_ref[:, sub * BLK:(sub + 1) * BLK], [(kc, vc, None)], sink_ref, out(sub))


def _attention(qt, kn, vt, sink8, with_ctx):
    assert CTX == QB * BLK
    steps = N_ATT_STEPS + (1 if with_ctx else 0)
    rows = R_ALL if with_ctx else T_LAT
    qrows = QB * BLK

    def qblk(b, j):
        return jnp.where(j < N_ATT_STEPS, b * N_ATT_STEPS + j, T_LAT // qrows + b)

    def band(off):
        return lambda b, j: b * NBLK + jnp.clip(QB * j + off, 0, NBLK - 1)

    kspec = lambda f: pl.BlockSpec((BLK, KV_W), lambda b, j: (f(b, j), 0))
    vspec = lambda f: pl.BlockSpec((KV_W, BLK), lambda b, j: (0, f(b, j)))
    offs = range(-1, QB + 1)
    return pl.pallas_call(
        _att_body,
        grid=(NB_BATCH, steps),
        in_specs=[pl.BlockSpec((ATT_W, qrows), lambda b, j: (0, qblk(b, j)))]
                 + [kspec(band(o)) for o in offs] + [vspec(band(o)) for o in offs]
                 + [pl.BlockSpec((CTX, KV_W), lambda b, j: (T_LAT // CTX + b, 0)),
                    pl.BlockSpec((KV_W, CTX), lambda b, j: (0, T_LAT // CTX + b)),
                    pl.BlockSpec((8, 128), lambda b, j: (0, 0))],
        out_specs=pl.BlockSpec((qrows, ATT_W), lambda b, j: (qblk(b, j), 0)),
        out_shape=jax.ShapeDtypeStruct((rows, ATT_W), F32),
        compiler_params=_cparams(("parallel", "parallel")),
        name="attention",
    )(qt, *([kn] * 4), *([vt] * 4), kn, vt, sink8)


def _short_conv_rows(u, prev_row, next_row, w, b, first, last):
    n = u.shape[0]
    row = lax.broadcasted_iota(jnp.int32, u.shape, 0)
    up = jnp.where(row == 0, prev_row, pltpu.roll(u, 1, 0))
    un = jnp.where(row == n - 1, next_row, pltpu.roll(u, n - 1, 0))
    up = jnp.where(first, 0.0, up)
    un = jnp.where(last, 0.0, un)
    return up * w[0:1, :] + u * w[1:2, :] + un * w[2:3, :] + b


def _filt_body(n, tl, f_ref, w1_ref, b1_ref, fr_ref, w2_ref, b2_ref, w3_ref, dec_ref, o_ref):
    f = f_ref[...]
    fr = fr_ref[...]
    z = jnp.sin(fr * (_dot(f, w1_ref[...]) + b1_ref[...]))
    z = jnp.sin(fr * (_dot(z, w2_ref[...]) + b2_ref[...]))
    filt = _dot(z, w3_ref[...]) * jnp.exp(-f[:, 0:1] * jnp.abs(dec_ref[...]))
    row = pl.program_id(0) * tl + lax.broadcasted_iota(jnp.int32, filt.shape, 0)
    o_ref[...] = jnp.where(row == n, 0.0, filt)


def _circular_filters(feats_circ, w1p, b1p, frp, w2p, b2p, w3p, dec):
    n = feats_circ.shape[0] // 2
    tl = min(n, 512)
    const = lambda shape: pl.BlockSpec(shape, lambda i: (0, 0))
    half = lambda rows: pl.BlockSpec((rows, 2 * HY), lambda i: (0, i // (n // tl)))
    return pl.pallas_call(
        functools.partial(_filt_body, n, tl),
        grid=(2 * n // tl,),
        in_specs=[pl.BlockSpec((tl, 128), lambda i: (i, 0)), const((128, 128)), const((1, 128)), const((1, 128)),
                  const((128, 128)), const((1, 128)), half(128), half(1)],
        out_specs=pl.BlockSpec((tl, 2 * HY), lambda i: (i, 0)),
        out_shape=jax.ShapeDtypeStruct((2 * n, 2 * HY), F32),
        compiler_params=_cparams(("parallel",)),
        name="hyena_filters",
    )(feats_circ, w1p, b1p, frp, w2p, b2p, w3p, dec)


def _pos_feats_circ(n):
    t = jnp.linspace(0.0, 1.0, n, dtype=F32)[:, None]
    ang = (2.0 * math.pi / n) * jnp.arange(n, dtype=F32)[:, None]
    bands = jnp.linspace(1e-4, POS_BANDS - 1, POS_BANDS, dtype=F32)[None, :]
    feats = jnp.concatenate([t, jnp.cos(bands * ang), -jnp.sin(bands * ang)], axis=-1)
    feats = jnp.pad(feats, ((0, 0), (0, 128 - feats.shape[1])))
    return jnp.concatenate([feats, feats[:1], jnp.flip(feats[:n - 1], axis=0)], axis=0)


def _s1_body(x_ref, m_ref, are_ref, aim_ref):
    half = FFT_A * FFT_T
    for p in range(S1_TILES):
        xs = x_ref[:, p].reshape(HALF_A * FFT_T, HY)
        r = _dot(m_ref[p], xs)
        are_ref[:, p] = r[:half].reshape(FFT_A, FFT_T, HY).astype(BF16)
        aim_ref[:, p] = r[half:].reshape(FFT_A, FFT_T, HY).astype(BF16)


def _dft_stage1(x5, s1_tab):
    nb, ncb = x5.shape[0], x5.shape[-1] // HY
    out = pl.BlockSpec((None, None, FFT_A, S1_TILES, FFT_T, HY), lambda b, cb, j: (b, cb, 0, j, 0, 0))
    return pl.pallas_call(
        _s1_body,
        grid=(nb, ncb, FFT_B // S1_TILES),
        in_specs=[pl.BlockSpec((None, HALF_A, S1_TILES, FFT_T, HY), lambda b, cb, j: (b, 0, j, 0, cb)),
                  pl.BlockSpec((S1_TILES, 2 * FFT_A * FFT_T, HALF_A * FFT_T), lambda b, cb, j: (j, 0, 0))],
        out_specs=[out, out],
        out_shape=[jax.ShapeDtypeStruct((nb, ncb, FFT_A, FFT_B, FFT_T, HY), BF16)] * 2,
        compiler_params=_cparams(("parallel", "parallel", "parallel")),
        name="dft_stage1",
    )(x5, s1_tab)


def _s1_pair_body(x_ref, m_ref, are_ref, aim_ref):
    half = FFT_A * FFT_T
    for p in range(S1_TILES):
        ra = _dot(m_ref[p], x_ref[0:HALF_A, p].reshape(HALF_A * FFT_T, HY))
        rb = _dot(m_ref[p], x_ref[HALF_A:2 * HALF_A, p].reshape(HALF_A * FFT_T, HY))
        are_ref[:, p] = (ra[:half] - rb[half:]).reshape(FFT_A, FFT_T, HY).astype(BF16)
        aim_ref[:, p] = (ra[half:] + rb[:half]).reshape(FFT_A, FFT_T, HY).astype(BF16)


def _row_blocks(x):
    return x.reshape(x.shape[0] // FFT_R, FFT_B, FFT_T, HY)


_PAIR_SPEC = pl.BlockSpec((2 * HALF_A, S1_TILES, FFT_T, HY), lambda b, j: (b, j, 0, 0))


def _dft_stage1_pair(x4, s1_tab):
    npair = NB_BATCH // 2
    out = pl.BlockSpec((None, FFT_A, S1_TILES, FFT_T, HY), lambda b, j: (b, 0, j, 0, 0))
    return pl.pallas_call(
        _s1_pair_body,
        grid=(npair, FFT_B // S1_TILES),
        in_specs=[_PAIR_SPEC, pl.BlockSpec((S1_TILES, 2 * FFT_A * FFT_T, HALF_A * FFT_T), lambda b, j: (j, 0, 0))],
        out_specs=[out, out],
        out_shape=[jax.ShapeDtypeStruct((npair, FFT_A, FFT_B, FFT_T, HY), BF16)] * 2,
        compiler_params=_cparams(("parallel", "parallel")),
        name="dft_stage1_pair",
    )(x4, s1_tab)


def _block_rows(r):
    return slice(r * FFT_R, (r + 1) * FFT_R)


def _stack_complex(re_ref, im_ref, sl):
    return jnp.concatenate([re_ref[sl, :], im_ref[sl, :]], axis=0)


def _spec_body(are_ref, aim_ref, wf_ref, hre_ref, him_ref):
    wf = wf_ref[...]
    for r in range(HB_ROWS // FFT_R):
        sl = _block_rows(r)
        first = jnp.dot(wf, _stack_complex(are_ref.at[0], aim_ref.at[0], sl), preferred_element_type=F32)
        second = jnp.dot(wf, _stack_complex(are_ref.at[1], aim_ref.at[1], sl), preferred_element_type=F32)
        x = first + second if r % 2 == 0 else first - second
        hre_ref[sl, :] = x[:FFT_R]
        him_ref[sl, :] = x[FFT_R:]


def _filter_spectrum(a_re, a_im, wf):
    assert (HB_ROWS // FFT_R) % 2 == 0
    blk_in = pl.BlockSpec((2, None, HB_ROWS, HY), lambda r, o: (0, o, r, 0))
    blk_out = pl.BlockSpec((None, HB_ROWS, HY), lambda r, o: (o, r, 0))
    return pl.pallas_call(
        _spec_body,
        grid=(N_FFT // HB_ROWS, 2),
        in_specs=[blk_in, blk_in, pl.BlockSpec((2 * FFT_R, 2 * FFT_R), lambda r, o: (0, 0))],
        out_specs=[blk_out, blk_out],
        out_shape=[jax.ShapeDtypeStruct((2, N_FFT, HY), F32)] * 2,
        compiler_params=_cparams(("parallel", "parallel")),
        name="filter_spectrum",
    )(a_re, a_im, wf)


def _s2_body(are_ref, aim_ref, hre_ref, him_ref, wf_ref, wi_ref, cre_ref, cim_ref):
    wf, wi = wf_ref[...], wi_ref[...]
    for r in range(HB_ROWS // FFT_R):
        sl = _block_rows(r)
        x = jnp.dot(wf, _stack_complex(are_ref, aim_ref, sl), preferred_element_type=F32)
        xre, xim = x[:FFT_R], x[FFT_R:]
        hre, him = hre_ref[sl, :], him_ref[sl, :]
        y = jnp.concatenate([(xre * hre - xim * him).astype(BF16), (xre * him + xim * hre).astype(BF16)], axis=0)
        c = jnp.dot(wi, y, preferred_element_type=F32)
        cre_ref[sl, :] = c[:FFT_R].astype(BF16)
        cim_ref[sl, :] = c[FFT_R:].astype(BF16)


def _spectral_filter(a_re, a_im, h_re, h_im, order, wf, wi):
    nb = a_re.shape[0]
    blk = pl.BlockSpec((None, HB_ROWS, HY), lambda r, b: (b, r, 0))
    hblk = pl.BlockSpec((None, HB_ROWS, HY), lambda r, b: (order, r, 0))
    mat = pl.BlockSpec((2 * FFT_R, 2 * FFT_R), lambda r, b: (0, 0))
    return pl.pallas_call(
        _s2_body,
        grid=(N_FFT // HB_ROWS, nb),
        in_specs=[blk, blk, hblk, hblk, mat, mat],
        out_specs=[blk, blk],
        out_shape=[jax.ShapeDtypeStruct((nb, N_FFT, HY), BF16)] * 2,
        compiler_params=_cparams(("parallel", "parallel")),
        name="spectral_filter",
    )(a_re, a_im, h_re, h_im, wf, wi)


def _s1i_body(cre_ref, cim_ref, g_ref, a_ref, b_ref, sk_ref, o_ref):
    rows = FFT_A * FFT_T
    for p in range(S1_TILES):
        cre, cim = cre_ref[:, p].reshape(rows, HY), cim_ref[:, p].reshape(rows, HY)
        parts = (jnp.concatenate([cre, cim], axis=0), jnp.concatenate([cim, -cre], axis=0))
        for i, c in enumerate(parts):
            seq = slice(i * HALF_A, (i + 1) * HALF_A)
            y = jnp.dot(g_ref[p], c, preferred_element_type=F32).reshape(HALF_A, FFT_T, HY)
            o_ref[seq, p] = a_ref[seq, p] * (y + b_ref[seq, p] * sk_ref[...])


def _idft_stage1_gate(c_re, c_im, s1i_tab, a4, b4, skip):
    data = _PAIR_SPEC
    spec = pl.BlockSpec((None, FFT_A, S1_TILES, FFT_T, HY), lambda b, j: (b, 0, j, 0, 0))
    return pl.pallas_call(
        _s1i_body,
        grid=(NB_BATCH // 2, FFT_B // S1_TILES),
        in_specs=[spec, spec, pl.BlockSpec((S1_TILES, HALF_A * FFT_T, 2 * FFT_A * FFT_T), lambda b, j: (j, 0, 0)),
                  data, data, pl.BlockSpec((1, 1, HY), lambda b, j: (0, 0, 0))],
        out_specs=data,
        out_shape=jax.ShapeDtypeStruct((T_LAT // FFT_R, FFT_B, FFT_T, HY), F32),
        compiler_params=_cparams(("parallel", "parallel")),
        name="idft_stage1_gate",
    )(c_re, c_im, s1i_tab, a4, b4, skip.reshape(1, 1, HY))


def _hyc_body(v_ref, x1_ref, x2_ref, circ_ref, fc_ref, fs_ref, sk_ref, o_ref):
    fc, fs = fc_ref[...], fs_ref[...]
    circ = circ_ref[...]
    h_re, h_im = _dot(fc, circ), -_dot(fs, circ)

    def long_conv(s, o):
        sl = slice(o * HY, (o + 1) * HY)
        s_re, s_im = _dot(fc[:, :CTX], s), -_dot(fs[:, :CTX], s)
        hre, him = h_re[:, sl], h_im[:, sl]
        y_re = s_re * hre - s_im * him
        y_im = s_re * him + s_im * hre
        y = (_dot(fc[:CTX, :], y_re) - _dot(fs[:CTX, :], y_im)) * (1.0 / N_CTXF)
        return y + s * sk_ref[o:o + 1, :]

    o_ref[...] = x2_ref[...] * long_conv(x1_ref[...] * long_conv(v_ref[...], 0), 1)


def _hyena_ctx(hv, hx1, hx2, circ_c, fc, fs, skip):
    const = lambda shape: pl.BlockSpec(shape, lambda b: (0, 0))
    seq = pl.BlockSpec((CTX, HY), lambda b: (T_LAT // CTX + b, 0))
    return pl.pallas_call(
        _hyc_body,
        grid=(NB_BATCH,),
        in_specs=[seq, seq, seq, const((N_CTXF, 2 * HY)), const((N_CTXF, N_CTXF)), const((N_CTXF, N_CTXF)), const((2, HY))],
        out_specs=pl.BlockSpec((CTX, HY), lambda b: (b, 0)),
        out_shape=jax.ShapeDtypeStruct((T_CTX, HY), F32),
        compiler_params=_cparams(("parallel",)),
        name="hyena_ctx",
    )(hv, hx1, hx2, circ_c, fc, fs, skip)


N_ML_STEPS = CTX // CHUNK + SEQ // CHUNK


def _ml_chain(direction, qv_ref, k_ref, src_col, cum_r, gates_t, c_scr, m_scr, mask, ones_rows):
    base = 8 * direction
    outs = []
    for h in range(ML_H):
        ic, fc = base + h, base + 4 + h
        chain = direction * ML_H + h
        b_row, li_row = cum_r[fc:fc + 1, :], gates_t[ic:ic + 1, :]
        b_end = b_row[:, CHUNK - 1:CHUNK] if direction == 0 else b_row[:, 0:1]
        q_t = qv_ref[HD * h:HD * (h + 1), :]
        vext_t = jnp.concatenate([qv_ref[ML_W + HD * h:ML_W + HD * (h + 1), :], ones_rows], axis=0)
        k = k_ref[:, HD * h:HD * (h + 1)]
        c_prev, m_prev = c_scr[chain], m_scr[chain]
        dmat = jnp.where(mask, src_col[:, fc:fc + 1] + b_row, NEG)
        m_intra = jnp.max(dmat, axis=0, keepdims=True)
        s_t = jnp.dot(k, q_t, preferred_element_type=F32) * jnp.exp(dmat - m_intra)
        inter = b_row + m_prev
        m_t = jnp.maximum(inter, m_intra)
        hx = jnp.exp(m_intra - m_t) * _dot(vext_t, s_t) + jnp.exp(inter - m_t) * _dot(c_prev, q_t)
        den = jnp.maximum(jnp.abs(hx[HD:HD + 1, :]), jnp.exp(-m_t))
        outs.append(hx[:HD, :] / den)
        g_row = b_end - b_row + li_row
        m_new = jnp.maximum(b_end + m_prev, jnp.max(g_row, axis=1, keepdims=True))
        c_scr[chain] = jnp.exp(b_end + m_prev - m_new) * c_prev + _dot(vext_t * jnp.exp(g_row - m_new), k)
        m_scr[chain] = m_new
    return jnp.concatenate(outs, axis=0)


ML_BPS = 2
ML_GROUP_COLS = (NB_BATCH // ML_BPS) * (SEQ + CTX)


def _ml_body(*refs):
    n_in = 8 * ML_BPS
    seq_refs, (bias_ref, bias_t_ref, tril_ref, triu_ref) = refs[:n_in], refs[n_in:n_in + 4]
    (hf_ref, hb_ref), (c_scr, m_scr) = refs[n_in + 4:n_in + 6], refs[n_in + 6:]

    @pl.when(pl.program_id(1) == 0)
    def _():
        c_scr[...] = jnp.zeros_like(c_scr)
        m_scr[...] = jnp.zeros_like(m_scr)

    tril, triu = tril_ref[...], triu_ref[...]
    src = lax.broadcasted_iota(jnp.int32, (CHUNK, CHUNK), 0)
    dst = lax.broadcasted_iota(jnp.int32, (CHUNK, CHUNK), 1)
    ones_rows = (lax.broadcasted_iota(jnp.int32, (HD, CHUNK), 0) == 0).astype(BF16)
    for sub in range(ML_BPS):
        c_sub, m_sub = c_scr.at[sub], m_scr.at[sub]
        for direction in range(2):
            qv_ref, k_ref, g_ref, gt_ref = seq_refs[8 * sub + 4 * direction:8 * sub + 4 * direction + 4]
            gates = g_ref[...] + bias_ref[...]
            gates_t = gt_ref[...] + bias_t_ref[...]
            ls, ls_t = _log_sigmoid(gates), _log_sigmoid(gates_t)
            if direction == 0:
                cum_c, cum_r, mask = _dot_sel_l(tril, ls), _dot_sel_r(ls_t, triu), src <= dst
            else:
                cum_c, cum_r, mask = _dot_sel_l(triu, ls), _dot_sel_r(ls_t, tril), src >= dst
            src_col = pltpu.roll(gates, 4, 1) - cum_c
            o_ref = hf_ref if direction == 0 else hb_ref
            o_ref[sub] = _ml_chain(direction, qv_ref, k_ref, src_col, cum_r, gates_t, c_sub, m_sub, mask, ones_rows)


def _mlstm(mqv_t, mk, mg, mg_t, gate_b):
    nctx = CTX // CHUNK
    nlat = SEQ // CHUNK
    per_group = NB_BATCH // ML_BPS

    def step_chunk(i, backward):
        ctx_chunk = (nctx - 1 - i) if backward else i
        lat_chunk = (N_ML_STEPS - 1 - i) if backward else (i - nctx)
        return i < nctx, ctx_chunk, lat_chunk

    def in_chunk(sub, backward):
        def f(p, i):
            b = p + per_group * sub
            is_ctx, cc, lc = step_chunk(i, backward)
            return jnp.where(is_ctx, T_LAT // CHUNK + nctx * b + cc, nlat * b + lc)
        return f

    def out_chunk(backward):
        def f(p, i):
            is_ctx, cc, lc = step_chunk(i, backward)
            return jnp.where(is_ctx, per_group * nlat + nctx * p + cc, nlat * p + lc)
        return f

    bias = jnp.pad(gate_b, (0, 128 - N_GATES)).reshape(1, 128)
    bias_t = gate_b.reshape(N_GATES, 1)
    tril = jnp.asarray(np.tril(np.ones((CHUNK, CHUNK), np.float32))).astype(BF16)
    triu = jnp.asarray(np.triu(np.ones((CHUNK, CHUNK), np.float32))).astype(BF16)
    const = lambda shape: pl.BlockSpec(shape, lambda p, i: (0, 0))
    ins, args = [], []
    for sub in range(ML_BPS):
        for backward in (False, True):
            f = in_chunk(sub, backward)
            ins += [pl.BlockSpec((2 * ML_W, CHUNK), lambda p, i, f=f: (0, f(p, i))),
                    pl.BlockSpec((CHUNK, ML_W), lambda p, i, f=f: (f(p, i), 0)),
                    pl.BlockSpec((CHUNK, 128), lambda p, i, f=f: (f(p, i), 0)),
                    pl.BlockSpec((N_GATES, CHUNK), lambda p, i, f=f: (0, f(p, i)))]
            args += [mqv_t, mk, mg, mg_t]
    outs = [pl.BlockSpec((ML_BPS, ML_W, CHUNK), lambda p, i, f=out_chunk(bw): (0, 0, f(p, i))) for bw in (False, True)]
    return pl.pallas_call(
        _ml_body,
        grid=(per_group, N_ML_STEPS),
        in_specs=ins + [const((1, 128)), const((N_GATES, 1)), const((CHUNK, CHUNK)), const((CHUNK, CHUNK))],
        out_specs=outs,
        out_shape=[jax.ShapeDtypeStruct((ML_BPS, ML_W, ML_GROUP_COLS), F32)] * 2,
        scratch_shapes=[pltpu.VMEM((ML_BPS, 2 * ML_H, 2 * HD, HD), F32), pltpu.VMEM((ML_BPS, 2 * ML_H, 1, 1), F32)],
        compiler_params=_cparams(("parallel", "arbitrary")),
        name="mlstm",
    )(*args, bias, bias_t, tril, triu)


FF_CHUNKS = ((0, 1536), (1536, D_FF))


def _out_ffn_body(att_ref, hl_ref, hc_ref, hf_ref, hb_ref, mo_ref, mnw_ref, bd_ref, wo_ref, x_ref,
                  g1_ref, sh2_ref, sc2_ref, g2_ref, nw2_ref, wg_ref, wu_ref, wd_ref, o_ref):
    i = pl.program_id(0)
    hn = _head_rms((hf_ref[...] + hb_ref[...]).T, bd_ref[...], mnw_ref[...])
    mlo = hn * _sigmoid(mo_ref[...])
    hy = jnp.where(i < NT_LAT, hl_ref[...], hc_ref[...])
    mixed = jnp.concatenate([att_ref[...].astype(BF16), hy.astype(BF16), mlo.astype(BF16)], axis=1)
    x = x_ref[...] + g1_ref[...] * jnp.dot(mixed, wo_ref[...], preferred_element_type=F32)
    hb = _modulated_norm(x, nw2_ref[...], sc2_ref[...], sh2_ref[...]).astype(BF16)
    acc = None
    for lo, hi in FF_CHUNKS:
        sl = slice(lo, hi)
        a = jnp.dot(hb, wg_ref[:, sl], preferred_element_type=F32)
        u = jnp.dot(hb, wu_ref[:, sl], preferred_element_type=F32)
        part = _dot(a * _sigmoid(a) * u, wd_ref[sl, :])
        acc = part if acc is None else acc + part
    o_ref[...] = x + g2_ref[...] * acc


def _out_ffn(att, hyo, hyoc, hf_t, hb_t, mo, ml_nw, w_out_b, xs, mod4, layer, nw2, wg, wu, wd, with_ctx):
    tiles = NT_ALL if with_ctx else NT_LAT
    row = lambda w: pl.BlockSpec((TM, w), lambda i: (i, 0))
    lat_tiles = (NB_BATCH // ML_BPS) * TILES_PER_SEQ
    col = pl.BlockSpec((None, ML_W, TM), lambda i: (jnp.where(i < NT_LAT, i // lat_tiles, i - NT_LAT), 0,
                                                    jnp.where(i < NT_LAT, i % lat_tiles, lat_tiles)))
    resident = lambda shape: pl.BlockSpec(shape, lambda i: (0, 0), pipeline_mode=pl.Buffered(1))
    vec = lambda w: pl.BlockSpec((1, w), lambda i: (0, 0))
    bd = jnp.asarray(_blockdiag_ones(ML_W)).astype(BF16)
    return pl.pallas_call(
        _out_ffn_body,
        grid=(tiles,),
        in_specs=[row(ATT_W),
                  pl.BlockSpec((TM, HY), lambda i: (jnp.minimum(i, NT_LAT - 1), 0)),
                  pl.BlockSpec((TM, HY), lambda i: (jnp.maximum(i - NT_LAT, 0), 0)),
                  col, col, row(ML_W), vec(ML_W), resident((ML_W, ML_W)), resident((D, D)), row(D),
                  _mod_spec(layer, 2), _mod_spec(layer, 3), _mod_spec(layer, 4), _mod_spec(layer, 5), vec(D),
                  resident((D, D_FF)), resident((D, D_FF)), resident((D_FF, D))],
        out_specs=row(D),
        out_shape=jax.ShapeDtypeStruct((tiles * TM, D), F32),
        compiler_params=_cparams(("parallel",), 56),
        name="out_ffn",
    )(att, hyo, hyoc, hf_t, hb_t, mo, ml_nw.reshape(1, ML_W), bd, w_out_b, xs, mod4, mod4, mod4, mod4,
      nw2.reshape(1, D), wg, wu, wd)


def _rope_tables():
    n_rows = SEQ // GRID_W
    row = jnp.repeat(jnp.arange(n_rows), GRID_W)
    col = jnp.tile(jnp.arange(GRID_W), n_rows)
    nf = HD // 4
    inv_freq = ROPE_BASE ** (-jnp.arange(nf, dtype=F32) / nf)
    ang = jnp.stack([row[:, None] * inv_freq, col[:, None] * inv_freq], axis=1)
    cos, sin = jnp.cos(ang), jnp.sin(ang)
    cos_h = jnp.concatenate([cos, cos], axis=-1).reshape(SEQ, HD)
    sin_h = jnp.concatenate([-sin, sin], axis=-1).reshape(SEQ, HD)
    cos_t = jnp.concatenate([jnp.tile(cos_h, (1, 2)), jnp.ones((TM, 128), F32)], axis=0)
    sin_t = jnp.concatenate([jnp.tile(sin_h, (1, 2)), jnp.zeros((TM, 128), F32)], axis=0)
    return cos_t, sin_t


def kernel(x, c, ctx, c_ctx, w_mod, b_mod, norm1_w, norm2_w, w_in, w_out, q_norm_w, k_norm_w, attn_sink, hy_conv_w, hy_conv_b, hy_w1, hy_b1, hy_freq, hy_w2, hy_b2, hy_w3, hy_decay, hy_skip, ml_gate_b, ml_norm_w, ffn_w_gate, ffn_w_up, ffn_w_down):
    xs = jnp.concatenate([x.reshape(T_LAT, D), ctx.reshape(T_CTX, D)], axis=0)
    cc8 = jnp.concatenate([c, c_ctx[None, :], jnp.zeros((8 - NB_BATCH - 1, D), F32)], axis=0)
    mod4 = _modulation(cc8, w_mod, b_mod).reshape(DEPTH, 8, 1, 6 * D)

    cos_t, sin_t = _rope_tables()
    feats_l, feats_c = _pos_feats_circ(SEQ), _pos_feats_circ(CTX)
    s1_tab = jnp.asarray(_TAB["s1"]).astype(BF16)
    s1i_tab = jnp.asarray(_TAB["s1i"]).astype(BF16)
    wf, wi = jnp.asarray(_TAB["wf"]).astype(BF16), jnp.asarray(_TAB["wi"]).astype(BF16)
    fc, fs = jnp.asarray(_TAB["fc"]).astype(BF16), jnp.asarray(_TAB["fs"]).astype(BF16)

    w_in_p = _reorder_w_in(w_in)
    w_out_b = w_out.astype(BF16)
    wg_b, wu_b, wd_b = ffn_w_gate.astype(BF16), ffn_w_up.astype(BF16), ffn_w_down.astype(BF16)
    hpad = 128 - FILT_HID
    w1p = jnp.pad(hy_w1, ((0, 0), (0, 128 - hy_w1.shape[1]), (0, hpad)))
    b1p = jnp.pad(hy_b1, ((0, 0), (0, hpad))).reshape(DEPTH, 1, 128)
    frp = jnp.pad(hy_freq, ((0, 0), (0, hpad))).reshape(DEPTH, 1, 128)
    w2p = jnp.pad(hy_w2, ((0, 0), (0, hpad), (0, hpad)))
    b2p = jnp.pad(hy_b2, ((0, 0), (0, hpad))).reshape(DEPTH, 1, 128)
    w3p = jnp.pad(hy_w3, ((0, 0), (0, hpad), (0, 0)))
    sink8 = jnp.pad(attn_sink, ((0, 0), (0, 128 - ATT_H)))[:, None, :] * jnp.ones((1, 8, 1), F32)
    zeros_tile = jnp.zeros((TM, HY), F32)

    out = None
    for l in range(DEPTH):
        last = l == DEPTH - 1
        qt, kn, vt, hv, hx1, hx2, mqv_t, mk, mo, mg, mg_t = _in_proj(
            xs, mod4, l, norm1_w[l], w_in_p[l], cos_t, sin_t, q_norm_w[l], k_norm_w[l], hy_conv_w[l], hy_conv_b[l])

        att = _attention(qt, kn, vt, sink8[l], not last)

        dec = hy_decay[l].reshape(1, N_FILT)
        circ = _circular_filters(feats_l, w1p[l], b1p[l], frp[l], w2p[l], b2p[l], w3p[l], dec)
        f_re, f_im = _dft_stage1(circ.reshape(2, HALF_A, FFT_B, FFT_T, 2 * HY), s1_tab)
        h_re, h_im = _filter_spectrum(f_re.reshape(2, 2, N_FFT, HY), f_im.reshape(2, 2, N_FFT, HY), wf)

        s_in, gate_a = _row_blocks(hv), _row_blocks(hx1)
        for order in range(2):
            a_re, a_im = _dft_stage1_pair(s_in, s1_tab)
            rows = lambda t: t.reshape(NB_BATCH // 2, N_FFT, HY)
            c_re, c_im = _spectral_filter(rows(a_re), rows(a_im), h_re, h_im, order, wf, wi)
            tiles = lambda t: t.reshape(NB_BATCH // 2, FFT_A, FFT_B, FFT_T, HY)
            s_in = _idft_stage1_gate(tiles(c_re), tiles(c_im), s1i_tab, gate_a, s_in, hy_skip[l, order])
            gate_a = _row_blocks(hx2)
        hyo = s_in.reshape(T_LAT, HY)

        if last:
            hyoc = zeros_tile
        else:
            circ_c = _circular_filters(feats_c, w1p[l], b1p[l], frp[l], w2p[l], b2p[l], w3p[l], dec)
            hyoc = _hyena_ctx(hv, hx1, hx2, circ_c, fc, fs, hy_skip[l])

        hf_t, hb_t = _mlstm(mqv_t, mk, mg, mg_t, ml_gate_b[l])

        xs = _out_ffn(att, hyo, hyoc, hf_t, hb_t, mo, ml_norm_w[l], w_out_b[l], xs, mod4, l,
                      norm2_w[l], wg_b[l], wu_b[l], wd_b[l], not last)
        out = xs
    return out.reshape(NB_BATCH, SEQ, D)
```

```python
import functools
import math

import numpy as np
import jax
import jax.numpy as jnp
from jax import lax
from jax.experimental import pallas as pl
from jax.experimental.pallas import tpu as pltpu

F32 = jnp.float32
BF16 = jnp.bfloat16

D = 1024
NB_BATCH = 4
SEQ = 4096
DEPTH = 4
GRID_W = 64
CTX = 256
T_LAT = NB_BATCH * SEQ
T_CTX = NB_BATCH * CTX
R_ALL = T_LAT + T_CTX

HD = 64
ATT_H = 6
ATT_W = ATT_H * HD
KV_W = 2 * HD
BLK = 128
NBLK = SEQ // BLK
ROPE_BASE = 10000.0

HY = 384
N_FILT = 4 * HY
POS_BANDS = 16
FILT_HID = 64

ML_H = 4
ML_W = 256
CHUNK = 128

D_FF = 2816
P_IN = 2832
P_PAD = 2944
NEG = -1e30
EPS = 1e-6

TM = 512
NT_LAT = T_LAT // TM
NT_ALL = R_ALL // TM
TILES_PER_SEQ = SEQ // TM

N_FFT = 2 * SEQ
FFT_A = 32
FFT_R = 256
FFT_T = 16
FFT_B = FFT_R // FFT_T
HALF_A = FFT_A // 2
S1_TILES = 4
HB_ROWS = 1024
N_CTXF = 2 * CTX


def _np_tables():
    ka = np.arange(FFT_A)[None, :, None, None]
    cc = np.arange(FFT_T)[None, None, :, None]
    aa = np.arange(HALF_A)[None, None, None, :]
    bb = np.arange(FFT_B)[:, None, None, None]
    ph = (ka * (FFT_R * aa + FFT_T * bb + cc)) % N_FFT
    th = 2.0 * np.pi * ph / N_FFT
    eye = np.eye(FFT_T)
    cos = np.einsum("bkca,cd->bkcad", np.cos(th), eye)
    sin = np.einsum("bkca,cd->bkcad", np.sin(th), eye)
    rows, cols = FFT_A * FFT_T, HALF_A * FFT_T
    s1 = np.concatenate([cos.reshape(FFT_B, rows, cols), -sin.reshape(FFT_B, rows, cols)], axis=1)
    s1i = np.concatenate([cos.reshape(FFT_B, rows, cols).transpose(0, 2, 1),
                          -sin.reshape(FFT_B, rows, cols).transpose(0, 2, 1)], axis=2) / N_FFT
    r = np.arange(FFT_R)
    th2 = 2.0 * np.pi * ((r[:, None] * r[None, :]) % FFT_R) / FFT_R
    c2, s2 = np.cos(th2), np.sin(th2)
    wf = np.block([[c2, s2], [-s2, c2]])
    wi = np.block([[c2, -s2], [s2, c2]])
    c = np.arange(N_CTXF)
    th3 = 2.0 * np.pi * ((c[:, None] * c[None, :]) % N_CTXF) / N_CTXF
    return dict(s1=s1.astype(np.float32), s1i=s1i.astype(np.float32), wf=wf.astype(np.float32), wi=wi.astype(np.float32),
                fc=np.cos(th3).astype(np.float32), fs=np.sin(th3).astype(np.float32))


_TAB = _np_tables()


def _blockdiag_ones(width):
    return np.kron(np.eye(width // HD), np.ones((HD, HD))).astype(np.float32)


def _cparams(sem, vmem_mb=48):
    return pltpu.CompilerParams(dimension_semantics=sem, vmem_limit_bytes=vmem_mb * 1024 * 1024)


def _dot(a, b):
    return jnp.dot(a.astype(BF16), b.astype(BF16), preferred_element_type=F32)


def _dot_nt(a, b):
    return lax.dot_general(a.astype(BF16), b.astype(BF16), (((1,), (1,)), ((), ())), preferred_element_type=F32)


def _dot_tn(a, b):
    return lax.dot_general(a.astype(BF16), b.astype(BF16), (((0,), (0,)), ((), ())), preferred_element_type=F32)


def _split3(x):
    x1 = x.astype(BF16)
    r1 = x - x1.astype(F32)
    x2 = r1.astype(BF16)
    x3 = (r1 - x2.astype(F32)).astype(BF16)
    return x1, x2, x3


def _dot_sel_l(sel, x):
    return sum(jnp.dot(sel, p, preferred_element_type=F32) for p in _split3(x))


def _dot_sel_r(x, sel):
    return sum(jnp.dot(p, sel, preferred_element_type=F32) for p in _split3(x))


def _sigmoid(x):
    return 1.0 / (1.0 + jnp.exp(-x))


def _log_sigmoid(x):
    return jnp.minimum(x, 0.0) - jnp.log(1.0 + jnp.exp(-jnp.abs(x)))


def _head_rms(t, bd, w):
    ss = _dot_sel_r(t * t, bd)
    return t * lax.rsqrt(ss * (1.0 / HD) + EPS) * w


def _mod_row(i):
    return jnp.where(i < NT_LAT, i // TILES_PER_SEQ, NB_BATCH)


def _mod_spec(layer, k):
    return pl.BlockSpec((None, None, 1, D), lambda i: (layer, _mod_row(i), 0, k))


def _mod_body(s_ref, w_ref, b_ref, o_ref):
    s = s_ref[...]
    s = s * _sigmoid(s)
    o_ref[...] = _dot(s, w_ref[...]) + b_ref[...]


def _modulation(cc8, w_mod, b_mod):
    nc = 1536
    return pl.pallas_call(
        _mod_body,
        grid=(DEPTH, 6 * D // nc),
        in_specs=[pl.BlockSpec((8, D), lambda l, j: (0, 0)),
                  pl.BlockSpec((None, D, nc), lambda l, j: (l, 0, j)),
                  pl.BlockSpec((None, 1, nc), lambda l, j: (l, 0, j))],
        out_specs=pl.BlockSpec((None, 8, nc), lambda l, j: (l, 0, j)),
        out_shape=jax.ShapeDtypeStruct((DEPTH, 8, 6 * D), F32),
        compiler_params=_cparams(("parallel", "parallel")),
        name="modulation",
    )(cc8, w_mod, b_mod.reshape(DEPTH, 1, 6 * D))


_COL_ATT, _COL_HY, _COL_MQV, _COL_MK, _COL_MO, _COL_MG = (0, 640), (640, 1792), (1792, 2304), (2304, 2560), (2560, 2816), (2816, 2944)
N_GATES = 4 * ML_H


def _reorder_w_in(w_in):
    mq_end, mk_end, mv_end = 2048, 2304, 2560
    w = jnp.concatenate([w_in[:, :, :mq_end], w_in[:, :, mk_end:mv_end], w_in[:, :, mq_end:mk_end], w_in[:, :, mv_end:]], axis=2)
    return jnp.pad(w, ((0, 0), (0, 0), (0, P_PAD - P_IN))).astype(BF16)


def _modulated_norm(x, nw, sc, sh):
    ms = jnp.mean(x * x, axis=-1, keepdims=True)
    return (x * lax.rsqrt(ms + EPS) * nw) * (1.0 + sc) + sh


HALO = 8


def _in_body(x_ref, xp_ref, xn_ref, sh_ref, sc_ref, nw_ref, w_ref, cos_ref, sin_ref, qw_ref, kw_ref, bdq_ref, bdk_ref,
             cw_ref, cb_ref, qt_ref, k_ref, vt_ref, hv_ref, hx1_ref, hx2_ref, mqvt_ref, mk_ref, mo_ref, mg_ref, mgt_ref):
    i = pl.program_id(0)
    nw, sc, sh = nw_ref[...], sc_ref[...], sh_ref[...]
    hb = _modulated_norm(x_ref[...], nw, sc, sh).astype(BF16)
    proj = lambda cols: jnp.dot(hb, w_ref[:, cols[0]:cols[1]], preferred_element_type=F32)

    att = proj(_COL_ATT)
    cos, sin = cos_ref[...], sin_ref[...]
    lane = lax.broadcasted_iota(jnp.int32, (TM, 128), 1)
    first_half = (lane & 31) < 16
    q = _head_rms(att[:, 0:ATT_W], bdq_ref[...], qw_ref[...])
    k = _head_rms(att[:, ATT_W:ATT_W + KV_W], bdk_ref[...], kw_ref[...])
    qt_ref[...] = (_rope(q, cos, sin, first_half) * (HD ** -0.5)).T.astype(BF16)
    k_ref[...] = _rope(k, cos, sin, first_half).astype(BF16)
    vt_ref[...] = att[:, ATT_W + KV_W:ATT_W + 2 * KV_W].T.astype(BF16)

    hy = proj(_COL_HY)
    halo = jnp.concatenate([xp_ref[...], xn_ref[...]], axis=0)
    hy_halo = jnp.dot(_modulated_norm(halo, nw, sc, sh).astype(BF16), w_ref[:, _COL_HY[0]:_COL_HY[1]],
                      preferred_element_type=F32)
    seq_len = jnp.where(i < NT_LAT, SEQ, CTX)
    pos = (i * TM + lax.broadcasted_iota(jnp.int32, (TM, 1), 0)) & (seq_len - 1)
    usc = _short_conv_rows(hy, hy_halo[HALO - 1:HALO, :], hy_halo[HALO:HALO + 1, :], cw_ref[...], cb_ref[...],
                           pos == 0, pos == seq_len - 1)
    hv_ref[...] = usc[:, 0:HY]
    hx1_ref[...] = usc[:, HY:2 * HY]
    hx2_ref[...] = usc[:, 2 * HY:3 * HY]

    mqvt_ref[...] = proj(_COL_MQV).T.astype(BF16)
    mk_ref[...] = (proj(_COL_MK) * (HD ** -0.5)).astype(BF16)
    mo_ref[...] = proj(_COL_MO)
    mg = proj(_COL_MG)
    mg_ref[...] = mg
    mgt_ref[...] = mg.T[0:N_GATES, :]


def _in_proj(xs, mod4, layer, nw, w_in_p, cos_t, sin_t, qw, kw, conv_w, conv_b):
    rows = lambda w: pl.BlockSpec((TM, w), lambda i: (i, 0))
    cols = lambda h: pl.BlockSpec((h, TM), lambda i: (0, i))
    const = lambda shape: pl.BlockSpec(shape, lambda i: (0, 0))
    tab = lambda i: (jnp.where(i < NT_LAT, i % TILES_PER_SEQ, TILES_PER_SEQ), 0)
    per_tile = TM // HALO
    prev = pl.BlockSpec((HALO, D), lambda i: (jnp.maximum(i * per_tile - 1, 0), 0))
    nxt = pl.BlockSpec((HALO, D), lambda i: (jnp.minimum((i + 1) * per_tile, R_ALL // HALO - 1), 0))
    bdq = jnp.asarray(_blockdiag_ones(ATT_W)).astype(BF16)
    bdk = jnp.asarray(_blockdiag_ones(KV_W)).astype(BF16)
    bf = lambda shape: jax.ShapeDtypeStruct(shape, BF16)
    f32 = lambda shape: jax.ShapeDtypeStruct(shape, F32)
    return pl.pallas_call(
        _in_body,
        grid=(NT_ALL,),
        in_specs=[rows(D), prev, nxt, _mod_spec(layer, 0), _mod_spec(layer, 1), const((1, D)),
                  pl.BlockSpec((D, P_PAD), lambda i: (0, 0), pipeline_mode=pl.Buffered(1)),
                  pl.BlockSpec((TM, 128), tab), pl.BlockSpec((TM, 128), tab),
                  const((1, ATT_W)), const((1, KV_W)), const((ATT_W, ATT_W)), const((KV_W, KV_W)),
                  const((3, 3 * HY)), const((1, 3 * HY))],
        out_specs=[cols(ATT_W), rows(KV_W), cols(KV_W), rows(HY), rows(HY), rows(HY),
                   cols(2 * ML_W), rows(ML_W), rows(ML_W), rows(128), cols(N_GATES)],
        out_shape=[bf((ATT_W, R_ALL)), bf((R_ALL, KV_W)), bf((KV_W, R_ALL)), f32((R_ALL, HY)), f32((R_ALL, HY)), f32((R_ALL, HY)),
                   bf((2 * ML_W, R_ALL)), bf((R_ALL, ML_W)), f32((R_ALL, ML_W)), f32((R_ALL, 128)), f32((N_GATES, R_ALL))],
        compiler_params=_cparams(("parallel",), 56),
        name="in_proj",
    )(xs, xs, xs, mod4, mod4, nw.reshape(1, D), w_in_p, cos_t, sin_t,
      jnp.tile(qw, ATT_H).reshape(1, ATT_W), jnp.tile(kw, 2).reshape(1, KV_W), bdq, bdk, conv_w, conv_b.reshape(1, 3 * HY))


def _rope(t, cos, sin_signed, first_half):
    outs = []
    for c in range(t.shape[1] // 128):
        tc = t[:, c * 128:(c + 1) * 128]
        partner = jnp.where(first_half, pltpu.roll(tc, 112, 1), pltpu.roll(tc, 16, 1))
        outs.append(tc * cos + partner * sin_signed)
    return outs[0] if len(outs) == 1 else jnp.concatenate(outs, axis=1)


def _att_heads(qt, kv_list, sink_ref, o_ref):
    nq = qt.shape[1]
    rep = ATT_H // 2
    zeros = jnp.zeros((HD, rep * nq), BF16)
    outs = []
    for g in range(2):
        heads = range(rep * g, rep * (g + 1))
        qg = jnp.concatenate([qt[HD * h:HD * (h + 1), :] for h in heads], axis=1)
        qe = jnp.concatenate([qg, zeros] if g == 0 else [zeros, qg], axis=0)
        sink = jnp.concatenate([jnp.broadcast_to(sink_ref[0:1, h:h + 1], (1, nq)) for h in heads], axis=1)
        scores = []
        m = sink
        for k, _, mask in kv_list:
            s = jnp.dot(k, qe, preferred_element_type=F32)
            if mask is not None:
                s = jnp.where(mask, s, NEG)
            scores.append(s)
            m = jnp.maximum(m, jnp.max(s, axis=0, keepdims=True))
        l = jnp.exp(sink - m)
        acc = None
        for s, (_, vt, _) in zip(scores, kv_list):
            p = jnp.exp(s - m)
            l = l + jnp.sum(p, axis=0, keepdims=True)
            pv = jnp.dot(vt[HD * g:HD * (g + 1), :], p.astype(BF16), preferred_element_type=F32)
            acc = pv if acc is None else acc + pv
        og = acc / l
        outs += [og[:, nq * i:nq * (i + 1)] for i in range(rep)]
    o_ref[...] = jnp.concatenate(outs, axis=0).T


QB = 2
N_ATT_STEPS = NBLK // QB


def _att_body(qt_ref, k0_ref, k1_ref, k2_ref, k3_ref, v0_ref, v1_ref, v2_ref, v3_ref, kc_ref, vc_ref, sink_ref, o_ref):
    j = pl.program_id(1)
    kc, vc = kc_ref[...], vc_ref[...]
    out = lambda sub: o_ref.at[sub * BLK:(sub + 1) * BLK, :]

    @pl.when(j < N_ATT_STEPS)
    def _():
        width = (ATT_H // 2) * BLK
        key = lax.broadcasted_iota(jnp.int32, (BLK, width), 0)
        qry = lax.broadcasted_iota(jnp.int32, (BLK, width), 1) & (BLK - 1)
        ks = (k0_ref, k1_ref, k2_ref, k3_ref)
        vs = (v0_ref, v1_ref, v2_ref, v3_ref)
        for sub in range(QB):
            blk = QB * j + sub
            mask_prev = jnp.logical_and(key >= qry, blk > 0)
            mask_next = jnp.logical_and(key <= qry, blk < NBLK - 1)
            _att_heads(qt_ref[:, sub * BLK:(sub + 1) * BLK],
                       [(ks[sub][...], vs[sub][...], mask_prev), (ks[sub + 1][...], vs[sub + 1][...], None),
                        (ks[sub + 2][...], vs[sub + 2][...], mask_next), (kc, vc, None)], sink_ref, out(sub))

    @pl.when(j >= N_ATT_STEPS)
    def _():
        for sub in range(QB):
            _att_heads(qt_ref[:, sub * BLK:(sub + 1) * BLK], [(kc, vc, None)], sink_ref, out(sub))


def _attention(qt, kn, vt, sink8, with_ctx):
    assert CTX == QB * BLK
    steps = N_ATT_STEPS + (1 if with_ctx else 0)
    rows = R_ALL if with_ctx else T_LAT
    qrows = QB * BLK

    def qblk(b, j):
        return jnp.where(j < N_ATT_STEPS, b * N_ATT_STEPS + j, T_LAT // qrows + b)

    def band(off):
        return lambda b, j: b * NBLK + jnp.clip(QB * j + off, 0, NBLK - 1)

    kspec = lambda f: pl.BlockSpec((BLK, KV_W), lambda b, j: (f(b, j), 0))
    vspec = lambda f: pl.BlockSpec((KV_W, BLK), lambda b, j: (0, f(b, j)))
    offs = range(-1, QB + 1)
    return pl.pallas_call(
        _att_body,
        grid=(NB_BATCH, steps),
        in_specs=[pl.BlockSpec((ATT_W, qrows), lambda b, j: (0, qblk(b, j)))]
                 + [kspec(band(o)) for o in offs] + [vspec(band(o)) for o in offs]
                 + [pl.BlockSpec((CTX, KV_W), lambda b, j: (T_LAT // CTX + b, 0)),
                    pl.BlockSpec((KV_W, CTX), lambda b, j: (0, T_LAT // CTX + b)),
                    pl.BlockSpec((8, 128), lambda b, j: (0, 0))],
        out_specs=pl.BlockSpec((qrows, ATT_W), lambda b, j: (qblk(b, j), 0)),
        out_shape=jax.ShapeDtypeStruct((rows, ATT_W), F32),
        compiler_params=_cparams(("parallel", "parallel")),
        name="attention",
    )(qt, *([kn] * 4), *([vt] * 4), kn, vt, sink8)


def _short_conv_rows(u, prev_row, next_row, w, b, first, last):
    n = u.shape[0]
    row = lax.broadcasted_iota(jnp.int32, u.shape, 0)
    up = jnp.where(row == 0, prev_row, pltpu.roll(u, 1, 0))
    un = jnp.where(row == n - 1, next_row, pltpu.roll(u, n - 1, 0))
    up = jnp.where(first, 0.0, up)
    un = jnp.where(last, 0.0, un)
    return up * w[0:1, :] + u * w[1:2, :] + un * w[2:3, :] + b


def _filt_body(n, tl, f_ref, w1_ref, b1_ref, fr_ref, w2_ref, b2_ref, w3_ref, dec_ref, o_ref):
    f = f_ref[...]
    fr = fr_ref[...]
    z = jnp.sin(fr * (_dot(f, w1_ref[...]) + b1_ref[...]))
    z = jnp.sin(fr * (_dot(z, w2_ref[...]) + b2_ref[...]))
    filt = _dot(z, w3_ref[...]) * jnp.exp(-f[:, 0:1] * jnp.abs(dec_ref[...]))
    row = pl.program_id(0) * tl + lax.broadcasted_iota(jnp.int32, filt.shape, 0)
    o_ref[...] = jnp.where(row == n, 0.0, filt)


def _circular_filters(feats_circ, w1p, b1p, frp, w2p, b2p, w3p, dec):
    n = feats_circ.shape[0] // 2
    tl = min(n, 512)
    const = lambda shape: pl.BlockSpec(shape, lambda i: (0, 0))
    half = lambda rows: pl.BlockSpec((rows, 2 * HY), lambda i: (0, i // (n // tl)))
    return pl.pallas_call(
        functools.partial(_filt_body, n, tl),
        grid=(2 * n // tl,),
        in_specs=[pl.BlockSpec((tl, 128), lambda i: (i, 0)), const((128, 128)), const((1, 128)), const((1, 128)),
                  const((128, 128)), const((1, 128)), half(128), half(1)],
        out_specs=pl.BlockSpec((tl, 2 * HY), lambda i: (i, 0)),
        out_shape=jax.ShapeDtypeStruct((2 * n, 2 * HY), F32),
        compiler_params=_cparams(("parallel",)),
        name="hyena_filters",
    )(feats_circ, w1p, b1p, frp, w2p, b2p, w3p, dec)


def _pos_feats_circ(n):
    t = jnp.linspace(0.0, 1.0, n, dtype=F32)[:, None]
    ang = (2.0 * math.pi / n) * jnp.arange(n, dtype=F32)[:, None]
    bands = jnp.linspace(1e-4, POS_BANDS - 1, POS_BANDS, dtype=F32)[None, :]
    feats = jnp.concatenate([t, jnp.cos(bands * ang), -jnp.sin(bands * ang)], axis=-1)
    feats = jnp.pad(feats, ((0, 0), (0, 128 - feats.shape[1])))
    return jnp.concatenate([feats, feats[:1], jnp.flip(feats[:n - 1], axis=0)], axis=0)


def _s1_body(x_ref, m_ref, are_ref, aim_ref):
    half = FFT_A * FFT_T
    ncb = x_ref.shape[-1] // HY
    for p in range(S1_TILES):
        xs = x_ref[:, p].reshape(HALF_A * FFT_T, ncb * HY)
        r = _dot(m_ref[p], xs)
        for cb in range(ncb):
            cols = slice(cb * HY, (cb + 1) * HY)
            are_ref[cb, :, p] = r[:half, cols].reshape(FFT_A, FFT_T, HY).astype(BF16)
            aim_ref[cb, :, p] = r[half:, cols].reshape(FFT_A, FFT_T, HY).astype(BF16)


def _dft_stage1(x5, s1_tab):
    nb, ncb = x5.shape[0], x5.shape[-1] // HY
    out = pl.BlockSpec((None, ncb, FFT_A, S1_TILES, FFT_T, HY), lambda b, j: (b, 0, 0, j, 0, 0))
    return pl.pallas_call(
        _s1_body,
        grid=(nb, FFT_B // S1_TILES),
        in_specs=[pl.BlockSpec((None, HALF_A, S1_TILES, FFT_T, ncb * HY), lambda b, j: (b, 0, j, 0, 0)),
                  pl.BlockSpec((S1_TILES, 2 * FFT_A * FFT_T, HALF_A * FFT_T), lambda b, j: (j, 0, 0))],
        out_specs=[out, out],
        out_shape=[jax.ShapeDtypeStruct((nb, ncb, FFT_A, FFT_B, FFT_T, HY), BF16)] * 2,
        compiler_params=_cparams(("parallel", "parallel")),
        name="dft_stage1",
    )(x5, s1_tab)


def _s1_pair_body(x_ref, m_ref, are_ref, aim_ref):
    half = FFT_A * FFT_T
    rows = HALF_A * FFT_T
    for p in range(S1_TILES):
        xab = jnp.concatenate([x_ref[0:HALF_A, p].reshape(rows, HY), x_ref[HALF_A:2 * HALF_A, p].reshape(rows, HY)], axis=1)
        r = _dot(m_ref[p], xab)
        are_ref[:, p] = (r[:half, :HY] - r[half:, HY:]).reshape(FFT_A, FFT_T, HY).astype(BF16)
        aim_ref[:, p] = (r[half:, :HY] + r[:half, HY:]).reshape(FFT_A, FFT_T, HY).astype(BF16)


def _row_blocks(x):
    return x.reshape(x.shape[0] // FFT_R, FFT_B, FFT_T, HY)


_PAIR_SPEC = pl.BlockSpec((2 * HALF_A, S1_TILES, FFT_T, HY), lambda b, j: (b, j, 0, 0))


def _dft_stage1_pair(x4, s1_tab):
    npair = NB_BATCH // 2
    out = pl.BlockSpec((None, FFT_A, S1_TILES, FFT_T, HY), lambda b, j: (b, 0, j, 0, 0))
    return pl.pallas_call(
        _s1_pair_body,
        grid=(npair, FFT_B // S1_TILES),
        in_specs=[_PAIR_SPEC, pl.BlockSpec((S1_TILES, 2 * FFT_A * FFT_T, HALF_A * FFT_T), lambda b, j: (j, 0, 0))],
        out_specs=[out, out],
        out_shape=[jax.ShapeDtypeStruct((npair, FFT_A, FFT_B, FFT_T, HY), BF16)] * 2,
        compiler_params=_cparams(("parallel", "parallel")),
        name="dft_stage1_pair",
    )(x4, s1_tab)


def _block_rows(r):
    return slice(r * FFT_R, (r + 1) * FFT_R)


def _stack_complex(re_ref, im_ref, sl):
    return jnp.concatenate([re_ref[sl, :], im_ref[sl, :]], axis=0)


def _spec_body(are_ref, aim_ref, wf_ref, hre_ref, him_ref):
    wf = wf_ref[...]
    for r in range(HB_ROWS // FFT_R):
        sl = _block_rows(r)
        both = jnp.concatenate([_stack_complex(are_ref.at[0], aim_ref.at[0], sl),
                                _stack_complex(are_ref.at[1], aim_ref.at[1], sl)], axis=1)
        x = jnp.dot(wf, both, preferred_element_type=F32)
        first, second = x[:, :HY], x[:, HY:]
        x = first + second if r % 2 == 0 else first - second
        hre_ref[sl, :] = x[:FFT_R]
        him_ref[sl, :] = x[FFT_R:]


def _filter_spectrum(a_re, a_im, wf):
    assert (HB_ROWS // FFT_R) % 2 == 0
    blk_in = pl.BlockSpec((2, None, HB_ROWS, HY), lambda r, o: (0, o, r, 0))
    blk_out = pl.BlockSpec((None, HB_ROWS, HY), lambda r, o: (o, r, 0))
    return pl.pallas_call(
        _spec_body,
        grid=(N_FFT // HB_ROWS, 2),
        in_specs=[blk_in, blk_in, pl.BlockSpec((2 * FFT_R, 2 * FFT_R), lambda r, o: (0, 0))],
        out_specs=[blk_out, blk_out],
        out_shape=[jax.ShapeDtypeStruct((2, N_FFT, HY), F32)] * 2,
        compiler_params=_cparams(("parallel", "parallel")),
        name="filter_spectrum",
    )(a_re, a_im, wf)


def _s2_body(are_ref, aim_ref, hre_ref, him_ref, wf_ref, wi_ref, cre_ref, cim_ref):
    wf, wi = wf_ref[...], wi_ref[...]
    for r in range(0, HB_ROWS // FFT_R, 2):
        sl0, sl1 = _block_rows(r), _block_rows(r + 1)
        a = jnp.concatenate([_stack_complex(are_ref, aim_ref, sl0), _stack_complex(are_ref, aim_ref, sl1)], axis=1)
        x = jnp.dot(wf, a, preferred_element_type=F32)
        xre, xim = x[:FFT_R], x[FFT_R:]
        hre = jnp.concatenate([hre_ref[sl0, :], hre_ref[sl1, :]], axis=1)
        him = jnp.concatenate([him_ref[sl0, :], him_ref[sl1, :]], axis=1)
        y = jnp.concatenate([(xre * hre - xim * him).astype(BF16), (xre * him + xim * hre).astype(BF16)], axis=0)
        c = jnp.dot(wi, y, preferred_element_type=F32)
        for i, sl in enumerate((sl0, sl1)):
            cre_ref[sl, :] = c[:FFT_R, i * HY:(i + 1) * HY].astype(BF16)
            cim_ref[sl, :] = c[FFT_R:, i * HY:(i + 1) * HY].astype(BF16)


def _spectral_filter(a_re, a_im, h_re, h_im, order, wf, wi):
    nb = a_re.shape[0]
    blk = pl.BlockSpec((None, HB_ROWS, HY), lambda r, b: (b, r, 0))
    hblk = pl.BlockSpec((None, HB_ROWS, HY), lambda r, b: (order, r, 0))
    mat = pl.BlockSpec((2 * FFT_R, 2 * FFT_R), lambda r, b: (0, 0))
    return pl.pallas_call(
        _s2_body,
        grid=(N_FFT // HB_ROWS, nb),
        in_specs=[blk, blk, hblk, hblk, mat, mat],
        out_specs=[blk, blk],
        out_shape=[jax.ShapeDtypeStruct((nb, N_FFT, HY), BF16)] * 2,
        compiler_params=_cparams(("parallel", "parallel")),
        name="spectral_filter",
    )(a_re, a_im, h_re, h_im, wf, wi)


def _s1i_body(cre_ref, cim_ref, g_ref, a_ref, b_ref, sk_ref, o_ref):
    rows = FFT_A * FFT_T
    for p in range(S1_TILES):
        cre, cim = cre_ref[:, p].reshape(rows, HY), cim_ref[:, p].reshape(rows, HY)
        c = jnp.concatenate([jnp.concatenate([cre, cim], axis=1), jnp.concatenate([cim, -cre], axis=1)], axis=0)
        y = jnp.dot(g_ref[p], c, preferred_element_type=F32)
        for i in range(2):
            seq = slice(i * HALF_A, (i + 1) * HALF_A)
            yi = y[:, i * HY:(i + 1) * HY].reshape(HALF_A, FFT_T, HY)
            o_ref[seq, p] = a_ref[seq, p] * (yi + b_ref[seq, p] * sk_ref[...])


def _idft_stage1_gate(c_re, c_im, s1i_tab, a4, b4, skip):
    data = _PAIR_SPEC
    spec = pl.BlockSpec((None, FFT_A, S1_TILES, FFT_T, HY), lambda b, j: (b, 0, j, 0, 0))
    return pl.pallas_call(
        _s1i_body,
        grid=(NB_BATCH // 2, FFT_B // S1_TILES),
        in_specs=[spec, spec, pl.BlockSpec((S1_TILES, HALF_A * FFT_T, 2 * FFT_A * FFT_T), lambda b, j: (j, 0, 0)),
                  data, data, pl.BlockSpec((1, 1, HY), lambda b, j: (0, 0, 0))],
        out_specs=data,
        out_shape=jax.ShapeDtypeStruct((T_LAT // FFT_R, FFT_B, FFT_T, HY), F32),
        compiler_params=_cparams(("parallel", "parallel")),
        name="idft_stage1_gate",
    )(c_re, c_im, s1i_tab, a4, b4, skip.reshape(1, 1, HY))


def _hyc_body(v_ref, x1_ref, x2_ref, circ_ref, fc_ref, fs_ref, sk_ref, o_ref):
    fc, fs = fc_ref[...], fs_ref[...]
    circ = circ_ref[...]
    h_re, h_im = _dot(fc, circ), -_dot(fs, circ)

    def long_conv(s, o):
        sl = slice(o * HY, (o + 1) * HY)
        s_re, s_im = _dot(fc[:, :CTX], s), -_dot(fs[:, :CTX], s)
        hre, him = h_re[:, sl], h_im[:, sl]
        y_re = s_re * hre - s_im * him
        y_im = s_re * him + s_im * hre
        y = (_dot(fc[:CTX, :], y_re) - _dot(fs[:CTX, :], y_im)) * (1.0 / N_CTXF)
        return y + s * sk_ref[o:o + 1, :]

    o_ref[...] = x2_ref[...] * long_conv(x1_ref[...] * long_conv(v_ref[...], 0), 1)


def _hyena_ctx(hv, hx1, hx2, circ_c, fc, fs, skip):
    const = lambda shape: pl.BlockSpec(shape, lambda b: (0, 0))
    seq = pl.BlockSpec((CTX, HY), lambda b: (T_LAT // CTX + b, 0))
    return pl.pallas_call(
        _hyc_body,
        grid=(NB_BATCH,),
        in_specs=[seq, seq, seq, const((N_CTXF, 2 * HY)), const((N_CTXF, N_CTXF)), const((N_CTXF, N_CTXF)), const((2, HY))],
        out_specs=pl.BlockSpec((CTX, HY), lambda b: (b, 0)),
        out_shape=jax.ShapeDtypeStruct((T_CTX, HY), F32),
        compiler_params=_cparams(("parallel",)),
        name="hyena_ctx",
    )(hv, hx1, hx2, circ_c, fc, fs, skip)


N_ML_STEPS = CTX // CHUNK + SEQ // CHUNK


def _ml_chain(direction, qv_ref, k_ref, src_col, cum_r, gates_t, c_scr, m_scr, mask, ones_rows):
    base = 8 * direction
    outs = []
    for h in range(ML_H):
        ic, fc = base + h, base + 4 + h
        chain = direction * ML_H + h
        b_row, li_row = cum_r[fc:fc + 1, :], gates_t[ic:ic + 1, :]
        b_end = b_row[:, CHUNK - 1:CHUNK] if direction == 0 else b_row[:, 0:1]
        q_t = qv_ref[HD * h:HD * (h + 1), :]
        vext_t = jnp.concatenate([qv_ref[ML_W + HD * h:ML_W + HD * (h + 1), :], ones_rows], axis=0)
        k = k_ref[:, HD * h:HD * (h + 1)]
        c_prev, m_prev = c_scr[chain], m_scr[chain]
        dmat = jnp.where(mask, src_col[:, fc:fc + 1] + b_row, NEG)
        m_intra = jnp.max(dmat, axis=0, keepdims=True)
        s_t = jnp.dot(k, q_t, preferred_element_type=F32) * jnp.exp(dmat - m_intra)
        inter = b_row + m_prev
        m_t = jnp.maximum(inter, m_intra)
        hx = jnp.exp(m_intra - m_t) * _dot(vext_t, s_t) + jnp.exp(inter - m_t) * _dot(c_prev, q_t)
        den = jnp.maximum(jnp.abs(hx[HD:HD + 1, :]), jnp.exp(-m_t))
        outs.append(hx[:HD, :] / den)
        g_row = b_end - b_row + li_row
        m_new = jnp.maximum(b_end + m_prev, jnp.max(g_row, axis=1, keepdims=True))
        c_scr[chain] = jnp.exp(b_end + m_prev - m_new) * c_prev + _dot(vext_t * jnp.exp(g_row - m_new), k)
        m_scr[chain] = m_new
    return jnp.concatenate(outs, axis=0)


ML_BPS = 2
ML_GROUP_COLS = (NB_BATCH // ML_BPS) * (SEQ + CTX)


def _ml_body(*refs):
    n_in = 8 * ML_BPS
    seq_refs, (bias_ref, bias_t_ref, tril_ref, triu_ref) = refs[:n_in], refs[n_in:n_in + 4]
    (hf_ref, hb_ref), (c_scr, m_scr) = refs[n_in + 4:n_in + 6], refs[n_in + 6:]

    @pl.when(pl.program_id(1) == 0)
    def _():
        c_scr[...] = jnp.zeros_like(c_scr)
        m_scr[...] = jnp.zeros_like(m_scr)

    tril, triu = tril_ref[...], triu_ref[...]
    src = lax.broadcasted_iota(jnp.int32, (CHUNK, CHUNK), 0)
    dst = lax.broadcasted_iota(jnp.int32, (CHUNK, CHUNK), 1)
    ones_rows = (lax.broadcasted_iota(jnp.int32, (HD, CHUNK), 0) == 0).astype(BF16)
    for sub in range(ML_BPS):
        c_sub, m_sub = c_scr.at[sub], m_scr.at[sub]
        for direction in range(2):
            qv_ref, k_ref, g_ref, gt_ref = seq_refs[8 * sub + 4 * direction:8 * sub + 4 * direction + 4]
            gates = g_ref[...] + bias_ref[...]
            gates_t = gt_ref[...] + bias_t_ref[...]
            ls, ls_t = _log_sigmoid(gates), _log_sigmoid(gates_t)
            if direction == 0:
                cum_c, cum_r, mask = _dot_sel_l(tril, ls), _dot_sel_r(ls_t, triu), src <= dst
            else:
                cum_c, cum_r, mask = _dot_sel_l(triu, ls), _dot_sel_r(ls_t, tril), src >= dst
            src_col = pltpu.roll(gates, 4, 1) - cum_c
            o_ref = hf_ref if direction == 0 else hb_ref
            o_ref[sub] = _ml_chain(direction, qv_ref, k_ref, src_col, cum_r, gates_t, c_sub, m_sub, mask, ones_rows)


def _mlstm(mqv_t, mk, mg, mg_t, gate_b):
    nctx = CTX // CHUNK
    nlat = SEQ // CHUNK
    per_group = NB_BATCH // ML_BPS

    def step_chunk(i, backward):
        ctx_chunk = (nctx - 1 - i) if backward else i
        lat_chunk = (N_ML_STEPS - 1 - i) if backward else (i - nctx)
        return i < nctx, ctx_chunk, lat_chunk

    def in_chunk(sub, backward):
        def f(p, i):
            b = p + per_group * sub
            is_ctx, cc, lc = step_chunk(i, backward)
            return jnp.where(is_ctx, T_LAT // CHUNK + nctx * b + cc, nlat * b + lc)
        return f

    def out_chunk(backward):
        def f(p, i):
            is_ctx, cc, lc = step_chunk(i, backward)
            return jnp.where(is_ctx, per_group * nlat + nctx * p + cc, nlat * p + lc)
        return f

    bias = jnp.pad(gate_b, (0, 128 - N_GATES)).reshape(1, 128)
    bias_t = gate_b.reshape(N_GATES, 1)
    tril = jnp.asarray(np.tril(np.ones((CHUNK, CHUNK), np.float32))).astype(BF16)
    triu = jnp.asarray(np.triu(np.ones((CHUNK, CHUNK), np.float32))).astype(BF16)
    const = lambda shape: pl.BlockSpec(shape, lambda p, i: (0, 0))
    ins, args = [], []
    for sub in range(ML_BPS):
        for backward in (False, True):
            f = in_chunk(sub, backward)
            ins += [pl.BlockSpec((2 * ML_W, CHUNK), lambda p, i, f=f: (0, f(p, i))),
                    pl.BlockSpec((CHUNK, ML_W), lambda p, i, f=f: (f(p, i), 0)),
                    pl.BlockSpec((CHUNK, 128), lambda p, i, f=f: (f(p, i), 0)),
                    pl.BlockSpec((N_GATES, CHUNK), lambda p, i, f=f: (0, f(p, i)))]
            args += [mqv_t, mk, mg, mg_t]
    outs = [pl.BlockSpec((ML_BPS, ML_W, CHUNK), lambda p, i, f=out_chunk(bw): (0, 0, f(p, i))) for bw in (False, True)]
    return pl.pallas_call(
        _ml_body,
        grid=(per_group, N_ML_STEPS),
        in_specs=ins + [const((1, 128)), const((N_GATES, 1)), const((CHUNK, CHUNK)), const((CHUNK, CHUNK))],
        out_specs=outs,
        out_shape=[jax.ShapeDtypeStruct((ML_BPS, ML_W, ML_GROUP_COLS), F32)] * 2,
        scratch_shapes=[pltpu.VMEM((ML_BPS, 2 * ML_H, 2 * HD, HD), F32), pltpu.VMEM((ML_BPS, 2 * ML_H, 1, 1), F32)],
        compiler_params=_cparams(("parallel", "arbitrary")),
        name="mlstm",
    )(*args, bias, bias_t, tril, triu)


FF_CHUNKS = ((0, 1536), (1536, D_FF))


def _out_ffn_body(att_ref, hl_ref, hc_ref, hf_ref, hb_ref, mo_ref, mnw_ref, bd_ref, wo_ref, x_ref,
                  g1_ref, sh2_ref, sc2_ref, g2_ref, nw2_ref, wg_ref, wu_ref, wd_ref, o_ref):
    i = pl.program_id(0)
    hn = _head_rms((hf_ref[...] + hb_ref[...]).T, bd_ref[...], mnw_ref[...])
    mlo = hn * _sigmoid(mo_ref[...])
    hy = jnp.where(i < NT_LAT, hl_ref[...], hc_ref[...])
    mixed = jnp.concatenate([att_ref[...].astype(BF16), hy.astype(BF16), mlo.astype(BF16)], axis=1)
    x = x_ref[...] + g1_ref[...] * jnp.dot(mixed, wo_ref[...], preferred_element_type=F32)
    hb = _modulated_norm(x, nw2_ref[...], sc2_ref[...], sh2_ref[...]).astype(BF16)
    acc = None
    for lo, hi in FF_CHUNKS:
        sl = slice(lo, hi)
        a = jnp.dot(hb, wg_ref[:, sl], preferred_element_type=F32)
        u = jnp.dot(hb, wu_ref[:, sl], preferred_element_type=F32)
        part = _dot(a * _sigmoid(a) * u, wd_ref[sl, :])
        acc = part if acc is None else acc + part
    o_ref[...] = x + g2_ref[...] * acc


def _out_ffn(att, hyo, hyoc, hf_t, hb_t, mo, ml_nw, w_out_b, xs, mod4, layer, nw2, wg, wu, wd, with_ctx):
    tiles = NT_ALL if with_ctx else NT_LAT
    row = lambda w: pl.BlockSpec((TM, w), lambda i: (i, 0))
    lat_tiles = (NB_BATCH // ML_BPS) * TILES_PER_SEQ
    col = pl.BlockSpec((None, ML_W, TM), lambda i: (jnp.where(i < NT_LAT, i // lat_tiles, i - NT_LAT), 0,
                                                    jnp.where(i < NT_LAT, i % lat_tiles, lat_tiles)))
    resident = lambda shape: pl.BlockSpec(shape, lambda i: (0, 0), pipeline_mode=pl.Buffered(1))
    vec = lambda w: pl.BlockSpec((1, w), lambda i: (0, 0))
    bd = jnp.asarray(_blockdiag_ones(ML_W)).astype(BF16)
    return pl.pallas_call(
        _out_ffn_body,
        grid=(tiles,),
        in_specs=[row(ATT_W),
                  pl.BlockSpec((TM, HY), lambda i: (jnp.minimum(i, NT_LAT - 1), 0)),
                  pl.BlockSpec((TM, HY), lambda i: (jnp.maximum(i - NT_LAT, 0), 0)),
                  col, col, row(ML_W), vec(ML_W), resident((ML_W, ML_W)), resident((D, D)), row(D),
                  _mod_spec(layer, 2), _mod_spec(layer, 3), _mod_spec(layer, 4), _mod_spec(layer, 5), vec(D),
                  resident((D, D_FF)), resident((D, D_FF)), resident((D_FF, D))],
        out_specs=row(D),
        out_shape=jax.ShapeDtypeStruct((tiles * TM, D), F32),
        compiler_params=_cparams(("parallel",), 56),
        name="out_ffn",
    )(att, hyo, hyoc, hf_t, hb_t, mo, ml_nw.reshape(1, ML_W), bd, w_out_b, xs, mod4, mod4, mod4, mod4,
      nw2.reshape(1, D), wg, wu, wd)


def _rope_tables():
    n_rows = SEQ // GRID_W
    row = jnp.repeat(jnp.arange(n_rows), GRID_W)
    col = jnp.tile(jnp.arange(GRID_W), n_rows)
    nf = HD // 4
    inv_freq = ROPE_BASE ** (-jnp.arange(nf, dtype=F32) / nf)
    ang = jnp.stack([row[:, None] * inv_freq, col[:, None] * inv_freq], axis=1)
    cos, sin = jnp.cos(ang), jnp.sin(ang)
    cos_h = jnp.concatenate([cos, cos], axis=-1).reshape(SEQ, HD)
    sin_h = jnp.concatenate([-sin, sin], axis=-1).reshape(SEQ, HD)
    cos_t = jnp.concatenate([jnp.tile(cos_h, (1, 2)), jnp.ones((TM, 128), F32)], axis=0)
    sin_t = jnp.concatenate([jnp.tile(sin_h, (1, 2)), jnp.zeros((TM, 128), F32)], axis=0)
    return cos_t, sin_t


def kernel(x, c, ctx, c_ctx, w_mod, b_mod, norm1_w, norm2_w, w_in, w_out, q_norm_w, k_norm_w, attn_sink, hy_conv_w, hy_conv_b, hy_w1, hy_b1, hy_freq, hy_w2, hy_b2, hy_w3, hy_decay, hy_skip, ml_gate_b, ml_norm_w, ffn_w_gate, ffn_w_up, ffn_w_down):
    xs = jnp.concatenate([x.reshape(T_LAT, D), ctx.reshape(T_CTX, D)], axis=0)
    cc8 = jnp.concatenate([c, c_ctx[None, :], jnp.zeros((8 - NB_BATCH - 1, D), F32)], axis=0)
    mod4 = _modulation(cc8, w_mod, b_mod).reshape(DEPTH, 8, 1, 6 * D)

    cos_t, sin_t = _rope_tables()
    feats_l, feats_c = _pos_feats_circ(SEQ), _pos_feats_circ(CTX)
    s1_tab = jnp.asarray(_TAB["s1"]).astype(BF16)
    s1i_tab = jnp.asarray(_TAB["s1i"]).astype(BF16)
    wf, wi = jnp.asarray(_TAB["wf"]).astype(BF16), jnp.asarray(_TAB["wi"]).astype(BF16)
    fc, fs = jnp.asarray(_TAB["fc"]).astype(BF16), jnp.asarray(_TAB["fs"]).astype(BF16)

    w_in_p = _reorder_w_in(w_in)
    w_out_b = w_out.astype(BF16)
    wg_b, wu_b, wd_b = ffn_w_gate.astype(BF16), ffn_w_up.astype(BF16), ffn_w_down.astype(BF16)
    hpad = 128 - FILT_HID
    w1p = jnp.pad(hy_w1, ((0, 0), (0, 128 - hy_w1.shape[1]), (0, hpad)))
    b1p = jnp.pad(hy_b1, ((0, 0), (0, hpad))).reshape(DEPTH, 1, 128)
    frp = jnp.pad(hy_freq, ((0, 0), (0, hpad))).reshape(DEPTH, 1, 128)
    w2p = jnp.pad(hy_w2, ((0, 0), (0, hpad), (0, hpad)))
    b2p = jnp.pad(hy_b2, ((0, 0), (0, hpad))).reshape(DEPTH, 1, 128)
    w3p = jnp.pad(hy_w3, ((0, 0), (0, hpad), (0, 0)))
    sink8 = jnp.pad(attn_sink, ((0, 0), (0, 128 - ATT_H)))[:, None, :] * jnp.ones((1, 8, 1), F32)
    zeros_tile = jnp.zeros((TM, HY), F32)

    out = None
    for l in range(DEPTH):
        last = l == DEPTH - 1
        qt, kn, vt, hv, hx1, hx2, mqv_t, mk, mo, mg, mg_t = _in_proj(
            xs, mod4, l, norm1_w[l], w_in_p[l], cos_t, sin_t, q_norm_w[l], k_norm_w[l], hy_conv_w[l], hy_conv_b[l])

        att = _attention(qt, kn, vt, sink8[l], not last)

        dec = hy_decay[l].reshape(1, N_FILT)
        circ = _circular_filters(feats_l, w1p[l], b1p[l], frp[l], w2p[l], b2p[l], w3p[l], dec)
        f_re, f_im = _dft_stage1(circ.reshape(2, HALF_A, FFT_B, FFT_T, 2 * HY), s1_tab)
        h_re, h_im = _filter_spectrum(f_re.reshape(2, 2, N_FFT, HY), f_im.reshape(2, 2, N_FFT, HY), wf)

        s_in, gate_a = _row_blocks(hv), _row_blocks(hx1)
        for order in range(2):
            a_re, a_im = _dft_stage1_pair(s_in, s1_tab)
            rows = lambda t: t.reshape(NB_BATCH // 2, N_FFT, HY)
            c_re, c_im = _spectral_filter(rows(a_re), rows(a_im), h_re, h_im, order, wf, wi)
            tiles = lambda t: t.reshape(NB_BATCH // 2, FFT_A, FFT_B, FFT_T, HY)
            s_in = _idft_stage1_gate(tiles(c_re), tiles(c_im), s1i_tab, gate_a, s_in, hy_skip[l, order])
            gate_a = _row_blocks(hx2)
        hyo = s_in.reshape(T_LAT, HY)

        if last:
            hyoc = zeros_tile
        else:
            circ_c = _circular_filters(feats_c, w1p[l], b1p[l], frp[l], w2p[l], b2p[l], w3p[l], dec)
            hyoc = _hyena_ctx(hv, hx1, hx2, circ_c, fc, fs, hy_skip[l])

        hf_t, hb_t = _mlstm(mqv_t, mk, mg, mg_t, ml_gate_b[l])

        xs = _out_ffn(att, hyo, hyoc, hf_t, hb_t, mo, ml_norm_w[l], w_out_b[l], xs, mod4, l,
                      norm2_w[l], wg_b[l], wu_b[l], wd_b[l], not last)
        out = xs
    return out.reshape(NB_BATCH, SEQ, D)
```

```python
import functools
import math

import numpy as np
import jax
import jax.numpy as jnp
from jax import lax
from jax.experimental import pallas as pl
from jax.experimental.pallas import tpu as pltpu

F32 = jnp.float32
BF16 = jnp.bfloat16

D = 1024
NB_BATCH = 4
SEQ = 4096
DEPTH = 4
GRID_W = 64
CTX = 256
T_LAT = NB_BATCH * SEQ
T_CTX = NB_BATCH * CTX
R_ALL = T_LAT + T_CTX

HD = 64
ATT_H = 6
ATT_W = ATT_H * HD
KV_W = 2 * HD
BLK = 128
NBLK = SEQ // BLK
ROPE_BASE = 10000.0

HY = 384
N_FILT = 4 * HY
POS_BANDS = 16
FILT_HID = 64

ML_H = 4
ML_W = 256
CHUNK = 128

D_FF = 2816
P_IN = 2832
P_PAD = 2944
NEG = -1e30
EPS = 1e-6

TM = 512
NT_LAT = T_LAT // TM
NT_ALL = R_ALL // TM
TILES_PER_SEQ = SEQ // TM

N_FFT = 2 * SEQ
FFT_A = 32
FFT_R = 256
FFT_T = 16
FFT_B = FFT_R // FFT_T
HALF_A = FFT_A // 2
S1_TILES = 4
HB_ROWS = 1024
N_CTXF = 2 * CTX


def _np_tables():
    ka = np.arange(FFT_A)[None, :, None, None]
    cc = np.arange(FFT_T)[None, None, :, None]
    aa = np.arange(HALF_A)[None, None, None, :]
    bb = np.arange(FFT_B)[:, None, None, None]
    ph = (ka * (FFT_R * aa + FFT_T * bb + cc)) % N_FFT
    th = 2.0 * np.pi * ph / N_FFT
    eye = np.eye(FFT_T)
    cos = np.einsum("bkca,cd->bkcad", np.cos(th), eye)
    sin = np.einsum("bkca,cd->bkcad", np.sin(th), eye)
    rows, cols = FFT_A * FFT_T, HALF_A * FFT_T
    s1 = np.concatenate([cos.reshape(FFT_B, rows, cols), -sin.reshape(FFT_B, rows, cols)], axis=1)
    s1i = np.concatenate([cos.reshape(FFT_B, rows, cols).transpose(0, 2, 1),
                          -sin.reshape(FFT_B, rows, cols).transpose(0, 2, 1)], axis=2) / N_FFT
    r = np.arange(FFT_R)
    th2 = 2.0 * np.pi * ((r[:, None] * r[None, :]) % FFT_R) / FFT_R
    c2, s2 = np.cos(th2), np.sin(th2)
    wf = np.block([[c2, s2], [-s2, c2]])
    wi = np.block([[c2, -s2], [s2, c2]])
    c = np.arange(N_CTXF)
    th3 = 2.0 * np.pi * ((c[:, None] * c[None, :]) % N_CTXF) / N_CTXF
    return dict(s1=s1.astype(np.float32), s1i=s1i.astype(np.float32), wf=wf.astype(np.float32), wi=wi.astype(np.float32),
                fc=np.cos(th3).astype(np.float32), fs=np.sin(th3).astype(np.float32))


_TAB = _np_tables()


def _blockdiag_ones(width):
    return np.kron(np.eye(width // HD), np.ones((HD, HD))).astype(np.float32)


def _cparams(sem, vmem_mb=48):
    return pltpu.CompilerParams(dimension_semantics=sem, vmem_limit_bytes=vmem_mb * 1024 * 1024)


def _dot(a, b):
    return jnp.dot(a.astype(BF16), b.astype(BF16), preferred_element_type=F32)


def _dot_nt(a, b):
    return lax.dot_general(a.astype(BF16), b.astype(BF16), (((1,), (1,)), ((), ())), preferred_element_type=F32)


def _dot_tn(a, b):
    return lax.dot_general(a.astype(BF16), b.astype(BF16), (((0,), (0,)), ((), ())), preferred_element_type=F32)


def _split3(x):
    x1 = x.astype(BF16)
    r1 = x - x1.astype(F32)
    x2 = r1.astype(BF16)
    x3 = (r1 - x2.astype(F32)).astype(BF16)
    return x1, x2, x3


def _dot_sel_l(sel, x):
    return sum(jnp.dot(sel, p, preferred_element_type=F32) for p in _split3(x))


def _dot_sel_r(x, sel):
    return sum(jnp.dot(p, sel, preferred_element_type=F32) for p in _split3(x))


def _sigmoid(x):
    return 1.0 / (1.0 + jnp.exp(-x))


def _log_sigmoid(x):
    return jnp.minimum(x, 0.0) - jnp.log(1.0 + jnp.exp(-jnp.abs(x)))


def _head_rms(t, bd, w):
    sq = t * t
    hi = sq.astype(BF16)
    lo = (sq - hi.astype(F32)).astype(BF16)
    ss = jnp.dot(hi, bd, preferred_element_type=F32) + jnp.dot(lo, bd, preferred_element_type=F32)
    return t * lax.rsqrt(ss * (1.0 / HD) + EPS) * w


def _mod_row(i):
    return jnp.where(i < NT_LAT, i // TILES_PER_SEQ, NB_BATCH)


def _mod_spec(layer, k):
    return pl.BlockSpec((None, None, 1, D), lambda i: (layer, _mod_row(i), 0, k))


def _mod_body(s_ref, w_ref, b_ref, o_ref):
    s = s_ref[...]
    s = s * _sigmoid(s)
    o_ref[...] = _dot(s, w_ref[...]) + b_ref[...]


def _modulation(cc8, w_mod, b_mod):
    nc = 1536
    return pl.pallas_call(
        _mod_body,
        grid=(DEPTH, 6 * D // nc),
        in_specs=[pl.BlockSpec((8, D), lambda l, j: (0, 0)),
                  pl.BlockSpec((None, D, nc), lambda l, j: (l, 0, j)),
                  pl.BlockSpec((None, 1, nc), lambda l, j: (l, 0, j))],
        out_specs=pl.BlockSpec((None, 8, nc), lambda l, j: (l, 0, j)),
        out_shape=jax.ShapeDtypeStruct((DEPTH, 8, 6 * D), F32),
        compiler_params=_cparams(("parallel", "parallel")),
        name="modulation",
    )(cc8, w_mod, b_mod.reshape(DEPTH, 1, 6 * D))


_COL_ATT, _COL_HY, _COL_MQV, _COL_MK, _COL_MO, _COL_MG = (0, 640), (640, 1792), (1792, 2304), (2304, 2560), (2560, 2816), (2816, 2944)
N_GATES = 4 * ML_H


def _reorder_w_in(w_in):
    mq_end, mk_end, mv_end = 2048, 2304, 2560
    w = jnp.concatenate([w_in[:, :, :mq_end], w_in[:, :, mk_end:mv_end], w_in[:, :, mq_end:mk_end], w_in[:, :, mv_end:]], axis=2)
    return jnp.pad(w, ((0, 0), (0, 0), (0, P_PAD - P_IN))).astype(BF16)


def _modulated_norm(x, nw, sc, sh):
    ms = jnp.mean(x * x, axis=-1, keepdims=True)
    return (x * lax.rsqrt(ms + EPS)) * (nw * (1.0 + sc)) + sh


HALO = 8


def _in_body(x_ref, xp_ref, xn_ref, sh_ref, sc_ref, nw_ref, w_ref, cos_ref, sin_ref, qw_ref, kw_ref, bdq_ref, bdk_ref,
             cw_ref, cb_ref, qt_ref, k_ref, vt_ref, hv_ref, hx1_ref, hx2_ref, mqvt_ref, mk_ref, mo_ref, mg_ref, mgt_ref):
    i = pl.program_id(0)
    nw, sc, sh = nw_ref[...], sc_ref[...], sh_ref[...]
    hb = _modulated_norm(x_ref[...], nw, sc, sh).astype(BF16)
    proj = lambda cols: jnp.dot(hb, w_ref[:, cols[0]:cols[1]], preferred_element_type=F32)

    att = proj(_COL_ATT)
    cos, sin = cos_ref[...], sin_ref[...]
    lane = lax.broadcasted_iota(jnp.int32, (TM, 128), 1)
    first_half = (lane & 31) < 16
    q = _head_rms(att[:, 0:ATT_W], bdq_ref[...], qw_ref[...])
    k = _head_rms(att[:, ATT_W:ATT_W + KV_W], bdk_ref[...], kw_ref[...])
    qt_ref[...] = (_rope(q, cos, sin, first_half) * (HD ** -0.5)).T.astype(BF16)
    k_ref[...] = _rope(k, cos, sin, first_half).astype(BF16)
    vt_ref[...] = att[:, ATT_W + KV_W:ATT_W + 2 * KV_W].T.astype(BF16)

    hy = proj(_COL_HY)
    halo = jnp.concatenate([xp_ref[...], xn_ref[...]], axis=0)
    hy_halo = jnp.dot(_modulated_norm(halo, nw, sc, sh).astype(BF16), w_ref[:, _COL_HY[0]:_COL_HY[1]],
                      preferred_element_type=F32)
    seq_len = jnp.where(i < NT_LAT, SEQ, CTX)
    pos = (i * TM + lax.broadcasted_iota(jnp.int32, (TM, 1), 0)) & (seq_len - 1)
    usc = _short_conv_rows(hy, hy_halo[HALO - 1:HALO, :], hy_halo[HALO:HALO + 1, :], cw_ref[...], cb_ref[...],
                           pos == 0, pos == seq_len - 1)
    hv_ref[...] = usc[:, 0:HY]
    hx1_ref[...] = usc[:, HY:2 * HY]
    hx2_ref[...] = usc[:, 2 * HY:3 * HY]

    mqvt_ref[...] = proj(_COL_MQV).T.astype(BF16)
    mk_ref[...] = (proj(_COL_MK) * (HD ** -0.5)).astype(BF16)
    mo_ref[...] = proj(_COL_MO)
    mg = proj(_COL_MG)
    mg_ref[...] = mg
    mgt_ref[...] = mg.T[0:N_GATES, :]


def _in_proj(xs, mod4, layer, nw, w_in_p, cos_t, sin_t, qw, kw, conv_w, conv_b):
    rows = lambda w: pl.BlockSpec((TM, w), lambda i: (i, 0))
    cols = lambda h: pl.BlockSpec((h, TM), lambda i: (0, i))
    const = lambda shape: pl.BlockSpec(shape, lambda i: (0, 0))
    tab = lambda i: (jnp.where(i < NT_LAT, i % TILES_PER_SEQ, TILES_PER_SEQ), 0)
    per_tile = TM // HALO
    prev = pl.BlockSpec((HALO, D), lambda i: (jnp.maximum(i * per_tile - 1, 0), 0))
    nxt = pl.BlockSpec((HALO, D), lambda i: (jnp.minimum((i + 1) * per_tile, R_ALL // HALO - 1), 0))
    bdq = jnp.asarray(_blockdiag_ones(ATT_W)).astype(BF16)
    bdk = jnp.asarray(_blockdiag_ones(KV_W)).astype(BF16)
    bf = lambda shape: jax.ShapeDtypeStruct(shape, BF16)
    f32 = lambda shape: jax.ShapeDtypeStruct(shape, F32)
    return pl.pallas_call(
        _in_body,
        grid=(NT_ALL,),
        in_specs=[rows(D), prev, nxt, _mod_spec(layer, 0), _mod_spec(layer, 1), const((1, D)),
                  pl.BlockSpec((None, D, P_PAD), lambda i: (layer, 0, 0), pipeline_mode=pl.Buffered(1)),
                  pl.BlockSpec((TM, 128), tab), pl.BlockSpec((TM, 128), tab),
                  const((1, ATT_W)), const((1, KV_W)), const((ATT_W, ATT_W)), const((KV_W, KV_W)),
                  const((3, 3 * HY)), const((1, 3 * HY))],
        out_specs=[cols(ATT_W), rows(KV_W), cols(KV_W), rows(HY), rows(HY), rows(HY),
                   cols(2 * ML_W), rows(ML_W), rows(ML_W), rows(128), cols(N_GATES)],
        out_shape=[bf((ATT_W, R_ALL)), bf((R_ALL, KV_W)), bf((KV_W, R_ALL)), f32((R_ALL, HY)), f32((R_ALL, HY)), f32((R_ALL, HY)),
                   bf((2 * ML_W, R_ALL)), bf((R_ALL, ML_W)), f32((R_ALL, ML_W)), f32((R_ALL, 128)), f32((N_GATES, R_ALL))],
        compiler_params=_cparams(("parallel",), 56),
        name="in_proj",
    )(xs, xs, xs, mod4, mod4, nw.reshape(1, D), w_in_p, cos_t, sin_t,
      jnp.tile(qw, ATT_H).reshape(1, ATT_W), jnp.tile(kw, 2).reshape(1, KV_W), bdq, bdk, conv_w, conv_b.reshape(1, 3 * HY))


def _rope(t, cos, sin_signed, first_half):
    outs = []
    for c in range(t.shape[1] // 128):
        tc = t[:, c * 128:(c + 1) * 128]
        partner = jnp.where(first_half, pltpu.roll(tc, 112, 1), pltpu.roll(tc, 16, 1))
        outs.append(tc * cos + partner * sin_signed)
    return outs[0] if len(outs) == 1 else jnp.concatenate(outs, axis=1)


def _att_heads(qt, kv_list, sink_ref, o_ref):
    nq = qt.shape[1]
    rep = ATT_H // 2
    zeros = jnp.zeros((HD, rep * nq), BF16)
    outs = []
    for g in range(2):
        heads = range(rep * g, rep * (g + 1))
        qg = jnp.concatenate([qt[HD * h:HD * (h + 1), :] for h in heads], axis=1)
        qe = jnp.concatenate([qg, zeros] if g == 0 else [zeros, qg], axis=0)
        sink = jnp.concatenate([jnp.broadcast_to(sink_ref[0:1, h:h + 1], (1, nq)) for h in heads], axis=1)
        scores = []
        m = sink
        for k, _, mask in kv_list:
            s = jnp.dot(k, qe, preferred_element_type=F32)
            if mask is not None:
                s = jnp.where(mask, s, NEG)
            scores.append(s)
            m = jnp.maximum(m, jnp.max(s, axis=0, keepdims=True))
        l = jnp.exp(sink - m)
        acc = None
        for s, (_, vt, _) in zip(scores, kv_list):
            p = jnp.exp(s - m)
            l = l + jnp.sum(p, axis=0, keepdims=True)
            pv = jnp.dot(vt[HD * g:HD * (g + 1), :], p.astype(BF16), preferred_element_type=F32)
            acc = pv if acc is None else acc + pv
        og = acc / l
        outs += [og[:, nq * i:nq * (i + 1)] for i in range(rep)]
    o_ref[...] = jnp.concatenate(outs, axis=0).T


QB = 2
N_ATT_STEPS = NBLK // QB


def _att_body(qt_ref, k0_ref, k1_ref, k2_ref, k3_ref, v0_ref, v1_ref, v2_ref, v3_ref, kc_ref, vc_ref, sink_ref, o_ref):
    j = pl.program_id(1)
    kc, vc = kc_ref[...], vc_ref[...]
    out = lambda sub: o_ref.at[sub * BLK:(sub + 1) * BLK, :]

    @pl.when(j < N_ATT_STEPS)
    def _():
        width = (ATT_H // 2) * BLK
        key = lax.broadcasted_iota(jnp.int32, (BLK, width), 0)
        qry = lax.broadcasted_iota(jnp.int32, (BLK, width), 1) & (BLK - 1)
        ks = (k0_ref, k1_ref, k2_ref, k3_ref)
        vs = (v0_ref, v1_ref, v2_ref, v3_ref)
        for sub in range(QB):
            blk = QB * j + sub
            mask_prev = jnp.logical_and(key >= qry, blk > 0)
            mask_next = jnp.logical_and(key <= qry, blk < NBLK - 1)
            _att_heads(qt_ref[:, sub * BLK:(sub + 1) * BLK],
                       [(ks[sub][...], vs[sub][...], mask_prev), (ks[sub + 1][...], vs[sub + 1][...], None),
                        (ks[sub + 2][...], vs[sub + 2][...], mask_next), (kc, vc, None)], sink_ref, out(sub))

    @pl.when(j >= N_ATT_STEPS)
    def _():
        for sub in range(QB):
            _att_heads(qt_ref[:, sub * BLK:(sub + 1) * BLK], [(kc, vc, None)], sink_ref, out(sub))


def _attention(qt, kn, vt, sink8, with_ctx):
    assert CTX == QB * BLK
    steps = N_ATT_STEPS + (1 if with_ctx else 0)
    rows = R_ALL if with_ctx else T_LAT
    qrows = QB * BLK

    def qblk(b, j):
        return jnp.where(j < N_ATT_STEPS, b * N_ATT_STEPS + j, T_LAT // qrows + b)

    def band(off):
        return lambda b, j: b * NBLK + jnp.clip(QB * j + off, 0, NBLK - 1)

    kspec = lambda f: pl.BlockSpec((BLK, KV_W), lambda b, j: (f(b, j), 0))
    vspec = lambda f: pl.BlockSpec((KV_W, BLK), lambda b, j: (0, f(b, j)))
    offs = range(-1, QB + 1)
    return pl.pallas_call(
        _att_body,
        grid=(NB_BATCH, steps),
        in_specs=[pl.BlockSpec((ATT_W, qrows), lambda b, j: (0, qblk(b, j)))]
                 + [kspec(band(o)) for o in offs] + [vspec(band(o)) for o in offs]
                 + [pl.BlockSpec((CTX, KV_W), lambda b, j: (T_LAT // CTX + b, 0)),
                    pl.BlockSpec((KV_W, CTX), lambda b, j: (0, T_LAT // CTX + b)),
                    pl.BlockSpec((8, 128), lambda b, j: (0, 0))],
        out_specs=pl.BlockSpec((qrows, ATT_W), lambda b, j: (qblk(b, j), 0)),
        out_shape=jax.ShapeDtypeStruct((rows, ATT_W), F32),
        compiler_params=_cparams(("parallel", "parallel")),
        name="attention",
    )(qt, *([kn] * 4), *([vt] * 4), kn, vt, sink8)


def _short_conv_rows(u, prev_row, next_row, w, b, first, last):
    n = u.shape[0]
    row = lax.broadcasted_iota(jnp.int32, u.shape, 0)
    up = jnp.where(row == 0, prev_row, pltpu.roll(u, 1, 0))
    un = jnp.where(row == n - 1, next_row, pltpu.roll(u, n - 1, 0))
    up = jnp.where(first, 0.0, up)
    un = jnp.where(last, 0.0, un)
    return up * w[0:1, :] + u * w[1:2, :] + un * w[2:3, :] + b


def _filt_body(n, tl, f_ref, w1_ref, b1_ref, fr_ref, w2_ref, b2_ref, w3a_ref, w3b_ref, dec_ref, o_ref):
    h = tl // 2
    f = jnp.concatenate([f_ref[0:h, :], f_ref[h:tl, :]], axis=1)
    fr = fr_ref[...]
    z = jnp.sin(fr * (_dot(f, w1_ref[...]) + b1_ref[...]))
    z = jnp.sin(fr * (_dot(z, w2_ref[...]) + b2_ref[...]))
    dec = jnp.abs(dec_ref[...])
    top = _dot(z, w3a_ref[...]) * jnp.exp(-f[:, 0:1] * dec)
    bot = _dot(z, w3b_ref[...]) * jnp.exp(-f[:, 128:129] * dec)
    filt = jnp.concatenate([top, bot], axis=0)
    row = pl.program_id(0) * tl + lax.broadcasted_iota(jnp.int32, filt.shape, 0)
    o_ref[...] = jnp.where(row == n, 0.0, filt)


def _filter_params(hy_w1, hy_b1, hy_freq, hy_w2, hy_b2, hy_w3):
    hid = FILT_HID
    w1 = jnp.pad(hy_w1, ((0, 0), (0, 128 - hy_w1.shape[1]), (0, 0)))
    zero = jnp.zeros_like
    w1b = jnp.concatenate([jnp.concatenate([w1, zero(w1)], axis=2), jnp.concatenate([zero(w1), w1], axis=2)], axis=1)
    w2b = jnp.concatenate([jnp.concatenate([hy_w2, zero(hy_w2)], axis=2), jnp.concatenate([zero(hy_w2), hy_w2], axis=2)], axis=1)
    w3a = jnp.concatenate([hy_w3, zero(hy_w3)], axis=1)
    w3b = jnp.concatenate([zero(hy_w3), hy_w3], axis=1)
    twice = lambda v: jnp.concatenate([v, v], axis=1).reshape(DEPTH, 1, 2 * hid)
    return w1b, twice(hy_b1), twice(hy_freq), w2b, twice(hy_b2), w3a, w3b


def _circular_filters(feats_circ, params, layer, dec):
    n = feats_circ.shape[0] // 2
    tl = min(n, 512)
    w1b, b1, fr, w2b, b2, w3a, w3b = params
    per = lambda shape: pl.BlockSpec((None,) + shape, lambda i: (layer, 0, 0))
    half = pl.BlockSpec((None, 128, 2 * HY), lambda i: (layer, 0, i // (n // tl)))
    return pl.pallas_call(
        functools.partial(_filt_body, n, tl),
        grid=(2 * n // tl,),
        in_specs=[pl.BlockSpec((tl, 128), lambda i: (i, 0)), per((256, 128)), per((1, 128)), per((1, 128)),
                  per((128, 128)), per((1, 128)), half, half, pl.BlockSpec((1, 2 * HY), lambda i: (0, i // (n // tl)))],
        out_specs=pl.BlockSpec((tl, 2 * HY), lambda i: (i, 0)),
        out_shape=jax.ShapeDtypeStruct((2 * n, 2 * HY), F32),
        compiler_params=_cparams(("parallel",)),
        name="hyena_filters",
    )(feats_circ, w1b, b1, fr, w2b, b2, w3a, w3b, dec)


def _pos_feats_circ(n):
    t = jnp.linspace(0.0, 1.0, n, dtype=F32)[:, None]
    ang = (2.0 * math.pi / n) * jnp.arange(n, dtype=F32)[:, None]
    bands = jnp.linspace(1e-4, POS_BANDS - 1, POS_BANDS, dtype=F32)[None, :]
    feats = jnp.concatenate([t, jnp.cos(bands * ang), -jnp.sin(bands * ang)], axis=-1)
    feats = jnp.pad(feats, ((0, 0), (0, 128 - feats.shape[1])))
    return jnp.concatenate([feats, feats[:1], jnp.flip(feats[:n - 1], axis=0)], axis=0)


def _s1_body(x_ref, m_ref, are_ref, aim_ref):
    half = FFT_A * FFT_T
    ncb = x_ref.shape[-1] // HY
    for p in range(S1_TILES):
        xs = x_ref[:, p].reshape(HALF_A * FFT_T, ncb * HY)
        r = _dot(m_ref[p], xs)
        for cb in range(ncb):
            cols = slice(cb * HY, (cb + 1) * HY)
            are_ref[cb, :, p] = r[:half, cols].reshape(FFT_A, FFT_T, HY).astype(BF16)
            aim_ref[cb, :, p] = r[half:, cols].reshape(FFT_A, FFT_T, HY).astype(BF16)


def _dft_stage1(x5, s1_tab):
    nb, ncb = x5.shape[0], x5.shape[-1] // HY
    out = pl.BlockSpec((None, ncb, FFT_A, S1_TILES, FFT_T, HY), lambda b, j: (b, 0, 0, j, 0, 0))
    return pl.pallas_call(
        _s1_body,
        grid=(nb, FFT_B // S1_TILES),
        in_specs=[pl.BlockSpec((None, HALF_A, S1_TILES, FFT_T, ncb * HY), lambda b, j: (b, 0, j, 0, 0)),
                  pl.BlockSpec((S1_TILES, 2 * FFT_A * FFT_T, HALF_A * FFT_T), lambda b, j: (j, 0, 0))],
        out_specs=[out, out],
        out_shape=[jax.ShapeDtypeStruct((nb, ncb, FFT_A, FFT_B, FFT_T, HY), BF16)] * 2,
        compiler_params=_cparams(("parallel", "parallel")),
        name="dft_stage1",
    )(x5, s1_tab)


def _s1_pair_body(x_ref, m_ref, are_ref, aim_ref):
    half = FFT_A * FFT_T
    rows = HALF_A * FFT_T
    for p in range(S1_TILES):
        xab = jnp.concatenate([x_ref[0:HALF_A, p].reshape(rows, HY), x_ref[HALF_A:2 * HALF_A, p].reshape(rows, HY)], axis=1)
        r = _dot(m_ref[p], xab)
        are_ref[:, p] = (r[:half, :HY] - r[half:, HY:]).reshape(FFT_A, FFT_T, HY).astype(BF16)
        aim_ref[:, p] = (r[half:, :HY] + r[:half, HY:]).reshape(FFT_A, FFT_T, HY).astype(BF16)


def _row_blocks(x):
    return x.reshape(x.shape[0] // FFT_R, FFT_B, FFT_T, HY)


_PAIR_SPEC = pl.BlockSpec((2 * HALF_A, S1_TILES, FFT_T, HY), lambda b, j: (b, j, 0, 0))


def _dft_stage1_pair(x4, s1_tab):
    npair = NB_BATCH // 2
    out = pl.BlockSpec((None, FFT_A, S1_TILES, FFT_T, HY), lambda b, j: (b, 0, j, 0, 0))
    return pl.pallas_call(
        _s1_pair_body,
        grid=(npair, FFT_B // S1_TILES),
        in_specs=[_PAIR_SPEC, pl.BlockSpec((S1_TILES, 2 * FFT_A * FFT_T, HALF_A * FFT_T), lambda b, j: (j, 0, 0))],
        out_specs=[out, out],
        out_shape=[jax.ShapeDtypeStruct((npair, FFT_A, FFT_B, FFT_T, HY), BF16)] * 2,
        compiler_params=_cparams(("parallel", "parallel")),
        name="dft_stage1_pair",
    )(x4, s1_tab)


def _block_rows(r):
    return slice(r * FFT_R, (r + 1) * FFT_R)


def _stack_complex(re_ref, im_ref, sl):
    return jnp.concatenate([re_ref[sl, :], im_ref[sl, :]], axis=0)


def _spec_body(are_ref, aim_ref, wf_ref, hre_ref, him_ref):
    wf = wf_ref[...]
    for r in range(HB_ROWS // FFT_R):
        sl = _block_rows(r)
        both = jnp.concatenate([_stack_complex(are_ref.at[0], aim_ref.at[0], sl),
                                _stack_complex(are_ref.at[1], aim_ref.at[1], sl)], axis=1)
        x = jnp.dot(wf, both, preferred_element_type=F32)
        first, second = x[:, :HY], x[:, HY:]
        x = first + second if r % 2 == 0 else first - second
        hre_ref[sl, :] = x[:FFT_R]
        him_ref[sl, :] = x[FFT_R:]


def _filter_spectrum(a_re, a_im, wf):
    assert (HB_ROWS // FFT_R) % 2 == 0
    blk_in = pl.BlockSpec((2, None, HB_ROWS, HY), lambda r, o: (0, o, r, 0))
    blk_out = pl.BlockSpec((None, HB_ROWS, HY), lambda r, o: (o, r, 0))
    return pl.pallas_call(
        _spec_body,
        grid=(N_FFT // HB_ROWS, 2),
        in_specs=[blk_in, blk_in, pl.BlockSpec((2 * FFT_R, 2 * FFT_R), lambda r, o: (0, 0))],
        out_specs=[blk_out, blk_out],
        out_shape=[jax.ShapeDtypeStruct((2, N_FFT, HY), F32)] * 2,
        compiler_params=_cparams(("parallel", "parallel")),
        name="filter_spectrum",
    )(a_re, a_im, wf)


def _s2_body(are_ref, aim_ref, hre_ref, him_ref, wf_ref, wi_ref, cre_ref, cim_ref):
    wf, wi = wf_ref[...], wi_ref[...]
    for r in range(0, HB_ROWS // FFT_R, 2):
        sl0, sl1 = _block_rows(r), _block_rows(r + 1)
        a = jnp.concatenate([_stack_complex(are_ref, aim_ref, sl0), _stack_complex(are_ref, aim_ref, sl1)], axis=1)
        x = jnp.dot(wf, a, preferred_element_type=F32)
        xre, xim = x[:FFT_R], x[FFT_R:]
        hre = jnp.concatenate([hre_ref[sl0, :], hre_ref[sl1, :]], axis=1)
        him = jnp.concatenate([him_ref[sl0, :], him_ref[sl1, :]], axis=1)
        y = jnp.concatenate([(xre * hre - xim * him).astype(BF16), (xre * him + xim * hre).astype(BF16)], axis=0)
        c = jnp.dot(wi, y, preferred_element_type=F32)
        for i, sl in enumerate((sl0, sl1)):
            cre_ref[sl, :] = c[:FFT_R, i * HY:(i + 1) * HY].astype(BF16)
            cim_ref[sl, :] = c[FFT_R:, i * HY:(i + 1) * HY].astype(BF16)


def _spectral_filter(a_re, a_im, h_re, h_im, order, wf, wi):
    nb = a_re.shape[0]
    blk = pl.BlockSpec((None, HB_ROWS, HY), lambda r, b: (b, r, 0))
    hblk = pl.BlockSpec((None, HB_ROWS, HY), lambda r, b: (order, r, 0))
    mat = pl.BlockSpec((2 * FFT_R, 2 * FFT_R), lambda r, b: (0, 0))
    return pl.pallas_call(
        _s2_body,
        grid=(N_FFT // HB_ROWS, nb),
        in_specs=[blk, blk, hblk, hblk, mat, mat],
        out_specs=[blk, blk],
        out_shape=[jax.ShapeDtypeStruct((nb, N_FFT, HY), BF16)] * 2,
        compiler_params=_cparams(("parallel", "parallel")),
        name="spectral_filter",
    )(a_re, a_im, h_re, h_im, wf, wi)


def _s1i_body(cre_ref, cim_ref, g_ref, a_ref, b_ref, sk_ref, o_ref):
    rows = FFT_A * FFT_T
    for p in range(S1_TILES):
        cre, cim = cre_ref[:, p].reshape(rows, HY), cim_ref[:, p].reshape(rows, HY)
        c = jnp.concatenate([jnp.concatenate([cre, cim], axis=1), jnp.concatenate([cim, -cre], axis=1)], axis=0)
        y = jnp.dot(g_ref[p], c, preferred_element_type=F32)
        for i in range(2):
            seq = slice(i * HALF_A, (i + 1) * HALF_A)
            yi = y[:, i * HY:(i + 1) * HY].reshape(HALF_A, FFT_T, HY)
            o_ref[seq, p] = a_ref[seq, p] * (yi + b_ref[seq, p] * sk_ref[...])


def _idft_stage1_gate(c_re, c_im, s1i_tab, a4, b4, skip):
    data = _PAIR_SPEC
    spec = pl.BlockSpec((None, FFT_A, S1_TILES, FFT_T, HY), lambda b, j: (b, 0, j, 0, 0))
    return pl.pallas_call(
        _s1i_body,
        grid=(NB_BATCH // 2, FFT_B // S1_TILES),
        in_specs=[spec, spec, pl.BlockSpec((S1_TILES, HALF_A * FFT_T, 2 * FFT_A * FFT_T), lambda b, j: (j, 0, 0)),
                  data, data, pl.BlockSpec((1, 1, HY), lambda b, j: (0, 0, 0))],
        out_specs=data,
        out_shape=jax.ShapeDtypeStruct((T_LAT // FFT_R, FFT_B, FFT_T, HY), F32),
        compiler_params=_cparams(("parallel", "parallel")),
        name="idft_stage1_gate",
    )(c_re, c_im, s1i_tab, a4, b4, skip.reshape(1, 1, HY))


def _hyc_body(v_ref, x1_ref, x2_ref, circ_ref, fc_ref, fs_ref, sk_ref, o_ref):
    fc, fs = fc_ref[...], fs_ref[...]
    circ = circ_ref[...]
    h_re, h_im = _dot(fc, circ), -_dot(fs, circ)

    def long_conv(s, o):
        sl = slice(o * HY, (o + 1) * HY)
        s_re, s_im = _dot(fc[:, :CTX], s), -_dot(fs[:, :CTX], s)
        hre, him = h_re[:, sl], h_im[:, sl]
        y_re = s_re * hre - s_im * him
        y_im = s_re * him + s_im * hre
        y = (_dot(fc[:CTX, :], y_re) - _dot(fs[:CTX, :], y_im)) * (1.0 / N_CTXF)
        return y + s * sk_ref[o:o + 1, :]

    o_ref[...] = x2_ref[...] * long_conv(x1_ref[...] * long_conv(v_ref[...], 0), 1)


def _hyena_ctx(hv, hx1, hx2, circ_c, fc, fs, skip):
    const = lambda shape: pl.BlockSpec(shape, lambda b: (0, 0))
    seq = pl.BlockSpec((CTX, HY), lambda b: (T_LAT // CTX + b, 0))
    return pl.pallas_call(
        _hyc_body,
        grid=(NB_BATCH,),
        in_specs=[seq, seq, seq, const((N_CTXF, 2 * HY)), const((N_CTXF, N_CTXF)), const((N_CTXF, N_CTXF)), const((2, HY))],
        out_specs=pl.BlockSpec((CTX, HY), lambda b: (b, 0)),
        out_shape=jax.ShapeDtypeStruct((T_CTX, HY), F32),
        compiler_params=_cparams(("parallel",)),
        name="hyena_ctx",
    )(hv, hx1, hx2, circ_c, fc, fs, skip)


N_ML_STEPS = CTX // CHUNK + SEQ // CHUNK


def _ml_chain(direction, qv_ref, k_ref, src_col, cum_r, gates_t, c_scr, m_scr, mask, ones_rows):
    base = 8 * direction
    outs = []
    for h in range(ML_H):
        ic, fc = base + h, base + 4 + h
        chain = direction * ML_H + h
        b_row, li_row = cum_r[fc:fc + 1, :], gates_t[ic:ic + 1, :]
        b_end = b_row[:, CHUNK - 1:CHUNK] if direction == 0 else b_row[:, 0:1]
        q_t = qv_ref[HD * h:HD * (h + 1), :]
        vext_t = jnp.concatenate([qv_ref[ML_W + HD * h:ML_W + HD * (h + 1), :], ones_rows], axis=0)
        k = k_ref[:, HD * h:HD * (h + 1)]
        c_prev, m_prev = c_scr[chain], m_scr[chain]
        dmat = jnp.where(mask, src_col[:, fc:fc + 1] + b_row, NEG)
        m_intra = jnp.max(dmat, axis=0, keepdims=True)
        s_t = jnp.dot(k, q_t, preferred_element_type=F32) * jnp.exp(dmat - m_intra)
        inter = b_row + m_prev
        m_t = jnp.maximum(inter, m_intra)
        hx = jnp.exp(m_intra - m_t) * _dot(vext_t, s_t) + jnp.exp(inter - m_t) * _dot(c_prev, q_t)
        den = jnp.maximum(jnp.abs(hx[HD:HD + 1, :]), jnp.exp(-m_t))
        outs.append(hx[:HD, :] / den)
        g_row = b_end - b_row + li_row
        m_new = jnp.maximum(b_end + m_prev, jnp.max(g_row, axis=1, keepdims=True))
        c_scr[chain] = jnp.exp(b_end + m_prev - m_new) * c_prev + _dot(vext_t * jnp.exp(g_row - m_new), k)
        m_scr[chain] = m_new
    return jnp.concatenate(outs, axis=0)


ML_BPS = 2
ML_GROUP_COLS = (NB_BATCH // ML_BPS) * (SEQ + CTX)


def _ml_body(*refs):
    n_in = 8 * ML_BPS
    seq_refs, (bias_ref, bias_t_ref, tril_ref, triu_ref) = refs[:n_in], refs[n_in:n_in + 4]
    (hf_ref, hb_ref), (c_scr, m_scr) = refs[n_in + 4:n_in + 6], refs[n_in + 6:]

    @pl.when(pl.program_id(1) == 0)
    def _():
        c_scr[...] = jnp.zeros_like(c_scr)
        m_scr[...] = jnp.zeros_like(m_scr)

    tril, triu = tril_ref[...], triu_ref[...]
    src = lax.broadcasted_iota(jnp.int32, (CHUNK, CHUNK), 0)
    dst = lax.broadcasted_iota(jnp.int32, (CHUNK, CHUNK), 1)
    ones_rows = (lax.broadcasted_iota(jnp.int32, (HD, CHUNK), 0) == 0).astype(BF16)
    for sub in range(ML_BPS):
        c_sub, m_sub = c_scr.at[sub], m_scr.at[sub]
        for direction in range(2):
            qv_ref, k_ref, g_ref, gt_ref = seq_refs[8 * sub + 4 * direction:8 * sub + 4 * direction + 4]
            gates = g_ref[...] + bias_ref[...]
            gates_t = gt_ref[...] + bias_t_ref[...]
            ls, ls_t = _log_sigmoid(gates), _log_sigmoid(gates_t)
            if direction == 0:
                cum_c, cum_r, mask = _dot_sel_l(tril, ls), _dot_sel_r(ls_t, triu), src <= dst
            else:
                cum_c, cum_r, mask = _dot_sel_l(triu, ls), _dot_sel_r(ls_t, tril), src >= dst
            src_col = pltpu.roll(gates, 4, 1) - cum_c
            o_ref = hf_ref if direction == 0 else hb_ref
            o_ref[sub] = _ml_chain(direction, qv_ref, k_ref, src_col, cum_r, gates_t, c_sub, m_sub, mask, ones_rows)


def _mlstm(mqv_t, mk, mg, mg_t, gate_b):
    nctx = CTX // CHUNK
    nlat = SEQ // CHUNK
    per_group = NB_BATCH // ML_BPS

    def step_chunk(i, backward):
        ctx_chunk = (nctx - 1 - i) if backward else i
        lat_chunk = (N_ML_STEPS - 1 - i) if backward else (i - nctx)
        return i < nctx, ctx_chunk, lat_chunk

    def in_chunk(sub, backward):
        def f(p, i):
            b = p + per_group * sub
            is_ctx, cc, lc = step_chunk(i, backward)
            return jnp.where(is_ctx, T_LAT // CHUNK + nctx * b + cc, nlat * b + lc)
        return f

    def out_chunk(backward):
        def f(p, i):
            is_ctx, cc, lc = step_chunk(i, backward)
            return jnp.where(is_ctx, per_group * nlat + nctx * p + cc, nlat * p + lc)
        return f

    bias = jnp.pad(gate_b, (0, 128 - N_GATES)).reshape(1, 128)
    bias_t = gate_b.reshape(N_GATES, 1)
    tril = jnp.asarray(np.tril(np.ones((CHUNK, CHUNK), np.float32))).astype(BF16)
    triu = jnp.asarray(np.triu(np.ones((CHUNK, CHUNK), np.float32))).astype(BF16)
    const = lambda shape: pl.BlockSpec(shape, lambda p, i: (0, 0))
    ins, args = [], []
    for sub in range(ML_BPS):
        for backward in (False, True):
            f = in_chunk(sub, backward)
            ins += [pl.BlockSpec((2 * ML_W, CHUNK), lambda p, i, f=f: (0, f(p, i))),
                    pl.BlockSpec((CHUNK, ML_W), lambda p, i, f=f: (f(p, i), 0)),
                    pl.BlockSpec((CHUNK, 128), lambda p, i, f=f: (f(p, i), 0)),
                    pl.BlockSpec((N_GATES, CHUNK), lambda p, i, f=f: (0, f(p, i)))]
            args += [mqv_t, mk, mg, mg_t]
    outs = [pl.BlockSpec((ML_BPS, ML_W, CHUNK), lambda p, i, f=out_chunk(bw): (0, 0, f(p, i))) for bw in (False, True)]
    return pl.pallas_call(
        _ml_body,
        grid=(per_group, N_ML_STEPS),
        in_specs=ins + [const((1, 128)), const((N_GATES, 1)), const((CHUNK, CHUNK)), const((CHUNK, CHUNK))],
        out_specs=outs,
        out_shape=[jax.ShapeDtypeStruct((ML_BPS, ML_W, ML_GROUP_COLS), F32)] * 2,
        scratch_shapes=[pltpu.VMEM((ML_BPS, 2 * ML_H, 2 * HD, HD), F32), pltpu.VMEM((ML_BPS, 2 * ML_H, 1, 1), F32)],
        compiler_params=_cparams(("parallel", "arbitrary")),
        name="mlstm",
    )(*args, bias, bias_t, tril, triu)


FF_CHUNKS = ((0, 1536), (1536, D_FF))


def _out_ffn_body(att_ref, hl_ref, hc_ref, hf_ref, hb_ref, mo_ref, mnw_ref, bd_ref, wo_ref, x_ref,
                  g1_ref, sh2_ref, sc2_ref, g2_ref, nw2_ref, wg_ref, wu_ref, wd_ref, o_ref):
    i = pl.program_id(0)
    hn = _head_rms((hf_ref[...] + hb_ref[...]).T, bd_ref[...], mnw_ref[...])
    mlo = hn * _sigmoid(mo_ref[...])
    hy = jnp.where(i < NT_LAT, hl_ref[...], hc_ref[...])
    mixed = jnp.concatenate([att_ref[...].astype(BF16), hy.astype(BF16), mlo.astype(BF16)], axis=1)
    x = x_ref[...] + g1_ref[...] * jnp.dot(mixed, wo_ref[...], preferred_element_type=F32)
    hb = _modulated_norm(x, nw2_ref[...], sc2_ref[...], sh2_ref[...]).astype(BF16)
    acc = None
    for lo, hi in FF_CHUNKS:
        sl = slice(lo, hi)
        a = jnp.dot(hb, wg_ref[:, sl], preferred_element_type=F32)
        u = jnp.dot(hb, wu_ref[:, sl], preferred_element_type=F32)
        part = _dot(a * _sigmoid(a) * u, wd_ref[sl, :])
        acc = part if acc is None else acc + part
    o_ref[...] = x + g2_ref[...] * acc


def _out_ffn(att, hyo, hyoc, hf_t, hb_t, mo, ml_nw, w_out_b, xs, mod4, layer, nw2, wg, wu, wd, with_ctx):
    tiles = NT_ALL if with_ctx else NT_LAT
    row = lambda w: pl.BlockSpec((TM, w), lambda i: (i, 0))
    lat_tiles = (NB_BATCH // ML_BPS) * TILES_PER_SEQ
    col = pl.BlockSpec((None, ML_W, TM), lambda i: (jnp.where(i < NT_LAT, i // lat_tiles, i - NT_LAT), 0,
                                                    jnp.where(i < NT_LAT, i % lat_tiles, lat_tiles)))
    resident = lambda shape: pl.BlockSpec(shape, lambda i: (0, 0), pipeline_mode=pl.Buffered(1))
    layer_w = lambda shape: pl.BlockSpec((None,) + shape, lambda i: (layer, 0, 0), pipeline_mode=pl.Buffered(1))
    vec = lambda w: pl.BlockSpec((1, w), lambda i: (0, 0))
    bd = jnp.asarray(_blockdiag_ones(ML_W)).astype(BF16)
    return pl.pallas_call(
        _out_ffn_body,
        grid=(tiles,),
        in_specs=[row(ATT_W),
                  pl.BlockSpec((TM, HY), lambda i: (jnp.minimum(i, NT_LAT - 1), 0)),
                  pl.BlockSpec((TM, HY), lambda i: (jnp.maximum(i - NT_LAT, 0), 0)),
                  col, col, row(ML_W), vec(ML_W), resident((ML_W, ML_W)), layer_w((D, D)), row(D),
                  _mod_spec(layer, 2), _mod_spec(layer, 3), _mod_spec(layer, 4), _mod_spec(layer, 5), vec(D),
                  layer_w((D, D_FF)), layer_w((D, D_FF)), layer_w((D_FF, D))],
        out_specs=row(D),
        out_shape=jax.ShapeDtypeStruct((tiles * TM, D), F32),
        compiler_params=_cparams(("parallel",), 56),
        name="out_ffn",
    )(att, hyo, hyoc, hf_t, hb_t, mo, ml_nw.reshape(1, ML_W), bd, w_out_b, xs, mod4, mod4, mod4, mod4,
      nw2.reshape(1, D), wg, wu, wd)


def _rope_tables():
    n_rows = SEQ // GRID_W
    row = jnp.repeat(jnp.arange(n_rows), GRID_W)
    col = jnp.tile(jnp.arange(GRID_W), n_rows)
    nf = HD // 4
    inv_freq = ROPE_BASE ** (-jnp.arange(nf, dtype=F32) / nf)
    ang = jnp.stack([row[:, None] * inv_freq, col[:, None] * inv_freq], axis=1)
    cos, sin = jnp.cos(ang), jnp.sin(ang)
    cos_h = jnp.concatenate([cos, cos], axis=-1).reshape(SEQ, HD)
    sin_h = jnp.concatenate([-sin, sin], axis=-1).reshape(SEQ, HD)
    cos_t = jnp.concatenate([jnp.tile(cos_h, (1, 2)), jnp.ones((TM, 128), F32)], axis=0)
    sin_t = jnp.concatenate([jnp.tile(sin_h, (1, 2)), jnp.zeros((TM, 128), F32)], axis=0)
    return cos_t, sin_t


def kernel(x, c, ctx, c_ctx, w_mod, b_mod, norm1_w, norm2_w, w_in, w_out, q_norm_w, k_norm_w, attn_sink, hy_conv_w, hy_conv_b, hy_w1, hy_b1, hy_freq, hy_w2, hy_b2, hy_w3, hy_decay, hy_skip, ml_gate_b, ml_norm_w, ffn_w_gate, ffn_w_up, ffn_w_down):
    xs = jnp.concatenate([x.reshape(T_LAT, D), ctx.reshape(T_CTX, D)], axis=0)
    cc8 = jnp.concatenate([c, c_ctx[None, :], jnp.zeros((8 - NB_BATCH - 1, D), F32)], axis=0)
    mod4 = _modulation(cc8, w_mod, b_mod).reshape(DEPTH, 8, 1, 6 * D)

    cos_t, sin_t = _rope_tables()
    feats_l, feats_c = _pos_feats_circ(SEQ), _pos_feats_circ(CTX)
    s1_tab = jnp.asarray(_TAB["s1"]).astype(BF16)
    s1i_tab = jnp.asarray(_TAB["s1i"]).astype(BF16)
    wf, wi = jnp.asarray(_TAB["wf"]).astype(BF16), jnp.asarray(_TAB["wi"]).astype(BF16)
    fc, fs = jnp.asarray(_TAB["fc"]).astype(BF16), jnp.asarray(_TAB["fs"]).astype(BF16)

    w_in_p = _reorder_w_in(w_in)
    w_out_b = w_out.astype(BF16)
    wg_b, wu_b, wd_b = ffn_w_gate.astype(BF16), ffn_w_up.astype(BF16), ffn_w_down.astype(BF16)
    filt_params = _filter_params(hy_w1, hy_b1, hy_freq, hy_w2, hy_b2, hy_w3)
    sink8 = jnp.pad(attn_sink, ((0, 0), (0, 128 - ATT_H)))[:, None, :] * jnp.ones((1, 8, 1), F32)
    zeros_tile = jnp.zeros((TM, HY), F32)

    out = None
    for l in range(DEPTH):
        last = l == DEPTH - 1
        qt, kn, vt, hv, hx1, hx2, mqv_t, mk, mo, mg, mg_t = _in_proj(
            xs, mod4, l, norm1_w[l], w_in_p, cos_t, sin_t, q_norm_w[l], k_norm_w[l], hy_conv_w[l], hy_conv_b[l])

        att = _attention(qt, kn, vt, sink8[l], not last)

        dec = hy_decay[l].reshape(1, N_FILT)
        circ = _circular_filters(feats_l, filt_params, l, dec)
        f_re, f_im = _dft_stage1(circ.reshape(2, HALF_A, FFT_B, FFT_T, 2 * HY), s1_tab)
        h_re, h_im = _filter_spectrum(f_re.reshape(2, 2, N_FFT, HY), f_im.reshape(2, 2, N_FFT, HY), wf)

        s_in, gate_a = _row_blocks(hv), _row_blocks(hx1)
        for order in range(2):
            a_re, a_im = _dft_stage1_pair(s_in, s1_tab)
            rows = lambda t: t.reshape(NB_BATCH // 2, N_FFT, HY)
            c_re, c_im = _spectral_filter(rows(a_re), rows(a_im), h_re, h_im, order, wf, wi)
            tiles = lambda t: t.reshape(NB_BATCH // 2, FFT_A, FFT_B, FFT_T, HY)
            s_in = _idft_stage1_gate(tiles(c_re), tiles(c_im), s1i_tab, gate_a, s_in, hy_skip[l, order])
            gate_a = _row_blocks(hx2)
        hyo = s_in.reshape(T_LAT, HY)

        if last:
            hyoc = zeros_tile
        else:
            circ_c = _circular_filters(feats_c, filt_params, l, dec)
            hyoc = _hyena_ctx(hv, hx1, hx2, circ_c, fc, fs, hy_skip[l])

        hf_t, hb_t = _mlstm(mqv_t, mk, mg, mg_t, ml_gate_b[l])

        xs = _out_ffn(att, hyo, hyoc, hf_t, hb_t, mo, ml_norm_w[l], w_out_b, xs, mod4, l,
                      norm2_w[l], wg_b, wu_b, wd_b, not last)
        out = xs
    return out.reshape(NB_BATCH, SEQ, D)
```

```python
import functools
import math

import numpy as np
import jax
import jax.numpy as jnp
from jax import lax
from jax.experimental import pallas as pl
from jax.experimental.pallas import tpu as pltpu

F32 = jnp.float32
BF16 = jnp.bfloat16

D = 1024
NB_BATCH = 4
SEQ = 4096
DEPTH = 4
GRID_W = 64
CTX = 256
T_LAT = NB_BATCH * SEQ
T_CTX = NB_BATCH * CTX
R_ALL = T_LAT + T_CTX

HD = 64
ATT_H = 6
ATT_W = ATT_H * HD
KV_W = 2 * HD
BLK = 128
NBLK = SEQ // BLK
ROPE_BASE = 10000.0

HY = 384
N_FILT = 4 * HY
POS_BANDS = 16
FILT_HID = 64

ML_H = 4
ML_W = 256
CHUNK = 128

D_FF = 2816
P_IN = 2832
P_PAD = 2944
NEG = -1e30
EPS = 1e-6
LOG2E = 1.4426950408889634

TM = 512
NT_LAT = T_LAT // TM
NT_ALL = R_ALL // TM
TILES_PER_SEQ = SEQ // TM

N_FFT = 2 * SEQ
FFT_A = 32
FFT_R = 256
FFT_T = 16
FFT_B = FFT_R // FFT_T
HALF_A = FFT_A // 2
S1_TILES = 4
HB_ROWS = 1024
N_CTXF = 2 * CTX


def _np_tables():
    ka = np.arange(FFT_A)[None, :, None, None]
    cc = np.arange(FFT_T)[None, None, :, None]
    aa = np.arange(HALF_A)[None, None, None, :]
    bb = np.arange(FFT_B)[:, None, None, None]
    ph = (ka * (FFT_R * aa + FFT_T * bb + cc)) % N_FFT
    th = 2.0 * np.pi * ph / N_FFT
    eye = np.eye(FFT_T)
    cos = np.einsum("bkca,cd->bkcad", np.cos(th), eye)
    sin = np.einsum("bkca,cd->bkcad", np.sin(th), eye)
    rows, cols = FFT_A * FFT_T, HALF_A * FFT_T
    s1 = np.concatenate([cos.reshape(FFT_B, rows, cols), -sin.reshape(FFT_B, rows, cols)], axis=1)
    s1i = np.concatenate([cos.reshape(FFT_B, rows, cols).transpose(0, 2, 1),
                          -sin.reshape(FFT_B, rows, cols).transpose(0, 2, 1)], axis=2) / N_FFT
    r = np.arange(FFT_R)
    th2 = 2.0 * np.pi * ((r[:, None] * r[None, :]) % FFT_R) / FFT_R
    c2, s2 = np.cos(th2), np.sin(th2)
    wf = np.block([[c2, s2], [-s2, c2]])
    wi = np.block([[c2, -s2], [s2, c2]])
    c = np.arange(N_CTXF)
    th3 = 2.0 * np.pi * ((c[:, None] * c[None, :]) % N_CTXF) / N_CTXF
    return dict(s1=s1.astype(np.float32), s1i=s1i.astype(np.float32), wf=wf.astype(np.float32), wi=wi.astype(np.float32),
                fc=np.cos(th3).astype(np.float32), fs=np.sin(th3).astype(np.float32))


_TAB = _np_tables()


def _blockdiag_ones(width):
    return np.kron(np.eye(width // HD), np.ones((HD, HD))).astype(np.float32)


def _cparams(sem, vmem_mb=48):
    return pltpu.CompilerParams(dimension_semantics=sem, vmem_limit_bytes=vmem_mb * 1024 * 1024)


def _dot(a, b):
    return jnp.dot(a.astype(BF16), b.astype(BF16), preferred_element_type=F32)


def _dot_nt(a, b):
    return lax.dot_general(a.astype(BF16), b.astype(BF16), (((1,), (1,)), ((), ())), preferred_element_type=F32)


def _dot_tn(a, b):
    return lax.dot_general(a.astype(BF16), b.astype(BF16), (((0,), (0,)), ((), ())), preferred_element_type=F32)


def _split3(x):
    x1 = x.astype(BF16)
    r1 = x - x1.astype(F32)
    x2 = r1.astype(BF16)
    x3 = (r1 - x2.astype(F32)).astype(BF16)
    return x1, x2, x3


def _dot_sel_l(sel, x):
    return sum(jnp.dot(sel, p, preferred_element_type=F32) for p in _split3(x))


def _dot_sel_r(x, sel):
    return sum(jnp.dot(p, sel, preferred_element_type=F32) for p in _split3(x))


def _sigmoid(x):
    return 1.0 / (1.0 + jnp.exp(-x))


def _log_sigmoid(x):
    return jnp.minimum(x, 0.0) - jnp.log(1.0 + jnp.exp(-jnp.abs(x)))


def _head_rms(t, bd, w):
    sq = t * t
    hi = sq.astype(BF16)
    lo = (sq - hi.astype(F32)).astype(BF16)
    ss = jnp.dot(hi, bd, preferred_element_type=F32) + jnp.dot(lo, bd, preferred_element_type=F32)
    return t * lax.rsqrt(ss * (1.0 / HD) + EPS) * w


def _mod_row(i):
    return jnp.where(i < NT_LAT, i // TILES_PER_SEQ, NB_BATCH)


def _mod_spec(layer, k):
    return pl.BlockSpec((None, None, 1, D), lambda i: (layer, _mod_row(i), 0, k))


def _mod_body(s_ref, w_ref, b_ref, o_ref):
    s = s_ref[...]
    s = s * _sigmoid(s)
    o_ref[...] = _dot(s, w_ref[...]) + b_ref[...]


def _modulation(cc8, w_mod, b_mod):
    nc = 1536
    return pl.pallas_call(
        _mod_body,
        grid=(DEPTH, 6 * D // nc),
        in_specs=[pl.BlockSpec((8, D), lambda l, j: (0, 0)),
                  pl.BlockSpec((None, D, nc), lambda l, j: (l, 0, j)),
                  pl.BlockSpec((None, 1, nc), lambda l, j: (l, 0, j))],
        out_specs=pl.BlockSpec((None, 8, nc), lambda l, j: (l, 0, j)),
        out_shape=jax.ShapeDtypeStruct((DEPTH, 8, 6 * D), F32),
        compiler_params=_cparams(("parallel", "parallel")),
        name="modulation",
    )(cc8, w_mod, b_mod.reshape(DEPTH, 1, 6 * D))


_COL_ATT, _COL_HY, _COL_MQV, _COL_MK, _COL_MO, _COL_MG = (0, 640), (640, 1792), (1792, 2304), (2304, 2560), (2560, 2816), (2816, 2944)
N_GATES = 4 * ML_H


def _reorder_w_in(w_in):
    mq_end, mk_end, mv_end = 2048, 2304, 2560
    w = jnp.concatenate([w_in[:, :, :mq_end], w_in[:, :, mk_end:mv_end], w_in[:, :, mq_end:mk_end], w_in[:, :, mv_end:]], axis=2)
    return jnp.pad(w, ((0, 0), (0, 0), (0, P_PAD - P_IN))).astype(BF16)


def _modulated_norm(x, nw, sc, sh):
    ms = jnp.mean(x * x, axis=-1, keepdims=True)
    return (x * lax.rsqrt(ms + EPS)) * (nw * (1.0 + sc)) + sh


HALO = 8


def _in_body(x_ref, xp_ref, xn_ref, sh_ref, sc_ref, nw_ref, w_ref, cos_ref, sin_ref, qw_ref, kw_ref, bdq_ref, bdk_ref,
             cw_ref, cb_ref, qt_ref, k_ref, vt_ref, hv_ref, hx1_ref, hx2_ref, mqvt_ref, mk_ref, mo_ref, mg_ref, mgt_ref):
    i = pl.program_id(0)
    nw, sc, sh = nw_ref[...], sc_ref[...], sh_ref[...]
    hb = _modulated_norm(x_ref[...], nw, sc, sh).astype(BF16)
    proj = lambda cols: jnp.dot(hb, w_ref[:, cols[0]:cols[1]], preferred_element_type=F32)

    att = proj(_COL_ATT)
    cos, sin = cos_ref[...], sin_ref[...]
    lane = lax.broadcasted_iota(jnp.int32, (TM, 128), 1)
    first_half = (lane & 31) < 16
    q = _head_rms(att[:, 0:ATT_W], bdq_ref[...], qw_ref[...])
    k = _head_rms(att[:, ATT_W:ATT_W + KV_W], bdk_ref[...], kw_ref[...])
    qt_ref[...] = (_rope(q, cos, sin, first_half) * (HD ** -0.5 * LOG2E)).T.astype(BF16)
    k_ref[...] = _rope(k, cos, sin, first_half).astype(BF16)
    vt_ref[...] = att[:, ATT_W + KV_W:ATT_W + 2 * KV_W].T.astype(BF16)

    hy = proj(_COL_HY)
    halo = jnp.concatenate([xp_ref[...], xn_ref[...]], axis=0)
    hy_halo = jnp.dot(_modulated_norm(halo, nw, sc, sh).astype(BF16), w_ref[:, _COL_HY[0]:_COL_HY[1]],
                      preferred_element_type=F32)
    cw = cw_ref[...]
    usc = _short_conv_wrapped(hy, cw, cb_ref[...])
    seq_len = jnp.where(i < NT_LAT, SEQ, CTX)
    prev_row = jnp.where(((i * TM) & (seq_len - 1)) == 0, 0.0, hy_halo[HALO - 1:HALO, :])
    next_row = jnp.where((((i + 1) * TM) & (seq_len - 1)) == 0, 0.0, hy_halo[HALO:HALO + 1, :])
    row_first = usc[0:1, :] + (prev_row - hy[TM - 1:TM, :]) * cw[0:1, :]
    row_last = usc[TM - 1:TM, :] + (next_row - hy[0:1, :]) * cw[2:3, :]
    outs = ((hv_ref, slice(0, HY)), (hx1_ref, slice(HY, 2 * HY)), (hx2_ref, slice(2 * HY, 3 * HY)))
    for o_ref, cols in outs:
        o_ref[...] = usc[:, cols]
        o_ref[0:1, :] = row_first[:, cols]
        o_ref[TM - 1:TM, :] = row_last[:, cols]

    @pl.when(i >= NT_LAT)
    def _():
        for r in range(CTX, TM, CTX):
            end_row = usc[r - 1:r, :] - hy[r:r + 1, :] * cw[2:3, :]
            start_row = usc[r:r + 1, :] - hy[r - 1:r, :] * cw[0:1, :]
            for o_ref, cols in outs:
                o_ref[r - 1:r, :] = end_row[:, cols]
                o_ref[r:r + 1, :] = start_row[:, cols]

    mqvt_ref[...] = proj(_COL_MQV).T.astype(BF16)
    mk_ref[...] = (proj(_COL_MK) * (HD ** -0.5)).astype(BF16)
    mo_ref[...] = proj(_COL_MO)
    mg = proj(_COL_MG)
    mg_ref[...] = mg
    mgt_ref[...] = mg.T[0:N_GATES, :]


def _in_proj(xs, mod4, layer, nw, w_in_p, cos_t, sin_t, qw, kw, conv_w, conv_b):
    rows = lambda w: pl.BlockSpec((TM, w), lambda i: (i, 0))
    cols = lambda h: pl.BlockSpec((h, TM), lambda i: (0, i))
    const = lambda shape: pl.BlockSpec(shape, lambda i: (0, 0))
    tab = lambda i: (jnp.where(i < NT_LAT, i % TILES_PER_SEQ, TILES_PER_SEQ), 0)
    per_tile = TM // HALO
    prev = pl.BlockSpec((HALO, D), lambda i: (jnp.maximum(i * per_tile - 1, 0), 0))
    nxt = pl.BlockSpec((HALO, D), lambda i: (jnp.minimum((i + 1) * per_tile, R_ALL // HALO - 1), 0))
    bdq = jnp.asarray(_blockdiag_ones(ATT_W)).astype(BF16)
    bdk = jnp.asarray(_blockdiag_ones(KV_W)).astype(BF16)
    bf = lambda shape: jax.ShapeDtypeStruct(shape, BF16)
    f32 = lambda shape: jax.ShapeDtypeStruct(shape, F32)
    return pl.pallas_call(
        _in_body,
        grid=(NT_ALL,),
        in_specs=[rows(D), prev, nxt, _mod_spec(layer, 0), _mod_spec(layer, 1), const((1, D)),
                  pl.BlockSpec((None, D, P_PAD), lambda i: (layer, 0, 0), pipeline_mode=pl.Buffered(1)),
                  pl.BlockSpec((TM, 128), tab), pl.BlockSpec((TM, 128), tab),
                  const((1, ATT_W)), const((1, KV_W)), const((ATT_W, ATT_W)), const((KV_W, KV_W)),
                  const((3, 3 * HY)), const((1, 3 * HY))],
        out_specs=[cols(ATT_W), rows(KV_W), cols(KV_W), rows(HY), rows(HY), rows(HY),
                   cols(2 * ML_W), rows(ML_W), rows(ML_W), rows(128), cols(N_GATES)],
        out_shape=[bf((ATT_W, R_ALL)), bf((R_ALL, KV_W)), bf((KV_W, R_ALL)), f32((R_ALL, HY)), f32((R_ALL, HY)), f32((R_ALL, HY)),
                   bf((2 * ML_W, R_ALL)), bf((R_ALL, ML_W)), f32((R_ALL, ML_W)), f32((R_ALL, 128)), f32((N_GATES, R_ALL))],
        compiler_params=_cparams(("parallel",), 56),
        name="in_proj",
    )(xs, xs, xs, mod4, mod4, nw.reshape(1, D), w_in_p, cos_t, sin_t,
      jnp.tile(qw, ATT_H).reshape(1, ATT_W), jnp.tile(kw, 2).reshape(1, KV_W), bdq, bdk, conv_w, conv_b.reshape(1, 3 * HY))


def _rope(t, cos, sin_signed, first_half):
    outs = []
    for c in range(t.shape[1] // 128):
        tc = t[:, c * 128:(c + 1) * 128]
        partner = jnp.where(first_half, pltpu.roll(tc, 112, 1), pltpu.roll(tc, 16, 1))
        outs.append(tc * cos + partner * sin_signed)
    return outs[0] if len(outs) == 1 else jnp.concatenate(outs, axis=1)


def _att_heads(qt, kv_list, sink_ref, o_ref):
    nq = qt.shape[1]
    rep = ATT_H // 2
    zeros = jnp.zeros((HD, rep * nq), BF16)
    outs = []
    for g in range(2):
        heads = range(rep * g, rep * (g + 1))
        qg = jnp.concatenate([qt[HD * h:HD * (h + 1), :] for h in heads], axis=1)
        qe = jnp.concatenate([qg, zeros] if g == 0 else [zeros, qg], axis=0)
        sink = jnp.concatenate([jnp.broadcast_to(sink_ref[0:1, h:h + 1], (1, nq)) for h in heads], axis=1) * LOG2E
        scores = []
        m = sink
        for k, _, mask in kv_list:
            s = jnp.dot(k, qe, preferred_element_type=F32)
            if mask is not None:
                s = jnp.where(mask, s, NEG)
            scores.append(s)
            m = jnp.maximum(m, jnp.max(s, axis=0, keepdims=True))
        acc = None
        for s, (_, vt, _) in zip(scores, kv_list):
            p = jnp.exp2(s - m).astype(BF16)
            ones = (lax.broadcasted_iota(jnp.int32, (16, vt.shape[1]), 0) == 0).astype(BF16)
            pv = jnp.dot(jnp.concatenate([vt[HD * g:HD * (g + 1), :], ones], axis=0), p, preferred_element_type=F32)
            acc = pv if acc is None else acc + pv
        og = acc[:HD, :] / (acc[HD:HD + 1, :] + jnp.exp2(sink - m))
        outs += [og[:, nq * i:nq * (i + 1)] for i in range(rep)]
    o_ref[...] = jnp.concatenate(outs, axis=0).T


QB = 2
N_ATT_STEPS = NBLK // QB


def _att_body(qt_ref, k0_ref, k1_ref, k2_ref, k3_ref, v0_ref, v1_ref, v2_ref, v3_ref, kc_ref, vc_ref, sink_ref, o_ref):
    j = pl.program_id(1)
    kc, vc = kc_ref[...], vc_ref[...]
    out = lambda sub: o_ref.at[sub * BLK:(sub + 1) * BLK, :]

    @pl.when(j < N_ATT_STEPS)
    def _():
        width = (ATT_H // 2) * BLK
        key = lax.broadcasted_iota(jnp.int32, (BLK, width), 0)
        qry = lax.broadcasted_iota(jnp.int32, (BLK, width), 1) & (BLK - 1)
        ks = (k0_ref, k1_ref, k2_ref, k3_ref)
        vs = (v0_ref, v1_ref, v2_ref, v3_ref)
        for sub in range(QB):
            blk = QB * j + sub
            mask_prev = jnp.logical_and(key >= qry, blk > 0)
            mask_next = jnp.logical_and(key <= qry, blk < NBLK - 1)
            _att_heads(qt_ref[:, sub * BLK:(sub + 1) * BLK],
                       [(ks[sub][...], vs[sub][...], mask_prev), (ks[sub + 1][...], vs[sub + 1][...], None),
                        (ks[sub + 2][...], vs[sub + 2][...], mask_next), (kc, vc, None)], sink_ref, out(sub))

    @pl.when(j >= N_ATT_STEPS)
    def _():
        for sub in range(QB):
            _att_heads(qt_ref[:, sub * BLK:(sub + 1) * BLK], [(kc, vc, None)], sink_ref, out(sub))


def _attention(qt, kn, vt, sink8, with_ctx):
    assert CTX == QB * BLK
    steps = N_ATT_STEPS + (1 if with_ctx else 0)
    rows = R_ALL if with_ctx else T_LAT
    qrows = QB * BLK

    def qblk(b, j):
        return jnp.where(j < N_ATT_STEPS, b * N_ATT_STEPS + j, T_LAT // qrows + b)

    def band(off):
        return lambda b, j: b * NBLK + jnp.clip(QB * j + off, 0, NBLK - 1)

    kspec = lambda f: pl.BlockSpec((BLK, KV_W), lambda b, j: (f(b, j), 0))
    vspec = lambda f: pl.BlockSpec((KV_W, BLK), lambda b, j: (0, f(b, j)))
    offs = range(-1, QB + 1)
    return pl.pallas_call(
        _att_body,
        grid=(NB_BATCH, steps),
        in_specs=[pl.BlockSpec((ATT_W, qrows), lambda b, j: (0, qblk(b, j)))]
                 + [kspec(band(o)) for o in offs] + [vspec(band(o)) for o in offs]
                 + [pl.BlockSpec((CTX, KV_W), lambda b, j: (T_LAT // CTX + b, 0)),
                    pl.BlockSpec((KV_W, CTX), lambda b, j: (0, T_LAT // CTX + b)),
                    pl.BlockSpec((8, 128), lambda b, j: (0, 0))],
        out_specs=pl.BlockSpec((qrows, ATT_W), lambda b, j: (qblk(b, j), 0)),
        out_shape=jax.ShapeDtypeStruct((rows, ATT_W), F32),
        compiler_params=_cparams(("parallel", "parallel")),
        name="attention",
    )(qt, *([kn] * 4), *([vt] * 4), kn, vt, sink8)


def _short_conv_wrapped(u, w, b):
    n = u.shape[0]
    return pltpu.roll(u, 1, 0) * w[0:1, :] + u * w[1:2, :] + pltpu.roll(u, n - 1, 0) * w[2:3, :] + b


def _filt_body(n, tl, f_ref, w1_ref, b1_ref, fr_ref, w2_ref, b2_ref, w3a_ref, w3b_ref, dec_ref, o_ref):
    h = tl // 2
    f = jnp.concatenate([f_ref[0:h, :], f_ref[h:tl, :]], axis=1)
    fr = fr_ref[...]
    z = jnp.sin(fr * (_dot(f, w1_ref[...]) + b1_ref[...]))
    z = jnp.sin(fr * (_dot(z, w2_ref[...]) + b2_ref[...]))
    dec = jnp.abs(dec_ref[...])
    top = _dot(z, w3a_ref[...]) * jnp.exp(-f[:, 0:1] * dec)
    bot = _dot(z, w3b_ref[...]) * jnp.exp(-f[:, 128:129] * dec)
    filt = jnp.concatenate([top, bot], axis=0)
    row = pl.program_id(0) * tl + lax.broadcasted_iota(jnp.int32, filt.shape, 0)
    o_ref[...] = jnp.where(row == n, 0.0, filt)


def _filter_params(hy_w1, hy_b1, hy_freq, hy_w2, hy_b2, hy_w3):
    hid = FILT_HID
    w1 = jnp.pad(hy_w1, ((0, 0), (0, 128 - hy_w1.shape[1]), (0, 0)))
    zero = jnp.zeros_like
    w1b = jnp.concatenate([jnp.concatenate([w1, zero(w1)], axis=2), jnp.concatenate([zero(w1), w1], axis=2)], axis=1)
    w2b = jnp.concatenate([jnp.concatenate([hy_w2, zero(hy_w2)], axis=2), jnp.concatenate([zero(hy_w2), hy_w2], axis=2)], axis=1)
    w3a = jnp.concatenate([hy_w3, zero(hy_w3)], axis=1)
    w3b = jnp.concatenate([zero(hy_w3), hy_w3], axis=1)
    twice = lambda v: jnp.concatenate([v, v], axis=1).reshape(DEPTH, 1, 2 * hid)
    return w1b, twice(hy_b1), twice(hy_freq), w2b, twice(hy_b2), w3a, w3b


def _circular_filters(feats_circ, params, layer, dec):
    n = feats_circ.shape[0] // 2
    tl = min(n, 512)
    w1b, b1, fr, w2b, b2, w3a, w3b = params
    per = lambda shape: pl.BlockSpec((None,) + shape, lambda i: (layer, 0, 0))
    half = pl.BlockSpec((None, 128, 2 * HY), lambda i: (layer, 0, i // (n // tl)))
    return pl.pallas_call(
        functools.partial(_filt_body, n, tl),
        grid=(2 * n // tl,),
        in_specs=[pl.BlockSpec((tl, 128), lambda i: (i, 0)), per((256, 128)), per((1, 128)), per((1, 128)),
                  per((128, 128)), per((1, 128)), half, half, pl.BlockSpec((1, 2 * HY), lambda i: (0, i // (n // tl)))],
        out_specs=pl.BlockSpec((tl, 2 * HY), lambda i: (i, 0)),
        out_shape=jax.ShapeDtypeStruct((2 * n, 2 * HY), F32),
        compiler_params=_cparams(("parallel",)),
        name="hyena_filters",
    )(feats_circ, w1b, b1, fr, w2b, b2, w3a, w3b, dec)


def _pos_feats_circ(n):
    t = jnp.linspace(0.0, 1.0, n, dtype=F32)[:, None]
    ang = (2.0 * math.pi / n) * jnp.arange(n, dtype=F32)[:, None]
    bands = jnp.linspace(1e-4, POS_BANDS - 1, POS_BANDS, dtype=F32)[None, :]
    feats = jnp.concatenate([t, jnp.cos(bands * ang), -jnp.sin(bands * ang)], axis=-1)
    feats = jnp.pad(feats, ((0, 0), (0, 128 - feats.shape[1])))
    return jnp.concatenate([feats, feats[:1], jnp.flip(feats[:n - 1], axis=0)], axis=0)


def _s1_body(x_ref, m_ref, are_ref, aim_ref):
    half = FFT_A * FFT_T
    ncb = x_ref.shape[-1] // HY
    for p in range(S1_TILES):
        xs = x_ref[:, p].reshape(HALF_A * FFT_T, ncb * HY)
        r = _dot(m_ref[p], xs)
        for cb in range(ncb):
            cols = slice(cb * HY, (cb + 1) * HY)
            are_ref[cb, :, p] = r[:half, cols].reshape(FFT_A, FFT_T, HY).astype(BF16)
            aim_ref[cb, :, p] = r[half:, cols].reshape(FFT_A, FFT_T, HY).astype(BF16)


def _dft_stage1(x5, s1_tab):
    nb, ncb = x5.shape[0], x5.shape[-1] // HY
    out = pl.BlockSpec((None, ncb, FFT_A, S1_TILES, FFT_T, HY), lambda b, j: (b, 0, 0, j, 0, 0))
    return pl.pallas_call(
        _s1_body,
        grid=(nb, FFT_B // S1_TILES),
        in_specs=[pl.BlockSpec((None, HALF_A, S1_TILES, FFT_T, ncb * HY), lambda b, j: (b, 0, j, 0, 0)),
                  pl.BlockSpec((S1_TILES, 2 * FFT_A * FFT_T, HALF_A * FFT_T), lambda b, j: (j, 0, 0))],
        out_specs=[out, out],
        out_shape=[jax.ShapeDtypeStruct((nb, ncb, FFT_A, FFT_B, FFT_T, HY), BF16)] * 2,
        compiler_params=_cparams(("parallel", "parallel")),
        name="dft_stage1",
    )(x5, s1_tab)


def _s1_pair_body(x_ref, m_ref, are_ref, aim_ref):
    half = FFT_A * FFT_T
    rows = HALF_A * FFT_T
    for p in range(S1_TILES):
        xab = jnp.concatenate([x_ref[0:HALF_A, p].reshape(rows, HY), x_ref[HALF_A:2 * HALF_A, p].reshape(rows, HY)], axis=1)
        r = _dot(m_ref[p], xab)
        are_ref[:, p] = (r[:half, :HY] - r[half:, HY:]).reshape(FFT_A, FFT_T, HY).astype(BF16)
        aim_ref[:, p] = (r[half:, :HY] + r[:half, HY:]).reshape(FFT_A, FFT_T, HY).astype(BF16)


def _row_blocks(x):
    return x.reshape(x.shape[0] // FFT_R, FFT_B, FFT_T, HY)


_PAIR_SPEC = pl.BlockSpec((2 * HALF_A, S1_TILES, FFT_T, HY), lambda b, j: (b, j, 0, 0))


def _dft_stage1_pair(x4, s1_tab):
    npair = NB_BATCH // 2
    out = pl.BlockSpec((None, FFT_A, S1_TILES, FFT_T, HY), lambda b, j: (b, 0, j, 0, 0))
    return pl.pallas_call(
        _s1_pair_body,
        grid=(npair, FFT_B // S1_TILES),
        in_specs=[_PAIR_SPEC, pl.BlockSpec((S1_TILES, 2 * FFT_A * FFT_T, HALF_A * FFT_T), lambda b, j: (j, 0, 0))],
        out_specs=[out, out],
        out_shape=[jax.ShapeDtypeStruct((npair, FFT_A, FFT_B, FFT_T, HY), BF16)] * 2,
        compiler_params=_cparams(("parallel", "parallel")),
        name="dft_stage1_pair",
    )(x4, s1_tab)


def _block_rows(r):
    return slice(r * FFT_R, (r + 1) * FFT_R)


def _stack_complex(re_ref, im_ref, sl):
    return jnp.concatenate([re_ref[sl, :], im_ref[sl, :]], axis=0)


def _spec_body(are_ref, aim_ref, wf_ref, hre_ref, him_ref):
    wf = wf_ref[...]
    for r in range(HB_ROWS // FFT_R):
        sl = _block_rows(r)
        both = jnp.concatenate([_stack_complex(are_ref.at[0], aim_ref.at[0], sl),
                                _stack_complex(are_ref.at[1], aim_ref.at[1], sl)], axis=1)
        x = jnp.dot(wf, both, preferred_element_type=F32)
        first, second = x[:, :HY], x[:, HY:]
        x = first + second if r % 2 == 0 else first - second
        hre_ref[sl, :] = x[:FFT_R]
        him_ref[sl, :] = x[FFT_R:]


def _filter_spectrum(a_re, a_im, wf):
    assert (HB_ROWS // FFT_R) % 2 == 0
    blk_in = pl.BlockSpec((2, None, HB_ROWS, HY), lambda r, o: (0, o, r, 0))
    blk_out = pl.BlockSpec((None, HB_ROWS, HY), lambda r, o: (o, r, 0))
    return pl.pallas_call(
        _spec_body,
        grid=(N_FFT // HB_ROWS, 2),
        in_specs=[blk_in, blk_in, pl.BlockSpec((2 * FFT_R, 2 * FFT_R), lambda r, o: (0, 0))],
        out_specs=[blk_out, blk_out],
        out_shape=[jax.ShapeDtypeStruct((2, N_FFT, HY), F32)] * 2,
        compiler_params=_cparams(("parallel", "parallel")),
        name="filter_spectrum",
    )(a_re, a_im, wf)


def _s2_body(are_ref, aim_ref, hre_ref, him_ref, wf_ref, wi_ref, cre_ref, cim_ref):
    wf, wi = wf_ref[...], wi_ref[...]
    for r in range(0, HB_ROWS // FFT_R, 2):
        sl0, sl1 = _block_rows(r), _block_rows(r + 1)
        a = jnp.concatenate([_stack_complex(are_ref, aim_ref, sl0), _stack_complex(are_ref, aim_ref, sl1)], axis=1)
        x = jnp.dot(wf, a, preferred_element_type=F32)
        xre, xim = x[:FFT_R], x[FFT_R:]
        hre = jnp.concatenate([hre_ref[sl0, :], hre_ref[sl1, :]], axis=1)
        him = jnp.concatenate([him_ref[sl0, :], him_ref[sl1, :]], axis=1)
        y = jnp.concatenate([(xre * hre - xim * him).astype(BF16), (xre * him + xim * hre).astype(BF16)], axis=0)
        c = jnp.dot(wi, y, preferred_element_type=F32)
        for i, sl in enumerate((sl0, sl1)):
            cre_ref[sl, :] = c[:FFT_R, i * HY:(i + 1) * HY].astype(BF16)
            cim_ref[sl, :] = c[FFT_R:, i * HY:(i + 1) * HY].astype(BF16)


def _spectral_filter(a_re, a_im, h_re, h_im, order, wf, wi):
    nb = a_re.shape[0]
    blk = pl.BlockSpec((None, HB_ROWS, HY), lambda r, b: (b, r, 0))
    hblk = pl.BlockSpec((None, HB_ROWS, HY), lambda r, b: (order, r, 0))
    mat = pl.BlockSpec((2 * FFT_R, 2 * FFT_R), lambda r, b: (0, 0))
    return pl.pallas_call(
        _s2_body,
        grid=(N_FFT // HB_ROWS, nb),
        in_specs=[blk, blk, hblk, hblk, mat, mat],
        out_specs=[blk, blk],
        out_shape=[jax.ShapeDtypeStruct((nb, N_FFT, HY), BF16)] * 2,
        compiler_params=_cparams(("parallel", "parallel")),
        name="spectral_filter",
    )(a_re, a_im, h_re, h_im, wf, wi)


def _s1i_body(cre_ref, cim_ref, g_ref, a_ref, b_ref, sk_ref, o_ref):
    rows = FFT_A * FFT_T
    for p in range(S1_TILES):
        cre, cim = cre_ref[:, p].reshape(rows, HY), cim_ref[:, p].reshape(rows, HY)
        c = jnp.concatenate([jnp.concatenate([cre, cim], axis=1), jnp.concatenate([cim, -cre], axis=1)], axis=0)
        y = jnp.dot(g_ref[p], c, preferred_element_type=F32)
        for i in range(2):
            seq = slice(i * HALF_A, (i + 1) * HALF_A)
            yi = y[:, i * HY:(i + 1) * HY].reshape(HALF_A, FFT_T, HY)
            o_ref[seq, p] = a_ref[seq, p] * (yi + b_ref[seq, p] * sk_ref[...])


def _idft_stage1_gate(c_re, c_im, s1i_tab, a4, b4, skip):
    data = _PAIR_SPEC
    spec = pl.BlockSpec((None, FFT_A, S1_TILES, FFT_T, HY), lambda b, j: (b, 0, j, 0, 0))
    return pl.pallas_call(
        _s1i_body,
        grid=(NB_BATCH // 2, FFT_B // S1_TILES),
        in_specs=[spec, spec, pl.BlockSpec((S1_TILES, HALF_A * FFT_T, 2 * FFT_A * FFT_T), lambda b, j: (j, 0, 0)),
                  data, data, pl.BlockSpec((1, 1, HY), lambda b, j: (0, 0, 0))],
        out_specs=data,
        out_shape=jax.ShapeDtypeStruct((T_LAT // FFT_R, FFT_B, FFT_T, HY), F32),
        compiler_params=_cparams(("parallel", "parallel")),
        name="idft_stage1_gate",
    )(c_re, c_im, s1i_tab, a4, b4, skip.reshape(1, 1, HY))


def _hyc_body(v_ref, x1_ref, x2_ref, circ_ref, fc_ref, fs_ref, sk_ref, o_ref):
    fc, fs = fc_ref[...], fs_ref[...]
    circ = circ_ref[...]
    h_re, h_im = _dot(fc, circ), -_dot(fs, circ)

    def long_conv(s, o):
        sl = slice(o * HY, (o + 1) * HY)
        s_re, s_im = _dot(fc[:, :CTX], s), -_dot(fs[:, :CTX], s)
        hre, him = h_re[:, sl], h_im[:, sl]
        y_re = s_re * hre - s_im * him
        y_im = s_re * him + s_im * hre
        y = (_dot(fc[:CTX, :], y_re) - _dot(fs[:CTX, :], y_im)) * (1.0 / N_CTXF)
        return y + s * sk_ref[o:o + 1, :]

    o_ref[...] = x2_ref[...] * long_conv(x1_ref[...] * long_conv(v_ref[...], 0), 1)


def _hyena_ctx(hv, hx1, hx2, circ_c, fc, fs, skip):
    const = lambda shape: pl.BlockSpec(shape, lambda b: (0, 0))
    seq = pl.BlockSpec((CTX, HY), lambda b: (T_LAT // CTX + b, 0))
    return pl.pallas_call(
        _hyc_body,
        grid=(NB_BATCH,),
        in_specs=[seq, seq, seq, const((N_CTXF, 2 * HY)), const((N_CTXF, N_CTXF)), const((N_CTXF, N_CTXF)), const((2, HY))],
        out_specs=pl.BlockSpec((CTX, HY), lambda b: (b, 0)),
        out_shape=jax.ShapeDtypeStruct((T_CTX, HY), F32),
        compiler_params=_cparams(("parallel",)),
        name="hyena_ctx",
    )(hv, hx1, hx2, circ_c, fc, fs, skip)


N_ML_STEPS = CTX // CHUNK + SEQ // CHUNK


def _ml_chain(direction, qv_ref, k_ref, src_col, cum_r, gates_t, c_scr, m_scr, mask, ones_rows):
    base = 8 * direction
    outs = []
    for h in range(ML_H):
        ic, fc = base + h, base + 4 + h
        chain = direction * ML_H + h
        b_row, li_row = cum_r[fc:fc + 1, :], gates_t[ic:ic + 1, :]
        b_end = b_row[:, CHUNK - 1:CHUNK] if direction == 0 else b_row[:, 0:1]
        q_t = qv_ref[HD * h:HD * (h + 1), :]
        vext_t = jnp.concatenate([qv_ref[ML_W + HD * h:ML_W + HD * (h + 1), :], ones_rows], axis=0)
        k = k_ref[:, HD * h:HD * (h + 1)]
        c_prev, m_prev = c_scr[chain], m_scr[chain]
        dmat = jnp.where(mask, src_col[:, fc:fc + 1] + b_row, NEG)
        m_intra = jnp.max(dmat, axis=0, keepdims=True)
        s_t = jnp.dot(k, q_t, preferred_element_type=F32) * jnp.exp(dmat - m_intra)
        inter = b_row + m_prev
        m_t = jnp.maximum(inter, m_intra)
        hx = jnp.exp(m_intra - m_t) * _dot(vext_t, s_t) + jnp.exp(inter - m_t) * _dot(c_prev, q_t)
        den = jnp.maximum(jnp.abs(hx[HD:HD + 1, :]), jnp.exp(-m_t))
        outs.append(hx[:HD, :] / den)
        g_row = b_end - b_row + li_row
        m_new = jnp.maximum(b_end + m_prev, jnp.max(g_row, axis=1, keepdims=True))
        c_scr[chain] = jnp.exp(b_end + m_prev - m_new) * c_prev + _dot(vext_t * jnp.exp(g_row - m_new), k)
        m_scr[chain] = m_new
    return jnp.concatenate(outs, axis=0)


ML_BPS = 2
ML_GROUP_COLS = (NB_BATCH // ML_BPS) * (SEQ + CTX)


def _ml_body(*refs):
    n_in = 8 * ML_BPS
    seq_refs, (bias_ref, bias_t_ref, tril_ref, triu_ref) = refs[:n_in], refs[n_in:n_in + 4]
    (hf_ref, hb_ref), (c_scr, m_scr) = refs[n_in + 4:n_in + 6], refs[n_in + 6:]

    @pl.when(pl.program_id(1) == 0)
    def _():
        c_scr[...] = jnp.zeros_like(c_scr)
        m_scr[...] = jnp.zeros_like(m_scr)

    tril, triu = tril_ref[...], triu_ref[...]
    src = lax.broadcasted_iota(jnp.int32, (CHUNK, CHUNK), 0)
    dst = lax.broadcasted_iota(jnp.int32, (CHUNK, CHUNK), 1)
    ones_rows = (lax.broadcasted_iota(jnp.int32, (HD, CHUNK), 0) == 0).astype(BF16)
    for sub in range(ML_BPS):
        c_sub, m_sub = c_scr.at[sub], m_scr.at[sub]
        for direction in range(2):
            qv_ref, k_ref, g_ref, gt_ref = seq_refs[8 * sub + 4 * direction:8 * sub + 4 * direction + 4]
            gates = g_ref[...] + bias_ref[...]
            gates_t = gt_ref[...] + bias_t_ref[...]
            ls, ls_t = _log_sigmoid(gates), _log_sigmoid(gates_t)
            if direction == 0:
                cum_c, cum_r, mask = _dot_sel_l(tril, ls), _dot_sel_r(ls_t, triu), src <= dst
            else:
                cum_c, cum_r, mask = _dot_sel_l(triu, ls), _dot_sel_r(ls_t, tril), src >= dst
            src_col = pltpu.roll(gates, 4, 1) - cum_c
            o_ref = hf_ref if direction == 0 else hb_ref
            o_ref[sub] = _ml_chain(direction, qv_ref, k_ref, src_col, cum_r, gates_t, c_sub, m_sub, mask, ones_rows)


def _mlstm(mqv_t, mk, mg, mg_t, gate_b):
    nctx = CTX // CHUNK
    nlat = SEQ // CHUNK
    per_group = NB_BATCH // ML_BPS

    def step_chunk(i, backward):
        ctx_chunk = (nctx - 1 - i) if backward else i
        lat_chunk = (N_ML_STEPS - 1 - i) if backward else (i - nctx)
        return i < nctx, ctx_chunk, lat_chunk

    def in_chunk(sub, backward):
        def f(p, i):
            b = p + per_group * sub
            is_ctx, cc, lc = step_chunk(i, backward)
            return jnp.where(is_ctx, T_LAT // CHUNK + nctx * b + cc, nlat * b + lc)
        return f

    def out_chunk(backward):
        def f(p, i):
            is_ctx, cc, lc = step_chunk(i, backward)
            return jnp.where(is_ctx, per_group * nlat + nctx * p + cc, nlat * p + lc)
        return f

    bias = jnp.pad(gate_b, (0, 128 - N_GATES)).reshape(1, 128)
    bias_t = gate_b.reshape(N_GATES, 1)
    tril = jnp.asarray(np.tril(np.ones((CHUNK, CHUNK), np.float32))).astype(BF16)
    triu = jnp.asarray(np.triu(np.ones((CHUNK, CHUNK), np.float32))).astype(BF16)
    const = lambda shape: pl.BlockSpec(shape, lambda p, i: (0, 0))
    ins, args = [], []
    for sub in range(ML_BPS):
        for backward in (False, True):
            f = in_chunk(sub, backward)
            ins += [pl.BlockSpec((2 * ML_W, CHUNK), lambda p, i, f=f: (0, f(p, i))),
                    pl.BlockSpec((CHUNK, ML_W), lambda p, i, f=f: (f(p, i), 0)),
                    pl.BlockSpec((CHUNK, 128), lambda p, i, f=f: (f(p, i), 0)),
                    pl.BlockSpec((N_GATES, CHUNK), lambda p, i, f=f: (0, f(p, i)))]
            args += [mqv_t, mk, mg, mg_t]
    outs = [pl.BlockSpec((ML_BPS, ML_W, CHUNK), lambda p, i, f=out_chunk(bw): (0, 0, f(p, i))) for bw in (False, True)]
    return pl.pallas_call(
        _ml_body,
        grid=(per_group, N_ML_STEPS),
        in_specs=ins + [const((1, 128)), const((N_GATES, 1)), const((CHUNK, CHUNK)), const((CHUNK, CHUNK))],
        out_specs=outs,
        out_shape=[jax.ShapeDtypeStruct((ML_BPS, ML_W, ML_GROUP_COLS), F32)] * 2,
        scratch_shapes=[pltpu.VMEM((ML_BPS, 2 * ML_H, 2 * HD, HD), F32), pltpu.VMEM((ML_BPS, 2 * ML_H, 1, 1), F32)],
        compiler_params=_cparams(("parallel", "arbitrary")),
        name="mlstm",
    )(*args, bias, bias_t, tril, triu)


FF_CHUNKS = ((0, 1536), (1536, D_FF))


def _out_ffn_body(att_ref, hl_ref, hc_ref, hf_ref, hb_ref, mo_ref, mnw_ref, bd_ref, wo_ref, x_ref,
                  g1_ref, sh2_ref, sc2_ref, g2_ref, nw2_ref, wg_ref, wu_ref, wd_ref, o_ref):
    i = pl.program_id(0)
    hn = _head_rms((hf_ref[...] + hb_ref[...]).T, bd_ref[...], mnw_ref[...])
    mlo = hn * _sigmoid(mo_ref[...])
    hy = jnp.where(i < NT_LAT, hl_ref[...], hc_ref[...])
    mixed = jnp.concatenate([att_ref[...].astype(BF16), hy.astype(BF16), mlo.astype(BF16)], axis=1)
    x = x_ref[...] + g1_ref[...] * jnp.dot(mixed, wo_ref[...], preferred_element_type=F32)
    hb = _modulated_norm(x, nw2_ref[...], sc2_ref[...], sh2_ref[...]).astype(BF16)
    acc = None
    for lo, hi in FF_CHUNKS:
        sl = slice(lo, hi)
        a = jnp.dot(hb, wg_ref[:, sl], preferred_element_type=F32)
        u = jnp.dot(hb, wu_ref[:, sl], preferred_element_type=F32)
        part = _dot(a * _sigmoid(a) * u, wd_ref[sl, :])
        acc = part if acc is None else acc + part
    o_ref[...] = x + g2_ref[...] * acc


def _out_ffn(att, hyo, hyoc, hf_t, hb_t, mo, ml_nw, w_out_b, xs, mod4, layer, nw2, wg, wu, wd, with_ctx):
    tiles = NT_ALL if with_ctx else NT_LAT
    row = lambda w: pl.BlockSpec((TM, w), lambda i: (i, 0))
    lat_tiles = (NB_BATCH // ML_BPS) * TILES_PER_SEQ
    col = pl.BlockSpec((None, ML_W, TM), lambda i: (jnp.where(i < NT_LAT, i // lat_tiles, i - NT_LAT), 0,
                                                    jnp.where(i < NT_LAT, i % lat_tiles, lat_tiles)))
    resident = lambda shape: pl.BlockSpec(shape, lambda i: (0, 0), pipeline_mode=pl.Buffered(1))
    layer_w = lambda shape: pl.BlockSpec((None,) + shape, lambda i: (layer, 0, 0), pipeline_mode=pl.Buffered(1))
    vec = lambda w: pl.BlockSpec((1, w), lambda i: (0, 0))
    bd = jnp.asarray(_blockdiag_ones(ML_W)).astype(BF16)
    return pl.pallas_call(
        _out_ffn_body,
        grid=(tiles,),
        in_specs=[row(ATT_W),
                  pl.BlockSpec((TM, HY), lambda i: (jnp.minimum(i, NT_LAT - 1), 0)),
                  pl.BlockSpec((TM, HY), lambda i: (jnp.maximum(i - NT_LAT, 0), 0)),
                  col, col, row(ML_W), vec(ML_W), resident((ML_W, ML_W)), layer_w((D, D)), row(D),
                  _mod_spec(layer, 2), _mod_spec(layer, 3), _mod_spec(layer, 4), _mod_spec(layer, 5), vec(D),
                  layer_w((D, D_FF)), layer_w((D, D_FF)), layer_w((D_FF, D))],
        out_specs=row(D),
        out_shape=jax.ShapeDtypeStruct((tiles * TM, D), F32),
        compiler_params=_cparams(("parallel",), 56),
        name="out_ffn",
    )(att, hyo, hyoc, hf_t, hb_t, mo, ml_nw.reshape(1, ML_W), bd, w_out_b, xs, mod4, mod4, mod4, mod4,
      nw2.reshape(1, D), wg, wu, wd)


def _rope_tables():
    n_rows = SEQ // GRID_W
    row = jnp.repeat(jnp.arange(n_rows), GRID_W)
    col = jnp.tile(jnp.arange(GRID_W), n_rows)
    nf = HD // 4
    inv_freq = ROPE_BASE ** (-jnp.arange(nf, dtype=F32) / nf)
    ang = jnp.stack([row[:, None] * inv_freq, col[:, None] * inv_freq], axis=1)
    cos, sin = jnp.cos(ang), jnp.sin(ang)
    cos_h = jnp.concatenate([cos, cos], axis=-1).reshape(SEQ, HD)
    sin_h = jnp.concatenate([-sin, sin], axis=-1).reshape(SEQ, HD)
    cos_t = jnp.concatenate([jnp.tile(cos_h, (1, 2)), jnp.ones((TM, 128), F32)], axis=0)
    sin_t = jnp.concatenate([jnp.tile(sin_h, (1, 2)), jnp.zeros((TM, 128), F32)], axis=0)
    return cos_t, sin_t


def kernel(x, c, ctx, c_ctx, w_mod, b_mod, norm1_w, norm2_w, w_in, w_out, q_norm_w, k_norm_w, attn_sink, hy_conv_w, hy_conv_b, hy_w1, hy_b1, hy_freq, hy_w2, hy_b2, hy_w3, hy_decay, hy_skip, ml_gate_b, ml_norm_w, ffn_w_gate, ffn_w_up, ffn_w_down):
    xs = jnp.concatenate([x.reshape(T_LAT, D), ctx.reshape(T_CTX, D)], axis=0)
    cc8 = jnp.concatenate([c, c_ctx[None, :], jnp.zeros((8 - NB_BATCH - 1, D), F32)], axis=0)
    mod4 = _modulation(cc8, w_mod, b_mod).reshape(DEPTH, 8, 1, 6 * D)

    cos_t, sin_t = _rope_tables()
    feats_l, feats_c = _pos_feats_circ(SEQ), _pos_feats_circ(CTX)
    s1_tab = jnp.asarray(_TAB["s1"]).astype(BF16)
    s1i_tab = jnp.asarray(_TAB["s1i"]).astype(BF16)
    wf, wi = jnp.asarray(_TAB["wf"]).astype(BF16), jnp.asarray(_TAB["wi"]).astype(BF16)
    fc, fs = jnp.asarray(_TAB["fc"]).astype(BF16), jnp.asarray(_TAB["fs"]).astype(BF16)

    w_in_p = _reorder_w_in(w_in)
    w_out_b = w_out.astype(BF16)
    wg_b, wu_b, wd_b = ffn_w_gate.astype(BF16), ffn_w_up.astype(BF16), ffn_w_down.astype(BF16)
    filt_params = _filter_params(hy_w1, hy_b1, hy_freq, hy_w2, hy_b2, hy_w3)
    sink8 = jnp.pad(attn_sink, ((0, 0), (0, 128 - ATT_H)))[:, None, :] * jnp.ones((1, 8, 1), F32)
    zeros_tile = jnp.zeros((TM, HY), F32)

    out = None
    for l in range(DEPTH):
        last = l == DEPTH - 1
        qt, kn, vt, hv, hx1, hx2, mqv_t, mk, mo, mg, mg_t = _in_proj(
            xs, mod4, l, norm1_w[l], w_in_p, cos_t, sin_t, q_norm_w[l], k_norm_w[l], hy_conv_w[l], hy_conv_b[l])

        att = _attention(qt, kn, vt, sink8[l], not last)

        dec = hy_decay[l].reshape(1, N_FILT)
        circ = _circular_filters(feats_l, filt_params, l, dec)
        f_re, f_im = _dft_stage1(circ.reshape(2, HALF_A, FFT_B, FFT_T, 2 * HY), s1_tab)
        h_re, h_im = _filter_spectrum(f_re.reshape(2, 2, N_FFT, HY), f_im.reshape(2, 2, N_FFT, HY), wf)

        s_in, gate_a = _row_blocks(hv), _row_blocks(hx1)
        for order in range(2):
            a_re, a_im = _dft_stage1_pair(s_in, s1_tab)
            rows = lambda t: t.reshape(NB_BATCH // 2, N_FFT, HY)
            c_re, c_im = _spectral_filter(rows(a_re), rows(a_im), h_re, h_im, order, wf, wi)
            tiles = lambda t: t.reshape(NB_BATCH // 2, FFT_A, FFT_B, FFT_T, HY)
            s_in = _idft_stage1_gate(tiles(c_re), tiles(c_im), s1i_tab, gate_a, s_in, hy_skip[l, order])
            gate_a = _row_blocks(hx2)
        hyo = s_in.reshape(T_LAT, HY)

        if last:
            hyoc = zeros_tile
        else:
            circ_c = _circular_filters(feats_c, filt_params, l, dec)
            hyoc = _hyena_ctx(hv, hx1, hx2, circ_c, fc, fs, hy_skip[l])

        hf_t, hb_t = _mlstm(mqv_t, mk, mg, mg_t, ml_gate_b[l])

        xs = _out_ffn(att, hyo, hyoc, hf_t, hb_t, mo, ml_norm_w[l], w_out_b, xs, mod4, l,
                      norm2_w[l], wg_b, wu_b, wd_b, not last)
        out = xs
    return out.reshape(NB_BATCH, SEQ, D)
```

```python
import functools
import math

import numpy as np
import jax
import jax.numpy as jnp
from jax import lax
from jax.experimental import pallas as pl
from jax.experimental.pallas import tpu as pltpu

F32 = jnp.float32
BF16 = jnp.bfloat16

D = 1024
NB_BATCH = 4
SEQ = 4096
DEPTH = 4
GRID_W = 64
CTX = 256
T_LAT = NB_BATCH * SEQ
T_CTX = NB_BATCH * CTX
R_ALL = T_LAT + T_CTX

HD = 64
ATT_H = 6
ATT_W = ATT_H * HD
KV_W = 2 * HD
BLK = 128
NBLK = SEQ // BLK
ROPE_BASE = 10000.0

HY = 384
N_FILT = 4 * HY
POS_BANDS = 16
FILT_HID = 64

ML_H = 4
ML_W = 256
CHUNK = 128

D_FF = 2816
P_IN = 2832
P_PAD = 2944
NEG = -1e30
EPS = 1e-6
LOG2E = 1.4426950408889634

TM = 512
NT_LAT = T_LAT // TM
NT_ALL = R_ALL // TM
TILES_PER_SEQ = SEQ // TM

N_FFT = 2 * SEQ
FFT_A = 32
FFT_R = 256
FFT_T = 16
FFT_B = FFT_R // FFT_T
HALF_A = FFT_A // 2
S1_TILES = 4
HB_ROWS = 1024
N_CTXF = 2 * CTX


def _np_tables():
    ka = np.arange(FFT_A)[None, :, None, None]
    cc = np.arange(FFT_T)[None, None, :, None]
    aa = np.arange(HALF_A)[None, None, None, :]
    bb = np.arange(FFT_B)[:, None, None, None]
    ph = (ka * (FFT_R * aa + FFT_T * bb + cc)) % N_FFT
    th = 2.0 * np.pi * ph / N_FFT
    eye = np.eye(FFT_T)
    cos = np.einsum("bkca,cd->bkcad", np.cos(th), eye)
    sin = np.einsum("bkca,cd->bkcad", np.sin(th), eye)
    rows, cols = FFT_A * FFT_T, HALF_A * FFT_T
    s1 = np.concatenate([cos.reshape(FFT_B, rows, cols), -sin.reshape(FFT_B, rows, cols)], axis=1)
    s1i = np.concatenate([cos.reshape(FFT_B, rows, cols).transpose(0, 2, 1),
                          -sin.reshape(FFT_B, rows, cols).transpose(0, 2, 1)], axis=2) / N_FFT
    r = np.arange(FFT_R)
    th2 = 2.0 * np.pi * ((r[:, None] * r[None, :]) % FFT_R) / FFT_R
    c2, s2 = np.cos(th2), np.sin(th2)
    wf = np.block([[c2, s2], [-s2, c2]])
    wi = np.block([[c2, -s2], [s2, c2]])
    c = np.arange(N_CTXF)
    th3 = 2.0 * np.pi * ((c[:, None] * c[None, :]) % N_CTXF) / N_CTXF
    return dict(s1=s1.astype(np.float32), s1i=s1i.astype(np.float32), wf=wf.astype(np.float32), wi=wi.astype(np.float32),
                fc=np.cos(th3).astype(np.float32), fs=np.sin(th3).astype(np.float32))


_TAB = _np_tables()


def _blockdiag_ones(width):
    return np.kron(np.eye(width // HD), np.ones((HD, HD))).astype(np.float32)


def _cparams(sem, vmem_mb=48):
    return pltpu.CompilerParams(dimension_semantics=sem, vmem_limit_bytes=vmem_mb * 1024 * 1024)


def _dot(a, b):
    return jnp.dot(a.astype(BF16), b.astype(BF16), preferred_element_type=F32)


def _dot_nt(a, b):
    return lax.dot_general(a.astype(BF16), b.astype(BF16), (((1,), (1,)), ((), ())), preferred_element_type=F32)


def _dot_tn(a, b):
    return lax.dot_general(a.astype(BF16), b.astype(BF16), (((0,), (0,)), ((), ())), preferred_element_type=F32)


def _split3(x):
    x1 = x.astype(BF16)
    r1 = x - x1.astype(F32)
    x2 = r1.astype(BF16)
    x3 = (r1 - x2.astype(F32)).astype(BF16)
    return x1, x2, x3


def _dot_sel_l(sel, x):
    return sum(jnp.dot(sel, p, preferred_element_type=F32) for p in _split3(x))


def _dot_sel_r(x, sel):
    return sum(jnp.dot(p, sel, preferred_element_type=F32) for p in _split3(x))


def _sigmoid(x):
    return 1.0 / (1.0 + jnp.exp(-x))


def _log_sigmoid(x):
    return jnp.minimum(x, 0.0) - jnp.log(1.0 + jnp.exp(-jnp.abs(x)))


def _head_rms(t, bd, w):
    sq = t * t
    hi = sq.astype(BF16)
    lo = (sq - hi.astype(F32)).astype(BF16)
    ss = jnp.dot(hi, bd, preferred_element_type=F32) + jnp.dot(lo, bd, preferred_element_type=F32)
    return t * lax.rsqrt(ss * (1.0 / HD) + EPS) * w


def _mod_row(i):
    return jnp.where(i < NT_LAT, i // TILES_PER_SEQ, NB_BATCH)


def _mod_spec(layer, k):
    return pl.BlockSpec((None, None, 1, D), lambda i: (layer, _mod_row(i), 0, k))


def _mod_body(s_ref, w_ref, b_ref, o_ref):
    s = s_ref[...]
    s = s * _sigmoid(s)
    o_ref[...] = _dot(s, w_ref[...]) + b_ref[...]


def _modulation(cc8, w_mod, b_mod):
    nc = 1536
    return pl.pallas_call(
        _mod_body,
        grid=(DEPTH, 6 * D // nc),
        in_specs=[pl.BlockSpec((8, D), lambda l, j: (0, 0)),
                  pl.BlockSpec((None, D, nc), lambda l, j: (l, 0, j)),
                  pl.BlockSpec((None, 1, nc), lambda l, j: (l, 0, j))],
        out_specs=pl.BlockSpec((None, 8, nc), lambda l, j: (l, 0, j)),
        out_shape=jax.ShapeDtypeStruct((DEPTH, 8, 6 * D), F32),
        compiler_params=_cparams(("parallel", "parallel")),
        name="modulation",
    )(cc8, w_mod, b_mod.reshape(DEPTH, 1, 6 * D))


_COL_ATT, _COL_HY, _COL_MQV, _COL_MK, _COL_MO, _COL_MG = (0, 640), (640, 1792), (1792, 2304), (2304, 2560), (2560, 2816), (2816, 2944)
N_GATES = 4 * ML_H


def _reorder_w_in(w_in):
    mq_end, mk_end, mv_end = 2048, 2304, 2560
    w = jnp.concatenate([w_in[:, :, :mq_end], w_in[:, :, mk_end:mv_end], w_in[:, :, mq_end:mk_end], w_in[:, :, mv_end:]], axis=2)
    return jnp.pad(w, ((0, 0), (0, 0), (0, P_PAD - P_IN))).astype(BF16)


def _modulated_norm(x, nw, sc, sh):
    ms = jnp.mean(x * x, axis=-1, keepdims=True)
    return (x * lax.rsqrt(ms + EPS)) * (nw * (1.0 + sc)) + sh


HALO = 8


def _in_body(x_ref, xp_ref, xn_ref, sh_ref, sc_ref, nw_ref, w_ref, cos_ref, sin_ref, qw_ref, kw_ref, bdq_ref, bdk_ref,
             cw_ref, cb_ref, qt_ref, k_ref, vt_ref, hv_ref, hx1_ref, hx2_ref, mqvt_ref, mk_ref, mo_ref, mg_ref, mgt_ref):
    i = pl.program_id(0)
    nw, sc, sh = nw_ref[...], sc_ref[...], sh_ref[...]
    hb = _modulated_norm(x_ref[...], nw, sc, sh).astype(BF16)
    proj = lambda cols: jnp.dot(hb, w_ref[:, cols[0]:cols[1]], preferred_element_type=F32)

    att = proj(_COL_ATT)
    cos, sin = cos_ref[...], sin_ref[...]
    lane = lax.broadcasted_iota(jnp.int32, (TM, 128), 1)
    first_half = (lane & 31) < 16
    q = _head_rms(att[:, 0:ATT_W], bdq_ref[...], qw_ref[...])
    k = _head_rms(att[:, ATT_W:ATT_W + KV_W], bdk_ref[...], kw_ref[...])
    qt_ref[...] = (_rope(q, cos, sin, first_half) * (HD ** -0.5 * LOG2E)).T.astype(BF16)
    k_ref[...] = _rope(k, cos, sin, first_half).astype(BF16)
    vt_ref[...] = att[:, ATT_W + KV_W:ATT_W + 2 * KV_W].T.astype(BF16)

    halo = _modulated_norm(jnp.concatenate([xp_ref[...], xn_ref[...]], axis=0), nw, sc, sh).astype(BF16)
    hy_all = jnp.dot(jnp.concatenate([hb, halo], axis=0), w_ref[:, _COL_HY[0]:_COL_HY[1]], preferred_element_type=F32)
    hy, hy_halo = hy_all[:TM, :], hy_all[TM:, :]
    seq_len = jnp.where(i < NT_LAT, SEQ, CTX)
    pos = (i * TM + lax.broadcasted_iota(jnp.int32, (TM, 1), 0)) & (seq_len - 1)
    usc = _short_conv_rows(hy, hy_halo[HALO - 1:HALO, :], hy_halo[HALO:HALO + 1, :], cw_ref[...], cb_ref[...],
                           pos == 0, pos == seq_len - 1)
    hv_ref[...] = usc[:, 0:HY]
    hx1_ref[...] = usc[:, HY:2 * HY]
    hx2_ref[...] = usc[:, 2 * HY:3 * HY]

    mqvt_ref[...] = proj(_COL_MQV).T.astype(BF16)
    mk_ref[...] = (proj(_COL_MK) * (HD ** -0.5)).astype(BF16)
    mo_ref[...] = proj(_COL_MO)
    mg = proj(_COL_MG)
    mg_ref[...] = mg
    mgt_ref[...] = mg.T[0:N_GATES, :]


def _in_proj(xs, mod4, layer, nw, w_in_p, cos_t, sin_t, qw, kw, conv_w, conv_b):
    rows = lambda w: pl.BlockSpec((TM, w), lambda i: (i, 0))
    cols = lambda h: pl.BlockSpec((h, TM), lambda i: (0, i))
    const = lambda shape: pl.BlockSpec(shape, lambda i: (0, 0))
    tab = lambda i: (jnp.where(i < NT_LAT, i % TILES_PER_SEQ, TILES_PER_SEQ), 0)
    per_tile = TM // HALO
    prev = pl.BlockSpec((HALO, D), lambda i: (jnp.maximum(i * per_tile - 1, 0), 0))
    nxt = pl.BlockSpec((HALO, D), lambda i: (jnp.minimum((i + 1) * per_tile, R_ALL // HALO - 1), 0))
    bdq = jnp.asarray(_blockdiag_ones(ATT_W)).astype(BF16)
    bdk = jnp.asarray(_blockdiag_ones(KV_W)).astype(BF16)
    bf = lambda shape: jax.ShapeDtypeStruct(shape, BF16)
    f32 = lambda shape: jax.ShapeDtypeStruct(shape, F32)
    return pl.pallas_call(
        _in_body,
        grid=(NT_ALL,),
        in_specs=[rows(D), prev, nxt, _mod_spec(layer, 0), _mod_spec(layer, 1), const((1, D)),
                  pl.BlockSpec((None, D, P_PAD), lambda i: (layer, 0, 0), pipeline_mode=pl.Buffered(1)),
                  pl.BlockSpec((TM, 128), tab), pl.BlockSpec((TM, 128), tab),
                  const((1, ATT_W)), const((1, KV_W)), const((ATT_W, ATT_W)), const((KV_W, KV_W)),
                  const((3, 3 * HY)), const((1, 3 * HY))],
        out_specs=[cols(ATT_W), rows(KV_W), cols(KV_W), rows(HY), rows(HY), rows(HY),
                   cols(2 * ML_W), rows(ML_W), rows(ML_W), rows(128), cols(N_GATES)],
        out_shape=[bf((ATT_W, R_ALL)), bf((R_ALL, KV_W)), bf((KV_W, R_ALL)), f32((R_ALL, HY)), f32((R_ALL, HY)), f32((R_ALL, HY)),
                   bf((2 * ML_W, R_ALL)), bf((R_ALL, ML_W)), f32((R_ALL, ML_W)), f32((R_ALL, 128)), f32((N_GATES, R_ALL))],
        compiler_params=_cparams(("parallel",), 56),
        name="in_proj",
    )(xs, xs, xs, mod4, mod4, nw.reshape(1, D), w_in_p, cos_t, sin_t,
      jnp.tile(qw, ATT_H).reshape(1, ATT_W), jnp.tile(kw, 2).reshape(1, KV_W), bdq, bdk, conv_w, conv_b.reshape(1, 3 * HY))


def _rope(t, cos, sin_signed, first_half):
    outs = []
    for c in range(t.shape[1] // 128):
        tc = t[:, c * 128:(c + 1) * 128]
        partner = jnp.where(first_half, pltpu.roll(tc, 112, 1), pltpu.roll(tc, 16, 1))
        outs.append(tc * cos + partner * sin_signed)
    return outs[0] if len(outs) == 1 else jnp.concatenate(outs, axis=1)


def _att_heads(qt, kv_list, sink_ref, o_ref):
    nq = qt.shape[1]
    rep = ATT_H // 2
    zeros = jnp.zeros((HD, rep * nq), BF16)
    outs = []
    for g in range(2):
        heads = range(rep * g, rep * (g + 1))
        qg = jnp.concatenate([qt[HD * h:HD * (h + 1), :] for h in heads], axis=1)
        qe = jnp.concatenate([qg, zeros] if g == 0 else [zeros, qg], axis=0)
        sink = jnp.concatenate([jnp.broadcast_to(sink_ref[0:1, h:h + 1], (1, nq)) for h in heads], axis=1) * LOG2E
        scores = []
        m = sink
        for k, _, mask in kv_list:
            s = jnp.dot(k, qe, preferred_element_type=F32)
            if mask is not None:
                s = jnp.where(mask, s, NEG)
            scores.append(s)
            m = jnp.maximum(m, jnp.max(s, axis=0, keepdims=True))
        acc = None
        for s, (_, vt, _) in zip(scores, kv_list):
            p = jnp.exp2(s - m).astype(BF16)
            ones = (lax.broadcasted_iota(jnp.int32, (16, vt.shape[1]), 0) == 0).astype(BF16)
            pv = jnp.dot(jnp.concatenate([vt[HD * g:HD * (g + 1), :], ones], axis=0), p, preferred_element_type=F32)
            acc = pv if acc is None else acc + pv
        og = acc[:HD, :] / (acc[HD:HD + 1, :] + jnp.exp2(sink - m))
        outs += [og[:, nq * i:nq * (i + 1)] for i in range(rep)]
    o_ref[...] = jnp.concatenate(outs, axis=0).T


QB = 2
N_ATT_STEPS = NBLK // QB


def _att_body(qt_ref, k0_ref, k1_ref, k2_ref, k3_ref, v0_ref, v1_ref, v2_ref, v3_ref, kc_ref, vc_ref, sink_ref, o_ref):
    j = pl.program_id(1)
    kc, vc = kc_ref[...], vc_ref[...]
    out = lambda sub: o_ref.at[sub * BLK:(sub + 1) * BLK, :]

    @pl.when(j < N_ATT_STEPS)
    def _():
        width = (ATT_H // 2) * BLK
        key = lax.broadcasted_iota(jnp.int32, (BLK, width), 0)
        qry = lax.broadcasted_iota(jnp.int32, (BLK, width), 1) & (BLK - 1)
        ks = (k0_ref, k1_ref, k2_ref, k3_ref)
        vs = (v0_ref, v1_ref, v2_ref, v3_ref)
        for sub in range(QB):
            blk = QB * j + sub
            mask_prev = jnp.logical_and(key >= qry, blk > 0)
            mask_next = jnp.logical_and(key <= qry, blk < NBLK - 1)
            _att_heads(qt_ref[:, sub * BLK:(sub + 1) * BLK],
                       [(ks[sub][...], vs[sub][...], mask_prev), (ks[sub + 1][...], vs[sub + 1][...], None),
                        (ks[sub + 2][...], vs[sub + 2][...], mask_next), (kc, vc, None)], sink_ref, out(sub))

    @pl.when(j >= N_ATT_STEPS)
    def _():
        for sub in range(QB):
            _att_heads(qt_ref[:, sub * BLK:(sub + 1) * BLK], [(kc, vc, None)], sink_ref, out(sub))


def _attention(qt, kn, vt, sink8, with_ctx):
    assert CTX == QB * BLK
    steps = N_ATT_STEPS + (1 if with_ctx else 0)
    rows = R_ALL if with_ctx else T_LAT
    qrows = QB * BLK

    def qblk(b, j):
        return jnp.where(j < N_ATT_STEPS, b * N_ATT_STEPS + j, T_LAT // qrows + b)

    def band(off):
        return lambda b, j: b * NBLK + jnp.clip(QB * j + off, 0, NBLK - 1)

    kspec = lambda f: pl.BlockSpec((BLK, KV_W), lambda b, j: (f(b, j), 0))
    vspec = lambda f: pl.BlockSpec((KV_W, BLK), lambda b, j: (0, f(b, j)))
    offs = range(-1, QB + 1)
    return pl.pallas_call(
        _att_body,
        grid=(NB_BATCH, steps),
        in_specs=[pl.BlockSpec((ATT_W, qrows), lambda b, j: (0, qblk(b, j)))]
                 + [kspec(band(o)) for o in offs] + [vspec(band(o)) for o in offs]
                 + [pl.BlockSpec((CTX, KV_W), lambda b, j: (T_LAT // CTX + b, 0)),
                    pl.BlockSpec((KV_W, CTX), lambda b, j: (0, T_LAT // CTX + b)),
                    pl.BlockSpec((8, 128), lambda b, j: (0, 0))],
        out_specs=pl.BlockSpec((qrows, ATT_W), lambda b, j: (qblk(b, j), 0)),
        out_shape=jax.ShapeDtypeStruct((rows, ATT_W), F32),
        compiler_params=_cparams(("parallel", "parallel")),
        name="attention",
    )(qt, *([kn] * 4), *([vt] * 4), kn, vt, sink8)


def _short_conv_rows(u, prev_row, next_row, w, b, first, last):
    n = u.shape[0]
    row = lax.broadcasted_iota(jnp.int32, u.shape, 0)
    up = jnp.where(row == 0, prev_row, pltpu.roll(u, 1, 0))
    un = jnp.where(row == n - 1, next_row, pltpu.roll(u, n - 1, 0))
    up = jnp.where(first, 0.0, up)
    un = jnp.where(last, 0.0, un)
    return up * w[0:1, :] + u * w[1:2, :] + un * w[2:3, :] + b


def _filt_body(n, tl, f_ref, w1_ref, b1_ref, fr_ref, w2_ref, b2_ref, w3a_ref, w3b_ref, dec_ref, o_ref):
    h = tl // 2
    f = jnp.concatenate([f_ref[0:h, :], f_ref[h:tl, :]], axis=1)
    fr = fr_ref[...]
    z = jnp.sin(fr * (_dot(f, w1_ref[...]) + b1_ref[...]))
    z = jnp.sin(fr * (_dot(z, w2_ref[...]) + b2_ref[...]))
    dec = jnp.abs(dec_ref[...])
    top = _dot(z, w3a_ref[...]) * jnp.exp(-f[:, 0:1] * dec)
    bot = _dot(z, w3b_ref[...]) * jnp.exp(-f[:, 128:129] * dec)
    filt = jnp.concatenate([top, bot], axis=0)
    row = pl.program_id(0) * tl + lax.broadcasted_iota(jnp.int32, filt.shape, 0)
    o_ref[...] = jnp.where(row == n, 0.0, filt)


def _filter_params(hy_w1, hy_b1, hy_freq, hy_w2, hy_b2, hy_w3):
    hid = FILT_HID
    w1 = jnp.pad(hy_w1, ((0, 0), (0, 128 - hy_w1.shape[1]), (0, 0)))
    zero = jnp.zeros_like
    w1b = jnp.concatenate([jnp.concatenate([w1, zero(w1)], axis=2), jnp.concatenate([zero(w1), w1], axis=2)], axis=1)
    w2b = jnp.concatenate([jnp.concatenate([hy_w2, zero(hy_w2)], axis=2), jnp.concatenate([zero(hy_w2), hy_w2], axis=2)], axis=1)
    w3a = jnp.concatenate([hy_w3, zero(hy_w3)], axis=1)
    w3b = jnp.concatenate([zero(hy_w3), hy_w3], axis=1)
    twice = lambda v: jnp.concatenate([v, v], axis=1).reshape(DEPTH, 1, 2 * hid)
    return w1b, twice(hy_b1), twice(hy_freq), w2b, twice(hy_b2), w3a, w3b


def _circular_filters(feats_circ, params, layer, dec):
    n = feats_circ.shape[0] // 2
    tl = min(n, 512)
    w1b, b1, fr, w2b, b2, w3a, w3b = params
    per = lambda shape: pl.BlockSpec((None,) + shape, lambda i: (layer, 0, 0))
    half = pl.BlockSpec((None, 128, 2 * HY), lambda i: (layer, 0, i // (n // tl)))
    return pl.pallas_call(
        functools.partial(_filt_body, n, tl),
        grid=(2 * n // tl,),
        in_specs=[pl.BlockSpec((tl, 128), lambda i: (i, 0)), per((256, 128)), per((1, 128)), per((1, 128)),
                  per((128, 128)), per((1, 128)), half, half, pl.BlockSpec((1, 2 * HY), lambda i: (0, i // (n // tl)))],
        out_specs=pl.BlockSpec((tl, 2 * HY), lambda i: (i, 0)),
        out_shape=jax.ShapeDtypeStruct((2 * n, 2 * HY), F32),
        compiler_params=_cparams(("parallel",)),
        name="hyena_filters",
    )(feats_circ, w1b, b1, fr, w2b, b2, w3a, w3b, dec)


def _pos_feats_circ(n):
    t = jnp.linspace(0.0, 1.0, n, dtype=F32)[:, None]
    ang = (2.0 * math.pi / n) * jnp.arange(n, dtype=F32)[:, None]
    bands = jnp.linspace(1e-4, POS_BANDS - 1, POS_BANDS, dtype=F32)[None, :]
    feats = jnp.concatenate([t, jnp.cos(bands * ang), -jnp.sin(bands * ang)], axis=-1)
    feats = jnp.pad(feats, ((0, 0), (0, 128 - feats.shape[1])))
    return jnp.concatenate([feats, feats[:1], jnp.flip(feats[:n - 1], axis=0)], axis=0)


def _s1_body(x_ref, m_ref, are_ref, aim_ref):
    half = FFT_A * FFT_T
    ncb = x_ref.shape[-1] // HY
    for p in range(S1_TILES):
        xs = x_ref[:, p].reshape(HALF_A * FFT_T, ncb * HY)
        r = _dot(m_ref[p], xs)
        for cb in range(ncb):
            cols = slice(cb * HY, (cb + 1) * HY)
            are_ref[cb, :, p] = r[:half, cols].reshape(FFT_A, FFT_T, HY).astype(BF16)
            aim_ref[cb, :, p] = r[half:, cols].reshape(FFT_A, FFT_T, HY).astype(BF16)


def _dft_stage1(x5, s1_tab):
    nb, ncb = x5.shape[0], x5.shape[-1] // HY
    out = pl.BlockSpec((None, ncb, FFT_A, S1_TILES, FFT_T, HY), lambda b, j: (b, 0, 0, j, 0, 0))
    return pl.pallas_call(
        _s1_body,
        grid=(nb, FFT_B // S1_TILES),
        in_specs=[pl.BlockSpec((None, HALF_A, S1_TILES, FFT_T, ncb * HY), lambda b, j: (b, 0, j, 0, 0)),
                  pl.BlockSpec((S1_TILES, 2 * FFT_A * FFT_T, HALF_A * FFT_T), lambda b, j: (j, 0, 0))],
        out_specs=[out, out],
        out_shape=[jax.ShapeDtypeStruct((nb, ncb, FFT_A, FFT_B, FFT_T, HY), BF16)] * 2,
        compiler_params=_cparams(("parallel", "parallel")),
        name="dft_stage1",
    )(x5, s1_tab)


def _s1_pair_body(x_ref, m_ref, are_ref, aim_ref):
    half = FFT_A * FFT_T
    rows = HALF_A * FFT_T
    for p in range(S1_TILES):
        xab = jnp.concatenate([x_ref[0:HALF_A, p].reshape(rows, HY), x_ref[HALF_A:2 * HALF_A, p].reshape(rows, HY)], axis=1)
        r = _dot(m_ref[p], xab)
        are_ref[:, p] = (r[:half, :HY] - r[half:, HY:]).reshape(FFT_A, FFT_T, HY).astype(BF16)
        aim_ref[:, p] = (r[half:, :HY] + r[:half, HY:]).reshape(FFT_A, FFT_T, HY).astype(BF16)


def _row_blocks(x):
    return x.reshape(x.shape[0] // FFT_R, FFT_B, FFT_T, HY)


_PAIR_SPEC = pl.BlockSpec((2 * HALF_A, S1_TILES, FFT_T, HY), lambda b, j: (b, j, 0, 0))


def _dft_stage1_pair(x4, s1_tab):
    npair = NB_BATCH // 2
    out = pl.BlockSpec((None, FFT_A, S1_TILES, FFT_T, HY), lambda b, j: (b, 0, j, 0, 0))
    return pl.pallas_call(
        _s1_pair_body,
        grid=(npair, FFT_B // S1_TILES),
        in_specs=[_PAIR_SPEC, pl.BlockSpec((S1_TILES, 2 * FFT_A * FFT_T, HALF_A * FFT_T), lambda b, j: (j, 0, 0))],
        out_specs=[out, out],
        out_shape=[jax.ShapeDtypeStruct((npair, FFT_A, FFT_B, FFT_T, HY), BF16)] * 2,
        compiler_params=_cparams(("parallel", "parallel")),
        name="dft_stage1_pair",
    )(x4, s1_tab)


def _block_rows(r):
    return slice(r * FFT_R, (r + 1) * FFT_R)


def _stack_complex(re_ref, im_ref, sl):
    return jnp.concatenate([re_ref[sl, :], im_ref[sl, :]], axis=0)


def _spec_body(are_ref, aim_ref, wf_ref, hre_ref, him_ref):
    wf = wf_ref[...]
    for r in range(HB_ROWS // FFT_R):
        sl = _block_rows(r)
        both = jnp.concatenate([_stack_complex(are_ref.at[0], aim_ref.at[0], sl),
                                _stack_complex(are_ref.at[1], aim_ref.at[1], sl)], axis=1)
        x = jnp.dot(wf, both, preferred_element_type=F32)
        first, second = x[:, :HY], x[:, HY:]
        x = first + second if r % 2 == 0 else first - second
        hre_ref[sl, :] = x[:FFT_R]
        him_ref[sl, :] = x[FFT_R:]


def _filter_spectrum(a_re, a_im, wf):
    assert (HB_ROWS // FFT_R) % 2 == 0
    blk_in = pl.BlockSpec((2, None, HB_ROWS, HY), lambda r, o: (0, o, r, 0))
    blk_out = pl.BlockSpec((None, HB_ROWS, HY), lambda r, o: (o, r, 0))
    return pl.pallas_call(
        _spec_body,
        grid=(N_FFT // HB_ROWS, 2),
        in_specs=[blk_in, blk_in, pl.BlockSpec((2 * FFT_R, 2 * FFT_R), lambda r, o: (0, 0))],
        out_specs=[blk_out, blk_out],
        out_shape=[jax.ShapeDtypeStruct((2, N_FFT, HY), F32)] * 2,
        compiler_params=_cparams(("parallel", "parallel")),
        name="filter_spectrum",
    )(a_re, a_im, wf)


def _s2_body(are_ref, aim_ref, hre_ref, him_ref, wf_ref, wi_ref, cre_ref, cim_ref):
    wf, wi = wf_ref[...], wi_ref[...]
    for r in range(0, HB_ROWS // FFT_R, 2):
        sl0, sl1 = _block_rows(r), _block_rows(r + 1)
        a = jnp.concatenate([_stack_complex(are_ref, aim_ref, sl0), _stack_complex(are_ref, aim_ref, sl1)], axis=1)
        x = jnp.dot(wf, a, preferred_element_type=F32)
        xre, xim = x[:FFT_R], x[FFT_R:]
        hre = jnp.concatenate([hre_ref[sl0, :], hre_ref[sl1, :]], axis=1)
        him = jnp.concatenate([him_ref[sl0, :], him_ref[sl1, :]], axis=1)
        y = jnp.concatenate([(xre * hre - xim * him).astype(BF16), (xre * him + xim * hre).astype(BF16)], axis=0)
        c = jnp.dot(wi, y, preferred_element_type=F32)
        for i, sl in enumerate((sl0, sl1)):
            cre_ref[sl, :] = c[:FFT_R, i * HY:(i + 1) * HY].astype(BF16)
            cim_ref[sl, :] = c[FFT_R:, i * HY:(i + 1) * HY].astype(BF16)


def _spectral_filter(a_re, a_im, h_re, h_im, order, wf, wi):
    nb = a_re.shape[0]
    blk = pl.BlockSpec((None, HB_ROWS, HY), lambda r, b: (b, r, 0))
    hblk = pl.BlockSpec((None, HB_ROWS, HY), lambda r, b: (order, r, 0))
    mat = pl.BlockSpec((2 * FFT_R, 2 * FFT_R), lambda r, b: (0, 0))
    return pl.pallas_call(
        _s2_body,
        grid=(N_FFT // HB_ROWS, nb),
        in_specs=[blk, blk, hblk, hblk, mat, mat],
        out_specs=[blk, blk],
        out_shape=[jax.ShapeDtypeStruct((nb, N_FFT, HY), BF16)] * 2,
        compiler_params=_cparams(("parallel", "parallel")),
        name="spectral_filter",
    )(a_re, a_im, h_re, h_im, wf, wi)


def _s1i_body(cre_ref, cim_ref, g_ref, a_ref, b_ref, sk_ref, o_ref):
    rows = FFT_A * FFT_T
    for p in range(S1_TILES):
        cre, cim = cre_ref[:, p].reshape(rows, HY), cim_ref[:, p].reshape(rows, HY)
        c = jnp.concatenate([jnp.concatenate([cre, cim], axis=1), jnp.concatenate([cim, -cre], axis=1)], axis=0)
        y = jnp.dot(g_ref[p], c, preferred_element_type=F32)
        for i in range(2):
            seq = slice(i * HALF_A, (i + 1) * HALF_A)
            yi = y[:, i * HY:(i + 1) * HY].reshape(HALF_A, FFT_T, HY)
            o_ref[seq, p] = a_ref[seq, p] * (yi + b_ref[seq, p] * sk_ref[...])


def _idft_stage1_gate(c_re, c_im, s1i_tab, a4, b4, skip):
    data = _PAIR_SPEC
    spec = pl.BlockSpec((None, FFT_A, S1_TILES, FFT_T, HY), lambda b, j: (b, 0, j, 0, 0))
    return pl.pallas_call(
        _s1i_body,
        grid=(NB_BATCH // 2, FFT_B // S1_TILES),
        in_specs=[spec, spec, pl.BlockSpec((S1_TILES, HALF_A * FFT_T, 2 * FFT_A * FFT_T), lambda b, j: (j, 0, 0)),
                  data, data, pl.BlockSpec((1, 1, HY), lambda b, j: (0, 0, 0))],
        out_specs=data,
        out_shape=jax.ShapeDtypeStruct((T_LAT // FFT_R, FFT_B, FFT_T, HY), F32),
        compiler_params=_cparams(("parallel", "parallel")),
        name="idft_stage1_gate",
    )(c_re, c_im, s1i_tab, a4, b4, skip.reshape(1, 1, HY))


def _hyc_body(v_ref, x1_ref, x2_ref, circ_ref, fc_ref, fs_ref, sk_ref, o_ref):
    fc, fs = fc_ref[...], fs_ref[...]
    circ = circ_ref[...]
    h_re, h_im = _dot(fc, circ), -_dot(fs, circ)

    def long_conv(s, o):
        sl = slice(o * HY, (o + 1) * HY)
        s_re, s_im = _dot(fc[:, :CTX], s), -_dot(fs[:, :CTX], s)
        hre, him = h_re[:, sl], h_im[:, sl]
        y_re = s_re * hre - s_im * him
        y_im = s_re * him + s_im * hre
        y = (_dot(fc[:CTX, :], y_re) - _dot(fs[:CTX, :], y_im)) * (1.0 / N_CTXF)
        return y + s * sk_ref[o:o + 1, :]

    o_ref[...] = x2_ref[...] * long_conv(x1_ref[...] * long_conv(v_ref[...], 0), 1)


def _hyena_ctx(hv, hx1, hx2, circ_c, fc, fs, skip):
    const = lambda shape: pl.BlockSpec(shape, lambda b: (0, 0))
    seq = pl.BlockSpec((CTX, HY), lambda b: (T_LAT // CTX + b, 0))
    return pl.pallas_call(
        _hyc_body,
        grid=(NB_BATCH,),
        in_specs=[seq, seq, seq, const((N_CTXF, 2 * HY)), const((N_CTXF, N_CTXF)), const((N_CTXF, N_CTXF)), const((2, HY))],
        out_specs=pl.BlockSpec((CTX, HY), lambda b: (b, 0)),
        out_shape=jax.ShapeDtypeStruct((T_CTX, HY), F32),
        compiler_params=_cparams(("parallel",)),
        name="hyena_ctx",
    )(hv, hx1, hx2, circ_c, fc, fs, skip)


N_ML_STEPS = CTX // CHUNK + SEQ // CHUNK


def _ml_chain(direction, qv_ref, k_ref, src_col, cum_r, gates_t, c_scr, m_scr, mask, ones_rows):
    base = 8 * direction
    outs = []
    for h in range(ML_H):
        ic, fc = base + h, base + 4 + h
        chain = direction * ML_H + h
        b_row, li_row = cum_r[fc:fc + 1, :], gates_t[ic:ic + 1, :]
        b_end = b_row[:, CHUNK - 1:CHUNK] if direction == 0 else b_row[:, 0:1]
        q_t = qv_ref[HD * h:HD * (h + 1), :]
        vext_t = jnp.concatenate([qv_ref[ML_W + HD * h:ML_W + HD * (h + 1), :], ones_rows], axis=0)
        k = k_ref[:, HD * h:HD * (h + 1)]
        c_prev, m_prev = c_scr[chain], m_scr[chain]
        dmat = jnp.where(mask, src_col[:, fc:fc + 1] + b_row, NEG)
        m_intra = jnp.max(dmat, axis=0, keepdims=True)
        s_t = jnp.dot(k, q_t, preferred_element_type=F32) * jnp.exp(dmat - m_intra)
        inter = b_row + m_prev
        m_t = jnp.maximum(inter, m_intra)
        hx = jnp.exp(m_intra - m_t) * _dot(vext_t, s_t) + jnp.exp(inter - m_t) * _dot(c_prev, q_t)
        den = jnp.maximum(jnp.abs(hx[HD:HD + 1, :]), jnp.exp(-m_t))
        outs.append(hx[:HD, :] / den)
        g_row = b_end - b_row + li_row
        m_new = jnp.maximum(b_end + m_prev, jnp.max(g_row, axis=1, keepdims=True))
        c_scr[chain] = jnp.exp(b_end + m_prev - m_new) * c_prev + _dot(vext_t * jnp.exp(g_row - m_new), k)
        m_scr[chain] = m_new
    return jnp.concatenate(outs, axis=0)


ML_BPS = 2
ML_GROUP_COLS = (NB_BATCH // ML_BPS) * (SEQ + CTX)


def _ml_body(*refs):
    n_in = 8 * ML_BPS
    seq_refs, (bias_ref, bias_t_ref, tril_ref, triu_ref) = refs[:n_in], refs[n_in:n_in + 4]
    (hf_ref, hb_ref), (c_scr, m_scr) = refs[n_in + 4:n_in + 6], refs[n_in + 6:]

    @pl.when(pl.program_id(1) == 0)
    def _():
        c_scr[...] = jnp.zeros_like(c_scr)
        m_scr[...] = jnp.zeros_like(m_scr)

    tril, triu = tril_ref[...], triu_ref[...]
    src = lax.broadcasted_iota(jnp.int32, (CHUNK, CHUNK), 0)
    dst = lax.broadcasted_iota(jnp.int32, (CHUNK, CHUNK), 1)
    ones_rows = (lax.broadcasted_iota(jnp.int32, (HD, CHUNK), 0) == 0).astype(BF16)
    for sub in range(ML_BPS):
        c_sub, m_sub = c_scr.at[sub], m_scr.at[sub]
        for direction in range(2):
            qv_ref, k_ref, g_ref, gt_ref = seq_refs[8 * sub + 4 * direction:8 * sub + 4 * direction + 4]
            gates = g_ref[...] + bias_ref[...]
            gates_t = gt_ref[...] + bias_t_ref[...]
            ls, ls_t = _log_sigmoid(gates), _log_sigmoid(gates_t)
            if direction == 0:
                cum_c, cum_r, mask = _dot_sel_l(tril, ls), _dot_sel_r(ls_t, triu), src <= dst
            else:
                cum_c, cum_r, mask = _dot_sel_l(triu, ls), _dot_sel_r(ls_t, tril), src >= dst
            src_col = pltpu.roll(gates, 4, 1) - cum_c
            o_ref = hf_ref if direction == 0 else hb_ref
            o_ref[sub] = _ml_chain(direction, qv_ref, k_ref, src_col, cum_r, gates_t, c_sub, m_sub, mask, ones_rows)


def _mlstm(mqv_t, mk, mg, mg_t, gate_b):
    nctx = CTX // CHUNK
    nlat = SEQ // CHUNK
    per_group = NB_BATCH // ML_BPS

    def step_chunk(i, backward):
        ctx_chunk = (nctx - 1 - i) if backward else i
        lat_chunk = (N_ML_STEPS - 1 - i) if backward else (i - nctx)
        return i < nctx, ctx_chunk, lat_chunk

    def in_chunk(sub, backward):
        def f(p, i):
            b = p + per_group * sub
            is_ctx, cc, lc = step_chunk(i, backward)
            return jnp.where(is_ctx, T_LAT // CHUNK + nctx * b + cc, nlat * b + lc)
        return f

    def out_chunk(backward):
        def f(p, i):
            is_ctx, cc, lc = step_chunk(i, backward)
            return jnp.where(is_ctx, per_group * nlat + nctx * p + cc, nlat * p + lc)
        return f

    bias = jnp.pad(gate_b, (0, 128 - N_GATES)).reshape(1, 128)
    bias_t = gate_b.reshape(N_GATES, 1)
    tril = jnp.asarray(np.tril(np.ones((CHUNK, CHUNK), np.float32))).astype(BF16)
    triu = jnp.asarray(np.triu(np.ones((CHUNK, CHUNK), np.float32))).astype(BF16)
    const = lambda shape: pl.BlockSpec(shape, lambda p, i: (0, 0))
    ins, args = [], []
    for sub in range(ML_BPS):
        for backward in (False, True):
            f = in_chunk(sub, backward)
            ins += [pl.BlockSpec((2 * ML_W, CHUNK), lambda p, i, f=f: (0, f(p, i))),
                    pl.BlockSpec((CHUNK, ML_W), lambda p, i, f=f: (f(p, i), 0)),
                    pl.BlockSpec((CHUNK, 128), lambda p, i, f=f: (f(p, i), 0)),
                    pl.BlockSpec((N_GATES, CHUNK), lambda p, i, f=f: (0, f(p, i)))]
            args += [mqv_t, mk, mg, mg_t]
    outs = [pl.BlockSpec((ML_BPS, ML_W, CHUNK), lambda p, i, f=out_chunk(bw): (0, 0, f(p, i))) for bw in (False, True)]
    return pl.pallas_call(
        _ml_body,
        grid=(per_group, N_ML_STEPS),
        in_specs=ins + [const((1, 128)), const((N_GATES, 1)), const((CHUNK, CHUNK)), const((CHUNK, CHUNK))],
        out_specs=outs,
        out_shape=[jax.ShapeDtypeStruct((ML_BPS, ML_W, ML_GROUP_COLS), F32)] * 2,
        scratch_shapes=[pltpu.VMEM((ML_BPS, 2 * ML_H, 2 * HD, HD), F32), pltpu.VMEM((ML_BPS, 2 * ML_H, 1, 1), F32)],
        compiler_params=_cparams(("parallel", "arbitrary")),
        name="mlstm",
    )(*args, bias, bias_t, tril, triu)


FF_CHUNKS = ((0, 1536), (1536, D_FF))


def _out_ffn_body(att_ref, hl_ref, hc_ref, hf_ref, hb_ref, mo_ref, mnw_ref, bd_ref, wo_ref, x_ref,
                  g1_ref, sh2_ref, sc2_ref, g2_ref, nw2_ref, wg_ref, wu_ref, wd_ref, o_ref):
    i = pl.program_id(0)
    hn = _head_rms((hf_ref[...] + hb_ref[...]).T, bd_ref[...], mnw_ref[...])
    mlo = hn * _sigmoid(mo_ref[...])
    hy = jnp.where(i < NT_LAT, hl_ref[...], hc_ref[...])
    mixed = jnp.concatenate([att_ref[...].astype(BF16), hy.astype(BF16), mlo.astype(BF16)], axis=1)
    x = x_ref[...] + g1_ref[...] * jnp.dot(mixed, wo_ref[...], preferred_element_type=F32)
    hb = _modulated_norm(x, nw2_ref[...], sc2_ref[...], sh2_ref[...]).astype(BF16)
    acc = None
    for lo, hi in FF_CHUNKS:
        sl = slice(lo, hi)
        a = jnp.dot(hb, wg_ref[:, sl], preferred_element_type=F32)
        u = jnp.dot(hb, wu_ref[:, sl], preferred_element_type=F32)
        part = _dot(a * _sigmoid(a) * u, wd_ref[sl, :])
        acc = part if acc is None else acc + part
    o_ref[...] = x + g2_ref[...] * acc


def _out_ffn(att, hyo, hyoc, hf_t, hb_t, mo, ml_nw, w_out_b, xs, mod4, layer, nw2, wg, wu, wd, with_ctx):
    tiles = NT_ALL if with_ctx else NT_LAT
    row = lambda w: pl.BlockSpec((TM, w), lambda i: (i, 0))
    lat_tiles = (NB_BATCH // ML_BPS) * TILES_PER_SEQ
    col = pl.BlockSpec((None, ML_W, TM), lambda i: (jnp.where(i < NT_LAT, i // lat_tiles, i - NT_LAT), 0,
                                                    jnp.where(i < NT_LAT, i % lat_tiles, lat_tiles)))
    resident = lambda shape: pl.BlockSpec(shape, lambda i: (0, 0), pipeline_mode=pl.Buffered(1))
    layer_w = lambda shape: pl.BlockSpec((None,) + shape, lambda i: (layer, 0, 0), pipeline_mode=pl.Buffered(1))
    vec = lambda w: pl.BlockSpec((1, w), lambda i: (0, 0))
    bd = jnp.asarray(_blockdiag_ones(ML_W)).astype(BF16)
    return pl.pallas_call(
        _out_ffn_body,
        grid=(tiles,),
        in_specs=[row(ATT_W),
                  pl.BlockSpec((TM, HY), lambda i: (jnp.minimum(i, NT_LAT - 1), 0)),
                  pl.BlockSpec((TM, HY), lambda i: (jnp.maximum(i - NT_LAT, 0), 0)),
                  col, col, row(ML_W), vec(ML_W), resident((ML_W, ML_W)), layer_w((D, D)), row(D),
                  _mod_spec(layer, 2), _mod_spec(layer, 3), _mod_spec(layer, 4), _mod_spec(layer, 5), vec(D),
                  layer_w((D, D_FF)), layer_w((D, D_FF)), layer_w((D_FF, D))],
        out_specs=row(D),
        out_shape=jax.ShapeDtypeStruct((tiles * TM, D), F32),
        compiler_params=_cparams(("parallel",), 56),
        name="out_ffn",
    )(att, hyo, hyoc, hf_t, hb_t, mo, ml_nw.reshape(1, ML_W), bd, w_out_b, xs, mod4, mod4, mod4, mod4,
      nw2.reshape(1, D), wg, wu, wd)


def _rope_tables():
    n_rows = SEQ // GRID_W
    row = jnp.repeat(jnp.arange(n_rows), GRID_W)
    col = jnp.tile(jnp.arange(GRID_W), n_rows)
    nf = HD // 4
    inv_freq = ROPE_BASE ** (-jnp.arange(nf, dtype=F32) / nf)
    ang = jnp.stack([row[:, None] * inv_freq, col[:, None] * inv_freq], axis=1)
    cos, sin = jnp.cos(ang), jnp.sin(ang)
    cos_h = jnp.concatenate([cos, cos], axis=-1).reshape(SEQ, HD)
    sin_h = jnp.concatenate([-sin, sin], axis=-1).reshape(SEQ, HD)
    cos_t = jnp.concatenate([jnp.tile(cos_h, (1, 2)), jnp.ones((TM, 128), F32)], axis=0)
    sin_t = jnp.concatenate([jnp.tile(sin_h, (1, 2)), jnp.zeros((TM, 128), F32)], axis=0)
    return cos_t, sin_t


def kernel(x, c, ctx, c_ctx, w_mod, b_mod, norm1_w, norm2_w, w_in, w_out, q_norm_w, k_norm_w, attn_sink, hy_conv_w, hy_conv_b, hy_w1, hy_b1, hy_freq, hy_w2, hy_b2, hy_w3, hy_decay, hy_skip, ml_gate_b, ml_norm_w, ffn_w_gate, ffn_w_up, ffn_w_down):
    xs = jnp.concatenate([x.reshape(T_LAT, D), ctx.reshape(T_CTX, D)], axis=0)
    cc8 = jnp.concatenate([c, c_ctx[None, :], jnp.zeros((8 - NB_BATCH - 1, D), F32)], axis=0)
    mod4 = _modulation(cc8, w_mod, b_mod).reshape(DEPTH, 8, 1, 6 * D)

    cos_t, sin_t = _rope_tables()
    feats_l, feats_c = _pos_feats_circ(SEQ), _pos_feats_circ(CTX)
    s1_tab = jnp.asarray(_TAB["s1"]).astype(BF16)
    s1i_tab = jnp.asarray(_TAB["s1i"]).astype(BF16)
    wf, wi = jnp.asarray(_TAB["wf"]).astype(BF16), jnp.asarray(_TAB["wi"]).astype(BF16)
    fc, fs = jnp.asarray(_TAB["fc"]).astype(BF16), jnp.asarray(_TAB["fs"]).astype(BF16)

    w_in_p = _reorder_w_in(w_in)
    w_out_b = w_out.astype(BF16)
    wg_b, wu_b, wd_b = ffn_w_gate.astype(BF16), ffn_w_up.astype(BF16), ffn_w_down.astype(BF16)
    filt_params = _filter_params(hy_w1, hy_b1, hy_freq, hy_w2, hy_b2, hy_w3)
    sink8 = jnp.pad(attn_sink, ((0, 0), (0, 128 - ATT_H)))[:, None, :] * jnp.ones((1, 8, 1), F32)
    zeros_tile = jnp.zeros((TM, HY), F32)

    out = None
    for l in range(DEPTH):
        last = l == DEPTH - 1
        qt, kn, vt, hv, hx1, hx2, mqv_t, mk, mo, mg, mg_t = _in_proj(
            xs, mod4, l, norm1_w[l], w_in_p, cos_t, sin_t, q_norm_w[l], k_norm_w[l], hy_conv_w[l], hy_conv_b[l])

        att = _attention(qt, kn, vt, sink8[l], not last)

        dec = hy_decay[l].reshape(1, N_FILT)
        circ = _circular_filters(feats_l, filt_params, l, dec)
        f_re, f_im = _dft_stage1(circ.reshape(2, HALF_A, FFT_B, FFT_T, 2 * HY), s1_tab)
        h_re, h_im = _filter_spectrum(f_re.reshape(2, 2, N_FFT, HY), f_im.reshape(2, 2, N_FFT, HY), wf)

        s_in, gate_a = _row_blocks(hv), _row_blocks(hx1)
        for order in range(2):
            a_re, a_im = _dft_stage1_pair(s_in, s1_tab)
            rows = lambda t: t.reshape(NB_BATCH // 2, N_FFT, HY)
            c_re, c_im = _spectral_filter(rows(a_re), rows(a_im), h_re, h_im, order, wf, wi)
            tiles = lambda t: t.reshape(NB_BATCH // 2, FFT_A, FFT_B, FFT_T, HY)
            s_in = _idft_stage1_gate(tiles(c_re), tiles(c_im), s1i_tab, gate_a, s_in, hy_skip[l, order])
            gate_a = _row_blocks(hx2)
        hyo = s_in.reshape(T_LAT, HY)

        if last:
            hyoc = zeros_tile
        else:
            circ_c = _circular_filters(feats_c, filt_params, l, dec)
            hyoc = _hyena_ctx(hv, hx1, hx2, circ_c, fc, fs, hy_skip[l])

        hf_t, hb_t = _mlstm(mqv_t, mk, mg, mg_t, ml_gate_b[l])

        xs = _out_ffn(att, hyo, hyoc, hf_t, hb_t, mo, ml_norm_w[l], w_out_b, xs, mod4, l,
                      norm2_w[l], wg_b, wu_b, wd_b, not last)
        out = xs
    return out.reshape(NB_BATCH, SEQ, D)
```

```python
import functools
import math

import numpy as np
import jax
import jax.numpy as jnp
from jax import lax
from jax.experimental import pallas as pl
from jax.experimental.pallas import tpu as pltpu

F32 = jnp.float32
BF16 = jnp.bfloat16

D = 1024
NB_BATCH = 4
SEQ = 4096
DEPTH = 4
GRID_W = 64
CTX = 256
T_LAT = NB_BATCH * SEQ
T_CTX = NB_BATCH * CTX
R_ALL = T_LAT + T_CTX

HD = 64
ATT_H = 6
ATT_W = ATT_H * HD
KV_W = 2 * HD
BLK = 128
NBLK = SEQ // BLK
ROPE_BASE = 10000.0

HY = 384
N_FILT = 4 * HY
POS_BANDS = 16
FILT_HID = 64

ML_H = 4
ML_W = 256
CHUNK = 128

D_FF = 2816
P_IN = 2832
P_PAD = 2944
NEG = -1e30
EPS = 1e-6
LOG2E = 1.4426950408889634

TM = 512
NT_LAT = T_LAT // TM
NT_ALL = R_ALL // TM
TILES_PER_SEQ = SEQ // TM

N_FFT = 2 * SEQ
FFT_A = 32
FFT_R = 256
FFT_T = 16
FFT_B = FFT_R // FFT_T
HALF_A = FFT_A // 2
S1_TILES = 4
HB_ROWS = 1024
N_CTXF = 2 * CTX


def _np_tables():
    ka = np.arange(FFT_A)[None, :, None, None]
    cc = np.arange(FFT_T)[None, None, :, None]
    aa = np.arange(HALF_A)[None, None, None, :]
    bb = np.arange(FFT_B)[:, None, None, None]
    ph = (ka * (FFT_R * aa + FFT_T * bb + cc)) % N_FFT
    th = 2.0 * np.pi * ph / N_FFT
    eye = np.eye(FFT_T)
    cos = np.einsum("bkca,cd->bkcad", np.cos(th), eye)
    sin = np.einsum("bkca,cd->bkcad", np.sin(th), eye)
    rows, cols = FFT_A * FFT_T, HALF_A * FFT_T
    s1 = np.concatenate([cos.reshape(FFT_B, rows, cols), -sin.reshape(FFT_B, rows, cols)], axis=1)
    s1i = np.concatenate([cos.reshape(FFT_B, rows, cols).transpose(0, 2, 1),
                          -sin.reshape(FFT_B, rows, cols).transpose(0, 2, 1)], axis=2) / N_FFT
    r = np.arange(FFT_R)
    th2 = 2.0 * np.pi * ((r[:, None] * r[None, :]) % FFT_R) / FFT_R
    c2, s2 = np.cos(th2), np.sin(th2)
    wf = np.block([[c2, s2], [-s2, c2]])
    wi = np.block([[c2, -s2], [s2, c2]])
    c = np.arange(N_CTXF)
    th3 = 2.0 * np.pi * ((c[:, None] * c[None, :]) % N_CTXF) / N_CTXF
    return dict(s1=s1.astype(np.float32), s1i=s1i.astype(np.float32), wf=wf.astype(np.float32), wi=wi.astype(np.float32),
                fc=np.cos(th3).astype(np.float32), fs=np.sin(th3).astype(np.float32))


_TAB = _np_tables()


def _blockdiag_ones(width):
    return np.kron(np.eye(width // HD), np.ones((HD, HD))).astype(np.float32)


def _cparams(sem, vmem_mb=48):
    return pltpu.CompilerParams(dimension_semantics=sem, vmem_limit_bytes=vmem_mb * 1024 * 1024)


def _dot(a, b):
    return jnp.dot(a.astype(BF16), b.astype(BF16), preferred_element_type=F32)


def _dot_nt(a, b):
    return lax.dot_general(a.astype(BF16), b.astype(BF16), (((1,), (1,)), ((), ())), preferred_element_type=F32)


def _dot_tn(a, b):
    return lax.dot_general(a.astype(BF16), b.astype(BF16), (((0,), (0,)), ((), ())), preferred_element_type=F32)


def _split3(x):
    x1 = x.astype(BF16)
    r1 = x - x1.astype(F32)
    x2 = r1.astype(BF16)
    x3 = (r1 - x2.astype(F32)).astype(BF16)
    return x1, x2, x3


def _dot_sel_l(sel, x):
    return sum(jnp.dot(sel, p, preferred_element_type=F32) for p in _split3(x))


def _dot_sel_r(x, sel):
    return sum(jnp.dot(p, sel, preferred_element_type=F32) for p in _split3(x))


def _sigmoid(x):
    return 1.0 / (1.0 + jnp.exp(-x))


def _log_sigmoid(x):
    return jnp.minimum(x, 0.0) - jnp.log(1.0 + jnp.exp(-jnp.abs(x)))


def _head_rms(t, bd, w):
    sq = t * t
    hi = sq.astype(BF16)
    lo = (sq - hi.astype(F32)).astype(BF16)
    ss = jnp.dot(hi, bd, preferred_element_type=F32) + jnp.dot(lo, bd, preferred_element_type=F32)
    return t * lax.rsqrt(ss * (1.0 / HD) + EPS) * w


def _mod_row(i):
    return jnp.where(i < NT_LAT, i // TILES_PER_SEQ, NB_BATCH)


def _mod_spec(layer, k):
    return pl.BlockSpec((None, None, 1, D), lambda i: (layer, _mod_row(i), 0, k))


def _mod_body(s_ref, w_ref, b_ref, o_ref):
    s = s_ref[...]
    s = s * _sigmoid(s)
    o_ref[...] = _dot(s, w_ref[...]) + b_ref[...]


def _modulation(cc8, w_mod, b_mod):
    nc = 1536
    return pl.pallas_call(
        _mod_body,
        grid=(DEPTH, 6 * D // nc),
        in_specs=[pl.BlockSpec((8, D), lambda l, j: (0, 0)),
                  pl.BlockSpec((None, D, nc), lambda l, j: (l, 0, j)),
                  pl.BlockSpec((None, 1, nc), lambda l, j: (l, 0, j))],
        out_specs=pl.BlockSpec((None, 8, nc), lambda l, j: (l, 0, j)),
        out_shape=jax.ShapeDtypeStruct((DEPTH, 8, 6 * D), F32),
        compiler_params=_cparams(("parallel", "parallel")),
        name="modulation",
    )(cc8, w_mod, b_mod.reshape(DEPTH, 1, 6 * D))


_COL_ATT, _COL_HY, _COL_MQV, _COL_MK, _COL_MO, _COL_MG = (0, 640), (640, 1792), (1792, 2304), (2304, 2560), (2560, 2816), (2816, 2944)
N_GATES = 4 * ML_H


def _reorder_w_in(w_in):
    mq_end, mk_end, mv_end = 2048, 2304, 2560
    w = jnp.concatenate([w_in[:, :, :mq_end], w_in[:, :, mk_end:mv_end], w_in[:, :, mq_end:mk_end], w_in[:, :, mv_end:]], axis=2)
    return jnp.pad(w, ((0, 0), (0, 0), (0, P_PAD - P_IN))).astype(BF16)


def _modulated_norm(x, nw, sc, sh):
    ms = jnp.mean(x * x, axis=-1, keepdims=True)
    return (x * lax.rsqrt(ms + EPS)) * (nw * (1.0 + sc)) + sh


HALO = 8


def _in_body(x_ref, xp_ref, xn_ref, sh_ref, sc_ref, nw_ref, w_ref, cos_ref, sin_ref, qw_ref, kw_ref, bdq_ref, bdk_ref,
             cw_ref, cb_ref, qt_ref, k_ref, vt_ref, hv_ref, hx1_ref, hx2_ref, mqvt_ref, mk_ref, mo_ref, mg_ref, mgt_ref):
    i = pl.program_id(0)
    nw, sc, sh = nw_ref[...], sc_ref[...], sh_ref[...]
    hb = _modulated_norm(x_ref[...], nw, sc, sh).astype(BF16)
    proj = lambda cols: jnp.dot(hb, w_ref[:, cols[0]:cols[1]], preferred_element_type=F32)

    att = proj(_COL_ATT)
    cos, sin = cos_ref[...], sin_ref[...]
    lane = lax.broadcasted_iota(jnp.int32, (TM, 128), 1)
    first_half = (lane & 31) < 16
    q = _head_rms(att[:, 0:ATT_W], bdq_ref[...], qw_ref[...])
    k = _head_rms(att[:, ATT_W:ATT_W + KV_W], bdk_ref[...], kw_ref[...])
    qt_ref[...] = (_rope(q, cos, sin, first_half) * (HD ** -0.5 * LOG2E)).T.astype(BF16)
    k_ref[...] = _rope(k, cos, sin, first_half).astype(BF16)
    vt_ref[...] = att[:, ATT_W + KV_W:ATT_W + 2 * KV_W].T.astype(BF16)

    halo = _modulated_norm(jnp.concatenate([xp_ref[...], xn_ref[...]], axis=0), nw, sc, sh).astype(BF16)
    hy_all = jnp.dot(jnp.concatenate([hb, halo], axis=0), w_ref[:, _COL_HY[0]:_COL_HY[1]], preferred_element_type=F32)
    hy, hy_halo = hy_all[:TM, :], hy_all[TM:, :]
    seq_len = jnp.where(i < NT_LAT, SEQ, CTX)
    pos = (i * TM + lax.broadcasted_iota(jnp.int32, (TM, 1), 0)) & (seq_len - 1)
    usc = _short_conv_rows(hy, hy_halo[HALO - 1:HALO, :], hy_halo[HALO:HALO + 1, :], cw_ref[...], cb_ref[...],
                           pos == 0, pos == seq_len - 1)
    hv_ref[...] = usc[:, 0:HY]
    hx1_ref[...] = usc[:, HY:2 * HY]
    hx2_ref[...] = usc[:, 2 * HY:3 * HY]

    mqvt_ref[...] = proj(_COL_MQV).T.astype(BF16)
    mk_ref[...] = (proj(_COL_MK) * (HD ** -0.5)).astype(BF16)
    mo_ref[...] = proj(_COL_MO)
    mg = proj(_COL_MG)
    mg_ref[...] = mg
    mgt_ref[...] = mg.T[0:N_GATES, :]


def _in_proj(xs, mod4, layer, nw, w_in_p, cos_t, sin_t, qw, kw, conv_w, conv_b):
    rows = lambda w: pl.BlockSpec((TM, w), lambda i: (i, 0))
    cols = lambda h: pl.BlockSpec((h, TM), lambda i: (0, i))
    const = lambda shape: pl.BlockSpec(shape, lambda i: (0, 0))
    tab = lambda i: (jnp.where(i < NT_LAT, i % TILES_PER_SEQ, TILES_PER_SEQ), 0)
    per_tile = TM // HALO
    prev = pl.BlockSpec((HALO, D), lambda i: (jnp.maximum(i * per_tile - 1, 0), 0))
    nxt = pl.BlockSpec((HALO, D), lambda i: (jnp.minimum((i + 1) * per_tile, R_ALL // HALO - 1), 0))
    bdq = jnp.asarray(_blockdiag_ones(ATT_W)).astype(BF16)
    bdk = jnp.asarray(_blockdiag_ones(KV_W)).astype(BF16)
    bf = lambda shape: jax.ShapeDtypeStruct(shape, BF16)
    f32 = lambda shape: jax.ShapeDtypeStruct(shape, F32)
    return pl.pallas_call(
        _in_body,
        grid=(NT_ALL,),
        in_specs=[rows(D), prev, nxt, _mod_spec(layer, 0), _mod_spec(layer, 1), const((1, D)),
                  pl.BlockSpec((None, D, P_PAD), lambda i: (layer, 0, 0), pipeline_mode=pl.Buffered(1)),
                  pl.BlockSpec((TM, 128), tab), pl.BlockSpec((TM, 128), tab),
                  const((1, ATT_W)), const((1, KV_W)), const((ATT_W, ATT_W)), const((KV_W, KV_W)),
                  const((3, 3 * HY)), const((1, 3 * HY))],
        out_specs=[cols(ATT_W), rows(KV_W), cols(KV_W), rows(HY), rows(HY), rows(HY),
                   cols(2 * ML_W), rows(ML_W), rows(ML_W), rows(128), cols(N_GATES)],
        out_shape=[bf((ATT_W, R_ALL)), bf((R_ALL, KV_W)), bf((KV_W, R_ALL)), f32((R_ALL, HY)), f32((R_ALL, HY)), f32((R_ALL, HY)),
                   bf((2 * ML_W, R_ALL)), bf((R_ALL, ML_W)), f32((R_ALL, ML_W)), f32((R_ALL, 128)), f32((N_GATES, R_ALL))],
        compiler_params=_cparams(("parallel",), 56),
        name="in_proj",
    )(xs, xs, xs, mod4, mod4, nw.reshape(1, D), w_in_p, cos_t, sin_t,
      jnp.tile(qw, ATT_H).reshape(1, ATT_W), jnp.tile(kw, 2).reshape(1, KV_W), bdq, bdk, conv_w, conv_b.reshape(1, 3 * HY))


def _rope(t, cos, sin_signed, first_half):
    outs = []
    for c in range(t.shape[1] // 128):
        tc = t[:, c * 128:(c + 1) * 128]
        partner = jnp.where(first_half, pltpu.roll(tc, 112, 1), pltpu.roll(tc, 16, 1))
        outs.append(tc * cos + partner * sin_signed)
    return outs[0] if len(outs) == 1 else jnp.concatenate(outs, axis=1)


def _att_heads(qt, kv_list, sink_ref, o_ref):
    nq = qt.shape[1]
    rep = ATT_H // 2
    zeros = jnp.zeros((HD, rep * nq), BF16)
    outs = []
    for g in range(2):
        heads = range(rep * g, rep * (g + 1))
        qg = jnp.concatenate([qt[HD * h:HD * (h + 1), :] for h in heads], axis=1)
        qe = jnp.concatenate([qg, zeros] if g == 0 else [zeros, qg], axis=0)
        sink = jnp.concatenate([jnp.broadcast_to(sink_ref[0:1, h:h + 1], (1, nq)) for h in heads], axis=1) * LOG2E
        scores = []
        m = sink
        for k, _, mask in kv_list:
            s = jnp.dot(k, qe, preferred_element_type=F32)
            if mask is not None:
                s = jnp.where(mask, s, NEG)
            scores.append(s)
            m = jnp.maximum(m, jnp.max(s, axis=0, keepdims=True))
        acc = None
        for s, (_, vt, _) in zip(scores, kv_list):
            p = jnp.exp2(s - m).astype(BF16)
            ones = (lax.broadcasted_iota(jnp.int32, (16, vt.shape[1]), 0) == 0).astype(BF16)
            pv = jnp.dot(jnp.concatenate([vt[HD * g:HD * (g + 1), :], ones], axis=0), p, preferred_element_type=F32)
            acc = pv if acc is None else acc + pv
        og = acc[:HD, :] / (acc[HD:HD + 1, :] + jnp.exp2(sink - m))
        outs += [og[:, nq * i:nq * (i + 1)] for i in range(rep)]
    o_ref[...] = jnp.concatenate(outs, axis=0).T


QB = 4


def _att_body(*refs):
    nband = QB + 2
    qt_ref, ks, vs = refs[0], refs[1:1 + nband], refs[1 + nband:1 + 2 * nband]
    kc_ref, vc_ref, sink_ref, o_ref = refs[1 + 2 * nband:]
    j = pl.program_id(1)
    kc, vc = kc_ref[...], vc_ref[...]
    width = (ATT_H // 2) * BLK
    key = lax.broadcasted_iota(jnp.int32, (BLK, width), 0)
    qry = lax.broadcasted_iota(jnp.int32, (BLK, width), 1) & (BLK - 1)
    for sub in range(QB):
        blk = QB * j + sub
        mask_prev = jnp.logical_and(key >= qry, blk > 0)
        mask_next = jnp.logical_and(key <= qry, blk < NBLK - 1)
        _att_heads(qt_ref[:, sub * BLK:(sub + 1) * BLK],
                   [(ks[sub][...], vs[sub][...], mask_prev), (ks[sub + 1][...], vs[sub + 1][...], None),
                    (ks[sub + 2][...], vs[sub + 2][...], mask_next), (kc, vc, None)],
                   sink_ref, o_ref.at[sub * BLK:(sub + 1) * BLK, :])


def _attention(qt, kn, vt, sink8):
    qrows = QB * BLK
    steps = NBLK // QB

    def band(off):
        return lambda b, j: b * NBLK + jnp.clip(QB * j + off, 0, NBLK - 1)

    kspec = lambda f: pl.BlockSpec((BLK, KV_W), lambda b, j: (f(b, j), 0))
    vspec = lambda f: pl.BlockSpec((KV_W, BLK), lambda b, j: (0, f(b, j)))
    offs = range(-1, QB + 1)
    return pl.pallas_call(
        _att_body,
        grid=(NB_BATCH, steps),
        in_specs=[pl.BlockSpec((ATT_W, qrows), lambda b, j: (0, b * steps + j))]
                 + [kspec(band(o)) for o in offs] + [vspec(band(o)) for o in offs]
                 + [pl.BlockSpec((CTX, KV_W), lambda b, j: (T_LAT // CTX + b, 0)),
                    pl.BlockSpec((KV_W, CTX), lambda b, j: (0, T_LAT // CTX + b)),
                    pl.BlockSpec((8, 128), lambda b, j: (0, 0))],
        out_specs=pl.BlockSpec((qrows, ATT_W), lambda b, j: (b * steps + j, 0)),
        out_shape=jax.ShapeDtypeStruct((T_LAT, ATT_W), F32),
        compiler_params=_cparams(("parallel", "parallel")),
        name="attention",
    )(qt, *([kn] * (QB + 2)), *([vt] * (QB + 2)), kn, vt, sink8)


def _att_ctx_body(qt_ref, kc_ref, vc_ref, sink_ref, o_ref):
    kv = [(kc_ref[...], vc_ref[...], None)]
    for sub in range(CTX // BLK):
        _att_heads(qt_ref[:, sub * BLK:(sub + 1) * BLK], kv, sink_ref, o_ref.at[sub * BLK:(sub + 1) * BLK, :])


def _attention_ctx(qt, kn, vt, sink8):
    return pl.pallas_call(
        _att_ctx_body,
        grid=(NB_BATCH,),
        in_specs=[pl.BlockSpec((ATT_W, CTX), lambda b: (0, T_LAT // CTX + b)),
                  pl.BlockSpec((CTX, KV_W), lambda b: (T_LAT // CTX + b, 0)),
                  pl.BlockSpec((KV_W, CTX), lambda b: (0, T_LAT // CTX + b)),
                  pl.BlockSpec((8, 128), lambda b: (0, 0))],
        out_specs=pl.BlockSpec((CTX, ATT_W), lambda b: (b, 0)),
        out_shape=jax.ShapeDtypeStruct((T_CTX, ATT_W), F32),
        compiler_params=_cparams(("parallel",)),
        name="attention_ctx",
    )(qt, kn, vt, sink8)


def _short_conv_rows(u, prev_row, next_row, w, b, first, last):
    n = u.shape[0]
    row = lax.broadcasted_iota(jnp.int32, u.shape, 0)
    up = jnp.where(row == 0, prev_row, pltpu.roll(u, 1, 0))
    un = jnp.where(row == n - 1, next_row, pltpu.roll(u, n - 1, 0))
    up = jnp.where(first, 0.0, up)
    un = jnp.where(last, 0.0, un)
    return up * w[0:1, :] + u * w[1:2, :] + un * w[2:3, :] + b


def _filt_body(n, tl, f_ref, w1_ref, b1_ref, fr_ref, w2_ref, b2_ref, w3a_ref, w3b_ref, dec_ref, o_ref):
    h = tl // 2
    f = jnp.concatenate([f_ref[0:h, :], f_ref[h:tl, :]], axis=1)
    fr = fr_ref[...]
    z = jnp.sin(fr * (_dot(f, w1_ref[...]) + b1_ref[...]))
    z = jnp.sin(fr * (_dot(z, w2_ref[...]) + b2_ref[...]))
    dec = jnp.abs(dec_ref[...])
    top = _dot(z, w3a_ref[...]) * jnp.exp(-f[:, 0:1] * dec)
    bot = _dot(z, w3b_ref[...]) * jnp.exp(-f[:, 128:129] * dec)
    filt = jnp.concatenate([top, bot], axis=0)
    row = pl.program_id(0) * tl + lax.broadcasted_iota(jnp.int32, filt.shape, 0)
    o_ref[...] = jnp.where(row == n, 0.0, filt)


def _filter_params(hy_w1, hy_b1, hy_freq, hy_w2, hy_b2, hy_w3):
    hid = FILT_HID
    w1 = jnp.pad(hy_w1, ((0, 0), (0, 128 - hy_w1.shape[1]), (0, 0)))
    zero = jnp.zeros_like
    w1b = jnp.concatenate([jnp.concatenate([w1, zero(w1)], axis=2), jnp.concatenate([zero(w1), w1], axis=2)], axis=1)
    w2b = jnp.concatenate([jnp.concatenate([hy_w2, zero(hy_w2)], axis=2), jnp.concatenate([zero(hy_w2), hy_w2], axis=2)], axis=1)
    w3a = jnp.concatenate([hy_w3, zero(hy_w3)], axis=1)
    w3b = jnp.concatenate([zero(hy_w3), hy_w3], axis=1)
    twice = lambda v: jnp.concatenate([v, v], axis=1).reshape(DEPTH, 1, 2 * hid)
    return w1b, twice(hy_b1), twice(hy_freq), w2b, twice(hy_b2), w3a, w3b


def _circular_filters(feats_circ, params, layer, dec):
    n = feats_circ.shape[0] // 2
    tl = min(n, 512)
    w1b, b1, fr, w2b, b2, w3a, w3b = params
    per = lambda shape: pl.BlockSpec((None,) + shape, lambda i: (layer, 0, 0))
    half = pl.BlockSpec((None, 128, 2 * HY), lambda i: (layer, 0, i // (n // tl)))
    return pl.pallas_call(
        functools.partial(_filt_body, n, tl),
        grid=(2 * n // tl,),
        in_specs=[pl.BlockSpec((tl, 128), lambda i: (i, 0)), per((256, 128)), per((1, 128)), per((1, 128)),
                  per((128, 128)), per((1, 128)), half, half, pl.BlockSpec((1, 2 * HY), lambda i: (0, i // (n // tl)))],
        out_specs=pl.BlockSpec((tl, 2 * HY), lambda i: (i, 0)),
        out_shape=jax.ShapeDtypeStruct((2 * n, 2 * HY), F32),
        compiler_params=_cparams(("parallel",)),
        name="hyena_filters",
    )(feats_circ, w1b, b1, fr, w2b, b2, w3a, w3b, dec)


def _pos_feats_circ(n):
    t = jnp.linspace(0.0, 1.0, n, dtype=F32)[:, None]
    ang = (2.0 * math.pi / n) * jnp.arange(n, dtype=F32)[:, None]
    bands = jnp.linspace(1e-4, POS_BANDS - 1, POS_BANDS, dtype=F32)[None, :]
    feats = jnp.concatenate([t, jnp.cos(bands * ang), -jnp.sin(bands * ang)], axis=-1)
    feats = jnp.pad(feats, ((0, 0), (0, 128 - feats.shape[1])))
    return jnp.concatenate([feats, feats[:1], jnp.flip(feats[:n - 1], axis=0)], axis=0)


def _s1_body(x_ref, m_ref, are_ref, aim_ref):
    half = FFT_A * FFT_T
    ncb = x_ref.shape[-1] // HY
    for p in range(S1_TILES):
        xs = x_ref[:, p].reshape(HALF_A * FFT_T, ncb * HY)
        r = _dot(m_ref[p], xs)
        for cb in range(ncb):
            cols = slice(cb * HY, (cb + 1) * HY)
            are_ref[cb, :, p] = r[:half, cols].reshape(FFT_A, FFT_T, HY).astype(BF16)
            aim_ref[cb, :, p] = r[half:, cols].reshape(FFT_A, FFT_T, HY).astype(BF16)


def _dft_stage1(x5, s1_tab):
    nb, ncb = x5.shape[0], x5.shape[-1] // HY
    out = pl.BlockSpec((None, ncb, FFT_A, S1_TILES, FFT_T, HY), lambda b, j: (b, 0, 0, j, 0, 0))
    return pl.pallas_call(
        _s1_body,
        grid=(nb, FFT_B // S1_TILES),
        in_specs=[pl.BlockSpec((None, HALF_A, S1_TILES, FFT_T, ncb * HY), lambda b, j: (b, 0, j, 0, 0)),
                  pl.BlockSpec((S1_TILES, 2 * FFT_A * FFT_T, HALF_A * FFT_T), lambda b, j: (j, 0, 0))],
        out_specs=[out, out],
        out_shape=[jax.ShapeDtypeStruct((nb, ncb, FFT_A, FFT_B, FFT_T, HY), BF16)] * 2,
        compiler_params=_cparams(("parallel", "parallel")),
        name="dft_stage1",
    )(x5, s1_tab)


def _s1_pair_body(x_ref, m_ref, are_ref, aim_ref):
    half = FFT_A * FFT_T
    rows = HALF_A * FFT_T
    for p in range(S1_TILES):
        xab = jnp.concatenate([x_ref[0:HALF_A, p].reshape(rows, HY), x_ref[HALF_A:2 * HALF_A, p].reshape(rows, HY)], axis=1)
        r = _dot(m_ref[p], xab)
        are_ref[:, p] = (r[:half, :HY] - r[half:, HY:]).reshape(FFT_A, FFT_T, HY).astype(BF16)
        aim_ref[:, p] = (r[half:, :HY] + r[:half, HY:]).reshape(FFT_A, FFT_T, HY).astype(BF16)


def _row_blocks(x):
    return x.reshape(x.shape[0] // FFT_R, FFT_B, FFT_T, HY)


_PAIR_SPEC = pl.BlockSpec((2 * HALF_A, S1_TILES, FFT_T, HY), lambda b, j: (b, j, 0, 0))


def _dft_stage1_pair(x4, s1_tab):
    npair = NB_BATCH // 2
    out = pl.BlockSpec((None, FFT_A, S1_TILES, FFT_T, HY), lambda b, j: (b, 0, j, 0, 0))
    return pl.pallas_call(
        _s1_pair_body,
        grid=(npair, FFT_B // S1_TILES),
        in_specs=[_PAIR_SPEC, pl.BlockSpec((S1_TILES, 2 * FFT_A * FFT_T, HALF_A * FFT_T), lambda b, j: (j, 0, 0))],
        out_specs=[out, out],
        out_shape=[jax.ShapeDtypeStruct((npair, FFT_A, FFT_B, FFT_T, HY), BF16)] * 2,
        compiler_params=_cparams(("parallel", "parallel")),
        name="dft_stage1_pair",
    )(x4, s1_tab)


def _block_rows(r):
    return slice(r * FFT_R, (r + 1) * FFT_R)


def _stack_complex(re_ref, im_ref, sl):
    return jnp.concatenate([re_ref[sl, :], im_ref[sl, :]], axis=0)


def _spec_body(are_ref, aim_ref, wf_ref, hre_ref, him_ref):
    wf = wf_ref[...]
    for r in range(HB_ROWS // FFT_R):
        sl = _block_rows(r)
        both = jnp.concatenate([_stack_complex(are_ref.at[0], aim_ref.at[0], sl),
                                _stack_complex(are_ref.at[1], aim_ref.at[1], sl)], axis=1)
        x = jnp.dot(wf, both, preferred_element_type=F32)
        first, second = x[:, :HY], x[:, HY:]
        x = first + second if r % 2 == 0 else first - second
        hre_ref[sl, :] = x[:FFT_R]
        him_ref[sl, :] = x[FFT_R:]


def _filter_spectrum(a_re, a_im, wf):
    assert (HB_ROWS // FFT_R) % 2 == 0
    blk_in = pl.BlockSpec((2, None, HB_ROWS, HY), lambda r, o: (0, o, r, 0))
    blk_out = pl.BlockSpec((None, HB_ROWS, HY), lambda r, o: (o, r, 0))
    return pl.pallas_call(
        _spec_body,
        grid=(N_FFT // HB_ROWS, 2),
        in_specs=[blk_in, blk_in, pl.BlockSpec((2 * FFT_R, 2 * FFT_R), lambda r, o: (0, 0))],
        out_specs=[blk_out, blk_out],
        out_shape=[jax.ShapeDtypeStruct((2, N_FFT, HY), F32)] * 2,
        compiler_params=_cparams(("parallel", "parallel")),
        name="filter_spectrum",
    )(a_re, a_im, wf)


def _s2_body(are_ref, aim_ref, hre_ref, him_ref, wf_ref, wi_ref, cre_ref, cim_ref):
    wf, wi = wf_ref[...], wi_ref[...]
    for r in range(0, HB_ROWS // FFT_R, 2):
        sl0, sl1 = _block_rows(r), _block_rows(r + 1)
        a = jnp.concatenate([_stack_complex(are_ref, aim_ref, sl0), _stack_complex(are_ref, aim_ref, sl1)], axis=1)
        x = jnp.dot(wf, a, preferred_element_type=F32)
        xre, xim = x[:FFT_R], x[FFT_R:]
        hre = jnp.concatenate([hre_ref[sl0, :], hre_ref[sl1, :]], axis=1)
        him = jnp.concatenate([him_ref[sl0, :], him_ref[sl1, :]], axis=1)
        y = jnp.concatenate([(xre * hre - xim * him).astype(BF16), (xre * him + xim * hre).astype(BF16)], axis=0)
        c = jnp.dot(wi, y, preferred_element_type=F32)
        for i, sl in enumerate((sl0, sl1)):
            cre_ref[sl, :] = c[:FFT_R, i * HY:(i + 1) * HY].astype(BF16)
            cim_ref[sl, :] = c[FFT_R:, i * HY:(i + 1) * HY].astype(BF16)


def _spectral_filter(a_re, a_im, h_re, h_im, order, wf, wi):
    nb = a_re.shape[0]
    blk = pl.BlockSpec((None, HB_ROWS, HY), lambda r, b: (b, r, 0))
    hblk = pl.BlockSpec((None, HB_ROWS, HY), lambda r, b: (order, r, 0))
    mat = pl.BlockSpec((2 * FFT_R, 2 * FFT_R), lambda r, b: (0, 0))
    return pl.pallas_call(
        _s2_body,
        grid=(N_FFT // HB_ROWS, nb),
        in_specs=[blk, blk, hblk, hblk, mat, mat],
        out_specs=[blk, blk],
        out_shape=[jax.ShapeDtypeStruct((nb, N_FFT, HY), BF16)] * 2,
        compiler_params=_cparams(("parallel", "parallel")),
        name="spectral_filter",
    )(a_re, a_im, h_re, h_im, wf, wi)


def _s1i_body(cre_ref, cim_ref, g_ref, a_ref, b_ref, sk_ref, o_ref):
    rows = FFT_A * FFT_T
    for p in range(S1_TILES):
        cre, cim = cre_ref[:, p].reshape(rows, HY), cim_ref[:, p].reshape(rows, HY)
        c = jnp.concatenate([jnp.concatenate([cre, cim], axis=1), jnp.concatenate([cim, -cre], axis=1)], axis=0)
        y = jnp.dot(g_ref[p], c, preferred_element_type=F32)
        for i in range(2):
            seq = slice(i * HALF_A, (i + 1) * HALF_A)
            yi = y[:, i * HY:(i + 1) * HY].reshape(HALF_A, FFT_T, HY)
            o_ref[seq, p] = a_ref[seq, p] * (yi + b_ref[seq, p] * sk_ref[...])


def _idft_stage1_gate(c_re, c_im, s1i_tab, a4, b4, skip):
    data = _PAIR_SPEC
    spec = pl.BlockSpec((None, FFT_A, S1_TILES, FFT_T, HY), lambda b, j: (b, 0, j, 0, 0))
    return pl.pallas_call(
        _s1i_body,
        grid=(NB_BATCH // 2, FFT_B // S1_TILES),
        in_specs=[spec, spec, pl.BlockSpec((S1_TILES, HALF_A * FFT_T, 2 * FFT_A * FFT_T), lambda b, j: (j, 0, 0)),
                  data, data, pl.BlockSpec((1, 1, HY), lambda b, j: (0, 0, 0))],
        out_specs=data,
        out_shape=jax.ShapeDtypeStruct((T_LAT // FFT_R, FFT_B, FFT_T, HY), F32),
        compiler_params=_cparams(("parallel", "parallel")),
        name="idft_stage1_gate",
    )(c_re, c_im, s1i_tab, a4, b4, skip.reshape(1, 1, HY))


def _hyc_body(v_ref, x1_ref, x2_ref, circ_ref, fc_ref, fs_ref, sk_ref, o_ref):
    fc, fs = fc_ref[...], fs_ref[...]
    circ = circ_ref[...]
    h_re, h_im = _dot(fc, circ), -_dot(fs, circ)

    def long_conv(s, o):
        sl = slice(o * HY, (o + 1) * HY)
        s_re, s_im = _dot(fc[:, :CTX], s), -_dot(fs[:, :CTX], s)
        hre, him = h_re[:, sl], h_im[:, sl]
        y_re = s_re * hre - s_im * him
        y_im = s_re * him + s_im * hre
        y = (_dot(fc[:CTX, :], y_re) - _dot(fs[:CTX, :], y_im)) * (1.0 / N_CTXF)
        return y + s * sk_ref[o:o + 1, :]

    o_ref[...] = x2_ref[...] * long_conv(x1_ref[...] * long_conv(v_ref[...], 0), 1)


def _hyena_ctx(hv, hx1, hx2, circ_c, fc, fs, skip):
    const = lambda shape: pl.BlockSpec(shape, lambda b: (0, 0))
    seq = pl.BlockSpec((CTX, HY), lambda b: (T_LAT // CTX + b, 0))
    return pl.pallas_call(
        _hyc_body,
        grid=(NB_BATCH,),
        in_specs=[seq, seq, seq, const((N_CTXF, 2 * HY)), const((N_CTXF, N_CTXF)), const((N_CTXF, N_CTXF)), const((2, HY))],
        out_specs=pl.BlockSpec((CTX, HY), lambda b: (b, 0)),
        out_shape=jax.ShapeDtypeStruct((T_CTX, HY), F32),
        compiler_params=_cparams(("parallel",)),
        name="hyena_ctx",
    )(hv, hx1, hx2, circ_c, fc, fs, skip)


N_ML_STEPS = CTX // CHUNK + SEQ // CHUNK


def _ml_chain(direction, qv_ref, k_ref, src_col, cum_r, gates_t, c_scr, m_scr, mask, ones_rows):
    base = 8 * direction
    outs = []
    for h in range(ML_H):
        ic, fc = base + h, base + 4 + h
        chain = direction * ML_H + h
        b_row, li_row = cum_r[fc:fc + 1, :], gates_t[ic:ic + 1, :]
        b_end = b_row[:, CHUNK - 1:CHUNK] if direction == 0 else b_row[:, 0:1]
        q_t = qv_ref[HD * h:HD * (h + 1), :]
        vext_t = jnp.concatenate([qv_ref[ML_W + HD * h:ML_W + HD * (h + 1), :], ones_rows], axis=0)
        k = k_ref[:, HD * h:HD * (h + 1)]
        c_prev, m_prev = c_scr[chain], m_scr[chain]
        dmat = jnp.where(mask, src_col[:, fc:fc + 1] + b_row, NEG)
        m_intra = jnp.max(dmat, axis=0, keepdims=True)
        s_t = jnp.dot(k, q_t, preferred_element_type=F32) * jnp.exp(dmat - m_intra)
        inter = b_row + m_prev
        m_t = jnp.maximum(inter, m_intra)
        hx = jnp.exp(m_intra - m_t) * _dot(vext_t, s_t) + jnp.exp(inter - m_t) * _dot(c_prev, q_t)
        den = jnp.maximum(jnp.abs(hx[HD:HD + 1, :]), jnp.exp(-m_t))
        outs.append(hx[:HD, :] / den)
        g_row = b_end - b_row + li_row
        m_new = jnp.maximum(b_end + m_prev, jnp.max(g_row, axis=1, keepdims=True))
        c_scr[chain] = jnp.exp(b_end + m_prev - m_new) * c_prev + _dot(vext_t * jnp.exp(g_row - m_new), k)
        m_scr[chain] = m_new
    return jnp.concatenate(outs, axis=0)


ML_BPS = 2
ML_GROUP_COLS = (NB_BATCH // ML_BPS) * (SEQ + CTX)


def _ml_body(*refs):
    n_in = 8 * ML_BPS
    seq_refs, (bias_ref, bias_t_ref, tril_ref, triu_ref) = refs[:n_in], refs[n_in:n_in + 4]
    (hf_ref, hb_ref), (c_scr, m_scr) = refs[n_in + 4:n_in + 6], refs[n_in + 6:]

    @pl.when(pl.program_id(1) == 0)
    def _():
        c_scr[...] = jnp.zeros_like(c_scr)
        m_scr[...] = jnp.zeros_like(m_scr)

    tril, triu = tril_ref[...], triu_ref[...]
    src = lax.broadcasted_iota(jnp.int32, (CHUNK, CHUNK), 0)
    dst = lax.broadcasted_iota(jnp.int32, (CHUNK, CHUNK), 1)
    ones_rows = (lax.broadcasted_iota(jnp.int32, (HD, CHUNK), 0) == 0).astype(BF16)
    for sub in range(ML_BPS):
        c_sub, m_sub = c_scr.at[sub], m_scr.at[sub]
        for direction in range(2):
            qv_ref, k_ref, g_ref, gt_ref = seq_refs[8 * sub + 4 * direction:8 * sub + 4 * direction + 4]
            gates = g_ref[...] + bias_ref[...]
            gates_t = gt_ref[...] + bias_t_ref[...]
            ls, ls_t = _log_sigmoid(gates), _log_sigmoid(gates_t)
            if direction == 0:
                cum_c, cum_r, mask = _dot_sel_l(tril, ls), _dot_sel_r(ls_t, triu), src <= dst
            else:
                cum_c, cum_r, mask = _dot_sel_l(triu, ls), _dot_sel_r(ls_t, tril), src >= dst
            src_col = pltpu.roll(gates, 4, 1) - cum_c
            o_ref = hf_ref if direction == 0 else hb_ref
            o_ref[sub] = _ml_chain(direction, qv_ref, k_ref, src_col, cum_r, gates_t, c_sub, m_sub, mask, ones_rows)


def _mlstm(mqv_t, mk, mg, mg_t, gate_b):
    nctx = CTX // CHUNK
    nlat = SEQ // CHUNK
    per_group = NB_BATCH // ML_BPS

    def step_chunk(i, backward):
        ctx_chunk = (nctx - 1 - i) if backward else i
        lat_chunk = (N_ML_STEPS - 1 - i) if backward else (i - nctx)
        return i < nctx, ctx_chunk, lat_chunk

    def in_chunk(sub, backward):
        def f(p, i):
            b = p + per_group * sub
            is_ctx, cc, lc = step_chunk(i, backward)
            return jnp.where(is_ctx, T_LAT // CHUNK + nctx * b + cc, nlat * b + lc)
        return f

    def out_chunk(backward):
        def f(p, i):
            is_ctx, cc, lc = step_chunk(i, backward)
            return jnp.where(is_ctx, per_group * nlat + nctx * p + cc, nlat * p + lc)
        return f

    bias = jnp.pad(gate_b, (0, 128 - N_GATES)).reshape(1, 128)
    bias_t = gate_b.reshape(N_GATES, 1)
    tril = jnp.asarray(np.tril(np.ones((CHUNK, CHUNK), np.float32))).astype(BF16)
    triu = jnp.asarray(np.triu(np.ones((CHUNK, CHUNK), np.float32))).astype(BF16)
    const = lambda shape: pl.BlockSpec(shape, lambda p, i: (0, 0))
    ins, args = [], []
    for sub in range(ML_BPS):
        for backward in (False, True):
            f = in_chunk(sub, backward)
            ins += [pl.BlockSpec((2 * ML_W, CHUNK), lambda p, i, f=f: (0, f(p, i))),
                    pl.BlockSpec((CHUNK, ML_W), lambda p, i, f=f: (f(p, i), 0)),
                    pl.BlockSpec((CHUNK, 128), lambda p, i, f=f: (f(p, i), 0)),
                    pl.BlockSpec((N_GATES, CHUNK), lambda p, i, f=f: (0, f(p, i)))]
            args += [mqv_t, mk, mg, mg_t]
    outs = [pl.BlockSpec((ML_BPS, ML_W, CHUNK), lambda p, i, f=out_chunk(bw): (0, 0, f(p, i))) for bw in (False, True)]
    return pl.pallas_call(
        _ml_body,
        grid=(per_group, N_ML_STEPS),
        in_specs=ins + [const((1, 128)), const((N_GATES, 1)), const((CHUNK, CHUNK)), const((CHUNK, CHUNK))],
        out_specs=outs,
        out_shape=[jax.ShapeDtypeStruct((ML_BPS, ML_W, ML_GROUP_COLS), F32)] * 2,
        scratch_shapes=[pltpu.VMEM((ML_BPS, 2 * ML_H, 2 * HD, HD), F32), pltpu.VMEM((ML_BPS, 2 * ML_H, 1, 1), F32)],
        compiler_params=_cparams(("parallel", "arbitrary")),
        name="mlstm",
    )(*args, bias, bias_t, tril, triu)


FF_CHUNKS = ((0, 1536), (1536, D_FF))


def _out_ffn_body(al_ref, ac_ref, hl_ref, hc_ref, hf_ref, hb_ref, mo_ref, mnw_ref, bd_ref, wo_ref, x_ref,
                  g1_ref, sh2_ref, sc2_ref, g2_ref, nw2_ref, wg_ref, wu_ref, wd_ref, o_ref):
    i = pl.program_id(0)
    hn = _head_rms((hf_ref[...] + hb_ref[...]).T, bd_ref[...], mnw_ref[...])
    mlo = hn * _sigmoid(mo_ref[...])
    att = jnp.where(i < NT_LAT, al_ref[...], ac_ref[...])
    hy = jnp.where(i < NT_LAT, hl_ref[...], hc_ref[...])
    mixed = jnp.concatenate([att.astype(BF16), hy.astype(BF16), mlo.astype(BF16)], axis=1)
    x = x_ref[...] + g1_ref[...] * jnp.dot(mixed, wo_ref[...], preferred_element_type=F32)
    hb = _modulated_norm(x, nw2_ref[...], sc2_ref[...], sh2_ref[...]).astype(BF16)
    acc = None
    for lo, hi in FF_CHUNKS:
        sl = slice(lo, hi)
        a = jnp.dot(hb, wg_ref[:, sl], preferred_element_type=F32)
        u = jnp.dot(hb, wu_ref[:, sl], preferred_element_type=F32)
        part = _dot(a * _sigmoid(a) * u, wd_ref[sl, :])
        acc = part if acc is None else acc + part
    o_ref[...] = x + g2_ref[...] * acc


def _out_ffn(att, attc, hyo, hyoc, hf_t, hb_t, mo, ml_nw, w_out_b, xs, mod4, layer, nw2, wg, wu, wd, with_ctx):
    assert ATT_W == HY
    tiles = NT_ALL if with_ctx else NT_LAT
    row = lambda w: pl.BlockSpec((TM, w), lambda i: (i, 0))
    lat_rows = pl.BlockSpec((TM, HY), lambda i: (jnp.minimum(i, NT_LAT - 1), 0))
    ctx_rows = pl.BlockSpec((TM, HY), lambda i: (jnp.maximum(i - NT_LAT, 0), 0))
    lat_tiles = (NB_BATCH // ML_BPS) * TILES_PER_SEQ
    col = pl.BlockSpec((None, ML_W, TM), lambda i: (jnp.where(i < NT_LAT, i // lat_tiles, i - NT_LAT), 0,
                                                    jnp.where(i < NT_LAT, i % lat_tiles, lat_tiles)))
    resident = lambda shape: pl.BlockSpec(shape, lambda i: (0, 0), pipeline_mode=pl.Buffered(1))
    layer_w = lambda shape: pl.BlockSpec((None,) + shape, lambda i: (layer, 0, 0), pipeline_mode=pl.Buffered(1))
    vec = lambda w: pl.BlockSpec((1, w), lambda i: (0, 0))
    bd = jnp.asarray(_blockdiag_ones(ML_W)).astype(BF16)
    return pl.pallas_call(
        _out_ffn_body,
        grid=(tiles,),
        in_specs=[lat_rows, ctx_rows, lat_rows, ctx_rows,
                  col, col, row(ML_W), vec(ML_W), resident((ML_W, ML_W)), layer_w((D, D)), row(D),
                  _mod_spec(layer, 2), _mod_spec(layer, 3), _mod_spec(layer, 4), _mod_spec(layer, 5), vec(D),
                  layer_w((D, D_FF)), layer_w((D, D_FF)), layer_w((D_FF, D))],
        out_specs=row(D),
        out_shape=jax.ShapeDtypeStruct((tiles * TM, D), F32),
        compiler_params=_cparams(("parallel",), 56),
        name="out_ffn",
    )(att, attc, hyo, hyoc, hf_t, hb_t, mo, ml_nw.reshape(1, ML_W), bd, w_out_b, xs, mod4, mod4, mod4, mod4,
      nw2.reshape(1, D), wg, wu, wd)


def _rope_tables():
    n_rows = SEQ // GRID_W
    row = jnp.repeat(jnp.arange(n_rows), GRID_W)
    col = jnp.tile(jnp.arange(GRID_W), n_rows)
    nf = HD // 4
    inv_freq = ROPE_BASE ** (-jnp.arange(nf, dtype=F32) / nf)
    ang = jnp.stack([row[:, None] * inv_freq, col[:, None] * inv_freq], axis=1)
    cos, sin = jnp.cos(ang), jnp.sin(ang)
    cos_h = jnp.concatenate([cos, cos], axis=-1).reshape(SEQ, HD)
    sin_h = jnp.concatenate([-sin, sin], axis=-1).reshape(SEQ, HD)
    cos_t = jnp.concatenate([jnp.tile(cos_h, (1, 2)), jnp.ones((TM, 128), F32)], axis=0)
    sin_t = jnp.concatenate([jnp.tile(sin_h, (1, 2)), jnp.zeros((TM, 128), F32)], axis=0)
    return cos_t, sin_t


def kernel(x, c, ctx, c_ctx, w_mod, b_mod, norm1_w, norm2_w, w_in, w_out, q_norm_w, k_norm_w, attn_sink, hy_conv_w, hy_conv_b, hy_w1, hy_b1, hy_freq, hy_w2, hy_b2, hy_w3, hy_decay, hy_skip, ml_gate_b, ml_norm_w, ffn_w_gate, ffn_w_up, ffn_w_down):
    xs = jnp.concatenate([x.reshape(T_LAT, D), ctx.reshape(T_CTX, D)], axis=0)
    cc8 = jnp.concatenate([c, c_ctx[None, :], jnp.zeros((8 - NB_BATCH - 1, D), F32)], axis=0)
    mod4 = _modulation(cc8, w_mod, b_mod).reshape(DEPTH, 8, 1, 6 * D)

    cos_t, sin_t = _rope_tables()
    feats_l, feats_c = _pos_feats_circ(SEQ), _pos_feats_circ(CTX)
    s1_tab = jnp.asarray(_TAB["s1"]).astype(BF16)
    s1i_tab = jnp.asarray(_TAB["s1i"]).astype(BF16)
    wf, wi = jnp.asarray(_TAB["wf"]).astype(BF16), jnp.asarray(_TAB["wi"]).astype(BF16)
    fc, fs = jnp.asarray(_TAB["fc"]).astype(BF16), jnp.asarray(_TAB["fs"]).astype(BF16)

    w_in_p = _reorder_w_in(w_in)
    w_out_b = w_out.astype(BF16)
    wg_b, wu_b, wd_b = ffn_w_gate.astype(BF16), ffn_w_up.astype(BF16), ffn_w_down.astype(BF16)
    filt_params = _filter_params(hy_w1, hy_b1, hy_freq, hy_w2, hy_b2, hy_w3)
    sink8 = jnp.pad(attn_sink, ((0, 0), (0, 128 - ATT_H)))[:, None, :] * jnp.ones((1, 8, 1), F32)
    zeros_tile = jnp.zeros((TM, HY), F32)

    out = None
    for l in range(DEPTH):
        last = l == DEPTH - 1
        qt, kn, vt, hv, hx1, hx2, mqv_t, mk, mo, mg, mg_t = _in_proj(
            xs, mod4, l, norm1_w[l], w_in_p, cos_t, sin_t, q_norm_w[l], k_norm_w[l], hy_conv_w[l], hy_conv_b[l])

        att = _attention(qt, kn, vt, sink8[l])
        attc = zeros_tile if last else _attention_ctx(qt, kn, vt, sink8[l])

        dec = hy_decay[l].reshape(1, N_FILT)
        circ = _circular_filters(feats_l, filt_params, l, dec)
        f_re, f_im = _dft_stage1(circ.reshape(2, HALF_A, FFT_B, FFT_T, 2 * HY), s1_tab)
        h_re, h_im = _filter_spectrum(f_re.reshape(2, 2, N_FFT, HY), f_im.reshape(2, 2, N_FFT, HY), wf)

        s_in, gate_a = _row_blocks(hv), _row_blocks(hx1)
        for order in range(2):
            a_re, a_im = _dft_stage1_pair(s_in, s1_tab)
            rows = lambda t: t.reshape(NB_BATCH // 2, N_FFT, HY)
            c_re, c_im = _spectral_filter(rows(a_re), rows(a_im), h_re, h_im, order, wf, wi)
            tiles = lambda t: t.reshape(NB_BATCH // 2, FFT_A, FFT_B, FFT_T, HY)
            s_in = _idft_stage1_gate(tiles(c_re), tiles(c_im), s1i_tab, gate_a, s_in, hy_skip[l, order])
            gate_a = _row_blocks(hx2)
        hyo = s_in.reshape(T_LAT, HY)

        if last:
            hyoc = zeros_tile
        else:
            circ_c = _circular_filters(feats_c, filt_params, l, dec)
            hyoc = _hyena_ctx(hv, hx1, hx2, circ_c, fc, fs, hy_skip[l])

        hf_t, hb_t = _mlstm(mqv_t, mk, mg, mg_t, ml_gate_b[l])

        xs = _out_ffn(att, attc, hyo, hyoc, hf_t, hb_t, mo, ml_norm_w[l], w_out_b, xs, mod4, l,
                      norm2_w[l], wg_b, wu_b, wd_b, not last)
        out = xs
    return out.reshape(NB_BATCH, SEQ, D)
```

```python
import functools
import math

import numpy as np
import jax
import jax.numpy as jnp
from jax import lax
from jax.experimental import pallas as pl
from jax.experimental.pallas import tpu as pltpu

F32 = jnp.float32
BF16 = jnp.bfloat16

D = 1024
NB_BATCH = 4
SEQ = 4096
DEPTH = 4
GRID_W = 64
CTX = 256
T_LAT = NB_BATCH * SEQ
T_CTX = NB_BATCH * CTX
R_ALL = T_LAT + T_CTX

HD = 64
ATT_H = 6
ATT_W = ATT_H * HD
KV_W = 2 * HD
BLK = 128
NBLK = SEQ // BLK
ROPE_BASE = 10000.0

HY = 384
N_FILT = 4 * HY
POS_BANDS = 16
FILT_HID = 64

ML_H = 4
ML_W = 256
CHUNK = 128

D_FF = 2816
P_IN = 2832
P_PAD = 2944
NEG = -1e30
EPS = 1e-6
LOG2E = 1.4426950408889634

TM = 512
NT_LAT = T_LAT // TM
NT_ALL = R_ALL // TM
TILES_PER_SEQ = SEQ // TM

N_FFT = 2 * SEQ
FFT_A = 32
FFT_R = 256
FFT_T = 16
FFT_B = FFT_R // FFT_T
HALF_A = FFT_A // 2
S1_TILES = 4
HB_ROWS = 1024
N_CTXF = 2 * CTX


def _np_tables():
    ka = np.arange(FFT_A)[None, :, None, None]
    cc = np.arange(FFT_T)[None, None, :, None]
    aa = np.arange(HALF_A)[None, None, None, :]
    bb = np.arange(FFT_B)[:, None, None, None]
    ph = (ka * (FFT_R * aa + FFT_T * bb + cc)) % N_FFT
    th = 2.0 * np.pi * ph / N_FFT
    eye = np.eye(FFT_T)
    cos = np.einsum("bkca,cd->bkcad", np.cos(th), eye)
    sin = np.einsum("bkca,cd->bkcad", np.sin(th), eye)
    rows, cols = FFT_A * FFT_T, HALF_A * FFT_T
    s1 = np.concatenate([cos.reshape(FFT_B, rows, cols), -sin.reshape(FFT_B, rows, cols)], axis=1)
    s1i = np.concatenate([cos.reshape(FFT_B, rows, cols).transpose(0, 2, 1),
                          -sin.reshape(FFT_B, rows, cols).transpose(0, 2, 1)], axis=2) / N_FFT
    r = np.arange(FFT_R)
    th2 = 2.0 * np.pi * ((r[:, None] * r[None, :]) % FFT_R) / FFT_R
    c2, s2 = np.cos(th2), np.sin(th2)
    wf = np.block([[c2, s2], [-s2, c2]])
    wi = np.block([[c2, -s2], [s2, c2]])
    c = np.arange(N_CTXF)
    th3 = 2.0 * np.pi * ((c[:, None] * c[None, :]) % N_CTXF) / N_CTXF
    return dict(s1=s1.astype(np.float32), s1i=s1i.astype(np.float32), wf=wf.astype(np.float32), wi=wi.astype(np.float32),
                fc=np.cos(th3).astype(np.float32), fs=np.sin(th3).astype(np.float32))


_TAB = _np_tables()


def _blockdiag_ones(width):
    return np.kron(np.eye(width // HD), np.ones((HD, HD))).astype(np.float32)


def _cparams(sem, vmem_mb=48):
    return pltpu.CompilerParams(dimension_semantics=sem, vmem_limit_bytes=vmem_mb * 1024 * 1024)


def _dot(a, b):
    return jnp.dot(a.astype(BF16), b.astype(BF16), preferred_element_type=F32)


def _dot_nt(a, b):
    return lax.dot_general(a.astype(BF16), b.astype(BF16), (((1,), (1,)), ((), ())), preferred_element_type=F32)


def _dot_tn(a, b):
    return lax.dot_general(a.astype(BF16), b.astype(BF16), (((0,), (0,)), ((), ())), preferred_element_type=F32)


def _split3(x):
    x1 = x.astype(BF16)
    r1 = x - x1.astype(F32)
    x2 = r1.astype(BF16)
    x3 = (r1 - x2.astype(F32)).astype(BF16)
    return x1, x2, x3


def _dot_sel_l(sel, x):
    return sum(jnp.dot(sel, p, preferred_element_type=F32) for p in _split3(x))


def _dot_sel_r(x, sel):
    return sum(jnp.dot(p, sel, preferred_element_type=F32) for p in _split3(x))


def _sigmoid(x):
    return 1.0 / (1.0 + jnp.exp(-x))


def _log_sigmoid(x):
    return jnp.minimum(x, 0.0) - jnp.log(1.0 + jnp.exp(-jnp.abs(x)))


def _head_rms(t, bd, w):
    sq = t * t
    hi = sq.astype(BF16)
    lo = (sq - hi.astype(F32)).astype(BF16)
    ss = jnp.dot(hi, bd, preferred_element_type=F32) + jnp.dot(lo, bd, preferred_element_type=F32)
    return t * lax.rsqrt(ss * (1.0 / HD) + EPS) * w


def _mod_row(i):
    return jnp.where(i < NT_LAT, i // TILES_PER_SEQ, NB_BATCH)


def _mod_spec(layer, k):
    return pl.BlockSpec((None, None, 1, D), lambda i: (layer, _mod_row(i), 0, k))


def _mod_body(s_ref, w_ref, b_ref, o_ref):
    s = s_ref[...]
    s = s * _sigmoid(s)
    o_ref[...] = _dot(s, w_ref[...]) + b_ref[...]


def _modulation(cc8, w_mod, b_mod):
    nc = 1536
    return pl.pallas_call(
        _mod_body,
        grid=(DEPTH, 6 * D // nc),
        in_specs=[pl.BlockSpec((8, D), lambda l, j: (0, 0)),
                  pl.BlockSpec((None, D, nc), lambda l, j: (l, 0, j)),
                  pl.BlockSpec((None, 1, nc), lambda l, j: (l, 0, j))],
        out_specs=pl.BlockSpec((None, 8, nc), lambda l, j: (l, 0, j)),
        out_shape=jax.ShapeDtypeStruct((DEPTH, 8, 6 * D), F32),
        compiler_params=_cparams(("parallel", "parallel")),
        name="modulation",
    )(cc8, w_mod, b_mod.reshape(DEPTH, 1, 6 * D))


_COL_ATT, _COL_HY, _COL_MQV, _COL_MK, _COL_MO, _COL_MG = (0, 640), (640, 1792), (1792, 2304), (2304, 2560), (2560, 2816), (2816, 2944)
N_GATES = 4 * ML_H


def _reorder_w_in(w_in):
    mq_end, mk_end, mv_end = 2048, 2304, 2560
    w = jnp.concatenate([w_in[:, :, :mq_end], w_in[:, :, mk_end:mv_end], w_in[:, :, mq_end:mk_end], w_in[:, :, mv_end:]], axis=2)
    return jnp.pad(w, ((0, 0), (0, 0), (0, P_PAD - P_IN))).astype(BF16)


def _modulated_norm(x, nw, sc, sh):
    ms = jnp.mean(x * x, axis=-1, keepdims=True)
    return (x * lax.rsqrt(ms + EPS)) * (nw * (1.0 + sc)) + sh


HALO = 8


def _in_body(x_ref, xp_ref, xn_ref, sh_ref, sc_ref, nw_ref, w_ref, cos_ref, sin_ref, qw_ref, kw_ref, bdq_ref, bdk_ref,
             cw_ref, cb_ref, qt_ref, k_ref, vt_ref, hv_ref, hx1_ref, hx2_ref, mqvt_ref, mk_ref, mo_ref, mg_ref, mgt_ref):
    i = pl.program_id(0)
    nw, sc, sh = nw_ref[...], sc_ref[...], sh_ref[...]
    hb = _modulated_norm(x_ref[...], nw, sc, sh).astype(BF16)
    proj = lambda cols: jnp.dot(hb, w_ref[:, cols[0]:cols[1]], preferred_element_type=F32)

    att = proj(_COL_ATT)
    cos, sin = cos_ref[...], sin_ref[...]
    lane = lax.broadcasted_iota(jnp.int32, (TM, 128), 1)
    first_half = (lane & 31) < 16
    q = _head_rms(att[:, 0:ATT_W], bdq_ref[...], qw_ref[...])
    k = _head_rms(att[:, ATT_W:ATT_W + KV_W], bdk_ref[...], kw_ref[...])
    qt_ref[...] = (_rope(q, cos, sin, first_half) * (HD ** -0.5 * LOG2E)).T.astype(BF16)
    k_ref[...] = _rope(k, cos, sin, first_half).astype(BF16)
    vt_ref[...] = att[:, ATT_W + KV_W:ATT_W + 2 * KV_W].T.astype(BF16)

    halo = _modulated_norm(jnp.concatenate([xp_ref[...], xn_ref[...]], axis=0), nw, sc, sh).astype(BF16)
    hy_all = jnp.dot(jnp.concatenate([hb, halo], axis=0), w_ref[:, _COL_HY[0]:_COL_HY[1]], preferred_element_type=F32)
    hy, hy_halo = hy_all[:TM, :], hy_all[TM:, :]
    seq_len = jnp.where(i < NT_LAT, SEQ, CTX)
    pos = (i * TM + lax.broadcasted_iota(jnp.int32, (TM, 1), 0)) & (seq_len - 1)
    usc = _short_conv_rows(hy, hy_halo[HALO - 1:HALO, :], hy_halo[HALO:HALO + 1, :], cw_ref[...], cb_ref[...],
                           pos == 0, pos == seq_len - 1)
    hv_ref[...] = usc[:, 0:HY]
    hx1_ref[...] = usc[:, HY:2 * HY]
    hx2_ref[...] = usc[:, 2 * HY:3 * HY]

    mqvt_ref[...] = proj(_COL_MQV).T.astype(BF16)
    mk_ref[...] = (proj(_COL_MK) * (HD ** -0.5)).astype(BF16)
    mo_ref[...] = proj(_COL_MO)
    mg = proj(_COL_MG)
    mg_ref[...] = mg
    mgt_ref[...] = mg.T[0:N_GATES, :]


def _in_proj(xs, mod4, layer, nw, w_in_p, cos_t, sin_t, qw, kw, conv_w, conv_b):
    rows = lambda w: pl.BlockSpec((TM, w), lambda i: (i, 0))
    cols = lambda h: pl.BlockSpec((h, TM), lambda i: (0, i))
    const = lambda shape: pl.BlockSpec(shape, lambda i: (0, 0))
    tab = lambda i: (jnp.where(i < NT_LAT, i % TILES_PER_SEQ, TILES_PER_SEQ), 0)
    per_tile = TM // HALO
    prev = pl.BlockSpec((HALO, D), lambda i: (jnp.maximum(i * per_tile - 1, 0), 0))
    nxt = pl.BlockSpec((HALO, D), lambda i: (jnp.minimum((i + 1) * per_tile, R_ALL // HALO - 1), 0))
    bdq = jnp.asarray(_blockdiag_ones(ATT_W)).astype(BF16)
    bdk = jnp.asarray(_blockdiag_ones(KV_W)).astype(BF16)
    bf = lambda shape: jax.ShapeDtypeStruct(shape, BF16)
    f32 = lambda shape: jax.ShapeDtypeStruct(shape, F32)
    return pl.pallas_call(
        _in_body,
        grid=(NT_ALL,),
        in_specs=[rows(D), prev, nxt, _mod_spec(layer, 0), _mod_spec(layer, 1), const((1, D)),
                  pl.BlockSpec((None, D, P_PAD), lambda i: (layer, 0, 0), pipeline_mode=pl.Buffered(1)),
                  pl.BlockSpec((TM, 128), tab), pl.BlockSpec((TM, 128), tab),
                  const((1, ATT_W)), const((1, KV_W)), const((ATT_W, ATT_W)), const((KV_W, KV_W)),
                  const((3, 3 * HY)), const((1, 3 * HY))],
        out_specs=[cols(ATT_W), rows(KV_W), cols(KV_W), rows(HY), rows(HY), rows(HY),
                   cols(2 * ML_W), rows(ML_W), rows(ML_W), rows(128), cols(N_GATES)],
        out_shape=[bf((ATT_W, R_ALL)), bf((R_ALL, KV_W)), bf((KV_W, R_ALL)), f32((R_ALL, HY)), f32((R_ALL, HY)), f32((R_ALL, HY)),
                   bf((2 * ML_W, R_ALL)), bf((R_ALL, ML_W)), f32((R_ALL, ML_W)), f32((R_ALL, 128)), f32((N_GATES, R_ALL))],
        compiler_params=_cparams(("parallel",), 56),
        name="in_proj",
    )(xs, xs, xs, mod4, mod4, nw.reshape(1, D), w_in_p, cos_t, sin_t,
      jnp.tile(qw, ATT_H).reshape(1, ATT_W), jnp.tile(kw, 2).reshape(1, KV_W), bdq, bdk, conv_w, conv_b.reshape(1, 3 * HY))


def _rope(t, cos, sin_signed, first_half):
    outs = []
    for c in range(t.shape[1] // 128):
        tc = t[:, c * 128:(c + 1) * 128]
        partner = jnp.where(first_half, pltpu.roll(tc, 112, 1), pltpu.roll(tc, 16, 1))
        outs.append(tc * cos + partner * sin_signed)
    return outs[0] if len(outs) == 1 else jnp.concatenate(outs, axis=1)


def _att_heads(qt, kv_list, sink_ref, o_ref):
    nq = qt.shape[1]
    rep = ATT_H // 2
    zeros = jnp.zeros((HD, rep * nq), BF16)
    outs = []
    for g in range(2):
        heads = range(rep * g, rep * (g + 1))
        qg = jnp.concatenate([qt[HD * h:HD * (h + 1), :] for h in heads], axis=1)
        qe = jnp.concatenate([qg, zeros] if g == 0 else [zeros, qg], axis=0)
        sink = jnp.concatenate([jnp.broadcast_to(sink_ref[0:1, h:h + 1], (1, nq)) for h in heads], axis=1) * LOG2E
        scores = []
        m = sink
        for k, _, mask in kv_list:
            s = jnp.dot(k, qe, preferred_element_type=F32)
            if mask is not None:
                s = jnp.where(mask, s, NEG)
            scores.append(s)
            m = jnp.maximum(m, jnp.max(s, axis=0, keepdims=True))
        acc = None
        for s, (_, vt, _) in zip(scores, kv_list):
            p = jnp.exp2(s - m).astype(BF16)
            ones = (lax.broadcasted_iota(jnp.int32, (16, vt.shape[1]), 0) == 0).astype(BF16)
            pv = jnp.dot(jnp.concatenate([vt[HD * g:HD * (g + 1), :], ones], axis=0), p, preferred_element_type=F32)
            acc = pv if acc is None else acc + pv
        og = acc[:HD, :] / (acc[HD:HD + 1, :] + jnp.exp2(sink - m))
        outs += [og[:, nq * i:nq * (i + 1)] for i in range(rep)]
    o_ref[...] = jnp.concatenate(outs, axis=0).T


QB = 8


def _att_body(*refs):
    nband = QB + 2
    qt_ref, ks, vs = refs[0], refs[1:1 + nband], refs[1 + nband:1 + 2 * nband]
    kc_ref, vc_ref, sink_ref, o_ref = refs[1 + 2 * nband:]
    j = pl.program_id(1)
    kc, vc = kc_ref[...], vc_ref[...]
    width = (ATT_H // 2) * BLK
    key = lax.broadcasted_iota(jnp.int32, (BLK, width), 0)
    qry = lax.broadcasted_iota(jnp.int32, (BLK, width), 1) & (BLK - 1)
    for sub in range(QB):
        blk = QB * j + sub
        mask_prev = jnp.logical_and(key >= qry, blk > 0)
        mask_next = jnp.logical_and(key <= qry, blk < NBLK - 1)
        _att_heads(qt_ref[:, sub * BLK:(sub + 1) * BLK],
                   [(ks[sub][...], vs[sub][...], mask_prev), (ks[sub + 1][...], vs[sub + 1][...], None),
                    (ks[sub + 2][...], vs[sub + 2][...], mask_next), (kc, vc, None)],
                   sink_ref, o_ref.at[sub * BLK:(sub + 1) * BLK, :])


def _attention(qt, kn, vt, sink8):
    qrows = QB * BLK
    steps = NBLK // QB

    def band(off):
        return lambda b, j: b * NBLK + jnp.clip(QB * j + off, 0, NBLK - 1)

    kspec = lambda f: pl.BlockSpec((BLK, KV_W), lambda b, j: (f(b, j), 0))
    vspec = lambda f: pl.BlockSpec((KV_W, BLK), lambda b, j: (0, f(b, j)))
    offs = range(-1, QB + 1)
    return pl.pallas_call(
        _att_body,
        grid=(NB_BATCH, steps),
        in_specs=[pl.BlockSpec((ATT_W, qrows), lambda b, j: (0, b * steps + j))]
                 + [kspec(band(o)) for o in offs] + [vspec(band(o)) for o in offs]
                 + [pl.BlockSpec((CTX, KV_W), lambda b, j: (T_LAT // CTX + b, 0)),
                    pl.BlockSpec((KV_W, CTX), lambda b, j: (0, T_LAT // CTX + b)),
                    pl.BlockSpec((8, 128), lambda b, j: (0, 0))],
        out_specs=pl.BlockSpec((qrows, ATT_W), lambda b, j: (b * steps + j, 0)),
        out_shape=jax.ShapeDtypeStruct((T_LAT, ATT_W), F32),
        compiler_params=_cparams(("parallel", "parallel")),
        name="attention",
    )(qt, *([kn] * (QB + 2)), *([vt] * (QB + 2)), kn, vt, sink8)


def _att_ctx_body(qt_ref, kc_ref, vc_ref, sink_ref, o_ref):
    kv = [(kc_ref[...], vc_ref[...], None)]
    for sub in range(CTX // BLK):
        _att_heads(qt_ref[:, sub * BLK:(sub + 1) * BLK], kv, sink_ref, o_ref.at[sub * BLK:(sub + 1) * BLK, :])


def _attention_ctx(qt, kn, vt, sink8):
    return pl.pallas_call(
        _att_ctx_body,
        grid=(NB_BATCH,),
        in_specs=[pl.BlockSpec((ATT_W, CTX), lambda b: (0, T_LAT // CTX + b)),
                  pl.BlockSpec((CTX, KV_W), lambda b: (T_LAT // CTX + b, 0)),
                  pl.BlockSpec((KV_W, CTX), lambda b: (0, T_LAT // CTX + b)),
                  pl.BlockSpec((8, 128), lambda b: (0, 0))],
        out_specs=pl.BlockSpec((CTX, ATT_W), lambda b: (b, 0)),
        out_shape=jax.ShapeDtypeStruct((T_CTX, ATT_W), F32),
        compiler_params=_cparams(("parallel",)),
        name="attention_ctx",
    )(qt, kn, vt, sink8)


def _short_conv_rows(u, prev_row, next_row, w, b, first, last):
    n = u.shape[0]
    row = lax.broadcasted_iota(jnp.int32, u.shape, 0)
    up = jnp.where(row == 0, prev_row, pltpu.roll(u, 1, 0))
    un = jnp.where(row == n - 1, next_row, pltpu.roll(u, n - 1, 0))
    up = jnp.where(first, 0.0, up)
    un = jnp.where(last, 0.0, un)
    return up * w[0:1, :] + u * w[1:2, :] + un * w[2:3, :] + b


def _filt_body(n, tl, f_ref, w1_ref, b1_ref, fr_ref, w2_ref, b2_ref, w3a_ref, w3b_ref, dec_ref, o_ref):
    h = tl // 2
    f = jnp.concatenate([f_ref[0:h, :], f_ref[h:tl, :]], axis=1)
    fr = fr_ref[...]
    z = jnp.sin(fr * (_dot(f, w1_ref[...]) + b1_ref[...]))
    z = jnp.sin(fr * (_dot(z, w2_ref[...]) + b2_ref[...]))
    dec = jnp.abs(dec_ref[...])
    top = _dot(z, w3a_ref[...]) * jnp.exp(-f[:, 0:1] * dec)
    bot = _dot(z, w3b_ref[...]) * jnp.exp(-f[:, 128:129] * dec)
    filt = jnp.concatenate([top, bot], axis=0)
    row = pl.program_id(0) * tl + lax.broadcasted_iota(jnp.int32, filt.shape, 0)
    o_ref[...] = jnp.where(row == n, 0.0, filt)


def _filter_params(hy_w1, hy_b1, hy_freq, hy_w2, hy_b2, hy_w3):
    hid = FILT_HID
    w1 = jnp.pad(hy_w1, ((0, 0), (0, 128 - hy_w1.shape[1]), (0, 0)))
    zero = jnp.zeros_like
    w1b = jnp.concatenate([jnp.concatenate([w1, zero(w1)], axis=2), jnp.concatenate([zero(w1), w1], axis=2)], axis=1)
    w2b = jnp.concatenate([jnp.concatenate([hy_w2, zero(hy_w2)], axis=2), jnp.concatenate([zero(hy_w2), hy_w2], axis=2)], axis=1)
    w3a = jnp.concatenate([hy_w3, zero(hy_w3)], axis=1)
    w3b = jnp.concatenate([zero(hy_w3), hy_w3], axis=1)
    twice = lambda v: jnp.concatenate([v, v], axis=1).reshape(DEPTH, 1, 2 * hid)
    return w1b, twice(hy_b1), twice(hy_freq), w2b, twice(hy_b2), w3a, w3b


def _circular_filters(feats_circ, params, layer, dec):
    n = feats_circ.shape[0] // 2
    tl = min(n, 512)
    w1b, b1, fr, w2b, b2, w3a, w3b = params
    per = lambda shape: pl.BlockSpec((None,) + shape, lambda i: (layer, 0, 0))
    half = pl.BlockSpec((None, 128, 2 * HY), lambda i: (layer, 0, i // (n // tl)))
    return pl.pallas_call(
        functools.partial(_filt_body, n, tl),
        grid=(2 * n // tl,),
        in_specs=[pl.BlockSpec((tl, 128), lambda i: (i, 0)), per((256, 128)), per((1, 128)), per((1, 128)),
                  per((128, 128)), per((1, 128)), half, half, pl.BlockSpec((1, 2 * HY), lambda i: (0, i // (n // tl)))],
        out_specs=pl.BlockSpec((tl, 2 * HY), lambda i: (i, 0)),
        out_shape=jax.ShapeDtypeStruct((2 * n, 2 * HY), F32),
        compiler_params=_cparams(("parallel",)),
        name="hyena_filters",
    )(feats_circ, w1b, b1, fr, w2b, b2, w3a, w3b, dec)


def _pos_feats_circ(n):
    t = jnp.linspace(0.0, 1.0, n, dtype=F32)[:, None]
    ang = (2.0 * math.pi / n) * jnp.arange(n, dtype=F32)[:, None]
    bands = jnp.linspace(1e-4, POS_BANDS - 1, POS_BANDS, dtype=F32)[None, :]
    feats = jnp.concatenate([t, jnp.cos(bands * ang), -jnp.sin(bands * ang)], axis=-1)
    feats = jnp.pad(feats, ((0, 0), (0, 128 - feats.shape[1])))
    return jnp.concatenate([feats, feats[:1], jnp.flip(feats[:n - 1], axis=0)], axis=0)


def _s1_body(x_ref, m_ref, are_ref, aim_ref):
    half = FFT_A * FFT_T
    ncb = x_ref.shape[-1] // HY
    for p in range(S1_TILES):
        xs = x_ref[:, p].reshape(HALF_A * FFT_T, ncb * HY)
        r = _dot(m_ref[p], xs)
        for cb in range(ncb):
            cols = slice(cb * HY, (cb + 1) * HY)
            are_ref[cb, :, p] = r[:half, cols].reshape(FFT_A, FFT_T, HY).astype(BF16)
            aim_ref[cb, :, p] = r[half:, cols].reshape(FFT_A, FFT_T, HY).astype(BF16)


def _dft_stage1(x5, s1_tab):
    nb, ncb = x5.shape[0], x5.shape[-1] // HY
    out = pl.BlockSpec((None, ncb, FFT_A, S1_TILES, FFT_T, HY), lambda b, j: (b, 0, 0, j, 0, 0))
    return pl.pallas_call(
        _s1_body,
        grid=(nb, FFT_B // S1_TILES),
        in_specs=[pl.BlockSpec((None, HALF_A, S1_TILES, FFT_T, ncb * HY), lambda b, j: (b, 0, j, 0, 0)),
                  pl.BlockSpec((S1_TILES, 2 * FFT_A * FFT_T, HALF_A * FFT_T), lambda b, j: (j, 0, 0))],
        out_specs=[out, out],
        out_shape=[jax.ShapeDtypeStruct((nb, ncb, FFT_A, FFT_B, FFT_T, HY), BF16)] * 2,
        compiler_params=_cparams(("parallel", "parallel")),
        name="dft_stage1",
    )(x5, s1_tab)


def _s1_pair_body(x_ref, m_ref, are_ref, aim_ref):
    half = FFT_A * FFT_T
    rows = HALF_A * FFT_T
    for p in range(S1_TILES):
        xab = jnp.concatenate([x_ref[0:HALF_A, p].reshape(rows, HY), x_ref[HALF_A:2 * HALF_A, p].reshape(rows, HY)], axis=1)
        r = _dot(m_ref[p], xab)
        are_ref[:, p] = (r[:half, :HY] - r[half:, HY:]).reshape(FFT_A, FFT_T, HY).astype(BF16)
        aim_ref[:, p] = (r[half:, :HY] + r[:half, HY:]).reshape(FFT_A, FFT_T, HY).astype(BF16)


def _row_blocks(x):
    return x.reshape(x.shape[0] // FFT_R, FFT_B, FFT_T, HY)


_PAIR_SPEC = pl.BlockSpec((2 * HALF_A, S1_TILES, FFT_T, HY), lambda b, j: (b, j, 0, 0))


def _dft_stage1_pair(x4, s1_tab):
    npair = NB_BATCH // 2
    out = pl.BlockSpec((None, FFT_A, S1_TILES, FFT_T, HY), lambda b, j: (b, 0, j, 0, 0))
    return pl.pallas_call(
        _s1_pair_body,
        grid=(npair, FFT_B // S1_TILES),
        in_specs=[_PAIR_SPEC, pl.BlockSpec((S1_TILES, 2 * FFT_A * FFT_T, HALF_A * FFT_T), lambda b, j: (j, 0, 0))],
        out_specs=[out, out],
        out_shape=[jax.ShapeDtypeStruct((npair, FFT_A, FFT_B, FFT_T, HY), BF16)] * 2,
        compiler_params=_cparams(("parallel", "parallel")),
        name="dft_stage1_pair",
    )(x4, s1_tab)


def _block_rows(r):
    return slice(r * FFT_R, (r + 1) * FFT_R)


def _stack_complex(re_ref, im_ref, sl):
    return jnp.concatenate([re_ref[sl, :], im_ref[sl, :]], axis=0)


def _spec_body(are_ref, aim_ref, wf_ref, hre_ref, him_ref):
    wf = wf_ref[...]
    for r in range(HB_ROWS // FFT_R):
        sl = _block_rows(r)
        both = jnp.concatenate([_stack_complex(are_ref.at[0], aim_ref.at[0], sl),
                                _stack_complex(are_ref.at[1], aim_ref.at[1], sl)], axis=1)
        x = jnp.dot(wf, both, preferred_element_type=F32)
        first, second = x[:, :HY], x[:, HY:]
        x = first + second if r % 2 == 0 else first - second
        hre_ref[sl, :] = x[:FFT_R]
        him_ref[sl, :] = x[FFT_R:]


def _filter_spectrum(a_re, a_im, wf):
    assert (HB_ROWS // FFT_R) % 2 == 0
    blk_in = pl.BlockSpec((2, None, HB_ROWS, HY), lambda r, o: (0, o, r, 0))
    blk_out = pl.BlockSpec((None, HB_ROWS, HY), lambda r, o: (o, r, 0))
    return pl.pallas_call(
        _spec_body,
        grid=(N_FFT // HB_ROWS, 2),
        in_specs=[blk_in, blk_in, pl.BlockSpec((2 * FFT_R, 2 * FFT_R), lambda r, o: (0, 0))],
        out_specs=[blk_out, blk_out],
        out_shape=[jax.ShapeDtypeStruct((2, N_FFT, HY), F32)] * 2,
        compiler_params=_cparams(("parallel", "parallel")),
        name="filter_spectrum",
    )(a_re, a_im, wf)


def _s2_body(are_ref, aim_ref, hre_ref, him_ref, wf_ref, wi_ref, cre_ref, cim_ref):
    wf, wi = wf_ref[...], wi_ref[...]
    for r in range(0, HB_ROWS // FFT_R, 2):
        sl0, sl1 = _block_rows(r), _block_rows(r + 1)
        a = jnp.concatenate([_stack_complex(are_ref, aim_ref, sl0), _stack_complex(are_ref, aim_ref, sl1)], axis=1)
        x = jnp.dot(wf, a, preferred_element_type=F32)
        xre, xim = x[:FFT_R], x[FFT_R:]
        hre = jnp.concatenate([hre_ref[sl0, :], hre_ref[sl1, :]], axis=1)
        him = jnp.concatenate([him_ref[sl0, :], him_ref[sl1, :]], axis=1)
        y = jnp.concatenate([(xre * hre - xim * him).astype(BF16), (xre * him + xim * hre).astype(BF16)], axis=0)
        c = jnp.dot(wi, y, preferred_element_type=F32)
        for i, sl in enumerate((sl0, sl1)):
            cre_ref[sl, :] = c[:FFT_R, i * HY:(i + 1) * HY].astype(BF16)
            cim_ref[sl, :] = c[FFT_R:, i * HY:(i + 1) * HY].astype(BF16)


def _spectral_filter(a_re, a_im, h_re, h_im, order, wf, wi):
    nb = a_re.shape[0]
    blk = pl.BlockSpec((None, HB_ROWS, HY), lambda r, b: (b, r, 0))
    hblk = pl.BlockSpec((None, HB_ROWS, HY), lambda r, b: (order, r, 0))
    mat = pl.BlockSpec((2 * FFT_R, 2 * FFT_R), lambda r, b: (0, 0))
    return pl.pallas_call(
        _s2_body,
        grid=(N_FFT // HB_ROWS, nb),
        in_specs=[blk, blk, hblk, hblk, mat, mat],
        out_specs=[blk, blk],
        out_shape=[jax.ShapeDtypeStruct((nb, N_FFT, HY), BF16)] * 2,
        compiler_params=_cparams(("parallel", "parallel")),
        name="spectral_filter",
    )(a_re, a_im, h_re, h_im, wf, wi)


def _s1i_body(chain, cre_ref, cim_ref, g_ref, a_ref, b_ref, sk_ref, *rest):
    if chain:
        m_ref, o_ref, are_ref, aim_ref = rest
    else:
        (o_ref,) = rest
    rows = FFT_A * FFT_T
    for p in range(S1_TILES):
        cre, cim = cre_ref[:, p].reshape(rows, HY), cim_ref[:, p].reshape(rows, HY)
        c = jnp.concatenate([jnp.concatenate([cre, cim], axis=1), jnp.concatenate([cim, -cre], axis=1)], axis=0)
        y = jnp.dot(g_ref[p], c, preferred_element_type=F32)
        gated = []
        for i in range(2):
            seq = slice(i * HALF_A, (i + 1) * HALF_A)
            yi = y[:, i * HY:(i + 1) * HY].reshape(HALF_A, FFT_T, HY)
            z = a_ref[seq, p] * (yi + b_ref[seq, p] * sk_ref[...])
            o_ref[seq, p] = z
            gated.append(z.reshape(HALF_A * FFT_T, HY))
        if chain:
            r = _dot(m_ref[p], jnp.concatenate(gated, axis=1))
            are_ref[:, p] = (r[:rows, :HY] - r[rows:, HY:]).reshape(FFT_A, FFT_T, HY).astype(BF16)
            aim_ref[:, p] = (r[rows:, :HY] + r[:rows, HY:]).reshape(FFT_A, FFT_T, HY).astype(BF16)


def _idft_stage1_gate(c_re, c_im, s1i_tab, a4, b4, skip, s1_tab=None):
    chain = s1_tab is not None
    npair = NB_BATCH // 2
    data = _PAIR_SPEC
    spec = pl.BlockSpec((None, FFT_A, S1_TILES, FFT_T, HY), lambda b, j: (b, 0, j, 0, 0))
    ins = [spec, spec, pl.BlockSpec((S1_TILES, HALF_A * FFT_T, 2 * FFT_A * FFT_T), lambda b, j: (j, 0, 0)),
           data, data, pl.BlockSpec((1, 1, HY), lambda b, j: (0, 0, 0))]
    args = [c_re, c_im, s1i_tab, a4, b4, skip.reshape(1, 1, HY)]
    outs, shapes = [data], [jax.ShapeDtypeStruct((T_LAT // FFT_R, FFT_B, FFT_T, HY), F32)]
    if chain:
        ins.append(pl.BlockSpec((S1_TILES, 2 * FFT_A * FFT_T, HALF_A * FFT_T), lambda b, j: (j, 0, 0)))
        args.append(s1_tab)
        outs += [spec, spec]
        shapes += [jax.ShapeDtypeStruct((npair, FFT_A, FFT_B, FFT_T, HY), BF16)] * 2
    res = pl.pallas_call(
        functools.partial(_s1i_body, chain),
        grid=(npair, FFT_B // S1_TILES),
        in_specs=ins,
        out_specs=outs,
        out_shape=shapes,
        compiler_params=_cparams(("parallel", "parallel"), 56),
        name="idft_stage1_gate",
    )(*args)
    return res if chain else res[0]


def _hyc_body(v_ref, x1_ref, x2_ref, circ_ref, fc_ref, fs_ref, sk_ref, o_ref):
    fc, fs = fc_ref[...], fs_ref[...]
    circ = circ_ref[...]
    h_re, h_im = _dot(fc, circ), -_dot(fs, circ)

    def long_conv(s, o):
        sl = slice(o * HY, (o + 1) * HY)
        s_re, s_im = _dot(fc[:, :CTX], s), -_dot(fs[:, :CTX], s)
        hre, him = h_re[:, sl], h_im[:, sl]
        y_re = s_re * hre - s_im * him
        y_im = s_re * him + s_im * hre
        y = (_dot(fc[:CTX, :], y_re) - _dot(fs[:CTX, :], y_im)) * (1.0 / N_CTXF)
        return y + s * sk_ref[o:o + 1, :]

    o_ref[...] = x2_ref[...] * long_conv(x1_ref[...] * long_conv(v_ref[...], 0), 1)


def _hyena_ctx(hv, hx1, hx2, circ_c, fc, fs, skip):
    const = lambda shape: pl.BlockSpec(shape, lambda b: (0, 0))
    seq = pl.BlockSpec((CTX, HY), lambda b: (T_LAT // CTX + b, 0))
    return pl.pallas_call(
        _hyc_body,
        grid=(NB_BATCH,),
        in_specs=[seq, seq, seq, const((N_CTXF, 2 * HY)), const((N_CTXF, N_CTXF)), const((N_CTXF, N_CTXF)), const((2, HY))],
        out_specs=pl.BlockSpec((CTX, HY), lambda b: (b, 0)),
        out_shape=jax.ShapeDtypeStruct((T_CTX, HY), F32),
        compiler_params=_cparams(("parallel",)),
        name="hyena_ctx",
    )(hv, hx1, hx2, circ_c, fc, fs, skip)


N_ML_STEPS = CTX // CHUNK + SEQ // CHUNK


def _ml_chain(direction, qv_ref, k_ref, src_col, cum_r, gates_t, c_scr, m_scr, mask, ones_rows):
    base = 8 * direction
    outs = []
    for h in range(ML_H):
        ic, fc = base + h, base + 4 + h
        chain = direction * ML_H + h
        b_row, li_row = cum_r[fc:fc + 1, :], gates_t[ic:ic + 1, :]
        b_end = b_row[:, CHUNK - 1:CHUNK] if direction == 0 else b_row[:, 0:1]
        q_t = qv_ref[HD * h:HD * (h + 1), :]
        vext_t = jnp.concatenate([qv_ref[ML_W + HD * h:ML_W + HD * (h + 1), :], ones_rows], axis=0)
        k = k_ref[:, HD * h:HD * (h + 1)]
        c_prev, m_prev = c_scr[chain], m_scr[chain]
        dmat = jnp.where(mask, src_col[:, fc:fc + 1] + b_row, NEG)
        m_intra = jnp.max(dmat, axis=0, keepdims=True)
        s_t = jnp.dot(k, q_t, preferred_element_type=F32) * jnp.exp(dmat - m_intra)
        inter = b_row + m_prev
        m_t = jnp.maximum(inter, m_intra)
        hx = jnp.exp(m_intra - m_t) * _dot(vext_t, s_t) + jnp.exp(inter - m_t) * _dot(c_prev, q_t)
        den = jnp.maximum(jnp.abs(hx[HD:HD + 1, :]), jnp.exp(-m_t))
        outs.append(hx[:HD, :] / den)
        g_row = b_end - b_row + li_row
        m_new = jnp.maximum(b_end + m_prev, jnp.max(g_row, axis=1, keepdims=True))
        c_scr[chain] = jnp.exp(b_end + m_prev - m_new) * c_prev + _dot(vext_t * jnp.exp(g_row - m_new), k)
        m_scr[chain] = m_new
    return jnp.concatenate(outs, axis=0)


ML_BPS = 2
ML_GROUP_COLS = (NB_BATCH // ML_BPS) * (SEQ + CTX)


def _ml_body(*refs):
    n_in = 8 * ML_BPS
    seq_refs, (bias_ref, bias_t_ref, tril_ref, triu_ref) = refs[:n_in], refs[n_in:n_in + 4]
    (hf_ref, hb_ref), (c_scr, m_scr) = refs[n_in + 4:n_in + 6], refs[n_in + 6:]

    @pl.when(pl.program_id(1) == 0)
    def _():
        c_scr[...] = jnp.zeros_like(c_scr)
        m_scr[...] = jnp.zeros_like(m_scr)

    tril, triu = tril_ref[...], triu_ref[...]
    src = lax.broadcasted_iota(jnp.int32, (CHUNK, CHUNK), 0)
    dst = lax.broadcasted_iota(jnp.int32, (CHUNK, CHUNK), 1)
    ones_rows = (lax.broadcasted_iota(jnp.int32, (HD, CHUNK), 0) == 0).astype(BF16)
    for sub in range(ML_BPS):
        c_sub, m_sub = c_scr.at[sub], m_scr.at[sub]
        for direction in range(2):
            qv_ref, k_ref, g_ref, gt_ref = seq_refs[8 * sub + 4 * direction:8 * sub + 4 * direction + 4]
            gates = g_ref[...] + bias_ref[...]
            gates_t = gt_ref[...] + bias_t_ref[...]
            ls, ls_t = _log_sigmoid(gates), _log_sigmoid(gates_t)
            if direction == 0:
                cum_c, cum_r, mask = _dot_sel_l(tril, ls), _dot_sel_r(ls_t, triu), src <= dst
            else:
                cum_c, cum_r, mask = _dot_sel_l(triu, ls), _dot_sel_r(ls_t, tril), src >= dst
            src_col = pltpu.roll(gates, 4, 1) - cum_c
            o_ref = hf_ref if direction == 0 else hb_ref
            o_ref[sub] = _ml_chain(direction, qv_ref, k_ref, src_col, cum_r, gates_t, c_sub, m_sub, mask, ones_rows)


def _mlstm(mqv_t, mk, mg, mg_t, gate_b):
    nctx = CTX // CHUNK
    nlat = SEQ // CHUNK
    per_group = NB_BATCH // ML_BPS

    def step_chunk(i, backward):
        ctx_chunk = (nctx - 1 - i) if backward else i
        lat_chunk = (N_ML_STEPS - 1 - i) if backward else (i - nctx)
        return i < nctx, ctx_chunk, lat_chunk

    def in_chunk(sub, backward):
        def f(p, i):
            b = p + per_group * sub
            is_ctx, cc, lc = step_chunk(i, backward)
            return jnp.where(is_ctx, T_LAT // CHUNK + nctx * b + cc, nlat * b + lc)
        return f

    def out_chunk(backward):
        def f(p, i):
            is_ctx, cc, lc = step_chunk(i, backward)
            return jnp.where(is_ctx, per_group * nlat + nctx * p + cc, nlat * p + lc)
        return f

    bias = jnp.pad(gate_b, (0, 128 - N_GATES)).reshape(1, 128)
    bias_t = gate_b.reshape(N_GATES, 1)
    tril = jnp.asarray(np.tril(np.ones((CHUNK, CHUNK), np.float32))).astype(BF16)
    triu = jnp.asarray(np.triu(np.ones((CHUNK, CHUNK), np.float32))).astype(BF16)
    const = lambda shape: pl.BlockSpec(shape, lambda p, i: (0, 0))
    ins, args = [], []
    for sub in range(ML_BPS):
        for backward in (False, True):
            f = in_chunk(sub, backward)
            ins += [pl.BlockSpec((2 * ML_W, CHUNK), lambda p, i, f=f: (0, f(p, i))),
                    pl.BlockSpec((CHUNK, ML_W), lambda p, i, f=f: (f(p, i), 0)),
                    pl.BlockSpec((CHUNK, 128), lambda p, i, f=f: (f(p, i), 0)),
                    pl.BlockSpec((N_GATES, CHUNK), lambda p, i, f=f: (0, f(p, i)))]
            args += [mqv_t, mk, mg, mg_t]
    outs = [pl.BlockSpec((ML_BPS, ML_W, CHUNK), lambda p, i, f=out_chunk(bw): (0, 0, f(p, i))) for bw in (False, True)]
    return pl.pallas_call(
        _ml_body,
        grid=(per_group, N_ML_STEPS),
        in_specs=ins + [const((1, 128)), const((N_GATES, 1)), const((CHUNK, CHUNK)), const((CHUNK, CHUNK))],
        out_specs=outs,
        out_shape=[jax.ShapeDtypeStruct((ML_BPS, ML_W, ML_GROUP_COLS), F32)] * 2,
        scratch_shapes=[pltpu.VMEM((ML_BPS, 2 * ML_H, 2 * HD, HD), F32), pltpu.VMEM((ML_BPS, 2 * ML_H, 1, 1), F32)],
        compiler_params=_cparams(("parallel", "arbitrary")),
        name="mlstm",
    )(*args, bias, bias_t, tril, triu)


FF_CHUNKS = ((0, 1536), (1536, D_FF))


def _out_ffn_body(al_ref, ac_ref, hl_ref, hc_ref, hf_ref, hb_ref, mo_ref, mnw_ref, bd_ref, wo_ref, x_ref,
                  g1_ref, sh2_ref, sc2_ref, g2_ref, nw2_ref, wg_ref, wu_ref, wd_ref, o_ref):
    i = pl.program_id(0)
    hn = _head_rms((hf_ref[...] + hb_ref[...]).T, bd_ref[...], mnw_ref[...])
    mlo = hn * _sigmoid(mo_ref[...])
    att = jnp.where(i < NT_LAT, al_ref[...], ac_ref[...])
    hy = jnp.where(i < NT_LAT, hl_ref[...], hc_ref[...])
    mixed = jnp.concatenate([att.astype(BF16), hy.astype(BF16), mlo.astype(BF16)], axis=1)
    x = x_ref[...] + g1_ref[...] * jnp.dot(mixed, wo_ref[...], preferred_element_type=F32)
    hb = _modulated_norm(x, nw2_ref[...], sc2_ref[...], sh2_ref[...]).astype(BF16)
    acc = None
    for lo, hi in FF_CHUNKS:
        sl = slice(lo, hi)
        a = jnp.dot(hb, wg_ref[:, sl], preferred_element_type=F32)
        u = jnp.dot(hb, wu_ref[:, sl], preferred_element_type=F32)
        part = _dot(a * _sigmoid(a) * u, wd_ref[sl, :])
        acc = part if acc is None else acc + part
    o_ref[...] = x + g2_ref[...] * acc


def _out_ffn(att, attc, hyo, hyoc, hf_t, hb_t, mo, ml_nw, w_out_b, xs, mod4, layer, nw2, wg, wu, wd, with_ctx):
    assert ATT_W == HY
    tiles = NT_ALL if with_ctx else NT_LAT
    row = lambda w: pl.BlockSpec((TM, w), lambda i: (i, 0))
    lat_rows = pl.BlockSpec((TM, HY), lambda i: (jnp.minimum(i, NT_LAT - 1), 0))
    ctx_rows = pl.BlockSpec((TM, HY), lambda i: (jnp.maximum(i - NT_LAT, 0), 0))
    lat_tiles = (NB_BATCH // ML_BPS) * TILES_PER_SEQ
    col = pl.BlockSpec((None, ML_W, TM), lambda i: (jnp.where(i < NT_LAT, i // lat_tiles, i - NT_LAT), 0,
                                                    jnp.where(i < NT_LAT, i % lat_tiles, lat_tiles)))
    resident = lambda shape: pl.BlockSpec(shape, lambda i: (0, 0), pipeline_mode=pl.Buffered(1))
    layer_w = lambda shape: pl.BlockSpec((None,) + shape, lambda i: (layer, 0, 0), pipeline_mode=pl.Buffered(1))
    vec = lambda w: pl.BlockSpec((1, w), lambda i: (0, 0))
    bd = jnp.asarray(_blockdiag_ones(ML_W)).astype(BF16)
    return pl.pallas_call(
        _out_ffn_body,
        grid=(tiles,),
        in_specs=[lat_rows, ctx_rows, lat_rows, ctx_rows,
                  col, col, row(ML_W), vec(ML_W), resident((ML_W, ML_W)), layer_w((D, D)), row(D),
                  _mod_spec(layer, 2), _mod_spec(layer, 3), _mod_spec(layer, 4), _mod_spec(layer, 5), vec(D),
                  layer_w((D, D_FF)), layer_w((D, D_FF)), layer_w((D_FF, D))],
        out_specs=row(D),
        out_shape=jax.ShapeDtypeStruct((tiles * TM, D), F32),
        compiler_params=_cparams(("parallel",), 56),
        name="out_ffn",
    )(att, attc, hyo, hyoc, hf_t, hb_t, mo, ml_nw.reshape(1, ML_W), bd, w_out_b, xs, mod4, mod4, mod4, mod4,
      nw2.reshape(1, D), wg, wu, wd)


def _rope_tables():
    n_rows = SEQ // GRID_W
    row = jnp.repeat(jnp.arange(n_rows), GRID_W)
    col = jnp.tile(jnp.arange(GRID_W), n_rows)
    nf = HD // 4
    inv_freq = ROPE_BASE ** (-jnp.arange(nf, dtype=F32) / nf)
    ang = jnp.stack([row[:, None] * inv_freq, col[:, None] * inv_freq], axis=1)
    cos, sin = jnp.cos(ang), jnp.sin(ang)
    cos_h = jnp.concatenate([cos, cos], axis=-1).reshape(SEQ, HD)
    sin_h = jnp.concatenate([-sin, sin], axis=-1).reshape(SEQ, HD)
    cos_t = jnp.concatenate([jnp.tile(cos_h, (1, 2)), jnp.ones((TM, 128), F32)], axis=0)
    sin_t = jnp.concatenate([jnp.tile(sin_h, (1, 2)), jnp.zeros((TM, 128), F32)], axis=0)
    return cos_t, sin_t


def kernel(x, c, ctx, c_ctx, w_mod, b_mod, norm1_w, norm2_w, w_in, w_out, q_norm_w, k_norm_w, attn_sink, hy_conv_w, hy_conv_b, hy_w1, hy_b1, hy_freq, hy_w2, hy_b2, hy_w3, hy_decay, hy_skip, ml_gate_b, ml_norm_w, ffn_w_gate, ffn_w_up, ffn_w_down):
    xs = jnp.concatenate([x.reshape(T_LAT, D), ctx.reshape(T_CTX, D)], axis=0)
    cc8 = jnp.concatenate([c, c_ctx[None, :], jnp.zeros((8 - NB_BATCH - 1, D), F32)], axis=0)
    mod4 = _modulation(cc8, w_mod, b_mod).reshape(DEPTH, 8, 1, 6 * D)

    cos_t, sin_t = _rope_tables()
    feats_l, feats_c = _pos_feats_circ(SEQ), _pos_feats_circ(CTX)
    s1_tab = jnp.asarray(_TAB["s1"]).astype(BF16)
    s1i_tab = jnp.asarray(_TAB["s1i"]).astype(BF16)
    wf, wi = jnp.asarray(_TAB["wf"]).astype(BF16), jnp.asarray(_TAB["wi"]).astype(BF16)
    fc, fs = jnp.asarray(_TAB["fc"]).astype(BF16), jnp.asarray(_TAB["fs"]).astype(BF16)

    w_in_p = _reorder_w_in(w_in)
    w_out_b = w_out.astype(BF16)
    wg_b, wu_b, wd_b = ffn_w_gate.astype(BF16), ffn_w_up.astype(BF16), ffn_w_down.astype(BF16)
    filt_params = _filter_params(hy_w1, hy_b1, hy_freq, hy_w2, hy_b2, hy_w3)
    sink8 = jnp.pad(attn_sink, ((0, 0), (0, 128 - ATT_H)))[:, None, :] * jnp.ones((1, 8, 1), F32)
    zeros_tile = jnp.zeros((TM, HY), F32)

    out = None
    for l in range(DEPTH):
        last = l == DEPTH - 1
        qt, kn, vt, hv, hx1, hx2, mqv_t, mk, mo, mg, mg_t = _in_proj(
            xs, mod4, l, norm1_w[l], w_in_p, cos_t, sin_t, q_norm_w[l], k_norm_w[l], hy_conv_w[l], hy_conv_b[l])

        att = _attention(qt, kn, vt, sink8[l])
        attc = zeros_tile if last else _attention_ctx(qt, kn, vt, sink8[l])

        dec = hy_decay[l].reshape(1, N_FILT)
        circ = _circular_filters(feats_l, filt_params, l, dec)
        f_re, f_im = _dft_stage1(circ.reshape(2, HALF_A, FFT_B, FFT_T, 2 * HY), s1_tab)
        h_re, h_im = _filter_spectrum(f_re.reshape(2, 2, N_FFT, HY), f_im.reshape(2, 2, N_FFT, HY), wf)

        rows = lambda t: t.reshape(NB_BATCH // 2, N_FFT, HY)
        tiles = lambda t: t.reshape(NB_BATCH // 2, FFT_A, FFT_B, FFT_T, HY)
        v4 = _row_blocks(hv)
        a_re, a_im = _dft_stage1_pair(v4, s1_tab)
        c_re, c_im = _spectral_filter(rows(a_re), rows(a_im), h_re, h_im, 0, wf, wi)
        z4, a_re, a_im = _idft_stage1_gate(tiles(c_re), tiles(c_im), s1i_tab, _row_blocks(hx1), v4, hy_skip[l, 0], s1_tab)
        c_re, c_im = _spectral_filter(rows(a_re), rows(a_im), h_re, h_im, 1, wf, wi)
        hyo = _idft_stage1_gate(tiles(c_re), tiles(c_im), s1i_tab, _row_blocks(hx2), z4, hy_skip[l, 1]).reshape(T_LAT, HY)

        if last:
            hyoc = zeros_tile
        else:
            circ_c = _circular_filters(feats_c, filt_params, l, dec)
            hyoc = _hyena_ctx(hv, hx1, hx2, circ_c, fc, fs, hy_skip[l])

        hf_t, hb_t = _mlstm(mqv_t, mk, mg, mg_t, ml_gate_b[l])

        xs = _out_ffn(att, attc, hyo, hyoc, hf_t, hb_t, mo, ml_norm_w[l], w_out_b, xs, mod4, l,
                      norm2_w[l], wg_b, wu_b, wd_b, not last)
        out = xs
    return out.reshape(NB_BATCH, SEQ, D)
```

```python
import functools
import math

import numpy as np
import jax
import jax.numpy as jnp
from jax import lax
from jax.experimental import pallas as pl
from jax.experimental.pallas import tpu as pltpu

F32 = jnp.float32
BF16 = jnp.bfloat16

D = 1024
NB_BATCH = 4
SEQ = 4096
DEPTH = 4
GRID_W = 64
CTX = 256
T_LAT = NB_BATCH * SEQ
T_CTX = NB_BATCH * CTX
R_ALL = T_LAT + T_CTX

HD = 64
ATT_H = 6
ATT_W = ATT_H * HD
KV_W = 2 * HD
BLK = 128
NBLK = SEQ // BLK
ROPE_BASE = 10000.0

HY = 384
N_FILT = 4 * HY
POS_BANDS = 16
FILT_HID = 64

ML_H = 4
ML_W = 256
CHUNK = 128

D_FF = 2816
P_IN = 2832
P_PAD = 2944
NEG = -1e30
EPS = 1e-6
LOG2E = 1.4426950408889634

TM = 512
NT_LAT = T_LAT // TM
NT_ALL = R_ALL // TM
TILES_PER_SEQ = SEQ // TM

N_FFT = 2 * SEQ
FFT_A = 32
FFT_R = 256
FFT_T = 16
FFT_B = FFT_R // FFT_T
HALF_A = FFT_A // 2
S1_TILES = 4
HB_ROWS = 1024
N_CTXF = 2 * CTX


def _np_tables():
    ka = np.arange(FFT_A)[None, :, None, None]
    cc = np.arange(FFT_T)[None, None, :, None]
    aa = np.arange(HALF_A)[None, None, None, :]
    bb = np.arange(FFT_B)[:, None, None, None]
    ph = (ka * (FFT_R * aa + FFT_T * bb + cc)) % N_FFT
    th = 2.0 * np.pi * ph / N_FFT
    eye = np.eye(FFT_T)
    cos = np.einsum("bkca,cd->bkcad", np.cos(th), eye)
    sin = np.einsum("bkca,cd->bkcad", np.sin(th), eye)
    rows, cols = FFT_A * FFT_T, HALF_A * FFT_T
    s1 = np.concatenate([cos.reshape(FFT_B, rows, cols), -sin.reshape(FFT_B, rows, cols)], axis=1)
    s1i = np.concatenate([cos.reshape(FFT_B, rows, cols).transpose(0, 2, 1),
                          -sin.reshape(FFT_B, rows, cols).transpose(0, 2, 1)], axis=2) / N_FFT
    r = np.arange(FFT_R)
    th2 = 2.0 * np.pi * ((r[:, None] * r[None, :]) % FFT_R) / FFT_R
    c2, s2 = np.cos(th2), np.sin(th2)
    wf = np.block([[c2, s2], [-s2, c2]])
    wi = np.block([[c2, -s2], [s2, c2]])
    c = np.arange(N_CTXF)
    th3 = 2.0 * np.pi * ((c[:, None] * c[None, :]) % N_CTXF) / N_CTXF
    return dict(s1=s1.astype(np.float32), s1i=s1i.astype(np.float32), wf=wf.astype(np.float32), wi=wi.astype(np.float32),
                fc=np.cos(th3).astype(np.float32), fs=np.sin(th3).astype(np.float32))


_TAB = _np_tables()


def _blockdiag_ones(width):
    return np.kron(np.eye(width // HD), np.ones((HD, HD))).astype(np.float32)


def _cparams(sem, vmem_mb=48):
    return pltpu.CompilerParams(dimension_semantics=sem, vmem_limit_bytes=vmem_mb * 1024 * 1024)


def _dot(a, b):
    return jnp.dot(a.astype(BF16), b.astype(BF16), preferred_element_type=F32)


def _dot_nt(a, b):
    return lax.dot_general(a.astype(BF16), b.astype(BF16), (((1,), (1,)), ((), ())), preferred_element_type=F32)


def _dot_tn(a, b):
    return lax.dot_general(a.astype(BF16), b.astype(BF16), (((0,), (0,)), ((), ())), preferred_element_type=F32)


def _split3(x):
    x1 = x.astype(BF16)
    r1 = x - x1.astype(F32)
    x2 = r1.astype(BF16)
    x3 = (r1 - x2.astype(F32)).astype(BF16)
    return x1, x2, x3


def _dot_sel_l(sel, x):
    return sum(jnp.dot(sel, p, preferred_element_type=F32) for p in _split3(x))


def _dot_sel_r(x, sel):
    return sum(jnp.dot(p, sel, preferred_element_type=F32) for p in _split3(x))


def _sigmoid(x):
    return 1.0 / (1.0 + jnp.exp(-x))


def _log_sigmoid(x):
    return jnp.minimum(x, 0.0) - jnp.log(1.0 + jnp.exp(-jnp.abs(x)))


def _head_rms(t, bd, w):
    sq = t * t
    hi = sq.astype(BF16)
    lo = (sq - hi.astype(F32)).astype(BF16)
    ss = jnp.dot(hi, bd, preferred_element_type=F32) + jnp.dot(lo, bd, preferred_element_type=F32)
    return t * lax.rsqrt(ss * (1.0 / HD) + EPS) * w


def _mod_row(i):
    return jnp.where(i < NT_LAT, i // TILES_PER_SEQ, NB_BATCH)


def _mod_spec(layer, k):
    return pl.BlockSpec((None, None, 1, D), lambda i: (layer, _mod_row(i), 0, k))


def _mod_body(s_ref, w_ref, b_ref, o_ref):
    s = s_ref[...]
    s = s * _sigmoid(s)
    o_ref[...] = _dot(s, w_ref[...]) + b_ref[...]


def _modulation(cc8, w_mod, b_mod):
    nc = 1536
    return pl.pallas_call(
        _mod_body,
        grid=(DEPTH, 6 * D // nc),
        in_specs=[pl.BlockSpec((8, D), lambda l, j: (0, 0)),
                  pl.BlockSpec((None, D, nc), lambda l, j: (l, 0, j)),
                  pl.BlockSpec((None, 1, nc), lambda l, j: (l, 0, j))],
        out_specs=pl.BlockSpec((None, 8, nc), lambda l, j: (l, 0, j)),
        out_shape=jax.ShapeDtypeStruct((DEPTH, 8, 6 * D), F32),
        compiler_params=_cparams(("parallel", "parallel")),
        name="modulation",
    )(cc8, w_mod, b_mod.reshape(DEPTH, 1, 6 * D))


_COL_ATT, _COL_HY, _COL_MQV, _COL_MK, _COL_MO, _COL_MG = (0, 640), (640, 1792), (1792, 2304), (2304, 2560), (2560, 2816), (2816, 2944)
N_GATES = 4 * ML_H


def _reorder_w_in(w_in):
    mq_end, mk_end, mv_end = 2048, 2304, 2560
    w = jnp.concatenate([w_in[:, :, :mq_end], w_in[:, :, mk_end:mv_end], w_in[:, :, mq_end:mk_end], w_in[:, :, mv_end:]], axis=2)
    return jnp.pad(w, ((0, 0), (0, 0), (0, P_PAD - P_IN))).astype(BF16)


def _modulated_norm(x, nw, sc, sh):
    ms = jnp.mean(x * x, axis=-1, keepdims=True)
    return (x * lax.rsqrt(ms + EPS)) * (nw * (1.0 + sc)) + sh


HALO = 8


def _in_body(x_ref, xp_ref, xn_ref, sh_ref, sc_ref, nw_ref, w_ref, cos_ref, sin_ref, qw_ref, kw_ref, bdq_ref, bdk_ref,
             cw_ref, cb_ref, qt_ref, k_ref, vt_ref, hv_ref, hx1_ref, hx2_ref, mqvt_ref, mk_ref, mo_ref, mg_ref, mgt_ref):
    i = pl.program_id(0)
    nw, sc, sh = nw_ref[...], sc_ref[...], sh_ref[...]
    hb = _modulated_norm(x_ref[...], nw, sc, sh).astype(BF16)
    proj = lambda cols: jnp.dot(hb, w_ref[:, cols[0]:cols[1]], preferred_element_type=F32)

    halo = _modulated_norm(jnp.concatenate([xp_ref[...], xn_ref[...]], axis=0), nw, sc, sh).astype(BF16)
    both = jnp.dot(jnp.concatenate([hb, halo], axis=0), w_ref[:, _COL_ATT[0]:_COL_HY[1]], preferred_element_type=F32)

    att = both[:TM, _COL_ATT[0]:_COL_ATT[1]]
    cos, sin = cos_ref[...], sin_ref[...]
    lane = lax.broadcasted_iota(jnp.int32, (TM, 128), 1)
    first_half = (lane & 31) < 16
    q = _head_rms(att[:, 0:ATT_W], bdq_ref[...], qw_ref[...])
    k = _head_rms(att[:, ATT_W:ATT_W + KV_W], bdk_ref[...], kw_ref[...])
    qt_ref[...] = (_rope(q, cos, sin, first_half) * (HD ** -0.5 * LOG2E)).T.astype(BF16)
    k_ref[...] = _rope(k, cos, sin, first_half).astype(BF16)
    vt_ref[...] = att[:, ATT_W + KV_W:ATT_W + 2 * KV_W].T.astype(BF16)

    hy, hy_halo = both[:TM, _COL_HY[0]:_COL_HY[1]], both[TM:, _COL_HY[0]:_COL_HY[1]]
    seq_len = jnp.where(i < NT_LAT, SEQ, CTX)
    pos = (i * TM + lax.broadcasted_iota(jnp.int32, (TM, 1), 0)) & (seq_len - 1)
    usc = _short_conv_rows(hy, hy_halo[HALO - 1:HALO, :], hy_halo[HALO:HALO + 1, :], cw_ref[...], cb_ref[...],
                           pos == 0, pos == seq_len - 1)
    hv_ref[...] = usc[:, 0:HY]
    hx1_ref[...] = usc[:, HY:2 * HY]
    hx2_ref[...] = usc[:, 2 * HY:3 * HY]

    mqvt_ref[...] = proj(_COL_MQV).T.astype(BF16)
    mk_ref[...] = (proj(_COL_MK) * (HD ** -0.5)).astype(BF16)
    mo_ref[...] = proj(_COL_MO)
    mg = proj(_COL_MG)
    mg_ref[...] = mg
    mgt_ref[...] = mg.T[0:N_GATES, :]


def _in_proj(xs, mod4, layer, nw, w_in_p, cos_t, sin_t, qw, kw, conv_w, conv_b):
    rows = lambda w: pl.BlockSpec((TM, w), lambda i: (i, 0))
    cols = lambda h: pl.BlockSpec((h, TM), lambda i: (0, i))
    const = lambda shape: pl.BlockSpec(shape, lambda i: (0, 0))
    tab = lambda i: (jnp.where(i < NT_LAT, i % TILES_PER_SEQ, TILES_PER_SEQ), 0)
    per_tile = TM // HALO
    prev = pl.BlockSpec((HALO, D), lambda i: (jnp.maximum(i * per_tile - 1, 0), 0))
    nxt = pl.BlockSpec((HALO, D), lambda i: (jnp.minimum((i + 1) * per_tile, R_ALL // HALO - 1), 0))
    bdq = jnp.asarray(_blockdiag_ones(ATT_W)).astype(BF16)
    bdk = jnp.asarray(_blockdiag_ones(KV_W)).astype(BF16)
    bf = lambda shape: jax.ShapeDtypeStruct(shape, BF16)
    f32 = lambda shape: jax.ShapeDtypeStruct(shape, F32)
    return pl.pallas_call(
        _in_body,
        grid=(NT_ALL,),
        in_specs=[rows(D), prev, nxt, _mod_spec(layer, 0), _mod_spec(layer, 1), const((1, D)),
                  pl.BlockSpec((None, D, P_PAD), lambda i: (layer, 0, 0), pipeline_mode=pl.Buffered(1)),
                  pl.BlockSpec((TM, 128), tab), pl.BlockSpec((TM, 128), tab),
                  const((1, ATT_W)), const((1, KV_W)), const((ATT_W, ATT_W)), const((KV_W, KV_W)),
                  const((3, 3 * HY)), const((1, 3 * HY))],
        out_specs=[cols(ATT_W), rows(KV_W), cols(KV_W), rows(HY), rows(HY), rows(HY),
                   cols(2 * ML_W), rows(ML_W), rows(ML_W), rows(128), cols(N_GATES)],
        out_shape=[bf((ATT_W, R_ALL)), bf((R_ALL, KV_W)), bf((KV_W, R_ALL)), f32((R_ALL, HY)), f32((R_ALL, HY)), f32((R_ALL, HY)),
                   bf((2 * ML_W, R_ALL)), bf((R_ALL, ML_W)), f32((R_ALL, ML_W)), f32((R_ALL, 128)), f32((N_GATES, R_ALL))],
        compiler_params=_cparams(("parallel",), 56),
        name="in_proj",
    )(xs, xs, xs, mod4, mod4, nw.reshape(1, D), w_in_p, cos_t, sin_t,
      jnp.tile(qw, ATT_H).reshape(1, ATT_W), jnp.tile(kw, 2).reshape(1, KV_W), bdq, bdk, conv_w, conv_b.reshape(1, 3 * HY))


def _rope(t, cos, sin_signed, first_half):
    outs = []
    for c in range(t.shape[1] // 128):
        tc = t[:, c * 128:(c + 1) * 128]
        partner = jnp.where(first_half, pltpu.roll(tc, 112, 1), pltpu.roll(tc, 16, 1))
        outs.append(tc * cos + partner * sin_signed)
    return outs[0] if len(outs) == 1 else jnp.concatenate(outs, axis=1)


def _att_heads(qt, kv_list, sink_ref, o_ref):
    nq = qt.shape[1]
    rep = ATT_H // 2
    zeros = jnp.zeros((HD, rep * nq), BF16)
    outs = []
    for g in range(2):
        heads = range(rep * g, rep * (g + 1))
        qg = jnp.concatenate([qt[HD * h:HD * (h + 1), :] for h in heads], axis=1)
        qe = jnp.concatenate([qg, zeros] if g == 0 else [zeros, qg], axis=0)
        sink = jnp.concatenate([jnp.broadcast_to(sink_ref[0:1, h:h + 1], (1, nq)) for h in heads], axis=1) * LOG2E
        scores = []
        m = sink
        for k, _, mask in kv_list:
            s = jnp.dot(k, qe, preferred_element_type=F32)
            if mask is not None:
                s = jnp.where(mask, s, NEG)
            scores.append(s)
            m = jnp.maximum(m, jnp.max(s, axis=0, keepdims=True))
        acc = None
        for s, (_, vt, _) in zip(scores, kv_list):
            p = jnp.exp2(s - m).astype(BF16)
            ones = (lax.broadcasted_iota(jnp.int32, (16, vt.shape[1]), 0) == 0).astype(BF16)
            pv = jnp.dot(jnp.concatenate([vt[HD * g:HD * (g + 1), :], ones], axis=0), p, preferred_element_type=F32)
            acc = pv if acc is None else acc + pv
        og = acc[:HD, :] / (acc[HD:HD + 1, :] + jnp.exp2(sink - m))
        outs += [og[:, nq * i:nq * (i + 1)] for i in range(rep)]
    o_ref[...] = jnp.concatenate(outs, axis=0).T


QB = 8


def _att_body(*refs):
    nband = QB + 2
    qt_ref, ks, vs = refs[0], refs[1:1 + nband], refs[1 + nband:1 + 2 * nband]
    kc_ref, vc_ref, sink_ref, o_ref = refs[1 + 2 * nband:]
    j = pl.program_id(1)
    kc, vc = kc_ref[...], vc_ref[...]
    width = (ATT_H // 2) * BLK
    key = lax.broadcasted_iota(jnp.int32, (BLK, width), 0)
    qry = lax.broadcasted_iota(jnp.int32, (BLK, width), 1) & (BLK - 1)
    for sub in range(QB):
        blk = QB * j + sub
        mask_prev = jnp.logical_and(key >= qry, blk > 0)
        mask_next = jnp.logical_and(key <= qry, blk < NBLK - 1)
        _att_heads(qt_ref[:, sub * BLK:(sub + 1) * BLK],
                   [(ks[sub][...], vs[sub][...], mask_prev), (ks[sub + 1][...], vs[sub + 1][...], None),
                    (ks[sub + 2][...], vs[sub + 2][...], mask_next), (kc, vc, None)],
                   sink_ref, o_ref.at[sub * BLK:(sub + 1) * BLK, :])


def _attention(qt, kn, vt, sink8):
    qrows = QB * BLK
    steps = NBLK // QB

    def band(off):
        return lambda b, j: b * NBLK + jnp.clip(QB * j + off, 0, NBLK - 1)

    kspec = lambda f: pl.BlockSpec((BLK, KV_W), lambda b, j: (f(b, j), 0))
    vspec = lambda f: pl.BlockSpec((KV_W, BLK), lambda b, j: (0, f(b, j)))
    offs = range(-1, QB + 1)
    return pl.pallas_call(
        _att_body,
        grid=(NB_BATCH, steps),
        in_specs=[pl.BlockSpec((ATT_W, qrows), lambda b, j: (0, b * steps + j))]
                 + [kspec(band(o)) for o in offs] + [vspec(band(o)) for o in offs]
                 + [pl.BlockSpec((CTX, KV_W), lambda b, j: (T_LAT // CTX + b, 0)),
                    pl.BlockSpec((KV_W, CTX), lambda b, j: (0, T_LAT // CTX + b)),
                    pl.BlockSpec((8, 128), lambda b, j: (0, 0))],
        out_specs=pl.BlockSpec((qrows, ATT_W), lambda b, j: (b * steps + j, 0)),
        out_shape=jax.ShapeDtypeStruct((T_LAT, ATT_W), F32),
        compiler_params=_cparams(("parallel", "parallel")),
        name="attention",
    )(qt, *([kn] * (QB + 2)), *([vt] * (QB + 2)), kn, vt, sink8)


def _att_ctx_body(qt_ref, kc_ref, vc_ref, sink_ref, o_ref):
    kv = [(kc_ref[...], vc_ref[...], None)]
    for sub in range(CTX // BLK):
        _att_heads(qt_ref[:, sub * BLK:(sub + 1) * BLK], kv, sink_ref, o_ref.at[sub * BLK:(sub + 1) * BLK, :])


def _attention_ctx(qt, kn, vt, sink8):
    return pl.pallas_call(
        _att_ctx_body,
        grid=(NB_BATCH,),
        in_specs=[pl.BlockSpec((ATT_W, CTX), lambda b: (0, T_LAT // CTX + b)),
                  pl.BlockSpec((CTX, KV_W), lambda b: (T_LAT // CTX + b, 0)),
                  pl.BlockSpec((KV_W, CTX), lambda b: (0, T_LAT // CTX + b)),
                  pl.BlockSpec((8, 128), lambda b: (0, 0))],
        out_specs=pl.BlockSpec((CTX, ATT_W), lambda b: (b, 0)),
        out_shape=jax.ShapeDtypeStruct((T_CTX, ATT_W), F32),
        compiler_params=_cparams(("parallel",)),
        name="attention_ctx",
    )(qt, kn, vt, sink8)


def _short_conv_rows(u, prev_row, next_row, w, b, first, last):
    n = u.shape[0]
    row = lax.broadcasted_iota(jnp.int32, u.shape, 0)
    up = jnp.where(row == 0, prev_row, pltpu.roll(u, 1, 0))
    un = jnp.where(row == n - 1, next_row, pltpu.roll(u, n - 1, 0))
    up = jnp.where(first, 0.0, up)
    un = jnp.where(last, 0.0, un)
    return up * w[0:1, :] + u * w[1:2, :] + un * w[2:3, :] + b


def _filt_body(n, tl, f_ref, w1_ref, b1_ref, fr_ref, w2_ref, b2_ref, w3a_ref, w3b_ref, dec_ref, o_ref):
    h = tl // 2
    f = jnp.concatenate([f_ref[0:h, :], f_ref[h:tl, :]], axis=1)
    fr = fr_ref[...]
    z = jnp.sin(fr * (_dot(f, w1_ref[...]) + b1_ref[...]))
    z = jnp.sin(fr * (_dot(z, w2_ref[...]) + b2_ref[...]))
    dec = jnp.abs(dec_ref[...])
    top = _dot(z, w3a_ref[...]) * jnp.exp(-f[:, 0:1] * dec)
    bot = _dot(z, w3b_ref[...]) * jnp.exp(-f[:, 128:129] * dec)
    filt = jnp.concatenate([top, bot], axis=0)
    row = pl.program_id(0) * tl + lax.broadcasted_iota(jnp.int32, filt.shape, 0)
    o_ref[...] = jnp.where(row == n, 0.0, filt)


def _filter_params(hy_w1, hy_b1, hy_freq, hy_w2, hy_b2, hy_w3):
    hid = FILT_HID
    w1 = jnp.pad(hy_w1, ((0, 0), (0, 128 - hy_w1.shape[1]), (0, 0)))
    zero = jnp.zeros_like
    w1b = jnp.concatenate([jnp.concatenate([w1, zero(w1)], axis=2), jnp.concatenate([zero(w1), w1], axis=2)], axis=1)
    w2b = jnp.concatenate([jnp.concatenate([hy_w2, zero(hy_w2)], axis=2), jnp.concatenate([zero(hy_w2), hy_w2], axis=2)], axis=1)
    w3a = jnp.concatenate([hy_w3, zero(hy_w3)], axis=1)
    w3b = jnp.concatenate([zero(hy_w3), hy_w3], axis=1)
    twice = lambda v: jnp.concatenate([v, v], axis=1).reshape(DEPTH, 1, 2 * hid)
    return w1b, twice(hy_b1), twice(hy_freq), w2b, twice(hy_b2), w3a, w3b


def _circular_filters(feats_circ, params, layer, dec):
    n = feats_circ.shape[0] // 2
    tl = min(n, 512)
    w1b, b1, fr, w2b, b2, w3a, w3b = params
    per = lambda shape: pl.BlockSpec((None,) + shape, lambda i: (layer, 0, 0))
    half = pl.BlockSpec((None, 128, 2 * HY), lambda i: (layer, 0, i // (n // tl)))
    return pl.pallas_call(
        functools.partial(_filt_body, n, tl),
        grid=(2 * n // tl,),
        in_specs=[pl.BlockSpec((tl, 128), lambda i: (i, 0)), per((256, 128)), per((1, 128)), per((1, 128)),
                  per((128, 128)), per((1, 128)), half, half, pl.BlockSpec((1, 2 * HY), lambda i: (0, i // (n // tl)))],
        out_specs=pl.BlockSpec((tl, 2 * HY), lambda i: (i, 0)),
        out_shape=jax.ShapeDtypeStruct((2 * n, 2 * HY), F32),
        compiler_params=_cparams(("parallel",)),
        name="hyena_filters",
    )(feats_circ, w1b, b1, fr, w2b, b2, w3a, w3b, dec)


def _pos_feats_circ(n):
    t = jnp.linspace(0.0, 1.0, n, dtype=F32)[:, None]
    ang = (2.0 * math.pi / n) * jnp.arange(n, dtype=F32)[:, None]
    bands = jnp.linspace(1e-4, POS_BANDS - 1, POS_BANDS, dtype=F32)[None, :]
    feats = jnp.concatenate([t, jnp.cos(bands * ang), -jnp.sin(bands * ang)], axis=-1)
    feats = jnp.pad(feats, ((0, 0), (0, 128 - feats.shape[1])))
    return jnp.concatenate([feats, feats[:1], jnp.flip(feats[:n - 1], axis=0)], axis=0)


def _s1_body(x_ref, m_ref, are_ref, aim_ref):
    half = FFT_A * FFT_T
    ncb = x_ref.shape[-1] // HY
    for p in range(S1_TILES):
        xs = x_ref[:, p].reshape(HALF_A * FFT_T, ncb * HY)
        r = _dot(m_ref[p], xs)
        for cb in range(ncb):
            cols = slice(cb * HY, (cb + 1) * HY)
            are_ref[cb, :, p] = r[:half, cols].reshape(FFT_A, FFT_T, HY).astype(BF16)
            aim_ref[cb, :, p] = r[half:, cols].reshape(FFT_A, FFT_T, HY).astype(BF16)


def _dft_stage1(x5, s1_tab):
    nb, ncb = x5.shape[0], x5.shape[-1] // HY
    out = pl.BlockSpec((None, ncb, FFT_A, S1_TILES, FFT_T, HY), lambda b, j: (b, 0, 0, j, 0, 0))
    return pl.pallas_call(
        _s1_body,
        grid=(nb, FFT_B // S1_TILES),
        in_specs=[pl.BlockSpec((None, HALF_A, S1_TILES, FFT_T, ncb * HY), lambda b, j: (b, 0, j, 0, 0)),
                  pl.BlockSpec((S1_TILES, 2 * FFT_A * FFT_T, HALF_A * FFT_T), lambda b, j: (j, 0, 0))],
        out_specs=[out, out],
        out_shape=[jax.ShapeDtypeStruct((nb, ncb, FFT_A, FFT_B, FFT_T, HY), BF16)] * 2,
        compiler_params=_cparams(("parallel", "parallel")),
        name="dft_stage1",
    )(x5, s1_tab)


def _s1_pair_body(x_ref, m_ref, are_ref, aim_ref):
    half = FFT_A * FFT_T
    rows = HALF_A * FFT_T
    for p in range(S1_TILES):
        xab = jnp.concatenate([x_ref[0:HALF_A, p].reshape(rows, HY), x_ref[HALF_A:2 * HALF_A, p].reshape(rows, HY)], axis=1)
        r = _dot(m_ref[p], xab)
        are_ref[:, p] = (r[:half, :HY] - r[half:, HY:]).reshape(FFT_A, FFT_T, HY).astype(BF16)
        aim_ref[:, p] = (r[half:, :HY] + r[:half, HY:]).reshape(FFT_A, FFT_T, HY).astype(BF16)


def _row_blocks(x):
    return x.reshape(x.shape[0] // FFT_R, FFT_B, FFT_T, HY)


_PAIR_SPEC = pl.BlockSpec((2 * HALF_A, S1_TILES, FFT_T, HY), lambda b, j: (b, j, 0, 0))


def _dft_stage1_pair(x4, s1_tab):
    npair = NB_BATCH // 2
    out = pl.BlockSpec((None, FFT_A, S1_TILES, FFT_T, HY), lambda b, j: (b, 0, j, 0, 0))
    return pl.pallas_call(
        _s1_pair_body,
        grid=(npair, FFT_B // S1_TILES),
        in_specs=[_PAIR_SPEC, pl.BlockSpec((S1_TILES, 2 * FFT_A * FFT_T, HALF_A * FFT_T), lambda b, j: (j, 0, 0))],
        out_specs=[out, out],
        out_shape=[jax.ShapeDtypeStruct((npair, FFT_A, FFT_B, FFT_T, HY), BF16)] * 2,
        compiler_params=_cparams(("parallel", "parallel")),
        name="dft_stage1_pair",
    )(x4, s1_tab)


def _block_rows(r):
    return slice(r * FFT_R, (r + 1) * FFT_R)


def _stack_complex(re_ref, im_ref, sl):
    return jnp.concatenate([re_ref[sl, :], im_ref[sl, :]], axis=0)


def _spec_body(are_ref, aim_ref, wf_ref, hre_ref, him_ref):
    wf = wf_ref[...]
    for r in range(HB_ROWS // FFT_R):
        sl = _block_rows(r)
        both = jnp.concatenate([_stack_complex(are_ref.at[0], aim_ref.at[0], sl),
                                _stack_complex(are_ref.at[1], aim_ref.at[1], sl)], axis=1)
        x = jnp.dot(wf, both, preferred_element_type=F32)
        first, second = x[:, :HY], x[:, HY:]
        x = first + second if r % 2 == 0 else first - second
        hre_ref[sl, :] = x[:FFT_R]
        him_ref[sl, :] = x[FFT_R:]


def _filter_spectrum(a_re, a_im, wf):
    assert (HB_ROWS // FFT_R) % 2 == 0
    blk_in = pl.BlockSpec((2, None, HB_ROWS, HY), lambda r, o: (0, o, r, 0))
    blk_out = pl.BlockSpec((None, HB_ROWS, HY), lambda r, o: (o, r, 0))
    return pl.pallas_call(
        _spec_body,
        grid=(N_FFT // HB_ROWS, 2),
        in_specs=[blk_in, blk_in, pl.BlockSpec((2 * FFT_R, 2 * FFT_R), lambda r, o: (0, 0))],
        out_specs=[blk_out, blk_out],
        out_shape=[jax.ShapeDtypeStruct((2, N_FFT, HY), F32)] * 2,
        compiler_params=_cparams(("parallel", "parallel")),
        name="filter_spectrum",
    )(a_re, a_im, wf)


def _s2_body(are_ref, aim_ref, hre_ref, him_ref, wf_ref, wi_ref, cre_ref, cim_ref):
    wf, wi = wf_ref[...], wi_ref[...]
    for r in range(0, HB_ROWS // FFT_R, 2):
        sl0, sl1 = _block_rows(r), _block_rows(r + 1)
        a = jnp.concatenate([_stack_complex(are_ref, aim_ref, sl0), _stack_complex(are_ref, aim_ref, sl1)], axis=1)
        x = jnp.dot(wf, a, preferred_element_type=F32)
        xre, xim = x[:FFT_R], x[FFT_R:]
        hre = jnp.concatenate([hre_ref[sl0, :], hre_ref[sl1, :]], axis=1)
        him = jnp.concatenate([him_ref[sl0, :], him_ref[sl1, :]], axis=1)
        y = jnp.concatenate([(xre * hre - xim * him).astype(BF16), (xre * him + xim * hre).astype(BF16)], axis=0)
        c = jnp.dot(wi, y, preferred_element_type=F32)
        for i, sl in enumerate((sl0, sl1)):
            cre_ref[sl, :] = c[:FFT_R, i * HY:(i + 1) * HY].astype(BF16)
            cim_ref[sl, :] = c[FFT_R:, i * HY:(i + 1) * HY].astype(BF16)


def _spectral_filter(a_re, a_im, h_re, h_im, order, wf, wi):
    nb = a_re.shape[0]
    blk = pl.BlockSpec((None, HB_ROWS, HY), lambda r, b: (b, r, 0))
    hblk = pl.BlockSpec((None, HB_ROWS, HY), lambda r, b: (order, r, 0))
    mat = pl.BlockSpec((2 * FFT_R, 2 * FFT_R), lambda r, b: (0, 0))
    return pl.pallas_call(
        _s2_body,
        grid=(N_FFT // HB_ROWS, nb),
        in_specs=[blk, blk, hblk, hblk, mat, mat],
        out_specs=[blk, blk],
        out_shape=[jax.ShapeDtypeStruct((nb, N_FFT, HY), BF16)] * 2,
        compiler_params=_cparams(("parallel", "parallel")),
        name="spectral_filter",
    )(a_re, a_im, h_re, h_im, wf, wi)


def _s1i_body(chain, cre_ref, cim_ref, g_ref, a_ref, b_ref, sk_ref, *rest):
    if chain:
        m_ref, o_ref, are_ref, aim_ref = rest
    else:
        (o_ref,) = rest
    rows = FFT_A * FFT_T
    for p in range(S1_TILES):
        cre, cim = cre_ref[:, p].reshape(rows, HY), cim_ref[:, p].reshape(rows, HY)
        c = jnp.concatenate([jnp.concatenate([cre, cim], axis=1), jnp.concatenate([cim, -cre], axis=1)], axis=0)
        y = jnp.dot(g_ref[p], c, preferred_element_type=F32)
        gated = []
        for i in range(2):
            seq = slice(i * HALF_A, (i + 1) * HALF_A)
            yi = y[:, i * HY:(i + 1) * HY].reshape(HALF_A, FFT_T, HY)
            z = a_ref[seq, p] * (yi + b_ref[seq, p] * sk_ref[...])
            o_ref[seq, p] = z
            gated.append(z.reshape(HALF_A * FFT_T, HY))
        if chain:
            r = _dot(m_ref[p], jnp.concatenate(gated, axis=1))
            are_ref[:, p] = (r[:rows, :HY] - r[rows:, HY:]).reshape(FFT_A, FFT_T, HY).astype(BF16)
            aim_ref[:, p] = (r[rows:, :HY] + r[:rows, HY:]).reshape(FFT_A, FFT_T, HY).astype(BF16)


def _idft_stage1_gate(c_re, c_im, s1i_tab, a4, b4, skip, s1_tab=None):
    chain = s1_tab is not None
    npair = NB_BATCH // 2
    data = _PAIR_SPEC
    spec = pl.BlockSpec((None, FFT_A, S1_TILES, FFT_T, HY), lambda b, j: (b, 0, j, 0, 0))
    ins = [spec, spec, pl.BlockSpec((S1_TILES, HALF_A * FFT_T, 2 * FFT_A * FFT_T), lambda b, j: (j, 0, 0)),
           data, data, pl.BlockSpec((1, 1, HY), lambda b, j: (0, 0, 0))]
    args = [c_re, c_im, s1i_tab, a4, b4, skip.reshape(1, 1, HY)]
    outs, shapes = [data], [jax.ShapeDtypeStruct((T_LAT // FFT_R, FFT_B, FFT_T, HY), F32)]
    if chain:
        ins.append(pl.BlockSpec((S1_TILES, 2 * FFT_A * FFT_T, HALF_A * FFT_T), lambda b, j: (j, 0, 0)))
        args.append(s1_tab)
        outs += [spec, spec]
        shapes += [jax.ShapeDtypeStruct((npair, FFT_A, FFT_B, FFT_T, HY), BF16)] * 2
    res = pl.pallas_call(
        functools.partial(_s1i_body, chain),
        grid=(npair, FFT_B // S1_TILES),
        in_specs=ins,
        out_specs=outs,
        out_shape=shapes,
        compiler_params=_cparams(("parallel", "parallel"), 56),
        name="idft_stage1_gate",
    )(*args)
    return res if chain else res[0]


def _hyc_body(v_ref, x1_ref, x2_ref, circ_ref, fc_ref, fs_ref, sk_ref, o_ref):
    fc, fs = fc_ref[...], fs_ref[...]
    circ = circ_ref[...]
    h_re, h_im = _dot(fc, circ), -_dot(fs, circ)

    def long_conv(s, o):
        sl = slice(o * HY, (o + 1) * HY)
        s_re, s_im = _dot(fc[:, :CTX], s), -_dot(fs[:, :CTX], s)
        hre, him = h_re[:, sl], h_im[:, sl]
        y_re = s_re * hre - s_im * him
        y_im = s_re * him + s_im * hre
        y = (_dot(fc[:CTX, :], y_re) - _dot(fs[:CTX, :], y_im)) * (1.0 / N_CTXF)
        return y + s * sk_ref[o:o + 1, :]

    o_ref[...] = x2_ref[...] * long_conv(x1_ref[...] * long_conv(v_ref[...], 0), 1)


def _hyena_ctx(hv, hx1, hx2, circ_c, fc, fs, skip):
    const = lambda shape: pl.BlockSpec(shape, lambda b: (0, 0))
    seq = pl.BlockSpec((CTX, HY), lambda b: (T_LAT // CTX + b, 0))
    return pl.pallas_call(
        _hyc_body,
        grid=(NB_BATCH,),
        in_specs=[seq, seq, seq, const((N_CTXF, 2 * HY)), const((N_CTXF, N_CTXF)), const((N_CTXF, N_CTXF)), const((2, HY))],
        out_specs=pl.BlockSpec((CTX, HY), lambda b: (b, 0)),
        out_shape=jax.ShapeDtypeStruct((T_CTX, HY), F32),
        compiler_params=_cparams(("parallel",)),
        name="hyena_ctx",
    )(hv, hx1, hx2, circ_c, fc, fs, skip)


N_ML_STEPS = CTX // CHUNK + SEQ // CHUNK


def _ml_chain(direction, qv_ref, k_ref, src_col, cum_r, gates_t, c_scr, m_scr, mask, ones_rows):
    base = 8 * direction
    outs = []
    for h in range(ML_H):
        ic, fc = base + h, base + 4 + h
        chain = direction * ML_H + h
        b_row, li_row = cum_r[fc:fc + 1, :], gates_t[ic:ic + 1, :]
        b_end = b_row[:, CHUNK - 1:CHUNK] if direction == 0 else b_row[:, 0:1]
        q_t = qv_ref[HD * h:HD * (h + 1), :]
        vext_t = jnp.concatenate([qv_ref[ML_W + HD * h:ML_W + HD * (h + 1), :], ones_rows], axis=0)
        k = k_ref[:, HD * h:HD * (h + 1)]
        c_prev, m_prev = c_scr[chain], m_scr[chain]
        dmat = jnp.where(mask, src_col[:, fc:fc + 1] + b_row, NEG)
        m_intra = jnp.max(dmat, axis=0, keepdims=True)
        s_t = jnp.dot(k, q_t, preferred_element_type=F32) * jnp.exp(dmat - m_intra)
        inter = b_row + m_prev
        m_t = jnp.maximum(inter, m_intra)
        hx = jnp.exp(m_intra - m_t) * _dot(vext_t, s_t) + jnp.exp(inter - m_t) * _dot(c_prev, q_t)
        den = jnp.maximum(jnp.abs(hx[HD:HD + 1, :]), jnp.exp(-m_t))
        outs.append(hx[:HD, :] / den)
        g_row = b_end - b_row + li_row
        m_new = jnp.maximum(b_end + m_prev, jnp.max(g_row, axis=1, keepdims=True))
        c_scr[chain] = jnp.exp(b_end + m_prev - m_new) * c_prev + _dot(vext_t * jnp.exp(g_row - m_new), k)
        m_scr[chain] = m_new
    return jnp.concatenate(outs, axis=0)


ML_BPS = 2
ML_GROUP_COLS = (NB_BATCH // ML_BPS) * (SEQ + CTX)


def _ml_body(*refs):
    n_in = 8 * ML_BPS
    seq_refs, (bias_ref, bias_t_ref, tril_ref, triu_ref) = refs[:n_in], refs[n_in:n_in + 4]
    (hf_ref, hb_ref), (c_scr, m_scr) = refs[n_in + 4:n_in + 6], refs[n_in + 6:]

    @pl.when(pl.program_id(1) == 0)
    def _():
        c_scr[...] = jnp.zeros_like(c_scr)
        m_scr[...] = jnp.zeros_like(m_scr)

    tril, triu = tril_ref[...], triu_ref[...]
    src = lax.broadcasted_iota(jnp.int32, (CHUNK, CHUNK), 0)
    dst = lax.broadcasted_iota(jnp.int32, (CHUNK, CHUNK), 1)
    ones_rows = (lax.broadcasted_iota(jnp.int32, (HD, CHUNK), 0) == 0).astype(BF16)
    for sub in range(ML_BPS):
        c_sub, m_sub = c_scr.at[sub], m_scr.at[sub]
        for direction in range(2):
            qv_ref, k_ref, g_ref, gt_ref = seq_refs[8 * sub + 4 * direction:8 * sub + 4 * direction + 4]
            gates = g_ref[...] + bias_ref[...]
            gates_t = gt_ref[...] + bias_t_ref[...]
            ls, ls_t = _log_sigmoid(gates), _log_sigmoid(gates_t)
            if direction == 0:
                cum_c, cum_r, mask = _dot_sel_l(tril, ls), _dot_sel_r(ls_t, triu), src <= dst
            else:
                cum_c, cum_r, mask = _dot_sel_l(triu, ls), _dot_sel_r(ls_t, tril), src >= dst
            src_col = pltpu.roll(gates, 4, 1) - cum_c
            o_ref = hf_ref if direction == 0 else hb_ref
            o_ref[sub] = _ml_chain(direction, qv_ref, k_ref, src_col, cum_r, gates_t, c_sub, m_sub, mask, ones_rows)


def _mlstm(mqv_t, mk, mg, mg_t, gate_b):
    nctx = CTX // CHUNK
    nlat = SEQ // CHUNK
    per_group = NB_BATCH // ML_BPS

    def step_chunk(i, backward):
        ctx_chunk = (nctx - 1 - i) if backward else i
        lat_chunk = (N_ML_STEPS - 1 - i) if backward else (i - nctx)
        return i < nctx, ctx_chunk, lat_chunk

    def in_chunk(sub, backward):
        def f(p, i):
            b = p + per_group * sub
            is_ctx, cc, lc = step_chunk(i, backward)
            return jnp.where(is_ctx, T_LAT // CHUNK + nctx * b + cc, nlat * b + lc)
        return f

    def out_chunk(backward):
        def f(p, i):
            is_ctx, cc, lc = step_chunk(i, backward)
            return jnp.where(is_ctx, per_group * nlat + nctx * p + cc, nlat * p + lc)
        return f

    bias = jnp.pad(gate_b, (0, 128 - N_GATES)).reshape(1, 128)
    bias_t = gate_b.reshape(N_GATES, 1)
    tril = jnp.asarray(np.tril(np.ones((CHUNK, CHUNK), np.float32))).astype(BF16)
    triu = jnp.asarray(np.triu(np.ones((CHUNK, CHUNK), np.float32))).astype(BF16)
    const = lambda shape: pl.BlockSpec(shape, lambda p, i: (0, 0))
    ins, args = [], []
    for sub in range(ML_BPS):
        for backward in (False, True):
            f = in_chunk(sub, backward)
            ins += [pl.BlockSpec((2 * ML_W, CHUNK), lambda p, i, f=f: (0, f(p, i))),
                    pl.BlockSpec((CHUNK, ML_W), lambda p, i, f=f: (f(p, i), 0)),
                    pl.BlockSpec((CHUNK, 128), lambda p, i, f=f: (f(p, i), 0)),
                    pl.BlockSpec((N_GATES, CHUNK), lambda p, i, f=f: (0, f(p, i)))]
            args += [mqv_t, mk, mg, mg_t]
    outs = [pl.BlockSpec((ML_BPS, ML_W, CHUNK), lambda p, i, f=out_chunk(bw): (0, 0, f(p, i))) for bw in (False, True)]
    return pl.pallas_call(
        _ml_body,
        grid=(per_group, N_ML_STEPS),
        in_specs=ins + [const((1, 128)), const((N_GATES, 1)), const((CHUNK, CHUNK)), const((CHUNK, CHUNK))],
        out_specs=outs,
        out_shape=[jax.ShapeDtypeStruct((ML_BPS, ML_W, ML_GROUP_COLS), F32)] * 2,
        scratch_shapes=[pltpu.VMEM((ML_BPS, 2 * ML_H, 2 * HD, HD), F32), pltpu.VMEM((ML_BPS, 2 * ML_H, 1, 1), F32)],
        compiler_params=_cparams(("parallel", "arbitrary")),
        name="mlstm",
    )(*args, bias, bias_t, tril, triu)


FF_CHUNKS = ((0, 1536), (1536, D_FF))


def _out_ffn_body(al_ref, ac_ref, hl_ref, hc_ref, hf_ref, hb_ref, mo_ref, mnw_ref, bd_ref, wo_ref, x_ref,
                  g1_ref, sh2_ref, sc2_ref, g2_ref, nw2_ref, wg_ref, wu_ref, wd_ref, o_ref):
    i = pl.program_id(0)
    hn = _head_rms((hf_ref[...] + hb_ref[...]).T, bd_ref[...], mnw_ref[...])
    mlo = hn * _sigmoid(mo_ref[...])
    att = jnp.where(i < NT_LAT, al_ref[...], ac_ref[...])
    hy = jnp.where(i < NT_LAT, hl_ref[...], hc_ref[...])
    mixed = jnp.concatenate([att.astype(BF16), hy.astype(BF16), mlo.astype(BF16)], axis=1)
    x = x_ref[...] + g1_ref[...] * jnp.dot(mixed, wo_ref[...], preferred_element_type=F32)
    hb = _modulated_norm(x, nw2_ref[...], sc2_ref[...], sh2_ref[...]).astype(BF16)
    acc = None
    for lo, hi in FF_CHUNKS:
        sl = slice(lo, hi)
        a = jnp.dot(hb, wg_ref[:, sl], preferred_element_type=F32)
        u = jnp.dot(hb, wu_ref[:, sl], preferred_element_type=F32)
        part = _dot(a * _sigmoid(a) * u, wd_ref[sl, :])
        acc = part if acc is None else acc + part
    o_ref[...] = x + g2_ref[...] * acc


def _out_ffn(att, attc, hyo, hyoc, hf_t, hb_t, mo, ml_nw, w_out_b, xs, mod4, layer, nw2, wg, wu, wd, with_ctx):
    assert ATT_W == HY
    tiles = NT_ALL if with_ctx else NT_LAT
    row = lambda w: pl.BlockSpec((TM, w), lambda i: (i, 0))
    lat_rows = pl.BlockSpec((TM, HY), lambda i: (jnp.minimum(i, NT_LAT - 1), 0))
    ctx_rows = pl.BlockSpec((TM, HY), lambda i: (jnp.maximum(i - NT_LAT, 0), 0))
    lat_tiles = (NB_BATCH // ML_BPS) * TILES_PER_SEQ
    col = pl.BlockSpec((None, ML_W, TM), lambda i: (jnp.where(i < NT_LAT, i // lat_tiles, i - NT_LAT), 0,
                                                    jnp.where(i < NT_LAT, i % lat_tiles, lat_tiles)))
    resident = lambda shape: pl.BlockSpec(shape, lambda i: (0, 0), pipeline_mode=pl.Buffered(1))
    layer_w = lambda shape: pl.BlockSpec((None,) + shape, lambda i: (layer, 0, 0), pipeline_mode=pl.Buffered(1))
    vec = lambda w: pl.BlockSpec((1, w), lambda i: (0, 0))
    bd = jnp.asarray(_blockdiag_ones(ML_W)).astype(BF16)
    return pl.pallas_call(
        _out_ffn_body,
        grid=(tiles,),
        in_specs=[lat_rows, ctx_rows, lat_rows, ctx_rows,
                  col, col, row(ML_W), vec(ML_W), resident((ML_W, ML_W)), layer_w((D, D)), row(D),
                  _mod_spec(layer, 2), _mod_spec(layer, 3), _mod_spec(layer, 4), _mod_spec(layer, 5), vec(D),
                  layer_w((D, D_FF)), layer_w((D, D_FF)), layer_w((D_FF, D))],
        out_specs=row(D),
        out_shape=jax.ShapeDtypeStruct((tiles * TM, D), F32),
        compiler_params=_cparams(("parallel",), 56),
        name="out_ffn",
    )(att, attc, hyo, hyoc, hf_t, hb_t, mo, ml_nw.reshape(1, ML_W), bd, w_out_b, xs, mod4, mod4, mod4, mod4,
      nw2.reshape(1, D), wg, wu, wd)


def _rope_tables():
    n_rows = SEQ // GRID_W
    row = jnp.repeat(jnp.arange(n_rows), GRID_W)
    col = jnp.tile(jnp.arange(GRID_W), n_rows)
    nf = HD // 4
    inv_freq = ROPE_BASE ** (-jnp.arange(nf, dtype=F32) / nf)
    ang = jnp.stack([row[:, None] * inv_freq, col[:, None] * inv_freq], axis=1)
    cos, sin = jnp.cos(ang), jnp.sin(ang)
    cos_h = jnp.concatenate([cos, cos], axis=-1).reshape(SEQ, HD)
    sin_h = jnp.concatenate([-sin, sin], axis=-1).reshape(SEQ, HD)
    cos_t = jnp.concatenate([jnp.tile(cos_h, (1, 2)), jnp.ones((TM, 128), F32)], axis=0)
    sin_t = jnp.concatenate([jnp.tile(sin_h, (1, 2)), jnp.zeros((TM, 128), F32)], axis=0)
    return cos_t, sin_t


def kernel(x, c, ctx, c_ctx, w_mod, b_mod, norm1_w, norm2_w, w_in, w_out, q_norm_w, k_norm_w, attn_sink, hy_conv_w, hy_conv_b, hy_w1, hy_b1, hy_freq, hy_w2, hy_b2, hy_w3, hy_decay, hy_skip, ml_gate_b, ml_norm_w, ffn_w_gate, ffn_w_up, ffn_w_down):
    xs = jnp.concatenate([x.reshape(T_LAT, D), ctx.reshape(T_CTX, D)], axis=0)
    cc8 = jnp.concatenate([c, c_ctx[None, :], jnp.zeros((8 - NB_BATCH - 1, D), F32)], axis=0)
    mod4 = _modulation(cc8, w_mod, b_mod).reshape(DEPTH, 8, 1, 6 * D)

    cos_t, sin_t = _rope_tables()
    feats_l, feats_c = _pos_feats_circ(SEQ), _pos_feats_circ(CTX)
    s1_tab = jnp.asarray(_TAB["s1"]).astype(BF16)
    s1i_tab = jnp.asarray(_TAB["s1i"]).astype(BF16)
    wf, wi = jnp.asarray(_TAB["wf"]).astype(BF16), jnp.asarray(_TAB["wi"]).astype(BF16)
    fc, fs = jnp.asarray(_TAB["fc"]).astype(BF16), jnp.asarray(_TAB["fs"]).astype(BF16)

    w_in_p = _reorder_w_in(w_in)
    w_out_b = w_out.astype(BF16)
    wg_b, wu_b, wd_b = ffn_w_gate.astype(BF16), ffn_w_up.astype(BF16), ffn_w_down.astype(BF16)
    filt_params = _filter_params(hy_w1, hy_b1, hy_freq, hy_w2, hy_b2, hy_w3)
    sink8 = jnp.pad(attn_sink, ((0, 0), (0, 128 - ATT_H)))[:, None, :] * jnp.ones((1, 8, 1), F32)
    zeros_tile = jnp.zeros((TM, HY), F32)

    out = None
    for l in range(DEPTH):
        last = l == DEPTH - 1
        qt, kn, vt, hv, hx1, hx2, mqv_t, mk, mo, mg, mg_t = _in_proj(
            xs, mod4, l, norm1_w[l], w_in_p, cos_t, sin_t, q_norm_w[l], k_norm_w[l], hy_conv_w[l], hy_conv_b[l])

        att = _attention(qt, kn, vt, sink8[l])
        attc = zeros_tile if last else _attention_ctx(qt, kn, vt, sink8[l])

        dec = hy_decay[l].reshape(1, N_FILT)
        circ = _circular_filters(feats_l, filt_params, l, dec)
        f_re, f_im = _dft_stage1(circ.reshape(2, HALF_A, FFT_B, FFT_T, 2 * HY), s1_tab)
        h_re, h_im = _filter_spectrum(f_re.reshape(2, 2, N_FFT, HY), f_im.reshape(2, 2, N_FFT, HY), wf)

        rows = lambda t: t.reshape(NB_BATCH // 2, N_FFT, HY)
        tiles = lambda t: t.reshape(NB_BATCH // 2, FFT_A, FFT_B, FFT_T, HY)
        v4 = _row_blocks(hv)
        a_re, a_im = _dft_stage1_pair(v4, s1_tab)
        c_re, c_im = _spectral_filter(rows(a_re), rows(a_im), h_re, h_im, 0, wf, wi)
        z4, a_re, a_im = _idft_stage1_gate(tiles(c_re), tiles(c_im), s1i_tab, _row_blocks(hx1), v4, hy_skip[l, 0], s1_tab)
        c_re, c_im = _spectral_filter(rows(a_re), rows(a_im), h_re, h_im, 1, wf, wi)
        hyo = _idft_stage1_gate(tiles(c_re), tiles(c_im), s1i_tab, _row_blocks(hx2), z4, hy_skip[l, 1]).reshape(T_LAT, HY)

        if last:
            hyoc = zeros_tile
        else:
            circ_c = _circular_filters(feats_c, filt_params, l, dec)
            hyoc = _hyena_ctx(hv, hx1, hx2, circ_c, fc, fs, hy_skip[l])

        hf_t, hb_t = _mlstm(mqv_t, mk, mg, mg_t, ml_gate_b[l])

        xs = _out_ffn(att, attc, hyo, hyoc, hf_t, hb_t, mo, ml_norm_w[l], w_out_b, xs, mod4, l,
                      norm2_w[l], wg_b, wu_b, wd_b, not last)
        out = xs
    return out.reshape(NB_BATCH, SEQ, D)
```

```python
import functools
import math

import numpy as np
import jax
import jax.numpy as jnp
from jax import lax
from jax.experimental import pallas as pl
from jax.experimental.pallas import tpu as pltpu

F32 = jnp.float32
BF16 = jnp.bfloat16

D = 1024
NB_BATCH = 4
SEQ = 4096
DEPTH = 4
GRID_W = 64
CTX = 256
T_LAT = NB_BATCH * SEQ
T_CTX = NB_BATCH * CTX
R_ALL = T_LAT + T_CTX

HD = 64
ATT_H = 6
ATT_W = ATT_H * HD
KV_W = 2 * HD
BLK = 128
NBLK = SEQ // BLK
ROPE_BASE = 10000.0

HY = 384
N_FILT = 4 * HY
POS_BANDS = 16
FILT_HID = 64

ML_H = 4
ML_W = 256
CHUNK = 128

D_FF = 2816
P_IN = 2832
P_PAD = 2944
NEG = -1e30
EPS = 1e-6
LOG2E = 1.4426950408889634

TM = 512
NT_LAT = T_LAT // TM
NT_ALL = R_ALL // TM
TILES_PER_SEQ = SEQ // TM

N_FFT = 2 * SEQ
FFT_A = 32
FFT_R = 256
FFT_T = 16
FFT_B = FFT_R // FFT_T
HALF_A = FFT_A // 2
S1_TILES = 4
HB_ROWS = 1024
N_CTXF = 2 * CTX


def _np_tables():
    ka = np.arange(FFT_A)[None, :, None, None]
    cc = np.arange(FFT_T)[None, None, :, None]
    aa = np.arange(HALF_A)[None, None, None, :]
    bb = np.arange(FFT_B)[:, None, None, None]
    ph = (ka * (FFT_R * aa + FFT_T * bb + cc)) % N_FFT
    th = 2.0 * np.pi * ph / N_FFT
    eye = np.eye(FFT_T)
    cos = np.einsum("bkca,cd->bkcad", np.cos(th), eye)
    sin = np.einsum("bkca,cd->bkcad", np.sin(th), eye)
    rows, cols = FFT_A * FFT_T, HALF_A * FFT_T
    s1 = np.concatenate([cos.reshape(FFT_B, rows, cols), -sin.reshape(FFT_B, rows, cols)], axis=1)
    s1i = np.concatenate([cos.reshape(FFT_B, rows, cols).transpose(0, 2, 1),
                          -sin.reshape(FFT_B, rows, cols).transpose(0, 2, 1)], axis=2) / N_FFT
    r = np.arange(FFT_R)
    th2 = 2.0 * np.pi * ((r[:, None] * r[None, :]) % FFT_R) / FFT_R
    c2, s2 = np.cos(th2), np.sin(th2)
    wf = np.block([[c2, s2], [-s2, c2]])
    wi = np.block([[c2, -s2], [s2, c2]])
    c = np.arange(N_CTXF)
    th3 = 2.0 * np.pi * ((c[:, None] * c[None, :]) % N_CTXF) / N_CTXF
    return dict(s1=s1.astype(np.float32), s1i=s1i.astype(np.float32), wf=wf.astype(np.float32), wi=wi.astype(np.float32),
                fc=np.cos(th3).astype(np.float32), fs=np.sin(th3).astype(np.float32))


_TAB = _np_tables()


def _blockdiag_ones(width):
    return np.kron(np.eye(width // HD), np.ones((HD, HD))).astype(np.float32)


def _cparams(sem, vmem_mb=48):
    return pltpu.CompilerParams(dimension_semantics=sem, vmem_limit_bytes=vmem_mb * 1024 * 1024)


def _dot(a, b):
    return jnp.dot(a.astype(BF16), b.astype(BF16), preferred_element_type=F32)


def _dot_nt(a, b):
    return lax.dot_general(a.astype(BF16), b.astype(BF16), (((1,), (1,)), ((), ())), preferred_element_type=F32)


def _dot_tn(a, b):
    return lax.dot_general(a.astype(BF16), b.astype(BF16), (((0,), (0,)), ((), ())), preferred_element_type=F32)


def _split3(x):
    x1 = x.astype(BF16)
    r1 = x - x1.astype(F32)
    x2 = r1.astype(BF16)
    x3 = (r1 - x2.astype(F32)).astype(BF16)
    return x1, x2, x3


def _dot_sel_l(sel, x):
    return sum(jnp.dot(sel, p, preferred_element_type=F32) for p in _split3(x))


def _dot_sel_r(x, sel):
    return sum(jnp.dot(p, sel, preferred_element_type=F32) for p in _split3(x))


def _sigmoid(x):
    return 1.0 / (1.0 + jnp.exp(-x))


def _log_sigmoid(x):
    return jnp.minimum(x, 0.0) - jnp.log(1.0 + jnp.exp(-jnp.abs(x)))


def _head_rms(t, bd, w):
    sq = t * t
    hi = sq.astype(BF16)
    lo = (sq - hi.astype(F32)).astype(BF16)
    ss = jnp.dot(hi, bd, preferred_element_type=F32) + jnp.dot(lo, bd, preferred_element_type=F32)
    return t * lax.rsqrt(ss * (1.0 / HD) + EPS) * w


def _mod_row(i):
    return jnp.where(i < NT_LAT, i // TILES_PER_SEQ, NB_BATCH)


def _mod_spec(layer, k):
    return pl.BlockSpec((None, None, 1, D), lambda i: (layer, _mod_row(i), 0, k))


def _mod_body(s_ref, w_ref, b_ref, o_ref):
    s = s_ref[...]
    s = s * _sigmoid(s)
    o_ref[...] = _dot(s, w_ref[...]) + b_ref[...]


def _modulation(cc8, w_mod, b_mod):
    nc = 1536
    return pl.pallas_call(
        _mod_body,
        grid=(DEPTH, 6 * D // nc),
        in_specs=[pl.BlockSpec((8, D), lambda l, j: (0, 0)),
                  pl.BlockSpec((None, D, nc), lambda l, j: (l, 0, j)),
                  pl.BlockSpec((None, 1, nc), lambda l, j: (l, 0, j))],
        out_specs=pl.BlockSpec((None, 8, nc), lambda l, j: (l, 0, j)),
        out_shape=jax.ShapeDtypeStruct((DEPTH, 8, 6 * D), F32),
        compiler_params=_cparams(("parallel", "parallel")),
        name="modulation",
    )(cc8, w_mod, b_mod.reshape(DEPTH, 1, 6 * D))


_COL_ATT, _COL_HY, _COL_MQV, _COL_MK, _COL_MO, _COL_MG = (0, 640), (640, 1792), (1792, 2304), (2304, 2560), (2560, 2816), (2816, 2944)
N_GATES = 4 * ML_H


def _reorder_w_in(w_in):
    mq_end, mk_end, mv_end = 2048, 2304, 2560
    w = jnp.concatenate([w_in[:, :, :mq_end], w_in[:, :, mk_end:mv_end], w_in[:, :, mq_end:mk_end], w_in[:, :, mv_end:]], axis=2)
    return jnp.pad(w, ((0, 0), (0, 0), (0, P_PAD - P_IN))).astype(BF16)


def _modulated_norm(x, nw, sc, sh):
    ms = jnp.mean(x * x, axis=-1, keepdims=True)
    return (x * lax.rsqrt(ms + EPS)) * (nw * (1.0 + sc)) + sh


HALO = 8


def _in_body(x_ref, xp_ref, xn_ref, sh_ref, sc_ref, nw_ref, w_ref, cos_ref, sin_ref, qw_ref, kw_ref, bdq_ref, bdk_ref,
             cw_ref, cb_ref, qt_ref, k_ref, vt_ref, hv_ref, hx1_ref, hx2_ref, mqvt_ref, mk_ref, mo_ref, mg_ref, mgt_ref):
    i = pl.program_id(0)
    nw, sc, sh = nw_ref[...], sc_ref[...], sh_ref[...]
    hb = _modulated_norm(x_ref[...], nw, sc, sh).astype(BF16)
    proj = lambda cols: jnp.dot(hb, w_ref[:, cols[0]:cols[1]], preferred_element_type=F32)

    halo = _modulated_norm(jnp.concatenate([xp_ref[...], xn_ref[...]], axis=0), nw, sc, sh).astype(BF16)
    both = jnp.dot(jnp.concatenate([hb, halo], axis=0), w_ref[:, _COL_ATT[0]:_COL_HY[1]], preferred_element_type=F32)

    att = both[:TM, _COL_ATT[0]:_COL_ATT[1]]
    cos, sin = cos_ref[...], sin_ref[...]
    lane = lax.broadcasted_iota(jnp.int32, (TM, 128), 1)
    first_half = (lane & 31) < 16
    q = _head_rms(att[:, 0:ATT_W], bdq_ref[...], qw_ref[...])
    k = _head_rms(att[:, ATT_W:ATT_W + KV_W], bdk_ref[...], kw_ref[...])
    qt_ref[...] = (_rope(q, cos, sin, first_half) * (HD ** -0.5 * LOG2E)).T.astype(BF16)
    k_ref[...] = _rope(k, cos, sin, first_half).astype(BF16)
    vt_ref[...] = att[:, ATT_W + KV_W:ATT_W + 2 * KV_W].T.astype(BF16)

    hy, hy_halo = both[:TM, _COL_HY[0]:_COL_HY[1]], both[TM:, _COL_HY[0]:_COL_HY[1]]
    seq_len = jnp.where(i < NT_LAT, SEQ, CTX)
    pos = (i * TM + lax.broadcasted_iota(jnp.int32, (TM, 1), 0)) & (seq_len - 1)
    usc = _short_conv_rows(hy, hy_halo[HALO - 1:HALO, :], hy_halo[HALO:HALO + 1, :], cw_ref[...], cb_ref[...],
                           pos == 0, pos == seq_len - 1)
    hv_ref[...] = usc[:, 0:HY]
    hx1_ref[...] = usc[:, HY:2 * HY]
    hx2_ref[...] = usc[:, 2 * HY:3 * HY]

    mqvt_ref[...] = proj(_COL_MQV).T.astype(BF16)
    mk_ref[...] = (proj(_COL_MK) * (HD ** -0.5)).astype(BF16)
    mo_ref[...] = proj(_COL_MO)
    mg = proj(_COL_MG)
    mg_ref[...] = mg
    mgt_ref[...] = mg.T[0:N_GATES, :]


def _in_proj(xs, mod4, layer, nw, w_in_p, cos_t, sin_t, qw, kw, conv_w, conv_b):
    rows = lambda w: pl.BlockSpec((TM, w), lambda i: (i, 0))
    cols = lambda h: pl.BlockSpec((h, TM), lambda i: (0, i))
    const = lambda shape: pl.BlockSpec(shape, lambda i: (0, 0))
    tab = lambda i: (jnp.where(i < NT_LAT, i % TILES_PER_SEQ, TILES_PER_SEQ), 0)
    per_tile = TM // HALO
    prev = pl.BlockSpec((HALO, D), lambda i: (jnp.maximum(i * per_tile - 1, 0), 0))
    nxt = pl.BlockSpec((HALO, D), lambda i: (jnp.minimum((i + 1) * per_tile, R_ALL // HALO - 1), 0))
    bdq = jnp.asarray(_blockdiag_ones(ATT_W)).astype(BF16)
    bdk = jnp.asarray(_blockdiag_ones(KV_W)).astype(BF16)
    bf = lambda shape: jax.ShapeDtypeStruct(shape, BF16)
    f32 = lambda shape: jax.ShapeDtypeStruct(shape, F32)
    return pl.pallas_call(
        _in_body,
        grid=(NT_ALL,),
        in_specs=[rows(D), prev, nxt, _mod_spec(layer, 0), _mod_spec(layer, 1), const((1, D)),
                  pl.BlockSpec((None, D, P_PAD), lambda i: (layer, 0, 0), pipeline_mode=pl.Buffered(1)),
                  pl.BlockSpec((TM, 128), tab), pl.BlockSpec((TM, 128), tab),
                  const((1, ATT_W)), const((1, KV_W)), const((ATT_W, ATT_W)), const((KV_W, KV_W)),
                  const((3, 3 * HY)), const((1, 3 * HY))],
        out_specs=[cols(ATT_W), rows(KV_W), cols(KV_W), rows(HY), rows(HY), rows(HY),
                   cols(2 * ML_W), rows(ML_W), rows(ML_W), rows(128), cols(N_GATES)],
        out_shape=[bf((ATT_W, R_ALL)), bf((R_ALL, KV_W)), bf((KV_W, R_ALL)), f32((R_ALL, HY)), f32((R_ALL, HY)), f32((R_ALL, HY)),
                   bf((2 * ML_W, R_ALL)), bf((R_ALL, ML_W)), f32((R_ALL, ML_W)), f32((R_ALL, 128)), f32((N_GATES, R_ALL))],
        compiler_params=_cparams(("parallel",), 56),
        name="in_proj",
    )(xs, xs, xs, mod4, mod4, nw.reshape(1, D), w_in_p, cos_t, sin_t,
      jnp.tile(qw, ATT_H).reshape(1, ATT_W), jnp.tile(kw, 2).reshape(1, KV_W), bdq, bdk, conv_w, conv_b.reshape(1, 3 * HY))


def _rope(t, cos, sin_signed, first_half):
    outs = []
    for c in range(t.shape[1] // 128):
        tc = t[:, c * 128:(c + 1) * 128]
        partner = jnp.where(first_half, pltpu.roll(tc, 112, 1), pltpu.roll(tc, 16, 1))
        outs.append(tc * cos + partner * sin_signed)
    return outs[0] if len(outs) == 1 else jnp.concatenate(outs, axis=1)


def _att_heads(qt, kv_list, sink_ref, o_ref):
    nq = qt.shape[1]
    rep = ATT_H // 2
    gw = rep * nq
    zeros = jnp.zeros((HD, gw), BF16)
    q0 = jnp.concatenate([qt[HD * h:HD * (h + 1), :] for h in range(rep)], axis=1)
    q1 = jnp.concatenate([qt[HD * h:HD * (h + 1), :] for h in range(rep, 2 * rep)], axis=1)
    qe = jnp.concatenate([jnp.concatenate([q0, zeros], axis=1), jnp.concatenate([zeros, q1], axis=1)], axis=0)
    all_scores = []
    for k, _, mask in kv_list:
        s = jnp.dot(k, qe, preferred_element_type=F32)
        all_scores.append(s if mask is None else jnp.where(mask, s, NEG))
    outs = []
    for g in range(2):
        heads = range(rep * g, rep * (g + 1))
        sink = jnp.concatenate([jnp.broadcast_to(sink_ref[0:1, h:h + 1], (1, nq)) for h in heads], axis=1) * LOG2E
        scores = [s[:, gw * g:gw * (g + 1)] for s in all_scores]
        m = sink
        for s in scores:
            m = jnp.maximum(m, jnp.max(s, axis=0, keepdims=True))
        acc = None
        for s, (_, vt, _) in zip(scores, kv_list):
            p = jnp.exp2(s - m).astype(BF16)
            ones = (lax.broadcasted_iota(jnp.int32, (16, vt.shape[1]), 0) == 0).astype(BF16)
            pv = jnp.dot(jnp.concatenate([vt[HD * g:HD * (g + 1), :], ones], axis=0), p, preferred_element_type=F32)
            acc = pv if acc is None else acc + pv
        og = acc[:HD, :] / (acc[HD:HD + 1, :] + jnp.exp2(sink - m))
        outs += [og[:, nq * i:nq * (i + 1)] for i in range(rep)]
    o_ref[...] = jnp.concatenate(outs, axis=0).T


QB = 8


def _att_body(*refs):
    nband = QB + 2
    qt_ref, ks, vs = refs[0], refs[1:1 + nband], refs[1 + nband:1 + 2 * nband]
    kc_ref, vc_ref, sink_ref, o_ref = refs[1 + 2 * nband:]
    j = pl.program_id(1)
    kc, vc = kc_ref[...], vc_ref[...]
    width = ATT_H * BLK
    key = lax.broadcasted_iota(jnp.int32, (BLK, width), 0)
    qry = lax.broadcasted_iota(jnp.int32, (BLK, width), 1) & (BLK - 1)
    for sub in range(QB):
        blk = QB * j + sub
        mask_prev = jnp.logical_and(key >= qry, blk > 0)
        mask_next = jnp.logical_and(key <= qry, blk < NBLK - 1)
        _att_heads(qt_ref[:, sub * BLK:(sub + 1) * BLK],
                   [(ks[sub][...], vs[sub][...], mask_prev), (ks[sub + 1][...], vs[sub + 1][...], None),
                    (ks[sub + 2][...], vs[sub + 2][...], mask_next), (kc, vc, None)],
                   sink_ref, o_ref.at[sub * BLK:(sub + 1) * BLK, :])


def _attention(qt, kn, vt, sink8):
    qrows = QB * BLK
    steps = NBLK // QB

    def band(off):
        return lambda b, j: b * NBLK + jnp.clip(QB * j + off, 0, NBLK - 1)

    kspec = lambda f: pl.BlockSpec((BLK, KV_W), lambda b, j: (f(b, j), 0))
    vspec = lambda f: pl.BlockSpec((KV_W, BLK), lambda b, j: (0, f(b, j)))
    offs = range(-1, QB + 1)
    return pl.pallas_call(
        _att_body,
        grid=(NB_BATCH, steps),
        in_specs=[pl.BlockSpec((ATT_W, qrows), lambda b, j: (0, b * steps + j))]
                 + [kspec(band(o)) for o in offs] + [vspec(band(o)) for o in offs]
                 + [pl.BlockSpec((CTX, KV_W), lambda b, j: (T_LAT // CTX + b, 0)),
                    pl.BlockSpec((KV_W, CTX), lambda b, j: (0, T_LAT // CTX + b)),
                    pl.BlockSpec((8, 128), lambda b, j: (0, 0))],
        out_specs=pl.BlockSpec((qrows, ATT_W), lambda b, j: (b * steps + j, 0)),
        out_shape=jax.ShapeDtypeStruct((T_LAT, ATT_W), F32),
        compiler_params=_cparams(("parallel", "parallel")),
        name="attention",
    )(qt, *([kn] * (QB + 2)), *([vt] * (QB + 2)), kn, vt, sink8)


def _att_ctx_body(qt_ref, kc_ref, vc_ref, sink_ref, o_ref):
    kv = [(kc_ref[...], vc_ref[...], None)]
    for sub in range(CTX // BLK):
        _att_heads(qt_ref[:, sub * BLK:(sub + 1) * BLK], kv, sink_ref, o_ref.at[sub * BLK:(sub + 1) * BLK, :])


def _attention_ctx(qt, kn, vt, sink8):
    return pl.pallas_call(
        _att_ctx_body,
        grid=(NB_BATCH,),
        in_specs=[pl.BlockSpec((ATT_W, CTX), lambda b: (0, T_LAT // CTX + b)),
                  pl.BlockSpec((CTX, KV_W), lambda b: (T_LAT // CTX + b, 0)),
                  pl.BlockSpec((KV_W, CTX), lambda b: (0, T_LAT // CTX + b)),
                  pl.BlockSpec((8, 128), lambda b: (0, 0))],
        out_specs=pl.BlockSpec((CTX, ATT_W), lambda b: (b, 0)),
        out_shape=jax.ShapeDtypeStruct((T_CTX, ATT_W), F32),
        compiler_params=_cparams(("parallel",)),
        name="attention_ctx",
    )(qt, kn, vt, sink8)


def _short_conv_rows(u, prev_row, next_row, w, b, first, last):
    n = u.shape[0]
    row = lax.broadcasted_iota(jnp.int32, u.shape, 0)
    up = jnp.where(row == 0, prev_row, pltpu.roll(u, 1, 0))
    un = jnp.where(row == n - 1, next_row, pltpu.roll(u, n - 1, 0))
    up = jnp.where(first, 0.0, up)
    un = jnp.where(last, 0.0, un)
    return up * w[0:1, :] + u * w[1:2, :] + un * w[2:3, :] + b


def _filt_body(n, tl, f_ref, w1_ref, b1_ref, fr_ref, w2_ref, b2_ref, w3a_ref, w3b_ref, dec_ref, o_ref):
    h = tl // 2
    f = jnp.concatenate([f_ref[0:h, :], f_ref[h:tl, :]], axis=1)
    fr = fr_ref[...]
    z = jnp.sin(fr * (_dot(f, w1_ref[...]) + b1_ref[...]))
    z = jnp.sin(fr * (_dot(z, w2_ref[...]) + b2_ref[...]))
    dec = jnp.abs(dec_ref[...])
    top = _dot(z, w3a_ref[...]) * jnp.exp(-f[:, 0:1] * dec)
    bot = _dot(z, w3b_ref[...]) * jnp.exp(-f[:, 128:129] * dec)
    filt = jnp.concatenate([top, bot], axis=0)
    row = pl.program_id(0) * tl + lax.broadcasted_iota(jnp.int32, filt.shape, 0)
    o_ref[...] = jnp.where(row == n, 0.0, filt)


def _filter_params(hy_w1, hy_b1, hy_freq, hy_w2, hy_b2, hy_w3):
    hid = FILT_HID
    w1 = jnp.pad(hy_w1, ((0, 0), (0, 128 - hy_w1.shape[1]), (0, 0)))
    zero = jnp.zeros_like
    w1b = jnp.concatenate([jnp.concatenate([w1, zero(w1)], axis=2), jnp.concatenate([zero(w1), w1], axis=2)], axis=1)
    w2b = jnp.concatenate([jnp.concatenate([hy_w2, zero(hy_w2)], axis=2), jnp.concatenate([zero(hy_w2), hy_w2], axis=2)], axis=1)
    w3a = jnp.concatenate([hy_w3, zero(hy_w3)], axis=1)
    w3b = jnp.concatenate([zero(hy_w3), hy_w3], axis=1)
    twice = lambda v: jnp.concatenate([v, v], axis=1).reshape(DEPTH, 1, 2 * hid)
    return w1b, twice(hy_b1), twice(hy_freq), w2b, twice(hy_b2), w3a, w3b


def _circular_filters(feats_circ, params, layer, dec):
    n = feats_circ.shape[0] // 2
    tl = min(n, 512)
    w1b, b1, fr, w2b, b2, w3a, w3b = params
    per = lambda shape: pl.BlockSpec((None,) + shape, lambda i: (layer, 0, 0))
    half = pl.BlockSpec((None, 128, 2 * HY), lambda i: (layer, 0, i // (n // tl)))
    return pl.pallas_call(
        functools.partial(_filt_body, n, tl),
        grid=(2 * n // tl,),
        in_specs=[pl.BlockSpec((tl, 128), lambda i: (i, 0)), per((256, 128)), per((1, 128)), per((1, 128)),
                  per((128, 128)), per((1, 128)), half, half, pl.BlockSpec((1, 2 * HY), lambda i: (0, i // (n // tl)))],
        out_specs=pl.BlockSpec((tl, 2 * HY), lambda i: (i, 0)),
        out_shape=jax.ShapeDtypeStruct((2 * n, 2 * HY), F32),
        compiler_params=_cparams(("parallel",)),
        name="hyena_filters",
    )(feats_circ, w1b, b1, fr, w2b, b2, w3a, w3b, dec)


def _pos_feats_circ(n):
    t = jnp.linspace(0.0, 1.0, n, dtype=F32)[:, None]
    ang = (2.0 * math.pi / n) * jnp.arange(n, dtype=F32)[:, None]
    bands = jnp.linspace(1e-4, POS_BANDS - 1, POS_BANDS, dtype=F32)[None, :]
    feats = jnp.concatenate([t, jnp.cos(bands * ang), -jnp.sin(bands * ang)], axis=-1)
    feats = jnp.pad(feats, ((0, 0), (0, 128 - feats.shape[1])))
    return jnp.concatenate([feats, feats[:1], jnp.flip(feats[:n - 1], axis=0)], axis=0)


def _s1_body(x_ref, m_ref, are_ref, aim_ref):
    half = FFT_A * FFT_T
    ncb = x_ref.shape[-1] // HY
    for p in range(S1_TILES):
        xs = x_ref[:, p].reshape(HALF_A * FFT_T, ncb * HY)
        r = _dot(m_ref[p], xs)
        for cb in range(ncb):
            cols = slice(cb * HY, (cb + 1) * HY)
            are_ref[cb, :, p] = r[:half, cols].reshape(FFT_A, FFT_T, HY).astype(BF16)
            aim_ref[cb, :, p] = r[half:, cols].reshape(FFT_A, FFT_T, HY).astype(BF16)


def _dft_stage1(x5, s1_tab):
    nb, ncb = x5.shape[0], x5.shape[-1] // HY
    out = pl.BlockSpec((None, ncb, FFT_A, S1_TILES, FFT_T, HY), lambda b, j: (b, 0, 0, j, 0, 0))
    return pl.pallas_call(
        _s1_body,
        grid=(nb, FFT_B // S1_TILES),
        in_specs=[pl.BlockSpec((None, HALF_A, S1_TILES, FFT_T, ncb * HY), lambda b, j: (b, 0, j, 0, 0)),
                  pl.BlockSpec((S1_TILES, 2 * FFT_A * FFT_T, HALF_A * FFT_T), lambda b, j: (j, 0, 0))],
        out_specs=[out, out],
        out_shape=[jax.ShapeDtypeStruct((nb, ncb, FFT_A, FFT_B, FFT_T, HY), BF16)] * 2,
        compiler_params=_cparams(("parallel", "parallel")),
        name="dft_stage1",
    )(x5, s1_tab)


def _s1_pair_body(x_ref, m_ref, are_ref, aim_ref):
    half = FFT_A * FFT_T
    rows = HALF_A * FFT_T
    for p in range(S1_TILES):
        xab = jnp.concatenate([x_ref[0:HALF_A, p].reshape(rows, HY), x_ref[HALF_A:2 * HALF_A, p].reshape(rows, HY)], axis=1)
        r = _dot(m_ref[p], xab)
        are_ref[:, p] = (r[:half, :HY] - r[half:, HY:]).reshape(FFT_A, FFT_T, HY).astype(BF16)
        aim_ref[:, p] = (r[half:, :HY] + r[:half, HY:]).reshape(FFT_A, FFT_T, HY).astype(BF16)


def _row_blocks(x):
    return x.reshape(x.shape[0] // FFT_R, FFT_B, FFT_T, HY)


_PAIR_SPEC = pl.BlockSpec((2 * HALF_A, S1_TILES, FFT_T, HY), lambda b, j: (b, j, 0, 0))


def _dft_stage1_pair(x4, s1_tab):
    npair = NB_BATCH // 2
    out = pl.BlockSpec((None, FFT_A, S1_TILES, FFT_T, HY), lambda b, j: (b, 0, j, 0, 0))
    return pl.pallas_call(
        _s1_pair_body,
        grid=(npair, FFT_B // S1_TILES),
        in_specs=[_PAIR_SPEC, pl.BlockSpec((S1_TILES, 2 * FFT_A * FFT_T, HALF_A * FFT_T), lambda b, j: (j, 0, 0))],
        out_specs=[out, out],
        out_shape=[jax.ShapeDtypeStruct((npair, FFT_A, FFT_B, FFT_T, HY), BF16)] * 2,
        compiler_params=_cparams(("parallel", "parallel")),
        name="dft_stage1_pair",
    )(x4, s1_tab)


def _block_rows(r):
    return slice(r * FFT_R, (r + 1) * FFT_R)


def _stack_complex(re_ref, im_ref, sl):
    return jnp.concatenate([re_ref[sl, :], im_ref[sl, :]], axis=0)


def _spec_body(are_ref, aim_ref, wf_ref, hre_ref, him_ref):
    wf = wf_ref[...]
    for r in range(HB_ROWS // FFT_R):
        sl = _block_rows(r)
        both = jnp.concatenate([_stack_complex(are_ref.at[0], aim_ref.at[0], sl),
                                _stack_complex(are_ref.at[1], aim_ref.at[1], sl)], axis=1)
        x = jnp.dot(wf, both, preferred_element_type=F32)
        first, second = x[:, :HY], x[:, HY:]
        x = first + second if r % 2 == 0 else first - second
        hre_ref[sl, :] = x[:FFT_R]
        him_ref[sl, :] = x[FFT_R:]


def _filter_spectrum(a_re, a_im, wf):
    assert (HB_ROWS // FFT_R) % 2 == 0
    blk_in = pl.BlockSpec((2, None, HB_ROWS, HY), lambda r, o: (0, o, r, 0))
    blk_out = pl.BlockSpec((None, HB_ROWS, HY), lambda r, o: (o, r, 0))
    return pl.pallas_call(
        _spec_body,
        grid=(N_FFT // HB_ROWS, 2),
        in_specs=[blk_in, blk_in, pl.BlockSpec((2 * FFT_R, 2 * FFT_R), lambda r, o: (0, 0))],
        out_specs=[blk_out, blk_out],
        out_shape=[jax.ShapeDtypeStruct((2, N_FFT, HY), F32)] * 2,
        compiler_params=_cparams(("parallel", "parallel")),
        name="filter_spectrum",
    )(a_re, a_im, wf)


def _s2_body(are_ref, aim_ref, hre_ref, him_ref, wf_ref, wi_ref, cre_ref, cim_ref):
    wf, wi = wf_ref[...], wi_ref[...]
    for r in range(0, HB_ROWS // FFT_R, 2):
        sl0, sl1 = _block_rows(r), _block_rows(r + 1)
        a = jnp.concatenate([_stack_complex(are_ref, aim_ref, sl0), _stack_complex(are_ref, aim_ref, sl1)], axis=1)
        x = jnp.dot(wf, a, preferred_element_type=F32)
        xre, xim = x[:FFT_R], x[FFT_R:]
        hre = jnp.concatenate([hre_ref[sl0, :], hre_ref[sl1, :]], axis=1)
        him = jnp.concatenate([him_ref[sl0, :], him_ref[sl1, :]], axis=1)
        y = jnp.concatenate([(xre * hre - xim * him).astype(BF16), (xre * him + xim * hre).astype(BF16)], axis=0)
        c = jnp.dot(wi, y, preferred_element_type=F32)
        for i, sl in enumerate((sl0, sl1)):
            cre_ref[sl, :] = c[:FFT_R, i * HY:(i + 1) * HY].astype(BF16)
            cim_ref[sl, :] = c[FFT_R:, i * HY:(i + 1) * HY].astype(BF16)


def _spectral_filter(a_re, a_im, h_re, h_im, order, wf, wi):
    nb = a_re.shape[0]
    blk = pl.BlockSpec((None, HB_ROWS, HY), lambda r, b: (b, r, 0))
    hblk = pl.BlockSpec((None, HB_ROWS, HY), lambda r, b: (order, r, 0))
    mat = pl.BlockSpec((2 * FFT_R, 2 * FFT_R), lambda r, b: (0, 0))
    return pl.pallas_call(
        _s2_body,
        grid=(N_FFT // HB_ROWS, nb),
        in_specs=[blk, blk, hblk, hblk, mat, mat],
        out_specs=[blk, blk],
        out_shape=[jax.ShapeDtypeStruct((nb, N_FFT, HY), BF16)] * 2,
        compiler_params=_cparams(("parallel", "parallel")),
        name="spectral_filter",
    )(a_re, a_im, h_re, h_im, wf, wi)


def _s1i_body(chain, cre_ref, cim_ref, g_ref, a_ref, b_ref, sk_ref, *rest):
    if chain:
        m_ref, o_ref, are_ref, aim_ref = rest
    else:
        (o_ref,) = rest
    rows = FFT_A * FFT_T
    for p in range(S1_TILES):
        cre, cim = cre_ref[:, p].reshape(rows, HY), cim_ref[:, p].reshape(rows, HY)
        c = jnp.concatenate([jnp.concatenate([cre, cim], axis=1), jnp.concatenate([cim, -cre], axis=1)], axis=0)
        y = jnp.dot(g_ref[p], c, preferred_element_type=F32)
        gated = []
        for i in range(2):
            seq = slice(i * HALF_A, (i + 1) * HALF_A)
            yi = y[:, i * HY:(i + 1) * HY].reshape(HALF_A, FFT_T, HY)
            z = a_ref[seq, p] * (yi + b_ref[seq, p] * sk_ref[...])
            o_ref[seq, p] = z
            gated.append(z.reshape(HALF_A * FFT_T, HY))
        if chain:
            r = _dot(m_ref[p], jnp.concatenate(gated, axis=1))
            are_ref[:, p] = (r[:rows, :HY] - r[rows:, HY:]).reshape(FFT_A, FFT_T, HY).astype(BF16)
            aim_ref[:, p] = (r[rows:, :HY] + r[:rows, HY:]).reshape(FFT_A, FFT_T, HY).astype(BF16)


def _idft_stage1_gate(c_re, c_im, s1i_tab, a4, b4, skip, s1_tab=None):
    chain = s1_tab is not None
    npair = NB_BATCH // 2
    data = _PAIR_SPEC
    spec = pl.BlockSpec((None, FFT_A, S1_TILES, FFT_T, HY), lambda b, j: (b, 0, j, 0, 0))
    ins = [spec, spec, pl.BlockSpec((S1_TILES, HALF_A * FFT_T, 2 * FFT_A * FFT_T), lambda b, j: (j, 0, 0)),
           data, data, pl.BlockSpec((1, 1, HY), lambda b, j: (0, 0, 0))]
    args = [c_re, c_im, s1i_tab, a4, b4, skip.reshape(1, 1, HY)]
    outs, shapes = [data], [jax.ShapeDtypeStruct((T_LAT // FFT_R, FFT_B, FFT_T, HY), F32)]
    if chain:
        ins.append(pl.BlockSpec((S1_TILES, 2 * FFT_A * FFT_T, HALF_A * FFT_T), lambda b, j: (j, 0, 0)))
        args.append(s1_tab)
        outs += [spec, spec]
        shapes += [jax.ShapeDtypeStruct((npair, FFT_A, FFT_B, FFT_T, HY), BF16)] * 2
    res = pl.pallas_call(
        functools.partial(_s1i_body, chain),
        grid=(npair, FFT_B // S1_TILES),
        in_specs=ins,
        out_specs=outs,
        out_shape=shapes,
        compiler_params=_cparams(("parallel", "parallel"), 56),
        name="idft_stage1_gate",
    )(*args)
    return res if chain else res[0]


def _hyc_body(v_ref, x1_ref, x2_ref, circ_ref, fc_ref, fs_ref, sk_ref, o_ref):
    fc, fs = fc_ref[...], fs_ref[...]
    circ = circ_ref[...]
    h_re, h_im = _dot(fc, circ), -_dot(fs, circ)

    def long_conv(s, o):
        sl = slice(o * HY, (o + 1) * HY)
        s_re, s_im = _dot(fc[:, :CTX], s), -_dot(fs[:, :CTX], s)
        hre, him = h_re[:, sl], h_im[:, sl]
        y_re = s_re * hre - s_im * him
        y_im = s_re * him + s_im * hre
        y = (_dot(fc[:CTX, :], y_re) - _dot(fs[:CTX, :], y_im)) * (1.0 / N_CTXF)
        return y + s * sk_ref[o:o + 1, :]

    o_ref[...] = x2_ref[...] * long_conv(x1_ref[...] * long_conv(v_ref[...], 0), 1)


def _hyena_ctx(hv, hx1, hx2, circ_c, fc, fs, skip):
    const = lambda shape: pl.BlockSpec(shape, lambda b: (0, 0))
    seq = pl.BlockSpec((CTX, HY), lambda b: (T_LAT // CTX + b, 0))
    return pl.pallas_call(
        _hyc_body,
        grid=(NB_BATCH,),
        in_specs=[seq, seq, seq, const((N_CTXF, 2 * HY)), const((N_CTXF, N_CTXF)), const((N_CTXF, N_CTXF)), const((2, HY))],
        out_specs=pl.BlockSpec((CTX, HY), lambda b: (b, 0)),
        out_shape=jax.ShapeDtypeStruct((T_CTX, HY), F32),
        compiler_params=_cparams(("parallel",)),
        name="hyena_ctx",
    )(hv, hx1, hx2, circ_c, fc, fs, skip)


N_ML_STEPS = CTX // CHUNK + SEQ // CHUNK


def _ml_chain(direction, qv_ref, k_ref, src_col, cum_r, gates_t, c_scr, m_scr, mask, ones_rows):
    base = 8 * direction
    outs = []
    for h in range(ML_H):
        ic, fc = base + h, base + 4 + h
        chain = direction * ML_H + h
        b_row, li_row = cum_r[fc:fc + 1, :], gates_t[ic:ic + 1, :]
        b_end = b_row[:, CHUNK - 1:CHUNK] if direction == 0 else b_row[:, 0:1]
        q_t = qv_ref[HD * h:HD * (h + 1), :]
        vext_t = jnp.concatenate([qv_ref[ML_W + HD * h:ML_W + HD * (h + 1), :], ones_rows], axis=0)
        k = k_ref[:, HD * h:HD * (h + 1)]
        c_prev, m_prev = c_scr[chain], m_scr[chain]
        dmat = jnp.where(mask, src_col[:, fc:fc + 1] + b_row, NEG)
        m_intra = jnp.max(dmat, axis=0, keepdims=True)
        s_t = jnp.dot(k, q_t, preferred_element_type=F32) * jnp.exp(dmat - m_intra)
        inter = b_row + m_prev
        m_t = jnp.maximum(inter, m_intra)
        hx = jnp.exp(m_intra - m_t) * _dot(vext_t, s_t) + jnp.exp(inter - m_t) * _dot(c_prev, q_t)
        den = jnp.maximum(jnp.abs(hx[HD:HD + 1, :]), jnp.exp(-m_t))
        outs.append(hx[:HD, :] / den)
        g_row = b_end - b_row + li_row
        m_new = jnp.maximum(b_end + m_prev, jnp.max(g_row, axis=1, keepdims=True))
        c_scr[chain] = jnp.exp(b_end + m_prev - m_new) * c_prev + _dot(vext_t * jnp.exp(g_row - m_new), k)
        m_scr[chain] = m_new
    return jnp.concatenate(outs, axis=0)


ML_BPS = 2
ML_GROUP_COLS = (NB_BATCH // ML_BPS) * (SEQ + CTX)


def _ml_body(*refs):
    n_in = 8 * ML_BPS
    seq_refs, (bias_ref, bias_t_ref, tril_ref, triu_ref) = refs[:n_in], refs[n_in:n_in + 4]
    (hf_ref, hb_ref), (c_scr, m_scr) = refs[n_in + 4:n_in + 6], refs[n_in + 6:]

    @pl.when(pl.program_id(1) == 0)
    def _():
        c_scr[...] = jnp.zeros_like(c_scr)
        m_scr[...] = jnp.zeros_like(m_scr)

    tril, triu = tril_ref[...], triu_ref[...]
    src = lax.broadcasted_iota(jnp.int32, (CHUNK, CHUNK), 0)
    dst = lax.broadcasted_iota(jnp.int32, (CHUNK, CHUNK), 1)
    ones_rows = (lax.broadcasted_iota(jnp.int32, (HD, CHUNK), 0) == 0).astype(BF16)
    for sub in range(ML_BPS):
        c_sub, m_sub = c_scr.at[sub], m_scr.at[sub]
        for direction in range(2):
            qv_ref, k_ref, g_ref, gt_ref = seq_refs[8 * sub + 4 * direction:8 * sub + 4 * direction + 4]
            gates = g_ref[...] + bias_ref[...]
            gates_t = gt_ref[...] + bias_t_ref[...]
            ls, ls_t = _log_sigmoid(gates), _log_sigmoid(gates_t)
            if direction == 0:
                cum_c, cum_r, mask = _dot_sel_l(tril, ls), _dot_sel_r(ls_t, triu), src <= dst
            else:
                cum_c, cum_r, mask = _dot_sel_l(triu, ls), _dot_sel_r(ls_t, tril), src >= dst
            src_col = pltpu.roll(gates, 4, 1) - cum_c
            o_ref = hf_ref if direction == 0 else hb_ref
            o_ref[sub] = _ml_chain(direction, qv_ref, k_ref, src_col, cum_r, gates_t, c_sub, m_sub, mask, ones_rows)


def _mlstm(mqv_t, mk, mg, mg_t, gate_b):
    nctx = CTX // CHUNK
    nlat = SEQ // CHUNK
    per_group = NB_BATCH // ML_BPS

    def step_chunk(i, backward):
        ctx_chunk = (nctx - 1 - i) if backward else i
        lat_chunk = (N_ML_STEPS - 1 - i) if backward else (i - nctx)
        return i < nctx, ctx_chunk, lat_chunk

    def in_chunk(sub, backward):
        def f(p, i):
            b = p + per_group * sub
            is_ctx, cc, lc = step_chunk(i, backward)
            return jnp.where(is_ctx, T_LAT // CHUNK + nctx * b + cc, nlat * b + lc)
        return f

    def out_chunk(backward):
        def f(p, i):
            is_ctx, cc, lc = step_chunk(i, backward)
            return jnp.where(is_ctx, per_group * nlat + nctx * p + cc, nlat * p + lc)
        return f

    bias = jnp.pad(gate_b, (0, 128 - N_GATES)).reshape(1, 128)
    bias_t = gate_b.reshape(N_GATES, 1)
    tril = jnp.asarray(np.tril(np.ones((CHUNK, CHUNK), np.float32))).astype(BF16)
    triu = jnp.asarray(np.triu(np.ones((CHUNK, CHUNK), np.float32))).astype(BF16)
    const = lambda shape: pl.BlockSpec(shape, lambda p, i: (0, 0))
    ins, args = [], []
    for sub in range(ML_BPS):
        for backward in (False, True):
            f = in_chunk(sub, backward)
            ins += [pl.BlockSpec((2 * ML_W, CHUNK), lambda p, i, f=f: (0, f(p, i))),
                    pl.BlockSpec((CHUNK, ML_W), lambda p, i, f=f: (f(p, i), 0)),
                    pl.BlockSpec((CHUNK, 128), lambda p, i, f=f: (f(p, i), 0)),
                    pl.BlockSpec((N_GATES, CHUNK), lambda p, i, f=f: (0, f(p, i)))]
            args += [mqv_t, mk, mg, mg_t]
    outs = [pl.BlockSpec((ML_BPS, ML_W, CHUNK), lambda p, i, f=out_chunk(bw): (0, 0, f(p, i))) for bw in (False, True)]
    return pl.pallas_call(
        _ml_body,
        grid=(per_group, N_ML_STEPS),
        in_specs=ins + [const((1, 128)), const((N_GATES, 1)), const((CHUNK, CHUNK)), const((CHUNK, CHUNK))],
        out_specs=outs,
        out_shape=[jax.ShapeDtypeStruct((ML_BPS, ML_W, ML_GROUP_COLS), F32)] * 2,
        scratch_shapes=[pltpu.VMEM((ML_BPS, 2 * ML_H, 2 * HD, HD), F32), pltpu.VMEM((ML_BPS, 2 * ML_H, 1, 1), F32)],
        compiler_params=_cparams(("parallel", "arbitrary")),
        name="mlstm",
    )(*args, bias, bias_t, tril, triu)


FF_CHUNKS = ((0, 1536), (1536, D_FF))


def _out_ffn_body(al_ref, ac_ref, hl_ref, hc_ref, hf_ref, hb_ref, mo_ref, mnw_ref, bd_ref, wo_ref, x_ref,
                  g1_ref, sh2_ref, sc2_ref, g2_ref, nw2_ref, wg_ref, wu_ref, wd_ref, o_ref):
    i = pl.program_id(0)
    hn = _head_rms((hf_ref[...] + hb_ref[...]).T, bd_ref[...], mnw_ref[...])
    mlo = hn * _sigmoid(mo_ref[...])
    att = jnp.where(i < NT_LAT, al_ref[...], ac_ref[...])
    hy = jnp.where(i < NT_LAT, hl_ref[...], hc_ref[...])
    mixed = jnp.concatenate([att.astype(BF16), hy.astype(BF16), mlo.astype(BF16)], axis=1)
    x = x_ref[...] + g1_ref[...] * jnp.dot(mixed, wo_ref[...], preferred_element_type=F32)
    hb = _modulated_norm(x, nw2_ref[...], sc2_ref[...], sh2_ref[...]).astype(BF16)
    acc = None
    for lo, hi in FF_CHUNKS:
        sl = slice(lo, hi)
        a = jnp.dot(hb, wg_ref[:, sl], preferred_element_type=F32)
        u = jnp.dot(hb, wu_ref[:, sl], preferred_element_type=F32)
        part = _dot(a * _sigmoid(a) * u, wd_ref[sl, :])
        acc = part if acc is None else acc + part
    o_ref[...] = x + g2_ref[...] * acc


def _out_ffn(att, attc, hyo, hyoc, hf_t, hb_t, mo, ml_nw, w_out_b, xs, mod4, layer, nw2, wg, wu, wd, with_ctx):
    assert ATT_W == HY
    tiles = NT_ALL if with_ctx else NT_LAT
    row = lambda w: pl.BlockSpec((TM, w), lambda i: (i, 0))
    lat_rows = pl.BlockSpec((TM, HY), lambda i: (jnp.minimum(i, NT_LAT - 1), 0))
    ctx_rows = pl.BlockSpec((TM, HY), lambda i: (jnp.maximum(i - NT_LAT, 0), 0))
    lat_tiles = (NB_BATCH // ML_BPS) * TILES_PER_SEQ
    col = pl.BlockSpec((None, ML_W, TM), lambda i: (jnp.where(i < NT_LAT, i // lat_tiles, i - NT_LAT), 0,
                                                    jnp.where(i < NT_LAT, i % lat_tiles, lat_tiles)))
    resident = lambda shape: pl.BlockSpec(shape, lambda i: (0, 0), pipeline_mode=pl.Buffered(1))
    layer_w = lambda shape: pl.BlockSpec((None,) + shape, lambda i: (layer, 0, 0), pipeline_mode=pl.Buffered(1))
    vec = lambda w: pl.BlockSpec((1, w), lambda i: (0, 0))
    bd = jnp.asarray(_blockdiag_ones(ML_W)).astype(BF16)
    return pl.pallas_call(
        _out_ffn_body,
        grid=(tiles,),
        in_specs=[lat_rows, ctx_rows, lat_rows, ctx_rows,
                  col, col, row(ML_W), vec(ML_W), resident((ML_W, ML_W)), layer_w((D, D)), row(D),
                  _mod_spec(layer, 2), _mod_spec(layer, 3), _mod_spec(layer, 4), _mod_spec(layer, 5), vec(D),
                  layer_w((D, D_FF)), layer_w((D, D_FF)), layer_w((D_FF, D))],
        out_specs=row(D),
        out_shape=jax.ShapeDtypeStruct((tiles * TM, D), F32),
        compiler_params=_cparams(("parallel",), 56),
        name="out_ffn",
    )(att, attc, hyo, hyoc, hf_t, hb_t, mo, ml_nw.reshape(1, ML_W), bd, w_out_b, xs, mod4, mod4, mod4, mod4,
      nw2.reshape(1, D), wg, wu, wd)


def _rope_tables():
    n_rows = SEQ // GRID_W
    row = jnp.repeat(jnp.arange(n_rows), GRID_W)
    col = jnp.tile(jnp.arange(GRID_W), n_rows)
    nf = HD // 4
    inv_freq = ROPE_BASE ** (-jnp.arange(nf, dtype=F32) / nf)
    ang = jnp.stack([row[:, None] * inv_freq, col[:, None] * inv_freq], axis=1)
    cos, sin = jnp.cos(ang), jnp.sin(ang)
    cos_h = jnp.concatenate([cos, cos], axis=-1).reshape(SEQ, HD)
    sin_h = jnp.concatenate([-sin, sin], axis=-1).reshape(SEQ, HD)
    cos_t = jnp.concatenate([jnp.tile(cos_h, (1, 2)), jnp.ones((TM, 128), F32)], axis=0)
    sin_t = jnp.concatenate([jnp.tile(sin_h, (1, 2)), jnp.zeros((TM, 128), F32)], axis=0)
    return cos_t, sin_t


def kernel(x, c, ctx, c_ctx, w_mod, b_mod, norm1_w, norm2_w, w_in, w_out, q_norm_w, k_norm_w, attn_sink, hy_conv_w, hy_conv_b, hy_w1, hy_b1, hy_freq, hy_w2, hy_b2, hy_w3, hy_decay, hy_skip, ml_gate_b, ml_norm_w, ffn_w_gate, ffn_w_up, ffn_w_down):
    xs = jnp.concatenate([x.reshape(T_LAT, D), ctx.reshape(T_CTX, D)], axis=0)
    cc8 = jnp.concatenate([c, c_ctx[None, :], jnp.zeros((8 - NB_BATCH - 1, D), F32)], axis=0)
    mod4 = _modulation(cc8, w_mod, b_mod).reshape(DEPTH, 8, 1, 6 * D)

    cos_t, sin_t = _rope_tables()
    feats_l, feats_c = _pos_feats_circ(SEQ), _pos_feats_circ(CTX)
    s1_tab = jnp.asarray(_TAB["s1"]).astype(BF16)
    s1i_tab = jnp.asarray(_TAB["s1i"]).astype(BF16)
    wf, wi = jnp.asarray(_TAB["wf"]).astype(BF16), jnp.asarray(_TAB["wi"]).astype(BF16)
    fc, fs = jnp.asarray(_TAB["fc"]).astype(BF16), jnp.asarray(_TAB["fs"]).astype(BF16)

    w_in_p = _reorder_w_in(w_in)
    w_out_b = w_out.astype(BF16)
    wg_b, wu_b, wd_b = ffn_w_gate.astype(BF16), ffn_w_up.astype(BF16), ffn_w_down.astype(BF16)
    filt_params = _filter_params(hy_w1, hy_b1, hy_freq, hy_w2, hy_b2, hy_w3)
    sink8 = jnp.pad(attn_sink, ((0, 0), (0, 128 - ATT_H)))[:, None, :] * jnp.ones((1, 8, 1), F32)
    zeros_tile = jnp.zeros((TM, HY), F32)

    out = None
    for l in range(DEPTH):
        last = l == DEPTH - 1
        qt, kn, vt, hv, hx1, hx2, mqv_t, mk, mo, mg, mg_t = _in_proj(
            xs, mod4, l, norm1_w[l], w_in_p, cos_t, sin_t, q_norm_w[l], k_norm_w[l], hy_conv_w[l], hy_conv_b[l])

        att = _attention(qt, kn, vt, sink8[l])
        attc = zeros_tile if last else _attention_ctx(qt, kn, vt, sink8[l])

        dec = hy_decay[l].reshape(1, N_FILT)
        circ = _circular_filters(feats_l, filt_params, l, dec)
        f_re, f_im = _dft_stage1(circ.reshape(2, HALF_A, FFT_B, FFT_T, 2 * HY), s1_tab)
        h_re, h_im = _filter_spectrum(f_re.reshape(2, 2, N_FFT, HY), f_im.reshape(2, 2, N_FFT, HY), wf)

        rows = lambda t: t.reshape(NB_BATCH // 2, N_FFT, HY)
        tiles = lambda t: t.reshape(NB_BATCH // 2, FFT_A, FFT_B, FFT_T, HY)
        v4 = _row_blocks(hv)
        a_re, a_im = _dft_stage1_pair(v4, s1_tab)
        c_re, c_im = _spectral_filter(rows(a_re), rows(a_im), h_re, h_im, 0, wf, wi)
        z4, a_re, a_im = _idft_stage1_gate(tiles(c_re), tiles(c_im), s1i_tab, _row_blocks(hx1), v4, hy_skip[l, 0], s1_tab)
        c_re, c_im = _spectral_filter(rows(a_re), rows(a_im), h_re, h_im, 1, wf, wi)
        hyo = _idft_stage1_gate(tiles(c_re), tiles(c_im), s1i_tab, _row_blocks(hx2), z4, hy_skip[l, 1]).reshape(T_LAT, HY)

        if last:
            hyoc = zeros_tile
        else:
            circ_c = _circular_filters(feats_c, filt_params, l, dec)
            hyoc = _hyena_ctx(hv, hx1, hx2, circ_c, fc, fs, hy_skip[l])

        hf_t, hb_t = _mlstm(mqv_t, mk, mg, mg_t, ml_gate_b[l])

        xs = _out_ffn(att, attc, hyo, hyoc, hf_t, hb_t, mo, ml_norm_w[l], w_out_b, xs, mod4, l,
                      norm2_w[l], wg_b, wu_b, wd_b, not last)
        out = xs
    return out.reshape(NB_BATCH, SEQ, D)
```
